```python
import math
import jax, jax.numpy as jnp
from jax import lax
import numpy as np

D_MODEL = 1024
BATCH = 8
SEQ = 2048
DEPTH = 1
DEC_BATCH = 128
DEC_SEQ = 4
PAST_LEN = 16384
PAGE_SIZE = 128

D_A = D_MODEL
HEAD_A = 64
N_HEADS_A = D_A // HEAD_A
DECAY_LORA = 64
ICLR_LORA = 64
GATE_LORA = 128
RWKV_COLS = 3 * D_A + DECAY_LORA + ICLR_LORA + GATE_LORA
GN_EPS = HEAD_A * 1e-5
D_LRU = D_MODEL
LRU_BLOCKS = 16
LRU_BS = D_LRU // LRU_BLOCKS
CONV_W = 4
LRU_C = 8.0
N_IN = RWKV_COLS + 2 * D_LRU + 2 * D_MODEL
D_FF = ((8 * D_MODEL // 3 + 255) // 256) * 256
ALPHA = (2.0 * DEPTH) ** 0.25
BETA = (8.0 * DEPTH) ** -0.25
LN_EPS = 1e-5

kernel_name = "rwkv7_rglru_gated_hybrid_step"


def _layer_norm(x, g, b):
    xf = x.astype(jnp.float32)
    mu = xf.mean(-1, keepdims=True)
    var = jnp.square(xf - mu).mean(-1, keepdims=True)
    return ((xf - mu) * lax.rsqrt(var + LN_EPS) * g + b).astype(x.dtype)


def _wkv7_scan(r, decay, k, v, kk, a, S0):
    def step(S, inp):
        r_t, w_t, k_t, v_t, kk_t, a_t = inp
        sa = jnp.einsum('bhij,bhj->bhi', S, -kk_t)
        S = (S * w_t[:, :, None, :]
             + sa[..., None] * (kk_t * a_t)[:, :, None, :]
             + v_t[..., None] * k_t[:, :, None, :])
        return S, jnp.einsum('bhij,bhj->bhi', S, r_t)
    xs = tuple(jnp.swapaxes(t.astype(jnp.float32), 0, 1) for t in (r, decay, k, v, kk, a))
    S, y = lax.scan(step, S0.astype(jnp.float32), xs)
    return jnp.swapaxes(y, 0, 1), S


def _rwkv7_branch(z_rw, z_prev_first, S0, mu, w0, w2, a0, a2, g2, k_k, k_a, r_k, lnx_g, lnx_b):
    B, T, _ = z_rw.shape
    z_prev = jnp.concatenate([z_prev_first[:, None].astype(z_rw.dtype), z_rw[:, :-1]], axis=1)
    zm = z_rw + mu * (z_prev - z_rw)
    o = 0
    r = zm[..., o:o + D_A]; o += D_A
    k = zm[..., o:o + D_A]; o += D_A
    v = zm[..., o:o + D_A]; o += D_A
    w_lo = zm[..., o:o + DECAY_LORA]; o += DECAY_LORA
    a_lo = zm[..., o:o + ICLR_LORA]; o += ICLR_LORA
    g_lo = zm[..., o:o + GATE_LORA]
    w = -jax.nn.softplus(-(w0 + jnp.tanh(w_lo) @ w2).astype(jnp.float32)) - 0.5
    decay = jnp.exp(-jnp.exp(w))
    a = jax.nn.sigmoid(a0 + a_lo @ a2)
    g = jax.nn.sigmoid(g_lo) @ g2
    heads = lambda t: t.reshape(B, T, N_HEADS_A, HEAD_A)
    kk = heads(k * k_k).astype(jnp.float32)
    kk = kk / jnp.maximum(jnp.sqrt(jnp.sum(kk * kk, -1, keepdims=True)), 1e-12)
    k = k * (1.0 + (a - 1.0) * k_a)
    rh, kh, vh = heads(r), heads(k), heads(v)
    y, S = _wkv7_scan(rh, heads(decay), kh, vh, kk, heads(a), S0)
    mu_y = y.mean(-1, keepdims=True)
    var_y = jnp.square(y - mu_y).mean(-1, keepdims=True)
    y = ((y - mu_y) * lax.rsqrt(var_y + GN_EPS)).reshape(B, T, D_A) * lnx_g + lnx_b
    bonus = (jnp.sum((rh * kh * r_k).astype(jnp.float32), -1, keepdims=True) * vh).reshape(B, T, D_A)
    return (y + bonus) * g, S


def _combine(c1, c2):
    a1, b1 = c1
    a2, b2 = c2
    return a1 * a2, a2 * b1 + b2


def _rglru_branch(xb, gate_in, conv_buf, h0, conv_w, conv_b, wa, ba, wi, bi, lam, reset_first):
    B, T, _ = xb.shape
    xpad = jnp.concatenate([conv_buf.astype(xb.dtype), xb], axis=1)
    u = conv_b + sum(conv_w[j] * xpad[:, j:j + T] for j in range(CONV_W))
    new_conv = xpad[:, T:]
    ub = u.reshape(B, T, LRU_BLOCKS, LRU_BS)
    r = jax.nn.sigmoid(jnp.einsum('btnc,ncd->btnd', ub, wa) + ba).reshape(B, T, D_LRU)
    i = jax.nn.sigmoid(jnp.einsum('btnc,ncd->btnd', ub, wi) + bi).reshape(B, T, D_LRU)
    log_a = -LRU_C * r.astype(jnp.float32) * jax.nn.softplus(-lam.astype(jnp.float32))
    a = jnp.exp(log_a)
    mult = jnp.sqrt(-jnp.expm1(2.0 * log_a))
    if reset_first:
        mult = mult.at[:, 0].set(1.0)
    bx = mult * i.astype(jnp.float32) * u.astype(jnp.float32)
    bx = bx.at[:, 0].add(a[:, 0] * h0.astype(jnp.float32))
    _, h = lax.associative_scan(_combine, (a, bx), axis=1)
    out = h * jax.nn.gelu(gate_in.astype(jnp.float32))
    return out, new_conv, h[:, -1]


def _layer(x, shift_buf, wkv0, conv_buf, h0, reset_first,
           w_in, tmix_mu, w0, w2_decay, a0, a2_iclr, g2_gate, k_k, k_a, r_k, lnx_g, lnx_b,
           conv_w, conv_b, lru_wa, lru_ba, lru_wi, lru_bi, lru_lambda, w_o,
           ln1_g, ln1_b, w_ffn_gate, w_ffn_up, w_ffn_down, ln2_g, ln2_b):
    z = jnp.einsum('btd,de->bte', x, w_in)
    z_prev_first = shift_buf.astype(x.dtype) @ w_in[:, :RWKV_COLS]
    yA, S = _rwkv7_branch(z[..., :RWKV_COLS], z_prev_first, wkv0, tmix_mu, w0, w2_decay, a0,
                          a2_iclr, g2_gate, k_k, k_a, r_k, lnx_g, lnx_b)
    o = RWKV_COLS
    xb = z[..., o:o + D_LRU]
    gb = z[..., o + D_LRU:o + 2 * D_LRU]
    o2 = o + 2 * D_LRU
    gate_a = jax.nn.sigmoid(z[..., o2:o2 + D_MODEL].astype(jnp.float32))
    gate_b = jax.nn.sigmoid(z[..., o2 + D_MODEL:].astype(jnp.float32))
    yB, new_conv, h_last = _rglru_branch(xb, gb, conv_buf, h0, conv_w, conv_b, lru_wa, lru_ba,
                                         lru_wi, lru_bi, lru_lambda, reset_first)
    merged = (gate_a * yA + gate_b * yB).astype(x.dtype)
    mix = jnp.einsum('bte,ed->btd', merged, w_o)
    h1 = _layer_norm(ALPHA * x + mix, ln1_g, ln1_b)
    ffn = (jax.nn.silu(h1 @ w_ffn_gate) * (h1 @ w_ffn_up)) @ w_ffn_down
    y = _layer_norm(ALPHA * h1 + ffn, ln2_g, ln2_b)
    return y, x[:, -1], S, new_conv, h_last


def setup_inputs(seed: int = 0) -> dict:
    key = jax.random.key(seed)
    ks = jax.random.split(key, 40)
    nrm = lambda k, shape, s: jax.random.normal(k, shape, jnp.float32) * s
    L = DEPTH
    u_lam = jax.random.uniform(ks[20], (L, D_LRU), jnp.float32, 0.9, 0.999) ** (1.0 / LRU_C)
    return {
        "x_prompt": nrm(ks[0], (BATCH, SEQ, D_MODEL), 1.0),
        "x_sample": nrm(ks[1], (DEC_BATCH, DEC_SEQ, D_MODEL), 1.0),
        "state_shift": nrm(ks[2], (L, DEC_BATCH, D_MODEL), 1.0),
        "state_wkv": nrm(ks[3], (L, DEC_BATCH, N_HEADS_A, HEAD_A, HEAD_A), 0.5),
        "state_conv": nrm(ks[4], (L, DEC_BATCH, CONV_W - 1, D_LRU), 1.0),
        "state_lru": nrm(ks[5], (L, DEC_BATCH, D_LRU), 0.5),
        "w_in": nrm(ks[6], (L, D_MODEL, N_IN), D_MODEL ** -0.5),
        "tmix_mu": jax.random.uniform(ks[7], (L, RWKV_COLS), jnp.float32, 0.0, 1.0),
        "w0": jax.random.uniform(ks[8], (L, D_A), jnp.float32, -4.0, 1.0),
        "w2_decay": nrm(ks[9], (L, DECAY_LORA, D_A), 0.1 * DECAY_LORA ** -0.5),
        "a0": nrm(ks[10], (L, D_A), 0.1),
        "a2_iclr": nrm(ks[11], (L, ICLR_LORA, D_A), 0.5 * ICLR_LORA ** -0.5),
        "g2_gate": nrm(ks[12], (L, GATE_LORA, D_A), GATE_LORA ** -0.5),
        "k_k": 0.85 + nrm(ks[13], (L, D_A), 0.02),
        "k_a": 1.0 + nrm(ks[14], (L, D_A), 0.02),
        "r_k": nrm(ks[15], (L, N_HEADS_A, HEAD_A), 0.1),
        "lnx_g": 1.0 + nrm(ks[16], (L, D_A), 0.02),
        "lnx_b": nrm(ks[17], (L, D_A), 0.02),
        "conv_w": nrm(ks[18], (L, CONV_W, D_LRU), CONV_W ** -0.5),
        "conv_b": nrm(ks[19], (L, D_LRU), 0.02),
        "lru_wa": nrm(ks[21], (L, LRU_BLOCKS, LRU_BS, LRU_BS), LRU_BS ** -0.5),
        "lru_ba": nrm(ks[22], (L, LRU_BLOCKS, LRU_BS), 0.02),
        "lru_wi": nrm(ks[23], (L, LRU_BLOCKS, LRU_BS, LRU_BS), LRU_BS ** -0.5),
        "lru_bi": nrm(ks[24], (L, LRU_BLOCKS, LRU_BS), 0.02),
        "lru_lambda": jnp.log(u_lam / (1.0 - u_lam)),
        "w_o": nrm(ks[25], (L, D_MODEL, D_MODEL), BETA * D_MODEL ** -0.5),
        "ln1_g": 1.0 + nrm(ks[26], (L, D_MODEL), 0.02),
        "ln1_b": nrm(ks[27], (L, D_MODEL), 0.02),
        "w_ffn_gate": nrm(ks[28], (L, D_MODEL, D_FF), D_MODEL ** -0.5),
        "w_ffn_up": nrm(ks[29], (L, D_MODEL, D_FF), D_MODEL ** -0.5),
        "w_ffn_down": nrm(ks[30], (L, D_FF, D_MODEL), BETA * D_FF ** -0.5),
        "ln2_g": 1.0 + nrm(ks[31], (L, D_MODEL), 0.02),
        "ln2_b": nrm(ks[32], (L, D_MODEL), 0.02),
    }


def reference(x_prompt, x_sample, state_shift, state_wkv, state_conv, state_lru,
              w_in, tmix_mu, w0, w2_decay, a0, a2_iclr, g2_gate, k_k, k_a, r_k, lnx_g, lnx_b,
              conv_w, conv_b, lru_wa, lru_ba, lru_wi, lru_bi, lru_lambda, w_o,
              ln1_g, ln1_b, w_ffn_gate, w_ffn_up, w_ffn_down, ln2_g, ln2_b):
    params = (w_in, tmix_mu, w0, w2_decay, a0, a2_iclr, g2_gate, k_k, k_a, r_k, lnx_g, lnx_b,
              conv_w, conv_b, lru_wa, lru_ba, lru_wi, lru_bi, lru_lambda, w_o,
              ln1_g, ln1_b, w_ffn_gate, w_ffn_up, w_ffn_down, ln2_g, ln2_b)
    Bp = x_prompt.shape[0]
    yp, ys = x_prompt, x_sample
    sp_shift, sp_wkv, sp_conv, sp_lru = [], [], [], []
    ss_shift, ss_wkv, ss_conv, ss_lru = [], [], [], []
    for l in range(DEPTH):
        p = [w[l] for w in params]
        yp, s1, s2, s3, s4 = _layer(
            yp, jnp.zeros((Bp, D_MODEL), yp.dtype),
            jnp.zeros((Bp, N_HEADS_A, HEAD_A, HEAD_A), jnp.float32),
            jnp.zeros((Bp, CONV_W - 1, D_LRU), yp.dtype),
            jnp.zeros((Bp, D_LRU), jnp.float32), True, *p)
        sp_shift.append(s1); sp_wkv.append(s2); sp_conv.append(s3); sp_lru.append(s4)
        ys, t1, t2, t3, t4 = _layer(ys, state_shift[l], state_wkv[l], state_conv[l], state_lru[l],
                                    False, *p)
        ss_shift.append(t1); ss_wkv.append(t2); ss_conv.append(t3); ss_lru.append(t4)
    return (yp, ys,
            jnp.stack(sp_shift), jnp.stack(sp_wkv), jnp.stack(sp_conv), jnp.stack(sp_lru),
            jnp.stack(ss_shift), jnp.stack(ss_wkv), jnp.stack(ss_conv), jnp.stack(ss_lru))
```

```python
import functools
import math

import jax
import jax.numpy as jnp
from jax import lax
from jax.experimental import pallas as pl
from jax.experimental.pallas import tpu as pltpu

F32 = jnp.float32
BF16 = jnp.bfloat16

D_MODEL = 1024
HEAD = 64
N_HEADS = D_MODEL // HEAD
PAIR = 2 * HEAD
N_PAIRS = N_HEADS // 2
DECAY_LORA = 64
ICLR_LORA = 64
GATE_LORA = 128
LORA = DECAY_LORA + ICLR_LORA + GATE_LORA
RWKV_COLS = 3 * D_MODEL + LORA
GN_EPS = HEAD * 1e-5
D_LRU = D_MODEL
LRU_BLOCKS = 16
LRU_BS = D_LRU // LRU_BLOCKS
LRU_GROUP = 256
CONV_W = 4
LRU_C = 8.0
LRU_COLS = 2 * D_LRU + 2 * D_MODEL
D_FF = 2816
ALPHA = 2.0 ** 0.25
LN_EPS = 1e-5

SUBLANES = 8
CHUNK = 64
VMEM_LIMIT = 56 * 1024 * 1024


def _softplus(x):
    return jnp.maximum(x, 0.0) + jnp.log1p(jnp.exp(-jnp.abs(x)))


def _gelu_tanh(x):
    c = math.sqrt(2.0 / math.pi)
    return x * (0.5 * (1.0 + jnp.tanh(c * (x + 0.044715 * (x * x * x)))))


def _layer_norm(x, g, b):
    mu = jnp.mean(x, axis=-1, keepdims=True)
    xc = x - mu
    var = jnp.mean(xc * xc, axis=-1, keepdims=True)
    return xc * lax.rsqrt(var + LN_EPS) * g + b


def _dot(a, b):
    return jnp.dot(a.astype(BF16), b.astype(BF16), preferred_element_type=F32)


def _dot_nt(a, b):
    return lax.dot_general(a.astype(BF16), b.astype(BF16), (((1,), (1,)), ((), ())),
                           preferred_element_type=F32)


def _dot_tn(a, b):
    return lax.dot_general(a.astype(BF16), b.astype(BF16), (((0,), (0,)), ((), ())),
                           preferred_element_type=F32)


def _dot_f32(a, b):
    return jnp.dot(a, b, preferred_element_type=F32, precision=lax.Precision.HIGHEST)


def _rwkv_prep(z, zprev, mu, w2ext, w0, a0, k_k, k_a):
    zm = z + mu * (zprev - z)
    r = zm[:, 0:D_MODEL]
    k = zm[:, D_MODEL:2 * D_MODEL]
    v = zm[:, 2 * D_MODEL:3 * D_MODEL]
    lo = zm[:, 3 * D_MODEL:RWKV_COLS]
    lane = lax.broadcasted_iota(jnp.int32, lo.shape, 1)
    act = jnp.where(lane < DECAY_LORA, jnp.tanh(lo),
                    jnp.where(lane < DECAY_LORA + ICLR_LORA, lo, jax.nn.sigmoid(lo)))
    lora = _dot(act, w2ext)
    w = -_softplus(-(w0 + lora[:, 0:D_MODEL])) - 0.5
    lw = -jnp.exp(w)
    a = jax.nn.sigmoid(a0 + lora[:, D_MODEL:2 * D_MODEL])
    g = lora[:, 2 * D_MODEL:3 * D_MODEL]
    kkraw = k * k_k
    k2 = k * (1.0 + (a - 1.0) * k_a)
    return r, lw, k2, v, kkraw, a, g


def _rwkv_proj_seq_kernel(x_ref, w_ref, mu_ref, w2_ref, w0_ref, a0_ref, kk_ref, ka_ref,
                          r_ref, lw_ref, k_ref, v_ref, kkraw_ref, a_ref, g_ref, hist_ref):
    @pl.when(pl.program_id(1) == 0)
    def _():
        hist_ref[...] = jnp.zeros_like(hist_ref)

    z = _dot(x_ref[0], w_ref[...])
    tm = z.shape[0]
    rolled = pltpu.roll(z, 1, 0)
    row = lax.broadcasted_iota(jnp.int32, (SUBLANES, 1), 0)
    first = jnp.where(row == 0, hist_ref[SUBLANES - 1:SUBLANES, :], rolled[0:SUBLANES])
    zprev = jnp.concatenate([first, rolled[SUBLANES:]], axis=0)
    hist_ref[...] = z[tm - SUBLANES:tm]
    outs = _rwkv_prep(z, zprev, mu_ref[...], w2_ref[...], w0_ref[...], a0_ref[...],
                      kk_ref[...], ka_ref[...])
    for o_ref, o in zip((r_ref, lw_ref, k_ref, v_ref, kkraw_ref, a_ref, g_ref), outs):
        o_ref[0] = o


def _rwkv_proj_step_kernel(x_ref, xp_ref, w_ref, mu_ref, w2_ref, w0_ref, a0_ref, kk_ref, ka_ref,
                           r_ref, lw_ref, k_ref, v_ref, kkraw_ref, a_ref, g_ref):
    z = _dot(x_ref[...], w_ref[...])
    zprev = _dot(xp_ref[...], w_ref[...])
    outs = _rwkv_prep(z, zprev, mu_ref[...], w2_ref[...], w0_ref[...], a0_ref[...],
                      kk_ref[...], ka_ref[...])
    for o_ref, o in zip((r_ref, lw_ref, k_ref, v_ref, kkraw_ref, a_ref, g_ref), outs):
        o_ref[...] = o


def _const_spec(shape):
    nd = len(shape)
    return pl.BlockSpec(shape, lambda *_: (0,) * nd, pipeline_mode=pl.Buffered(1))


def _rwkv_proj_seq(x, wp, tm):
    b, t, _ = x.shape
    row_spec = pl.BlockSpec((1, tm, D_MODEL), lambda i, j: (i, j, 0))
    vec = _const_spec((1, D_MODEL))
    return pl.pallas_call(
        _rwkv_proj_seq_kernel,
        grid=(b, t // tm),
        in_specs=[row_spec, _const_spec((D_MODEL, RWKV_COLS)), _const_spec((1, RWKV_COLS)),
                  _const_spec((LORA, 3 * D_MODEL)), vec, vec, vec, vec],
        out_specs=[row_spec] * 7,
        out_shape=[jax.ShapeDtypeStruct((b, t, D_MODEL), F32)] * 7,
        scratch_shapes=[pltpu.VMEM((SUBLANES, RWKV_COLS), F32)],
        compiler_params=pltpu.CompilerParams(
            dimension_semantics=("arbitrary", "arbitrary"), vmem_limit_bytes=VMEM_LIMIT),
        name="rwkv_proj_seq",
    )(x, wp["w_rw"], wp["mu"], wp["w2ext"], wp["w0"], wp["a0"], wp["k_k"], wp["k_a"])


def _rwkv_proj_step(x, xprev, wp):
    n = x.shape[0]
    full = pl.BlockSpec((n, D_MODEL), lambda i: (0, 0))
    vec = _const_spec((1, D_MODEL))
    return pl.pallas_call(
        _rwkv_proj_step_kernel,
        grid=(1,),
        in_specs=[full, full, _const_spec((D_MODEL, RWKV_COLS)), _const_spec((1, RWKV_COLS)),
                  _const_spec((LORA, 3 * D_MODEL)), vec, vec, vec, vec],
        out_specs=[full] * 7,
        out_shape=[jax.ShapeDtypeStruct((n, D_MODEL), F32)] * 7,
        compiler_params=pltpu.CompilerParams(
            dimension_semantics=("arbitrary",), vmem_limit_bytes=VMEM_LIMIT),
        name="rwkv_proj_step",
    )(x, xprev, wp["w_rw"], wp["mu"], wp["w2ext"], wp["w0"], wp["a0"], wp["k_k"], wp["k_a"])


def _lru_gates(u, wa_ref, ba, wi_ref, bi, sp):
    ub = u.astype(BF16)
    ra, ia = [], []
    for q in range(D_LRU // LRU_GROUP):
        uq = ub[:, q * LRU_GROUP:(q + 1) * LRU_GROUP]
        ra.append(jnp.dot(uq, wa_ref[q], preferred_element_type=F32))
        ia.append(jnp.dot(uq, wi_ref[q], preferred_element_type=F32))
    rg = jax.nn.sigmoid(jnp.concatenate(ra, axis=1) + ba)
    ig = jax.nn.sigmoid(jnp.concatenate(ia, axis=1) + bi)
    log_a = -LRU_C * rg * sp
    a = jnp.exp(log_a)
    mult = jnp.sqrt(-jnp.tanh(log_a) * (a * a + 1.0))
    return a, mult, ig


def _lru_seq_kernel(x_ref, w_ref, cw_ref, cb_ref, wa_ref, ba_ref, wi_ref, bi_ref, lam_ref,
                    ga_ref, gyb_ref, conv_ref, hlast_ref,
                    xhist_ref, hc_ref, a_s, bx_s, h_s):
    t = pl.program_id(1)

    @pl.when(t == 0)
    def _():
        xhist_ref[...] = jnp.zeros_like(xhist_ref)
        hc_ref[...] = jnp.zeros_like(hc_ref)

    z = _dot(x_ref[0], w_ref[...])
    tm = z.shape[0]
    xb = z[:, 0:D_LRU]
    gb = z[:, D_LRU:2 * D_LRU]
    zga = z[:, 2 * D_LRU:2 * D_LRU + D_MODEL]
    zgb = z[:, 2 * D_LRU + D_MODEL:]
    row8 = lax.broadcasted_iota(jnp.int32, (SUBLANES, 1), 0)
    hist = xhist_ref[...]

    def shifted(k):
        rolled = pltpu.roll(xb, k, 0)
        first = jnp.where(row8 < k, pltpu.roll(hist, k, 0), rolled[0:SUBLANES])
        return jnp.concatenate([first, rolled[SUBLANES:]], axis=0)

    cw = cw_ref[...]
    u = (cb_ref[...] + cw[0:1] * shifted(3) + cw[1:2] * shifted(2) + cw[2:3] * shifted(1)
         + cw[3:4] * xb)
    last8 = xb[tm - SUBLANES:tm]
    xhist_ref[...] = last8
    conv_ref[0] = last8

    sp = _softplus(-lam_ref[...])
    a, mult, ig = _lru_gates(u, wa_ref, ba_ref[...], wi_ref, bi_ref[...], sp)
    row = lax.broadcasted_iota(jnp.int32, (tm, 1), 0)
    mult = jnp.where(jnp.logical_and(row == 0, t == 0), 1.0, mult)
    a_s[...] = a
    bx_s[...] = mult * ig * u

    def blk(i, hc):
        o = pl.multiple_of(i * SUBLANES, SUBLANES)
        av = a_s[pl.ds(o, SUBLANES), :]
        bv = bx_s[pl.ds(o, SUBLANES), :]
        for d in (1, 2, 4):
            a_sh = jnp.where(row8 < d, 1.0, pltpu.roll(av, d, 0))
            b_sh = jnp.where(row8 < d, 0.0, pltpu.roll(bv, d, 0))
            bv = av * b_sh + bv
            av = av * a_sh
        h = bv + av * hc
        h_s[pl.ds(o, SUBLANES), :] = h
        return h[SUBLANES - 1:SUBLANES, :]

    hc = lax.fori_loop(0, tm // SUBLANES, blk, hc_ref[0:1, :])
    hc_ref[0:1, :] = hc
    h = h_s[...]
    hlast_ref[0] = h[tm - SUBLANES:tm]
    ga_ref[0] = jax.nn.sigmoid(zga)
    gyb_ref[0] = jax.nn.sigmoid(zgb) * (h * _gelu_tanh(gb))


def _lru_step_kernel(x_ref, conv0_ref, h0_ref, w_ref, cw_ref, cb_ref, wa_ref, ba_ref, wi_ref,
                     bi_ref, lam_ref, ga_ref, gyb_ref, conv_ref, hlast_ref, *, steps, nseq):
    z = _dot(x_ref[...], w_ref[...])
    xb = z[:, 0:D_LRU]
    gb = z[:, D_LRU:2 * D_LRU]
    zga = z[:, 2 * D_LRU:2 * D_LRU + D_MODEL]
    zgb = z[:, 2 * D_LRU + D_MODEL:]
    n = steps * nseq
    hist = (CONV_W - 1) * nseq
    xext = jnp.concatenate([conv0_ref[...], xb], axis=0)

    def shifted(k):
        return xext[hist - k * nseq:hist - k * nseq + n]

    cw = cw_ref[...]
    u = (cb_ref[...] + cw[0:1] * shifted(3) + cw[1:2] * shifted(2) + cw[2:3] * shifted(1)
         + cw[3:4] * xb)
    conv_ref[...] = xext[n:n + hist]
    sp = _softplus(-lam_ref[...])
    a, mult, ig = _lru_gates(u, wa_ref, ba_ref[...], wi_ref, bi_ref[...], sp)
    bx = mult * ig * u
    h = h0_ref[...]
    hs = []
    for s in range(steps):
        h = a[s * nseq:(s + 1) * nseq] * h + bx[s * nseq:(s + 1) * nseq]
        hs.append(h)
    hlast_ref[...] = h
    hall = jnp.concatenate(hs, axis=0)
    ga_ref[...] = jax.nn.sigmoid(zga)
    gyb_ref[...] = jax.nn.sigmoid(zgb) * (hall * _gelu_tanh(gb))


def _lru_weight_specs():
    vec = _const_spec((1, D_LRU))
    gate_w = _const_spec((D_LRU // LRU_GROUP, LRU_GROUP, LRU_GROUP))
    return [_const_spec((D_MODEL, LRU_COLS)), _const_spec((CONV_W, D_LRU)), vec,
            gate_w, vec, gate_w, vec, vec]


def _lru_weights(wp):
    return (wp["w_lru"], wp["conv_w"], wp["conv_b"], wp["wa_bd"], wp["ba"], wp["wi_bd"],
            wp["bi"], wp["lam"])


def _lru_seq(x, wp, tm):
    b, t, _ = x.shape
    row_spec = pl.BlockSpec((1, tm, D_MODEL), lambda i, j: (i, j, 0))
    tail_spec = pl.BlockSpec((1, SUBLANES, D_LRU), lambda i, j: (i, 0, 0))
    return pl.pallas_call(
        _lru_seq_kernel,
        grid=(b, t // tm),
        in_specs=[row_spec] + _lru_weight_specs(),
        out_specs=[row_spec, row_spec, tail_spec, tail_spec],
        out_shape=[jax.ShapeDtypeStruct((b, t, D_MODEL), F32)] * 2
        + [jax.ShapeDtypeStruct((b, SUBLANES, D_LRU), F32)] * 2,
        scratch_shapes=[pltpu.VMEM((SUBLANES, D_LRU), F32), pltpu.VMEM((SUBLANES, D_LRU), F32),
                        pltpu.VMEM((tm, D_LRU), F32), pltpu.VMEM((tm, D_LRU), F32),
                        pltpu.VMEM((tm, D_LRU), F32)],
        compiler_params=pltpu.CompilerParams(
            dimension_semantics=("arbitrary", "arbitrary"), vmem_limit_bytes=VMEM_LIMIT),
        name="lru_seq",
    )(x, *_lru_weights(wp))


def _lru_step(x_tm, conv_tm, h0, wp, steps, nseq):
    n = steps * nseq
    hist = (CONV_W - 1) * nseq

    def full(r):
        return pl.BlockSpec((r, D_MODEL), lambda i: (0, 0))

    return pl.pallas_call(
        functools.partial(_lru_step_kernel, steps=steps, nseq=nseq),
        grid=(1,),
        in_specs=[full(n), full(hist), full(nseq)] + _lru_weight_specs(),
        out_specs=[full(n), full(n), full(hist), full(nseq)],
        out_shape=[jax.ShapeDtypeStruct((n, D_MODEL), F32)] * 2
        + [jax.ShapeDtypeStruct((hist, D_LRU), F32), jax.ShapeDtypeStruct((nseq, D_LRU), F32)],
        compiler_params=pltpu.CompilerParams(
            dimension_semantics=("arbitrary",), vmem_limit_bytes=VMEM_LIMIT),
        name="lru_step",
    )(x_tm, conv_tm, h0, *_lru_weights(wp))


def _wkv_consts(nseq):
    c = CHUNK
    seq_len = c // nseq
    c2 = 2 * c
    lane_lo = lax.broadcasted_iota(jnp.int32, (c, PAIR), 1) < HEAD
    ri = lax.broadcasted_iota(jnp.int32, (c, c), 0)
    ci = lax.broadcasted_iota(jnp.int32, (c, c), 1)
    same = (ri // seq_len) == (ci // seq_len)
    tri = jnp.where(jnp.logical_and(same, ci <= ri), 1.0, 0.0).astype(F32)
    tot = jnp.where(same, 1.0, 0.0).astype(F32)
    r2 = lax.broadcasted_iota(jnp.int32, (c2, c2), 0)
    q2 = lax.broadcasted_iota(jnp.int32, (c2, c2), 1)
    same2 = jnp.logical_and((r2 // c) == (q2 // c), ((r2 % c) // seq_len) == ((q2 % c) // seq_len))
    strict = jnp.logical_and(same2, (q2 % c) < (r2 % c))
    incl = jnp.logical_and(same2, (q2 % c) <= (r2 % c))
    eye = jnp.where(r2 == q2, 1.0, 0.0).astype(F32)
    rp = lax.broadcasted_iota(jnp.int32, (PAIR, PAIR), 0)
    cp = lax.broadcasted_iota(jnp.int32, (PAIR, PAIR), 1)
    blockdiag = (rp // HEAD) == (cp // HEAD)
    return dict(lane_lo=lane_lo, tri=tri, tot=tot, strict=strict, incl=incl, eye=eye,
                blockdiag=blockdiag, seq_len=seq_len)


def _seg_sum(x, lane_lo):
    s0 = jnp.sum(jnp.where(lane_lo, x, 0.0), axis=-1, keepdims=True)
    s1 = jnp.sum(jnp.where(lane_lo, 0.0, x), axis=-1, keepdims=True)
    return jnp.where(lane_lo, s0, s1)


def _wkv_chunk(r, lw, k, v, kkraw, a, states, cn, nseq):
    c = CHUNK
    c2 = 2 * c
    lane_lo = cn["lane_lo"]
    ss = _seg_sum(kkraw * kkraw, lane_lo)
    kk = kkraw / jnp.maximum(jnp.sqrt(ss), 1e-12)
    al = -kk
    be = kk * a
    cum = _dot_f32(cn["tri"], lw)
    if nseq == 1:
        end = cum[c - 1:c, :]
    else:
        end = _dot_f32(cn["tot"], lw)
    e_cum = jnp.exp(cum)
    e_neg = jnp.exp(-cum)
    e_end = jnp.exp(end)
    rt = r * e_cum
    at = al * jnp.exp(cum - lw)
    bt = be * e_neg
    kt = k * e_neg
    zero = jnp.zeros_like(at)
    lhs4 = jnp.concatenate([jnp.where(lane_lo, at, zero), jnp.where(lane_lo, zero, at),
                            jnp.where(lane_lo, rt, zero), jnp.where(lane_lo, zero, rt)],
                           axis=0).astype(BF16)
    rhs4 = jnp.concatenate([bt, bt, kt, kt], axis=0).astype(BF16)
    gram = _dot_nt(lhs4, rhs4)
    a_ab = jnp.where(cn["strict"], gram[0:c2, 0:c2], 0.0)
    a_ak = jnp.where(cn["strict"], gram[0:c2, c2:], 0.0)
    a_rb = jnp.where(cn["incl"], gram[c2:, 0:c2], 0.0)
    a_rk = jnp.where(cn["incl"], gram[c2:, c2:], 0.0)
    inv = cn["eye"] + a_ab
    pw = a_ab
    for _ in range(int(math.log2(cn["seq_len"])) - 1):
        pw = _dot(pw, pw)
        inv = inv + _dot(inv, pw)
    v_st = jnp.concatenate([v, v], axis=0)
    av = _dot(jnp.concatenate([a_ak, a_rk], axis=0), v_st)
    if nseq == 1:
        from_state = _dot_nt(lhs4, states[0])
    else:
        seq_of_row = (lax.broadcasted_iota(jnp.int32, (4 * c, 1), 0) % c) // cn["seq_len"]
        from_state = jnp.zeros((4 * c, PAIR), F32)
        for s in range(nseq):
            from_state = jnp.where(seq_of_row == s, _dot_nt(lhs4, states[s]), from_state)
    u_st = _dot(inv, from_state[0:c2] + av[0:c2])
    y_st = from_state[c2:] + _dot(a_rb, u_st) + av[c2:]
    y = jnp.where(lane_lo, y_st[0:c], y_st[c:])
    u = jnp.where(lane_lo, u_st[0:c], u_st[c:])
    uv = jnp.concatenate([u, v], axis=0)
    bk = jnp.concatenate([bt * e_end, kt * e_end], axis=0)
    new_states = []
    if nseq == 1:
        inc = _dot_tn(uv, bk)
        new_states.append(states[0] * e_end + jnp.where(cn["blockdiag"], inc, 0.0))
    else:
        seq_of_uv = (lax.broadcasted_iota(jnp.int32, (c2, 1), 0) % c) // cn["seq_len"]
        for s in range(nseq):
            inc = _dot_tn(jnp.where(seq_of_uv == s, uv, 0.0), bk)
            decay = e_end[s * cn["seq_len"]:s * cn["seq_len"] + 1, :]
            new_states.append(states[s] * decay + jnp.where(cn["blockdiag"], inc, 0.0))
    return y, new_states


def _wkv_epilogue(y, r, k, v, g, rk, lnx_g, lnx_b, lane_lo):
    mean = _seg_sum(y, lane_lo) * (1.0 / HEAD)
    yc = y - mean
    var = _seg_sum(yc * yc, lane_lo) * (1.0 / HEAD)
    yn = yc * lax.rsqrt(var + GN_EPS) * lnx_g + lnx_b
    bonus = _seg_sum(r * k * rk, lane_lo) * v
    return (yn + bonus) * g


def _to_blockdiag(s2, lane_lo_half):
    zero = jnp.zeros_like(s2)
    return jnp.concatenate([jnp.where(lane_lo_half, s2, zero), jnp.where(lane_lo_half, zero, s2)],
                           axis=0)


def _wkv_seq_kernel(r_ref, lw_ref, k_ref, v_ref, kkraw_ref, a_ref, g_ref, rk_ref, lg_ref, lb_ref,
                    y_ref, s_out_ref, s_scr):
    t = pl.program_id(2)
    cn = _wkv_consts(1)

    @pl.when(t == 0)
    def _():
        s_scr[...] = jnp.zeros_like(s_scr)

    rk, lg, lb = rk_ref[...], lg_ref[...], lb_ref[...]
    n_chunks = r_ref.shape[1] // CHUNK

    def body(i, state):
        sl = pl.ds(pl.multiple_of(i * CHUNK, CHUNK), CHUNK)
        r, lw, k, v = r_ref[0, sl, :], lw_ref[0, sl, :], k_ref[0, sl, :], v_ref[0, sl, :]
        y, (state,) = _wkv_chunk(r, lw, k, v, kkraw_ref[0, sl, :], a_ref[0, sl, :], [state], cn, 1)
        y_ref[0, sl, :] = _wkv_epilogue(y, r, k, v, g_ref[0, sl, :], rk, lg, lb, cn["lane_lo"])
        return state

    state = lax.fori_loop(0, n_chunks, body, s_scr[...])
    s_scr[...] = state
    s_out_ref[0, 0] = state[0:HEAD] + state[HEAD:]


def _wkv_step_kernel(r_ref, lw_ref, k_ref, v_ref, kkraw_ref, a_ref, g_ref, rk_ref, lg_ref, lb_ref,
                     s_in_ref, y_ref, s_out_ref, *, nseq):
    cn = _wkv_consts(nseq)
    half = lax.broadcasted_iota(jnp.int32, (HEAD, PAIR), 1) < HEAD
    states = [_to_blockdiag(s_in_ref[s, 0], half) for s in range(nseq)]
    r, lw, k, v = r_ref[...], lw_ref[...], k_ref[...], v_ref[...]
    y, states = _wkv_chunk(r, lw, k, v, kkraw_ref[...], a_ref[...], states, cn, nseq)
    y_ref[...] = _wkv_epilogue(y, r, k, v, g_ref[...], rk_ref[...], lg_ref[...], lb_ref[...],
                               cn["lane_lo"])
    for s in range(nseq):
        s_out_ref[s, 0] = states[s][0:HEAD] + states[s][HEAD:]


def _wkv_seq(acts, wp, tt):
    b, t, _ = acts[0].shape
    row_spec = pl.BlockSpec((1, tt, PAIR), lambda i, p, j: (i, j, p))
    vec = pl.BlockSpec((1, PAIR), lambda i, p, j: (0, p))
    return pl.pallas_call(
        _wkv_seq_kernel,
        grid=(b, N_PAIRS, t // tt),
        in_specs=[row_spec] * 7 + [vec] * 3,
        out_specs=[row_spec, pl.BlockSpec((1, 1, HEAD, PAIR), lambda i, p, j: (i, p, 0, 0))],
        out_shape=[jax.ShapeDtypeStruct((b, t, D_MODEL), F32),
                   jax.ShapeDtypeStruct((b, N_PAIRS, HEAD, PAIR), F32)],
        scratch_shapes=[pltpu.VMEM((PAIR, PAIR), F32)],
        compiler_params=pltpu.CompilerParams(
            dimension_semantics=("arbitrary", "arbitrary", "arbitrary"),
            vmem_limit_bytes=VMEM_LIMIT),
        name="wkv_seq",
    )(*acts, wp["r_k"], wp["lnx_g"], wp["lnx_b"])


def _wkv_step(acts, s_pairs, wp, steps):
    n = acts[0].shape[0]
    nseq = CHUNK // steps
    row_spec = pl.BlockSpec((CHUNK, PAIR), lambda i, p: (i, p))
    vec = pl.BlockSpec((1, PAIR), lambda i, p: (0, p))
    st_spec = pl.BlockSpec((nseq, 1, HEAD, PAIR), lambda i, p: (i, p, 0, 0))
    return pl.pallas_call(
        functools.partial(_wkv_step_kernel, nseq=nseq),
        grid=(n // CHUNK, N_PAIRS),
        in_specs=[row_spec] * 7 + [vec] * 3 + [st_spec],
        out_specs=[row_spec, st_spec],
        out_shape=[jax.ShapeDtypeStruct((n, D_MODEL), F32),
                   jax.ShapeDtypeStruct(s_pairs.shape, F32)],
        compiler_params=pltpu.CompilerParams(
            dimension_semantics=("arbitrary", "arbitrary"), vmem_limit_bytes=VMEM_LIMIT),
        name="wkv_step",
    )(*acts, wp["r_k"], wp["lnx_g"], wp["lnx_b"], s_pairs)


def _post_kernel(x_ref, ya_ref, ga_ref, gyb_ref, wo_ref, l1g_ref, l1b_ref, wg_ref, wu_ref, wd_ref,
                 l2g_ref, l2b_ref, y_ref):
    x = x_ref[...]
    merged = ga_ref[...] * ya_ref[...] + gyb_ref[...]
    mix = _dot(merged, wo_ref[...])
    h1 = _layer_norm(ALPHA * x + mix, l1g_ref[...], l1b_ref[...])
    h1b = h1.astype(BF16)
    gate = jnp.dot(h1b, wg_ref[...], preferred_element_type=F32)
    up = jnp.dot(h1b, wu_ref[...], preferred_element_type=F32)
    act = (gate * jax.nn.sigmoid(gate)) * up
    ffn = _dot(act, wd_ref[...])
    y_ref[...] = _layer_norm(ALPHA * h1 + ffn, l2g_ref[...], l2b_ref[...])


def _post(x, ya, ga, gyb, wp, tm):
    n = x.shape[0]
    row_spec = pl.BlockSpec((tm, D_MODEL), lambda i: (i, 0))
    vec = _const_spec((1, D_MODEL))
    return pl.pallas_call(
        _post_kernel,
        grid=(n // tm,),
        in_specs=[row_spec] * 4 + [_const_spec((D_MODEL, D_MODEL)), vec, vec,
                                   _const_spec((D_MODEL, D_FF)), _const_spec((D_MODEL, D_FF)),
                                   _const_spec((D_FF, D_MODEL)), vec, vec],
        out_specs=row_spec,
        out_shape=jax.ShapeDtypeStruct((n, D_MODEL), F32),
        compiler_params=pltpu.CompilerParams(
            dimension_semantics=("arbitrary",), vmem_limit_bytes=VMEM_LIMIT),
        name="post",
    )(x, ya, ga, gyb, wp["w_o"], wp["ln1_g"], wp["ln1_b"], wp["w_gate"], wp["w_up"], wp["w_down"],
      wp["ln2_g"], wp["ln2_b"])


def _prep_weights(w_in, tmix_mu, w0, w2_decay, a0, a2_iclr, g2_gate, k_k, k_a, r_k, lnx_g, lnx_b,
                  conv_w, conv_b, lru_wa, lru_ba, lru_wi, lru_bi, lru_lambda, w_o,
                  ln1_g, ln1_b, w_ffn_gate, w_ffn_up, w_ffn_down, ln2_g, ln2_b):
    row = lambda v: v.reshape(1, -1).astype(F32)
    zeros = lambda r: jnp.zeros((r, D_MODEL), F32)
    w2ext = jnp.concatenate([
        jnp.concatenate([w2_decay, zeros(DECAY_LORA), zeros(DECAY_LORA)], axis=1),
        jnp.concatenate([zeros(ICLR_LORA), a2_iclr, zeros(ICLR_LORA)], axis=1),
        jnp.concatenate([zeros(GATE_LORA), zeros(GATE_LORA), g2_gate], axis=1)], axis=0)

    def gate_blockdiag(w):
        per = LRU_GROUP // LRU_BS
        w4 = w.reshape(LRU_BLOCKS // per, per, LRU_BS, LRU_BS)
        eye = jnp.eye(per, dtype=w.dtype)
        bd = jnp.einsum("gpcd,pq->gpcqd", w4, eye)
        return bd.reshape(LRU_BLOCKS // per, LRU_GROUP, LRU_GROUP).astype(BF16)

    return dict(
        w_rw=w_in[:, :RWKV_COLS].astype(BF16), w_lru=w_in[:, RWKV_COLS:].astype(BF16),
        mu=row(tmix_mu), w2ext=w2ext.astype(BF16), w0=row(w0), a0=row(a0), k_k=row(k_k),
        k_a=row(k_a), r_k=row(r_k), lnx_g=row(lnx_g), lnx_b=row(lnx_b),
        conv_w=conv_w.astype(F32), conv_b=row(conv_b), wa_bd=gate_blockdiag(lru_wa),
        ba=row(lru_ba), wi_bd=gate_blockdiag(lru_wi), bi=row(lru_bi), lam=row(lru_lambda),
        w_o=w_o.astype(BF16), ln1_g=row(ln1_g), ln1_b=row(ln1_b), w_gate=w_ffn_gate.astype(BF16),
        w_up=w_ffn_up.astype(BF16), w_down=w_ffn_down.astype(BF16), ln2_g=row(ln2_g),
        ln2_b=row(ln2_b))


def _pairs_from_heads(s):
    b = s.shape[0]
    return s.reshape(b, N_PAIRS, 2, HEAD, HEAD).transpose(0, 1, 3, 2, 4).reshape(
        b, N_PAIRS, HEAD, PAIR)


def _heads_from_pairs(s):
    b = s.shape[0]
    return s.reshape(b, N_PAIRS, HEAD, 2, HEAD).transpose(0, 1, 3, 2, 4).reshape(
        b, N_HEADS, HEAD, HEAD)


def _prompt_layer(x, wp, tm, tt, tm_post):
    b, t, _ = x.shape
    acts = _rwkv_proj_seq(x, wp, tm)
    ya, s_pairs = _wkv_seq(acts, wp, tt)
    ga, gyb, conv_tail, h_tail = _lru_seq(x, wp, tm)
    y = _post(x.reshape(b * t, D_MODEL), ya.reshape(b * t, D_MODEL), ga.reshape(b * t, D_MODEL),
              gyb.reshape(b * t, D_MODEL), wp, tm_post).reshape(b, t, D_MODEL)
    return (y, x[:, -1], _heads_from_pairs(s_pairs), conv_tail[:, SUBLANES - (CONV_W - 1):],
            h_tail[:, SUBLANES - 1])


def _sample_layer(x, shift_buf, wkv0, conv_buf, h0, wp):
    b, t, _ = x.shape
    n = b * t
    xf = x.reshape(n, D_MODEL)
    xprev = jnp.concatenate([shift_buf[:, None], x[:, :-1]], axis=1).reshape(n, D_MODEL)
    acts = _rwkv_proj_step(xf, xprev, wp)
    ya, s_pairs = _wkv_step(acts, _pairs_from_heads(wkv0), wp, t)
    x_tm = x.transpose(1, 0, 2).reshape(n, D_MODEL)
    conv_tm = conv_buf.transpose(1, 0, 2).reshape((CONV_W - 1) * b, D_LRU)
    ga_tm, gyb_tm, conv_new_tm, h_last = _lru_step(x_tm, conv_tm, h0, wp, t, b)
    to_seq_major = lambda v: v.reshape(t, b, D_MODEL).transpose(1, 0, 2).reshape(n, D_MODEL)
    y = _post(xf, ya, to_seq_major(ga_tm), to_seq_major(gyb_tm), wp, min(n, 256)).reshape(
        b, t, D_MODEL)
    conv_new = conv_new_tm.reshape(CONV_W - 1, b, D_LRU).transpose(1, 0, 2)
    return y, x[:, -1], _heads_from_pairs(s_pairs), conv_new, h_last


def kernel(x_prompt, x_sample, state_shift, state_wkv, state_conv, state_lru, w_in, tmix_mu, w0, w2_decay, a0, a2_iclr, g2_gate, k_k, k_a, r_k, lnx_g, lnx_b, conv_w, conv_b, lru_wa, lru_ba, lru_wi, lru_bi, lru_lambda, w_o, ln1_g, ln1_b, w_ffn_gate, w_ffn_up, w_ffn_down, ln2_g, ln2_b):
    params = (w_in, tmix_mu, w0, w2_decay, a0, a2_iclr, g2_gate, k_k, k_a, r_k, lnx_g, lnx_b,
              conv_w, conv_b, lru_wa, lru_ba, lru_wi, lru_bi, lru_lambda, w_o,
              ln1_g, ln1_b, w_ffn_gate, w_ffn_up, w_ffn_down, ln2_g, ln2_b)
    wp = _prep_weights(*[p[0] for p in params])
    yp, sh_p, wkv_p, conv_p, lru_p = _prompt_layer(x_prompt, wp, 256, 512, 256)
    ys, sh_s, wkv_s, conv_s, lru_s = _sample_layer(
        x_sample, state_shift[0], state_wkv[0], state_conv[0], state_lru[0], wp)
    return (yp, ys, sh_p[None], wkv_p[None], conv_p[None], lru_p[None],
            sh_s[None], wkv_s[None], conv_s[None], lru_s[None])
```

```python
import functools
import math

import jax
import jax.numpy as jnp
from jax import lax
from jax.experimental import pallas as pl
from jax.experimental.pallas import tpu as pltpu

F32 = jnp.float32
BF16 = jnp.bfloat16

D_MODEL = 1024
HEAD = 64
N_HEADS = D_MODEL // HEAD
PAIR = 2 * HEAD
N_PAIRS = N_HEADS // 2
DECAY_LORA = 64
ICLR_LORA = 64
GATE_LORA = 128
LORA = DECAY_LORA + ICLR_LORA + GATE_LORA
RWKV_COLS = 3 * D_MODEL + LORA
GN_EPS = HEAD * 1e-5
D_LRU = D_MODEL
LRU_BLOCKS = 16
LRU_BS = D_LRU // LRU_BLOCKS
LRU_GROUP = 256
CONV_W = 4
LRU_C = 8.0
LRU_COLS = 2 * D_LRU + 2 * D_MODEL
D_FF = 2816
ALPHA = 2.0 ** 0.25
LN_EPS = 1e-5

SUBLANES = 8
CHUNK = 64
VMEM_LIMIT = 56 * 1024 * 1024


def _softplus(x):
    return jnp.maximum(x, 0.0) + jnp.log1p(jnp.exp(-jnp.abs(x)))


def _gelu_tanh(x):
    c = math.sqrt(2.0 / math.pi)
    return x * (0.5 * (1.0 + jnp.tanh(c * (x + 0.044715 * (x * x * x)))))


def _layer_norm(x, g, b):
    mu = jnp.mean(x, axis=-1, keepdims=True)
    xc = x - mu
    var = jnp.mean(xc * xc, axis=-1, keepdims=True)
    return xc * lax.rsqrt(var + LN_EPS) * g + b


def _dot(a, b):
    return jnp.dot(a.astype(BF16), b.astype(BF16), preferred_element_type=F32)


def _dot_nt(a, b):
    return lax.dot_general(a.astype(BF16), b.astype(BF16), (((1,), (1,)), ((), ())),
                           preferred_element_type=F32)


def _dot_tn(a, b):
    return lax.dot_general(a.astype(BF16), b.astype(BF16), (((0,), (0,)), ((), ())),
                           preferred_element_type=F32)


def _dot_f32(a, b):
    return jnp.dot(a, b, preferred_element_type=F32, precision=lax.Precision.HIGHEST)


def _rwkv_prep(z, zprev, mu, w2ext, w0, a0, k_k, k_a):
    zm = z + mu * (zprev - z)
    r = zm[:, 0:D_MODEL]
    k = zm[:, D_MODEL:2 * D_MODEL]
    v = zm[:, 2 * D_MODEL:3 * D_MODEL]
    lo = zm[:, 3 * D_MODEL:RWKV_COLS]
    lane = lax.broadcasted_iota(jnp.int32, lo.shape, 1)
    act = jnp.where(lane < DECAY_LORA, jnp.tanh(lo),
                    jnp.where(lane < DECAY_LORA + ICLR_LORA, lo, jax.nn.sigmoid(lo)))
    lora = _dot(act, w2ext)
    w = -_softplus(-(w0 + lora[:, 0:D_MODEL])) - 0.5
    lw = -jnp.exp(w)
    a = jax.nn.sigmoid(a0 + lora[:, D_MODEL:2 * D_MODEL])
    g = lora[:, 2 * D_MODEL:3 * D_MODEL]
    kkraw = k * k_k
    k2 = k * (1.0 + (a - 1.0) * k_a)
    return r, lw, k2, v, kkraw, a, g


def _rwkv_proj_seq_kernel(x_ref, w_ref, mu_ref, w2_ref, w0_ref, a0_ref, kk_ref, ka_ref,
                          r_ref, lw_ref, k_ref, v_ref, kkraw_ref, a_ref, g_ref, hist_ref):
    @pl.when(pl.program_id(1) == 0)
    def _():
        hist_ref[...] = jnp.zeros_like(hist_ref)

    z = _dot(x_ref[0], w_ref[...])
    tm = z.shape[0]
    rolled = pltpu.roll(z, 1, 0)
    row = lax.broadcasted_iota(jnp.int32, (SUBLANES, 1), 0)
    first = jnp.where(row == 0, hist_ref[SUBLANES - 1:SUBLANES, :], rolled[0:SUBLANES])
    zprev = jnp.concatenate([first, rolled[SUBLANES:]], axis=0)
    hist_ref[...] = z[tm - SUBLANES:tm]
    outs = _rwkv_prep(z, zprev, mu_ref[...], w2_ref[...], w0_ref[...], a0_ref[...],
                      kk_ref[...], ka_ref[...])
    for o_ref, o in zip((r_ref, lw_ref, k_ref, v_ref, kkraw_ref, a_ref, g_ref), outs):
        o_ref[0] = o


def _rwkv_proj_step_kernel(x_ref, xp_ref, w_ref, mu_ref, w2_ref, w0_ref, a0_ref, kk_ref, ka_ref,
                           r_ref, lw_ref, k_ref, v_ref, kkraw_ref, a_ref, g_ref):
    z = _dot(x_ref[...], w_ref[...])
    zprev = _dot(xp_ref[...], w_ref[...])
    outs = _rwkv_prep(z, zprev, mu_ref[...], w2_ref[...], w0_ref[...], a0_ref[...],
                      kk_ref[...], ka_ref[...])
    for o_ref, o in zip((r_ref, lw_ref, k_ref, v_ref, kkraw_ref, a_ref, g_ref), outs):
        o_ref[...] = o


def _const_spec(shape):
    nd = len(shape)
    return pl.BlockSpec(shape, lambda *_: (0,) * nd, pipeline_mode=pl.Buffered(1))


def _rwkv_proj_seq(x, wp, tm):
    b, t, _ = x.shape
    row_spec = pl.BlockSpec((1, tm, D_MODEL), lambda i, j: (i, j, 0))
    vec = _const_spec((1, D_MODEL))
    return pl.pallas_call(
        _rwkv_proj_seq_kernel,
        grid=(b, t // tm),
        in_specs=[row_spec, _const_spec((D_MODEL, RWKV_COLS)), _const_spec((1, RWKV_COLS)),
                  _const_spec((LORA, 3 * D_MODEL)), vec, vec, vec, vec],
        out_specs=[row_spec] * 7,
        out_shape=[jax.ShapeDtypeStruct((b, t, D_MODEL), F32)] * 7,
        scratch_shapes=[pltpu.VMEM((SUBLANES, RWKV_COLS), F32)],
        compiler_params=pltpu.CompilerParams(
            dimension_semantics=("arbitrary", "arbitrary"), vmem_limit_bytes=VMEM_LIMIT),
        name="rwkv_proj_seq",
    )(x, wp["w_rw"], wp["mu"], wp["w2ext"], wp["w0"], wp["a0"], wp["k_k"], wp["k_a"])


def _rwkv_proj_step(x, xprev, wp):
    n = x.shape[0]
    full = pl.BlockSpec((n, D_MODEL), lambda i: (0, 0))
    vec = _const_spec((1, D_MODEL))
    return pl.pallas_call(
        _rwkv_proj_step_kernel,
        grid=(1,),
        in_specs=[full, full, _const_spec((D_MODEL, RWKV_COLS)), _const_spec((1, RWKV_COLS)),
                  _const_spec((LORA, 3 * D_MODEL)), vec, vec, vec, vec],
        out_specs=[full] * 7,
        out_shape=[jax.ShapeDtypeStruct((n, D_MODEL), F32)] * 7,
        compiler_params=pltpu.CompilerParams(
            dimension_semantics=("arbitrary",), vmem_limit_bytes=VMEM_LIMIT),
        name="rwkv_proj_step",
    )(x, xprev, wp["w_rw"], wp["mu"], wp["w2ext"], wp["w0"], wp["a0"], wp["k_k"], wp["k_a"])


def _lru_gates(u, wa_ref, ba, wi_ref, bi, sp):
    ub = u.astype(BF16)
    ra, ia = [], []
    for q in range(D_LRU // LRU_GROUP):
        uq = ub[:, q * LRU_GROUP:(q + 1) * LRU_GROUP]
        ra.append(jnp.dot(uq, wa_ref[q], preferred_element_type=F32))
        ia.append(jnp.dot(uq, wi_ref[q], preferred_element_type=F32))
    rg = jax.nn.sigmoid(jnp.concatenate(ra, axis=1) + ba)
    ig = jax.nn.sigmoid(jnp.concatenate(ia, axis=1) + bi)
    log_a = -LRU_C * rg * sp
    a = jnp.exp(log_a)
    mult = jnp.sqrt(-jnp.tanh(log_a) * (a * a + 1.0))
    return a, mult, ig


def _lru_seq_kernel(x_ref, w_ref, cw_ref, cb_ref, wa_ref, ba_ref, wi_ref, bi_ref, lam_ref,
                    ga_ref, gyb_ref, conv_ref, hlast_ref,
                    xhist_ref, hc_ref, a_s, bx_s, h_s):
    t = pl.program_id(1)

    @pl.when(t == 0)
    def _():
        xhist_ref[...] = jnp.zeros_like(xhist_ref)
        hc_ref[...] = jnp.zeros_like(hc_ref)

    z = _dot(x_ref[0], w_ref[...])
    tm = z.shape[0]
    xb = z[:, 0:D_LRU]
    gb = z[:, D_LRU:2 * D_LRU]
    zga = z[:, 2 * D_LRU:2 * D_LRU + D_MODEL]
    zgb = z[:, 2 * D_LRU + D_MODEL:]
    row8 = lax.broadcasted_iota(jnp.int32, (SUBLANES, 1), 0)
    hist = xhist_ref[...]

    def shifted(k):
        rolled = pltpu.roll(xb, k, 0)
        first = jnp.where(row8 < k, pltpu.roll(hist, k, 0), rolled[0:SUBLANES])
        return jnp.concatenate([first, rolled[SUBLANES:]], axis=0)

    cw = cw_ref[...]
    u = (cb_ref[...] + cw[0:1] * shifted(3) + cw[1:2] * shifted(2) + cw[2:3] * shifted(1)
         + cw[3:4] * xb)
    last8 = xb[tm - SUBLANES:tm]
    xhist_ref[...] = last8
    conv_ref[0] = last8

    sp = _softplus(-lam_ref[...])
    a, mult, ig = _lru_gates(u, wa_ref, ba_ref[...], wi_ref, bi_ref[...], sp)
    row = lax.broadcasted_iota(jnp.int32, (tm, 1), 0)
    mult = jnp.where(jnp.logical_and(row == 0, t == 0), 1.0, mult)
    a_s[...] = a
    bx_s[...] = mult * ig * u

    def blk(i, hc):
        o = pl.multiple_of(i * SUBLANES, SUBLANES)
        av = a_s[pl.ds(o, SUBLANES), :]
        bv = bx_s[pl.ds(o, SUBLANES), :]
        for d in (1, 2, 4):
            a_sh = jnp.where(row8 < d, 1.0, pltpu.roll(av, d, 0))
            b_sh = jnp.where(row8 < d, 0.0, pltpu.roll(bv, d, 0))
            bv = av * b_sh + bv
            av = av * a_sh
        h = bv + av * hc
        h_s[pl.ds(o, SUBLANES), :] = h
        return h[SUBLANES - 1:SUBLANES, :]

    hc = lax.fori_loop(0, tm // SUBLANES, blk, hc_ref[0:1, :])
    hc_ref[0:1, :] = hc
    h = h_s[...]
    hlast_ref[0] = h[tm - SUBLANES:tm]
    ga_ref[0] = jax.nn.sigmoid(zga)
    gyb_ref[0] = jax.nn.sigmoid(zgb) * (h * _gelu_tanh(gb))


def _lru_step_kernel(x_ref, conv0_ref, h0_ref, w_ref, cw_ref, cb_ref, wa_ref, ba_ref, wi_ref,
                     bi_ref, lam_ref, ga_ref, gyb_ref, conv_ref, hlast_ref, *, steps, nseq):
    z = _dot(x_ref[...], w_ref[...])
    xb = z[:, 0:D_LRU]
    gb = z[:, D_LRU:2 * D_LRU]
    zga = z[:, 2 * D_LRU:2 * D_LRU + D_MODEL]
    zgb = z[:, 2 * D_LRU + D_MODEL:]
    n = steps * nseq
    hist = (CONV_W - 1) * nseq
    xext = jnp.concatenate([conv0_ref[...], xb], axis=0)

    def shifted(k):
        return xext[hist - k * nseq:hist - k * nseq + n]

    cw = cw_ref[...]
    u = (cb_ref[...] + cw[0:1] * shifted(3) + cw[1:2] * shifted(2) + cw[2:3] * shifted(1)
         + cw[3:4] * xb)
    conv_ref[...] = xext[n:n + hist]
    sp = _softplus(-lam_ref[...])
    a, mult, ig = _lru_gates(u, wa_ref, ba_ref[...], wi_ref, bi_ref[...], sp)
    bx = mult * ig * u
    h = h0_ref[...]
    hs = []
    for s in range(steps):
        h = a[s * nseq:(s + 1) * nseq] * h + bx[s * nseq:(s + 1) * nseq]
        hs.append(h)
    hlast_ref[...] = h
    hall = jnp.concatenate(hs, axis=0)
    ga_ref[...] = jax.nn.sigmoid(zga)
    gyb_ref[...] = jax.nn.sigmoid(zgb) * (hall * _gelu_tanh(gb))


def _lru_weight_specs():
    vec = _const_spec((1, D_LRU))
    gate_w = _const_spec((D_LRU // LRU_GROUP, LRU_GROUP, LRU_GROUP))
    return [_const_spec((D_MODEL, LRU_COLS)), _const_spec((CONV_W, D_LRU)), vec,
            gate_w, vec, gate_w, vec, vec]


def _lru_weights(wp):
    return (wp["w_lru"], wp["conv_w"], wp["conv_b"], wp["wa_bd"], wp["ba"], wp["wi_bd"],
            wp["bi"], wp["lam"])


def _lru_seq(x, wp, tm):
    b, t, _ = x.shape
    row_spec = pl.BlockSpec((1, tm, D_MODEL), lambda i, j: (i, j, 0))
    tail_spec = pl.BlockSpec((1, SUBLANES, D_LRU), lambda i, j: (i, 0, 0))
    return pl.pallas_call(
        _lru_seq_kernel,
        grid=(b, t // tm),
        in_specs=[row_spec] + _lru_weight_specs(),
        out_specs=[row_spec, row_spec, tail_spec, tail_spec],
        out_shape=[jax.ShapeDtypeStruct((b, t, D_MODEL), F32)] * 2
        + [jax.ShapeDtypeStruct((b, SUBLANES, D_LRU), F32)] * 2,
        scratch_shapes=[pltpu.VMEM((SUBLANES, D_LRU), F32), pltpu.VMEM((SUBLANES, D_LRU), F32),
                        pltpu.VMEM((tm, D_LRU), F32), pltpu.VMEM((tm, D_LRU), F32),
                        pltpu.VMEM((tm, D_LRU), F32)],
        compiler_params=pltpu.CompilerParams(
            dimension_semantics=("arbitrary", "arbitrary"), vmem_limit_bytes=VMEM_LIMIT),
        name="lru_seq",
    )(x, *_lru_weights(wp))


def _lru_step(x_tm, conv_tm, h0, wp, steps, nseq):
    n = steps * nseq
    hist = (CONV_W - 1) * nseq

    def full(r):
        return pl.BlockSpec((r, D_MODEL), lambda i: (0, 0))

    return pl.pallas_call(
        functools.partial(_lru_step_kernel, steps=steps, nseq=nseq),
        grid=(1,),
        in_specs=[full(n), full(hist), full(nseq)] + _lru_weight_specs(),
        out_specs=[full(n), full(n), full(hist), full(nseq)],
        out_shape=[jax.ShapeDtypeStruct((n, D_MODEL), F32)] * 2
        + [jax.ShapeDtypeStruct((hist, D_LRU), F32), jax.ShapeDtypeStruct((nseq, D_LRU), F32)],
        compiler_params=pltpu.CompilerParams(
            dimension_semantics=("arbitrary",), vmem_limit_bytes=VMEM_LIMIT),
        name="lru_step",
    )(x_tm, conv_tm, h0, *_lru_weights(wp))


def _wkv_consts(nseq):
    c = CHUNK
    seq_len = c // nseq
    c2 = 2 * c
    lane_lo = lax.broadcasted_iota(jnp.int32, (c, PAIR), 1) < HEAD
    ri = lax.broadcasted_iota(jnp.int32, (c, c), 0)
    ci = lax.broadcasted_iota(jnp.int32, (c, c), 1)
    same = (ri // seq_len) == (ci // seq_len)
    tri = jnp.where(jnp.logical_and(same, ci <= ri), 1.0, 0.0).astype(F32)
    tot = jnp.where(same, 1.0, 0.0).astype(F32)
    r2 = lax.broadcasted_iota(jnp.int32, (c2, c2), 0)
    q2 = lax.broadcasted_iota(jnp.int32, (c2, c2), 1)
    same2 = jnp.logical_and((r2 // c) == (q2 // c), ((r2 % c) // seq_len) == ((q2 % c) // seq_len))
    strict = jnp.logical_and(same2, (q2 % c) < (r2 % c))
    incl = jnp.logical_and(same2, (q2 % c) <= (r2 % c))
    eye = jnp.where(r2 == q2, 1.0, 0.0).astype(F32)
    rp = lax.broadcasted_iota(jnp.int32, (PAIR, PAIR), 0)
    cp = lax.broadcasted_iota(jnp.int32, (PAIR, PAIR), 1)
    blockdiag = (rp // HEAD) == (cp // HEAD)
    return dict(lane_lo=lane_lo, tri=tri, tot=tot, strict=strict, incl=incl, eye=eye,
                blockdiag=blockdiag, seq_len=seq_len)


def _seg_sum(x, lane_lo):
    s0 = jnp.sum(jnp.where(lane_lo, x, 0.0), axis=-1, keepdims=True)
    s1 = jnp.sum(jnp.where(lane_lo, 0.0, x), axis=-1, keepdims=True)
    return jnp.where(lane_lo, s0, s1)


def _run_interleaved(stage_gens):
    results = [None] * len(stage_gens)
    live = list(range(len(stage_gens)))
    while live:
        still = []
        for idx in live:
            try:
                next(stage_gens[idx])
                still.append(idx)
            except StopIteration as done:
                results[idx] = done.value
        live = still
    return results


def _wkv_chunk(r, lw, k, v, kkraw, a, states, cn, nseq):
    c = CHUNK
    c2 = 2 * c
    lane_lo = cn["lane_lo"]
    ss = _seg_sum(kkraw * kkraw, lane_lo)
    kk = kkraw / jnp.maximum(jnp.sqrt(ss), 1e-12)
    al = -kk
    be = kk * a
    cum = _dot_f32(cn["tri"], lw)
    if nseq == 1:
        end = cum[c - 1:c, :]
    else:
        end = _dot_f32(cn["tot"], lw)
    yield
    e_cum = jnp.exp(cum)
    e_neg = jnp.exp(-cum)
    e_end = jnp.exp(end)
    rt = r * e_cum
    at = al * jnp.exp(cum - lw)
    bt = be * e_neg
    kt = k * e_neg
    zero = jnp.zeros_like(at)
    lhs4 = jnp.concatenate([jnp.where(lane_lo, at, zero), jnp.where(lane_lo, zero, at),
                            jnp.where(lane_lo, rt, zero), jnp.where(lane_lo, zero, rt)],
                           axis=0).astype(BF16)
    rhs4 = jnp.concatenate([bt, bt, kt, kt], axis=0).astype(BF16)
    gram = _dot_nt(lhs4, rhs4)
    yield
    a_ab = jnp.where(cn["strict"], gram[0:c2, 0:c2], 0.0)
    a_ak = jnp.where(cn["strict"], gram[0:c2, c2:], 0.0)
    a_rb = jnp.where(cn["incl"], gram[c2:, 0:c2], 0.0)
    a_rk = jnp.where(cn["incl"], gram[c2:, c2:], 0.0)
    inv = cn["eye"] + a_ab
    pw = a_ab
    for _ in range(int(math.log2(cn["seq_len"])) - 1):
        pw = _dot(pw, pw)
        yield
        inv = inv + _dot(inv, pw)
        yield
    v_st = jnp.concatenate([v, v], axis=0)
    av = _dot(jnp.concatenate([a_ak, a_rk], axis=0), v_st)
    if nseq == 1:
        from_state = _dot_nt(lhs4, states[0])
    else:
        seq_of_row = (lax.broadcasted_iota(jnp.int32, (4 * c, 1), 0) % c) // cn["seq_len"]
        from_state = jnp.zeros((4 * c, PAIR), F32)
        for s in range(nseq):
            from_state = jnp.where(seq_of_row == s, _dot_nt(lhs4, states[s]), from_state)
    yield
    u_st = _dot(inv, from_state[0:c2] + av[0:c2])
    yield
    y_st = from_state[c2:] + _dot(a_rb, u_st) + av[c2:]
    yield
    y = jnp.where(lane_lo, y_st[0:c], y_st[c:])
    u = jnp.where(lane_lo, u_st[0:c], u_st[c:])
    uv = jnp.concatenate([u, v], axis=0)
    bk = jnp.concatenate([bt * e_end, kt * e_end], axis=0)
    new_states = []
    if nseq == 1:
        inc = _dot_tn(uv, bk)
        new_states.append(states[0] * e_end + jnp.where(cn["blockdiag"], inc, 0.0))
    else:
        seq_of_uv = (lax.broadcasted_iota(jnp.int32, (c2, 1), 0) % c) // cn["seq_len"]
        for s in range(nseq):
            inc = _dot_tn(jnp.where(seq_of_uv == s, uv, 0.0), bk)
            decay = e_end[s * cn["seq_len"]:s * cn["seq_len"] + 1, :]
            new_states.append(states[s] * decay + jnp.where(cn["blockdiag"], inc, 0.0))
    return y, new_states


def _wkv_epilogue(y, r, k, v, g, rk, lnx_g, lnx_b, lane_lo):
    mean = _seg_sum(y, lane_lo) * (1.0 / HEAD)
    yc = y - mean
    var = _seg_sum(yc * yc, lane_lo) * (1.0 / HEAD)
    yn = yc * lax.rsqrt(var + GN_EPS) * lnx_g + lnx_b
    bonus = _seg_sum(r * k * rk, lane_lo) * v
    return (yn + bonus) * g


def _to_blockdiag(s2, lane_lo_half):
    zero = jnp.zeros_like(s2)
    return jnp.concatenate([jnp.where(lane_lo_half, s2, zero), jnp.where(lane_lo_half, zero, s2)],
                           axis=0)


def _wkv_seq_kernel(r_ref, lw_ref, k_ref, v_ref, kkraw_ref, a_ref, g_ref, rk_ref, lg_ref, lb_ref,
                    y_ref, s_out_ref, s_scr):
    t = pl.program_id(2)
    cn = _wkv_consts(1)
    n_pairs = s_scr.shape[0]

    @pl.when(t == 0)
    def _():
        s_scr[...] = jnp.zeros_like(s_scr)

    n_chunks = r_ref.shape[1] // CHUNK

    def body(i, states):
        sl = pl.ds(pl.multiple_of(i * CHUNK, CHUNK), CHUNK)

        def pair_stages(p):
            ln = pl.ds(p * PAIR, PAIR)
            r, lw, k, v = r_ref[0, sl, ln], lw_ref[0, sl, ln], k_ref[0, sl, ln], v_ref[0, sl, ln]
            y, (state,) = yield from _wkv_chunk(r, lw, k, v, kkraw_ref[0, sl, ln],
                                                a_ref[0, sl, ln], [states[p]], cn, 1)
            y_ref[0, sl, ln] = _wkv_epilogue(y, r, k, v, g_ref[0, sl, ln], rk_ref[:, ln],
                                             lg_ref[:, ln], lb_ref[:, ln], cn["lane_lo"])
            return state

        return tuple(_run_interleaved([pair_stages(p) for p in range(n_pairs)]))

    states = lax.fori_loop(0, n_chunks, body, tuple(s_scr[p] for p in range(n_pairs)))
    for p in range(n_pairs):
        s_scr[p] = states[p]
        s_out_ref[0, p] = states[p][0:HEAD] + states[p][HEAD:]


def _wkv_step_kernel(r_ref, lw_ref, k_ref, v_ref, kkraw_ref, a_ref, g_ref, rk_ref, lg_ref, lb_ref,
                     s_in_ref, y_ref, s_out_ref, *, nseq):
    cn = _wkv_consts(nseq)
    half = lax.broadcasted_iota(jnp.int32, (HEAD, PAIR), 1) < HEAD
    states = [_to_blockdiag(s_in_ref[s, 0], half) for s in range(nseq)]
    r, lw, k, v = r_ref[...], lw_ref[...], k_ref[...], v_ref[...]
    (y, states), = _run_interleaved(
        [_wkv_chunk(r, lw, k, v, kkraw_ref[...], a_ref[...], states, cn, nseq)])
    y_ref[...] = _wkv_epilogue(y, r, k, v, g_ref[...], rk_ref[...], lg_ref[...], lb_ref[...],
                               cn["lane_lo"])
    for s in range(nseq):
        s_out_ref[s, 0] = states[s][0:HEAD] + states[s][HEAD:]


def _wkv_seq(acts, wp, tt, n_pairs):
    b, t, _ = acts[0].shape
    width = n_pairs * PAIR
    row_spec = pl.BlockSpec((1, tt, width), lambda i, p, j: (i, j, p))
    vec = pl.BlockSpec((1, width), lambda i, p, j: (0, p))
    return pl.pallas_call(
        _wkv_seq_kernel,
        grid=(b, N_PAIRS // n_pairs, t // tt),
        in_specs=[row_spec] * 7 + [vec] * 3,
        out_specs=[row_spec,
                   pl.BlockSpec((1, n_pairs, HEAD, PAIR), lambda i, p, j: (i, p, 0, 0))],
        out_shape=[jax.ShapeDtypeStruct((b, t, D_MODEL), F32),
                   jax.ShapeDtypeStruct((b, N_PAIRS, HEAD, PAIR), F32)],
        scratch_shapes=[pltpu.VMEM((n_pairs, PAIR, PAIR), F32)],
        compiler_params=pltpu.CompilerParams(
            dimension_semantics=("arbitrary", "arbitrary", "arbitrary"),
            vmem_limit_bytes=VMEM_LIMIT),
        name="wkv_seq",
    )(*acts, wp["r_k"], wp["lnx_g"], wp["lnx_b"])


def _wkv_step(acts, s_pairs, wp, steps):
    n = acts[0].shape[0]
    nseq = CHUNK // steps
    row_spec = pl.BlockSpec((CHUNK, PAIR), lambda i, p: (i, p))
    vec = pl.BlockSpec((1, PAIR), lambda i, p: (0, p))
    st_spec = pl.BlockSpec((nseq, 1, HEAD, PAIR), lambda i, p: (i, p, 0, 0))
    return pl.pallas_call(
        functools.partial(_wkv_step_kernel, nseq=nseq),
        grid=(n // CHUNK, N_PAIRS),
        in_specs=[row_spec] * 7 + [vec] * 3 + [st_spec],
        out_specs=[row_spec, st_spec],
        out_shape=[jax.ShapeDtypeStruct((n, D_MODEL), F32),
                   jax.ShapeDtypeStruct(s_pairs.shape, F32)],
        compiler_params=pltpu.CompilerParams(
            dimension_semantics=("arbitrary", "arbitrary"), vmem_limit_bytes=VMEM_LIMIT),
        name="wkv_step",
    )(*acts, wp["r_k"], wp["lnx_g"], wp["lnx_b"], s_pairs)


def _post_kernel(x_ref, ya_ref, ga_ref, gyb_ref, wo_ref, l1g_ref, l1b_ref, wg_ref, wu_ref, wd_ref,
                 l2g_ref, l2b_ref, y_ref):
    x = x_ref[...]
    merged = ga_ref[...] * ya_ref[...] + gyb_ref[...]
    mix = _dot(merged, wo_ref[...])
    h1 = _layer_norm(ALPHA * x + mix, l1g_ref[...], l1b_ref[...])
    h1b = h1.astype(BF16)
    gate = jnp.dot(h1b, wg_ref[...], preferred_element_type=F32)
    up = jnp.dot(h1b, wu_ref[...], preferred_element_type=F32)
    act = (gate * jax.nn.sigmoid(gate)) * up
    ffn = _dot(act, wd_ref[...])
    y_ref[...] = _layer_norm(ALPHA * h1 + ffn, l2g_ref[...], l2b_ref[...])


def _post(x, ya, ga, gyb, wp, tm):
    n = x.shape[0]
    row_spec = pl.BlockSpec((tm, D_MODEL), lambda i: (i, 0))
    vec = _const_spec((1, D_MODEL))
    return pl.pallas_call(
        _post_kernel,
        grid=(n // tm,),
        in_specs=[row_spec] * 4 + [_const_spec((D_MODEL, D_MODEL)), vec, vec,
                                   _const_spec((D_MODEL, D_FF)), _const_spec((D_MODEL, D_FF)),
                                   _const_spec((D_FF, D_MODEL)), vec, vec],
        out_specs=row_spec,
        out_shape=jax.ShapeDtypeStruct((n, D_MODEL), F32),
        compiler_params=pltpu.CompilerParams(
            dimension_semantics=("arbitrary",), vmem_limit_bytes=VMEM_LIMIT),
        name="post",
    )(x, ya, ga, gyb, wp["w_o"], wp["ln1_g"], wp["ln1_b"], wp["w_gate"], wp["w_up"], wp["w_down"],
      wp["ln2_g"], wp["ln2_b"])


def _prep_weights(w_in, tmix_mu, w0, w2_decay, a0, a2_iclr, g2_gate, k_k, k_a, r_k, lnx_g, lnx_b,
                  conv_w, conv_b, lru_wa, lru_ba, lru_wi, lru_bi, lru_lambda, w_o,
                  ln1_g, ln1_b, w_ffn_gate, w_ffn_up, w_ffn_down, ln2_g, ln2_b):
    row = lambda v: v.reshape(1, -1).astype(F32)
    zeros = lambda r: jnp.zeros((r, D_MODEL), F32)
    w2ext = jnp.concatenate([
        jnp.concatenate([w2_decay, zeros(DECAY_LORA), zeros(DECAY_LORA)], axis=1),
        jnp.concatenate([zeros(ICLR_LORA), a2_iclr, zeros(ICLR_LORA)], axis=1),
        jnp.concatenate([zeros(GATE_LORA), zeros(GATE_LORA), g2_gate], axis=1)], axis=0)

    def gate_blockdiag(w):
        per = LRU_GROUP // LRU_BS
        w4 = w.reshape(LRU_BLOCKS // per, per, LRU_BS, LRU_BS)
        eye = jnp.eye(per, dtype=w.dtype)
        bd = jnp.einsum("gpcd,pq->gpcqd", w4, eye)
        return bd.reshape(LRU_BLOCKS // per, LRU_GROUP, LRU_GROUP).astype(BF16)

    return dict(
        w_rw=w_in[:, :RWKV_COLS].astype(BF16), w_lru=w_in[:, RWKV_COLS:].astype(BF16),
        mu=row(tmix_mu), w2ext=w2ext.astype(BF16), w0=row(w0), a0=row(a0), k_k=row(k_k),
        k_a=row(k_a), r_k=row(r_k), lnx_g=row(lnx_g), lnx_b=row(lnx_b),
        conv_w=conv_w.astype(F32), conv_b=row(conv_b), wa_bd=gate_blockdiag(lru_wa),
        ba=row(lru_ba), wi_bd=gate_blockdiag(lru_wi), bi=row(lru_bi), lam=row(lru_lambda),
        w_o=w_o.astype(BF16), ln1_g=row(ln1_g), ln1_b=row(ln1_b), w_gate=w_ffn_gate.astype(BF16),
        w_up=w_ffn_up.astype(BF16), w_down=w_ffn_down.astype(BF16), ln2_g=row(ln2_g),
        ln2_b=row(ln2_b))


def _pairs_from_heads(s):
    b = s.shape[0]
    return s.reshape(b, N_PAIRS, 2, HEAD, HEAD).transpose(0, 1, 3, 2, 4).reshape(
        b, N_PAIRS, HEAD, PAIR)


def _heads_from_pairs(s):
    b = s.shape[0]
    return s.reshape(b, N_PAIRS, HEAD, 2, HEAD).transpose(0, 1, 3, 2, 4).reshape(
        b, N_HEADS, HEAD, HEAD)


def _prompt_layer(x, wp, tm, tt, tm_post, n_pairs):
    b, t, _ = x.shape
    acts = _rwkv_proj_seq(x, wp, tm)
    ya, s_pairs = _wkv_seq(acts, wp, tt, n_pairs)
    ga, gyb, conv_tail, h_tail = _lru_seq(x, wp, tm)
    y = _post(x.reshape(b * t, D_MODEL), ya.reshape(b * t, D_MODEL), ga.reshape(b * t, D_MODEL),
              gyb.reshape(b * t, D_MODEL), wp, tm_post).reshape(b, t, D_MODEL)
    return (y, x[:, -1], _heads_from_pairs(s_pairs), conv_tail[:, SUBLANES - (CONV_W - 1):],
            h_tail[:, SUBLANES - 1])


def _sample_layer(x, shift_buf, wkv0, conv_buf, h0, wp):
    b, t, _ = x.shape
    n = b * t
    xf = x.reshape(n, D_MODEL)
    xprev = jnp.concatenate([shift_buf[:, None], x[:, :-1]], axis=1).reshape(n, D_MODEL)
    acts = _rwkv_proj_step(xf, xprev, wp)
    ya, s_pairs = _wkv_step(acts, _pairs_from_heads(wkv0), wp, t)
    x_tm = x.transpose(1, 0, 2).reshape(n, D_MODEL)
    conv_tm = conv_buf.transpose(1, 0, 2).reshape((CONV_W - 1) * b, D_LRU)
    ga_tm, gyb_tm, conv_new_tm, h_last = _lru_step(x_tm, conv_tm, h0, wp, t, b)
    to_seq_major = lambda v: v.reshape(t, b, D_MODEL).transpose(1, 0, 2).reshape(n, D_MODEL)
    y = _post(xf, ya, to_seq_major(ga_tm), to_seq_major(gyb_tm), wp, min(n, 256)).reshape(
        b, t, D_MODEL)
    conv_new = conv_new_tm.reshape(CONV_W - 1, b, D_LRU).transpose(1, 0, 2)
    return y, x[:, -1], _heads_from_pairs(s_pairs), conv_new, h_last


def kernel(x_prompt, x_sample, state_shift, state_wkv, state_conv, state_lru, w_in, tmix_mu, w0, w2_decay, a0, a2_iclr, g2_gate, k_k, k_a, r_k, lnx_g, lnx_b, conv_w, conv_b, lru_wa, lru_ba, lru_wi, lru_bi, lru_lambda, w_o, ln1_g, ln1_b, w_ffn_gate, w_ffn_up, w_ffn_down, ln2_g, ln2_b):
    params = (w_in, tmix_mu, w0, w2_decay, a0, a2_iclr, g2_gate, k_k, k_a, r_k, lnx_g, lnx_b,
              conv_w, conv_b, lru_wa, lru_ba, lru_wi, lru_bi, lru_lambda, w_o,
              ln1_g, ln1_b, w_ffn_gate, w_ffn_up, w_ffn_down, ln2_g, ln2_b)
    wp = _prep_weights(*[p[0] for p in params])
    yp, sh_p, wkv_p, conv_p, lru_p = _prompt_layer(x_prompt, wp, 256, 256, 256, 8)
    ys, sh_s, wkv_s, conv_s, lru_s = _sample_layer(
        x_sample, state_shift[0], state_wkv[0], state_conv[0], state_lru[0], wp)
    return (yp, ys, sh_p[None], wkv_p[None], conv_p[None], lru_p[None],
            sh_s[None], wkv_s[None], conv_s[None], lru_s[None])
```

```python
import functools
import math

import jax
import jax.numpy as jnp
from jax import lax
from jax.experimental import pallas as pl
from jax.experimental.pallas import tpu as pltpu

F32 = jnp.float32
BF16 = jnp.bfloat16

D_MODEL = 1024
HEAD = 64
N_HEADS = D_MODEL // HEAD
PAIR = 2 * HEAD
N_PAIRS = N_HEADS // 2
DECAY_LORA = 64
ICLR_LORA = 64
GATE_LORA = 128
LORA = DECAY_LORA + ICLR_LORA + GATE_LORA
RWKV_COLS = 3 * D_MODEL + LORA
GN_EPS = HEAD * 1e-5
D_LRU = D_MODEL
LRU_BLOCKS = 16
LRU_BS = D_LRU // LRU_BLOCKS
LRU_GROUP = 256
CONV_W = 4
LRU_C = 8.0
LRU_COLS = 2 * D_LRU + 2 * D_MODEL
D_FF = 2816
ALPHA = 2.0 ** 0.25
LN_EPS = 1e-5

SUBLANES = 8
CHUNK = 64
VMEM_LIMIT = 56 * 1024 * 1024


def _softplus(x):
    return jnp.maximum(x, 0.0) + jnp.log1p(jnp.exp(-jnp.abs(x)))


def _gelu_tanh(x):
    c = math.sqrt(2.0 / math.pi)
    return x * (0.5 * (1.0 + jnp.tanh(c * (x + 0.044715 * (x * x * x)))))


def _layer_norm(x, g, b):
    mu = jnp.mean(x, axis=-1, keepdims=True)
    xc = x - mu
    var = jnp.mean(xc * xc, axis=-1, keepdims=True)
    return xc * lax.rsqrt(var + LN_EPS) * g + b


def _dot(a, b):
    return jnp.dot(a.astype(BF16), b.astype(BF16), preferred_element_type=F32)


def _dot_nt(a, b):
    return lax.dot_general(a.astype(BF16), b.astype(BF16), (((1,), (1,)), ((), ())),
                           preferred_element_type=F32)


def _dot_tn(a, b):
    return lax.dot_general(a.astype(BF16), b.astype(BF16), (((0,), (0,)), ((), ())),
                           preferred_element_type=F32)


def _seg_sum(x, lane_lo):
    s0 = jnp.sum(jnp.where(lane_lo, x, 0.0), axis=-1, keepdims=True)
    s1 = jnp.sum(jnp.where(lane_lo, 0.0, x), axis=-1, keepdims=True)
    return jnp.where(lane_lo, s0, s1)


def _head_sums(x):
    lane_lo = lax.broadcasted_iota(jnp.int32, (x.shape[0], PAIR), 1) < HEAD
    return jnp.concatenate([_seg_sum(x[:, p * PAIR:(p + 1) * PAIR], lane_lo)
                            for p in range(N_PAIRS)], axis=1)


def _rwkv_prep(z, zprev, mu, w2ext, w0, a0, k_k, k_a):
    zm = z + mu * (zprev - z)
    r = zm[:, 0:D_MODEL]
    k = zm[:, D_MODEL:2 * D_MODEL]
    v = zm[:, 2 * D_MODEL:3 * D_MODEL]
    lo = zm[:, 3 * D_MODEL:RWKV_COLS]
    lane = lax.broadcasted_iota(jnp.int32, lo.shape, 1)
    act = jnp.where(lane < DECAY_LORA, jnp.tanh(lo),
                    jnp.where(lane < DECAY_LORA + ICLR_LORA, lo, jax.nn.sigmoid(lo)))
    lora = _dot(act, w2ext)
    w = -_softplus(-(w0 + lora[:, 0:D_MODEL])) - 0.5
    lw = -jnp.exp(w)
    a = jax.nn.sigmoid(a0 + lora[:, D_MODEL:2 * D_MODEL])
    g = lora[:, 2 * D_MODEL:3 * D_MODEL]
    kkraw = k * k_k
    kk = kkraw / jnp.maximum(jnp.sqrt(_head_sums(kkraw * kkraw)), 1e-12)
    k2 = k * (1.0 + (a - 1.0) * k_a)
    return r, lw, k2, v, -kk, kk * a, g


def _rwkv_proj_seq_kernel(x_ref, w_ref, mu_ref, w2_ref, w0_ref, a0_ref, kk_ref, ka_ref,
                          r_ref, lw_ref, k_ref, v_ref, kkraw_ref, a_ref, g_ref, hist_ref):
    @pl.when(pl.program_id(1) == 0)
    def _():
        hist_ref[...] = jnp.zeros_like(hist_ref)

    z = _dot(x_ref[0], w_ref[...])
    tm = z.shape[0]
    rolled = pltpu.roll(z, 1, 0)
    row = lax.broadcasted_iota(jnp.int32, (SUBLANES, 1), 0)
    first = jnp.where(row == 0, hist_ref[SUBLANES - 1:SUBLANES, :], rolled[0:SUBLANES])
    zprev = jnp.concatenate([first, rolled[SUBLANES:]], axis=0)
    hist_ref[...] = z[tm - SUBLANES:tm]
    outs = _rwkv_prep(z, zprev, mu_ref[...], w2_ref[...], w0_ref[...], a0_ref[...],
                      kk_ref[...], ka_ref[...])
    for o_ref, o in zip((r_ref, lw_ref, k_ref, v_ref, kkraw_ref, a_ref, g_ref), outs):
        o_ref[0] = o


def _rwkv_proj_step_kernel(x_ref, xp_ref, w_ref, mu_ref, w2_ref, w0_ref, a0_ref, kk_ref, ka_ref,
                           r_ref, lw_ref, k_ref, v_ref, kkraw_ref, a_ref, g_ref):
    z = _dot(x_ref[...], w_ref[...])
    zprev = _dot(xp_ref[...], w_ref[...])
    outs = _rwkv_prep(z, zprev, mu_ref[...], w2_ref[...], w0_ref[...], a0_ref[...],
                      kk_ref[...], ka_ref[...])
    for o_ref, o in zip((r_ref, lw_ref, k_ref, v_ref, kkraw_ref, a_ref, g_ref), outs):
        o_ref[...] = o


def _const_spec(shape):
    nd = len(shape)
    return pl.BlockSpec(shape, lambda *_: (0,) * nd, pipeline_mode=pl.Buffered(1))


def _rwkv_proj_seq(x, wp, tm):
    b, t, _ = x.shape
    row_spec = pl.BlockSpec((1, tm, D_MODEL), lambda i, j: (i, j, 0))
    vec = _const_spec((1, D_MODEL))
    return pl.pallas_call(
        _rwkv_proj_seq_kernel,
        grid=(b, t // tm),
        in_specs=[row_spec, _const_spec((D_MODEL, RWKV_COLS)), _const_spec((1, RWKV_COLS)),
                  _const_spec((LORA, 3 * D_MODEL)), vec, vec, vec, vec],
        out_specs=[row_spec] * 7,
        out_shape=[jax.ShapeDtypeStruct((b, t, D_MODEL), F32)] * 7,
        scratch_shapes=[pltpu.VMEM((SUBLANES, RWKV_COLS), F32)],
        compiler_params=pltpu.CompilerParams(
            dimension_semantics=("arbitrary", "arbitrary"), vmem_limit_bytes=VMEM_LIMIT),
        name="rwkv_proj_seq",
    )(x, wp["w_rw"], wp["mu"], wp["w2ext"], wp["w0"], wp["a0"], wp["k_k"], wp["k_a"])


def _rwkv_proj_step(x, xprev, wp):
    n = x.shape[0]
    full = pl.BlockSpec((n, D_MODEL), lambda i: (0, 0))
    vec = _const_spec((1, D_MODEL))
    return pl.pallas_call(
        _rwkv_proj_step_kernel,
        grid=(1,),
        in_specs=[full, full, _const_spec((D_MODEL, RWKV_COLS)), _const_spec((1, RWKV_COLS)),
                  _const_spec((LORA, 3 * D_MODEL)), vec, vec, vec, vec],
        out_specs=[full] * 7,
        out_shape=[jax.ShapeDtypeStruct((n, D_MODEL), F32)] * 7,
        compiler_params=pltpu.CompilerParams(
            dimension_semantics=("arbitrary",), vmem_limit_bytes=VMEM_LIMIT),
        name="rwkv_proj_step",
    )(x, xprev, wp["w_rw"], wp["mu"], wp["w2ext"], wp["w0"], wp["a0"], wp["k_k"], wp["k_a"])


def _lru_gates(u, wa_ref, ba, wi_ref, bi, sp):
    ub = u.astype(BF16)
    ra, ia = [], []
    for q in range(D_LRU // LRU_GROUP):
        uq = ub[:, q * LRU_GROUP:(q + 1) * LRU_GROUP]
        ra.append(jnp.dot(uq, wa_ref[q], preferred_element_type=F32))
        ia.append(jnp.dot(uq, wi_ref[q], preferred_element_type=F32))
    rg = jax.nn.sigmoid(jnp.concatenate(ra, axis=1) + ba)
    ig = jax.nn.sigmoid(jnp.concatenate(ia, axis=1) + bi)
    log_a = -LRU_C * rg * sp
    a = jnp.exp(log_a)
    mult = jnp.sqrt(-jnp.tanh(log_a) * (a * a + 1.0))
    return a, mult, ig


def _lru_seq_kernel(x_ref, w_ref, cw_ref, cb_ref, wa_ref, ba_ref, wi_ref, bi_ref, lam_ref,
                    ga_ref, gyb_ref, conv_ref, hlast_ref,
                    xhist_ref, hc_ref, a_s, bx_s, h_s):
    t = pl.program_id(1)

    @pl.when(t == 0)
    def _():
        xhist_ref[...] = jnp.zeros_like(xhist_ref)
        hc_ref[...] = jnp.zeros_like(hc_ref)

    z = _dot(x_ref[0], w_ref[...])
    tm = z.shape[0]
    xb = z[:, 0:D_LRU]
    gb = z[:, D_LRU:2 * D_LRU]
    zga = z[:, 2 * D_LRU:2 * D_LRU + D_MODEL]
    zgb = z[:, 2 * D_LRU + D_MODEL:]
    row8 = lax.broadcasted_iota(jnp.int32, (SUBLANES, 1), 0)
    hist = xhist_ref[...]

    def shifted(k):
        rolled = pltpu.roll(xb, k, 0)
        first = jnp.where(row8 < k, pltpu.roll(hist, k, 0), rolled[0:SUBLANES])
        return jnp.concatenate([first, rolled[SUBLANES:]], axis=0)

    cw = cw_ref[...]
    u = (cb_ref[...] + cw[0:1] * shifted(3) + cw[1:2] * shifted(2) + cw[2:3] * shifted(1)
         + cw[3:4] * xb)
    last8 = xb[tm - SUBLANES:tm]
    xhist_ref[...] = last8
    conv_ref[0] = last8

    sp = _softplus(-lam_ref[...])
    a, mult, ig = _lru_gates(u, wa_ref, ba_ref[...], wi_ref, bi_ref[...], sp)
    row = lax.broadcasted_iota(jnp.int32, (tm, 1), 0)
    mult = jnp.where(jnp.logical_and(row == 0, t == 0), 1.0, mult)
    a_s[...] = a
    bx_s[...] = mult * ig * u

    def blk(i, hc):
        o = pl.multiple_of(i * SUBLANES, SUBLANES)
        av = a_s[pl.ds(o, SUBLANES), :]
        bv = bx_s[pl.ds(o, SUBLANES), :]
        for d in (1, 2, 4):
            a_sh = jnp.where(row8 < d, 1.0, pltpu.roll(av, d, 0))
            b_sh = jnp.where(row8 < d, 0.0, pltpu.roll(bv, d, 0))
            bv = av * b_sh + bv
            av = av * a_sh
        h = bv + av * hc
        h_s[pl.ds(o, SUBLANES), :] = h
        return h[SUBLANES - 1:SUBLANES, :]

    hc = lax.fori_loop(0, tm // SUBLANES, blk, hc_ref[0:1, :])
    hc_ref[0:1, :] = hc
    h = h_s[...]
    hlast_ref[0] = h[tm - SUBLANES:tm]
    ga_ref[0] = jax.nn.sigmoid(zga)
    gyb_ref[0] = jax.nn.sigmoid(zgb) * (h * _gelu_tanh(gb))


def _lru_step_kernel(x_ref, conv0_ref, h0_ref, w_ref, cw_ref, cb_ref, wa_ref, ba_ref, wi_ref,
                     bi_ref, lam_ref, ga_ref, gyb_ref, conv_ref, hlast_ref, *, steps, nseq):
    z = _dot(x_ref[...], w_ref[...])
    xb = z[:, 0:D_LRU]
    gb = z[:, D_LRU:2 * D_LRU]
    zga = z[:, 2 * D_LRU:2 * D_LRU + D_MODEL]
    zgb = z[:, 2 * D_LRU + D_MODEL:]
    n = steps * nseq
    hist = (CONV_W - 1) * nseq
    xext = jnp.concatenate([conv0_ref[...], xb], axis=0)

    def shifted(k):
        return xext[hist - k * nseq:hist - k * nseq + n]

    cw = cw_ref[...]
    u = (cb_ref[...] + cw[0:1] * shifted(3) + cw[1:2] * shifted(2) + cw[2:3] * shifted(1)
         + cw[3:4] * xb)
    conv_ref[...] = xext[n:n + hist]
    sp = _softplus(-lam_ref[...])
    a, mult, ig = _lru_gates(u, wa_ref, ba_ref[...], wi_ref, bi_ref[...], sp)
    bx = mult * ig * u
    h = h0_ref[...]
    hs = []
    for s in range(steps):
        h = a[s * nseq:(s + 1) * nseq] * h + bx[s * nseq:(s + 1) * nseq]
        hs.append(h)
    hlast_ref[...] = h
    hall = jnp.concatenate(hs, axis=0)
    ga_ref[...] = jax.nn.sigmoid(zga)
    gyb_ref[...] = jax.nn.sigmoid(zgb) * (hall * _gelu_tanh(gb))


def _lru_weight_specs():
    vec = _const_spec((1, D_LRU))
    gate_w = _const_spec((D_LRU // LRU_GROUP, LRU_GROUP, LRU_GROUP))
    return [_const_spec((D_MODEL, LRU_COLS)), _const_spec((CONV_W, D_LRU)), vec,
            gate_w, vec, gate_w, vec, vec]


def _lru_weights(wp):
    return (wp["w_lru"], wp["conv_w"], wp["conv_b"], wp["wa_bd"], wp["ba"], wp["wi_bd"],
            wp["bi"], wp["lam"])


def _lru_seq(x, wp, tm):
    b, t, _ = x.shape
    row_spec = pl.BlockSpec((1, tm, D_MODEL), lambda i, j: (i, j, 0))
    tail_spec = pl.BlockSpec((1, SUBLANES, D_LRU), lambda i, j: (i, 0, 0))
    return pl.pallas_call(
        _lru_seq_kernel,
        grid=(b, t // tm),
        in_specs=[row_spec] + _lru_weight_specs(),
        out_specs=[row_spec, row_spec, tail_spec, tail_spec],
        out_shape=[jax.ShapeDtypeStruct((b, t, D_MODEL), F32)] * 2
        + [jax.ShapeDtypeStruct((b, SUBLANES, D_LRU), F32)] * 2,
        scratch_shapes=[pltpu.VMEM((SUBLANES, D_LRU), F32), pltpu.VMEM((SUBLANES, D_LRU), F32),
                        pltpu.VMEM((tm, D_LRU), F32), pltpu.VMEM((tm, D_LRU), F32),
                        pltpu.VMEM((tm, D_LRU), F32)],
        compiler_params=pltpu.CompilerParams(
            dimension_semantics=("arbitrary", "arbitrary"), vmem_limit_bytes=VMEM_LIMIT),
        name="lru_seq",
    )(x, *_lru_weights(wp))


def _lru_step(x_tm, conv_tm, h0, wp, steps, nseq):
    n = steps * nseq
    hist = (CONV_W - 1) * nseq

    def full(r):
        return pl.BlockSpec((r, D_MODEL), lambda i: (0, 0))

    return pl.pallas_call(
        functools.partial(_lru_step_kernel, steps=steps, nseq=nseq),
        grid=(1,),
        in_specs=[full(n), full(hist), full(nseq)] + _lru_weight_specs(),
        out_specs=[full(n), full(n), full(hist), full(nseq)],
        out_shape=[jax.ShapeDtypeStruct((n, D_MODEL), F32)] * 2
        + [jax.ShapeDtypeStruct((hist, D_LRU), F32), jax.ShapeDtypeStruct((nseq, D_LRU), F32)],
        compiler_params=pltpu.CompilerParams(
            dimension_semantics=("arbitrary",), vmem_limit_bytes=VMEM_LIMIT),
        name="lru_step",
    )(x_tm, conv_tm, h0, *_lru_weights(wp))


def _wkv_consts(nseq):
    c = CHUNK
    seq_len = c // nseq
    lane = lax.broadcasted_iota(jnp.int32, (c, PAIR), 1)
    row = lax.broadcasted_iota(jnp.int32, (c, PAIR), 0)
    lane_lo = lane < HEAD
    col = lane % c
    same = (row // seq_len) == (col // seq_len)
    strict = jnp.logical_and(same, col < row)
    incl = jnp.logical_and(same, col <= row)
    eye = jnp.where(col == row, 1.0, 0.0).astype(F32)
    r3 = lax.broadcasted_iota(jnp.int32, (c, 3 * c), 0)
    c3 = lax.broadcasted_iota(jnp.int32, (c, 3 * c), 1) % c
    same3 = (r3 // seq_len) == (c3 // seq_len)
    tri3 = jnp.where(jnp.logical_and(same3, c3 <= r3), 1.0, 0.0).astype(BF16)
    tot3 = jnp.where(same3, 1.0, 0.0).astype(BF16)
    rp = lax.broadcasted_iota(jnp.int32, (PAIR, PAIR), 0)
    cp = lax.broadcasted_iota(jnp.int32, (PAIR, PAIR), 1)
    blockdiag = (rp // HEAD) == (cp // HEAD)
    return dict(lane_lo=lane_lo, tri3=tri3, tot3=tot3, strict=strict, incl=incl, eye=eye,
                blockdiag=blockdiag, seq_len=seq_len)


def _split3(x):
    hi = x.astype(BF16)
    rest = x - hi.astype(F32)
    mid = rest.astype(BF16)
    lo = (rest - mid.astype(F32)).astype(BF16)
    return jnp.concatenate([hi, mid, lo], axis=0)


def _split_heads(x, lane_lo):
    zero = jnp.zeros_like(x)
    return jnp.concatenate([jnp.where(lane_lo, x, zero), jnp.where(lane_lo, zero, x)], axis=0)


def _run_interleaved(stage_gens):
    results = [None] * len(stage_gens)
    live = list(range(len(stage_gens)))
    while live:
        still = []
        for idx in live:
            try:
                next(stage_gens[idx])
                still.append(idx)
            except StopIteration as done:
                results[idx] = done.value
        live = still
    return results


def _wkv_chunk(r, lw, k, v, al, be, states, cn, nseq):
    c = CHUNK
    lane_lo = cn["lane_lo"]
    lw3 = _split3(lw)
    cum = jnp.dot(cn["tri3"], lw3, preferred_element_type=F32)
    if nseq == 1:
        end = cum[c - 1:c, :]
    else:
        end = jnp.dot(cn["tot3"], lw3, preferred_element_type=F32)
    yield
    e_cum = jnp.exp(cum)
    e_neg = jnp.exp(-cum)
    e_end = jnp.exp(end)
    rt = r * e_cum
    at = al * jnp.exp(cum - lw)
    bt = be * e_neg
    kt = k * e_neg
    lhs2 = jnp.concatenate([at, rt], axis=0).astype(BF16)
    keys = jnp.concatenate([_split_heads(bt, lane_lo), _split_heads(kt, lane_lo)], axis=0)
    gram = _dot_nt(lhs2, keys)
    yield
    a_ab = jnp.where(cn["strict"], gram[0:c, 0:PAIR], 0.0)
    a_ak = jnp.where(cn["strict"], gram[0:c, PAIR:], 0.0)
    a_rb = jnp.where(cn["incl"], gram[c:, 0:PAIR], 0.0)
    a_rk = jnp.where(cn["incl"], gram[c:, PAIR:], 0.0)
    inv = cn["eye"] + a_ab
    n_iter = int(math.log2(cn["seq_len"])) - 1
    pw = _dot(a_ab, _split_heads(a_ab, lane_lo))
    yield
    for it in range(n_iter):
        pw_heads = _split_heads(pw, lane_lo)
        if it < n_iter - 1:
            both = _dot(jnp.concatenate([inv, pw], axis=0), pw_heads)
            inv = inv + both[0:c]
            pw = both[c:]
        else:
            inv = inv + _dot(inv, pw_heads)
        yield
    av = _dot(jnp.concatenate([a_ak, a_rk], axis=0), _split_heads(v, lane_lo))
    if nseq == 1:
        from_state = _dot_nt(lhs2, states[0])
    else:
        seq_of_row = (lax.broadcasted_iota(jnp.int32, (2 * c, 1), 0) % c) // cn["seq_len"]
        from_state = jnp.zeros((2 * c, PAIR), F32)
        for s in range(nseq):
            from_state = jnp.where(seq_of_row == s, _dot_nt(lhs2, states[s]), from_state)
    yield
    u = _dot(inv, _split_heads(from_state[0:c] + av[0:c], lane_lo))
    yield
    y = from_state[c:] + av[c:] + _dot(a_rb, _split_heads(u, lane_lo))
    yield
    uv = jnp.concatenate([u, v], axis=0)
    bk = jnp.concatenate([bt * e_end, kt * e_end], axis=0)
    new_states = []
    if nseq == 1:
        inc = _dot_tn(uv, bk)
        new_states.append(states[0] * e_end + jnp.where(cn["blockdiag"], inc, 0.0))
    else:
        seq_of_uv = (lax.broadcasted_iota(jnp.int32, (2 * c, 1), 0) % c) // cn["seq_len"]
        for s in range(nseq):
            inc = _dot_tn(jnp.where(seq_of_uv == s, uv, 0.0), bk)
            decay = e_end[s * cn["seq_len"]:s * cn["seq_len"] + 1, :]
            new_states.append(states[s] * decay + jnp.where(cn["blockdiag"], inc, 0.0))
    return y, new_states


def _wkv_readout(y, r, k, v, g, rk, lnx_g, lnx_b, lane_lo):
    mean = _seg_sum(y, lane_lo) * (1.0 / HEAD)
    bonus = _seg_sum(r * k * rk, lane_lo) * v
    yield
    yc = y - mean
    var = _seg_sum(yc * yc, lane_lo) * (1.0 / HEAD)
    yield
    yn = yc * lax.rsqrt(var + GN_EPS) * lnx_g + lnx_b
    return (yn + bonus) * g


def _to_blockdiag(s2, lane_lo_half):
    zero = jnp.zeros_like(s2)
    return jnp.concatenate([jnp.where(lane_lo_half, s2, zero), jnp.where(lane_lo_half, zero, s2)],
                           axis=0)


def _wkv_seq_kernel(r_ref, lw_ref, k_ref, v_ref, al_ref, be_ref, g_ref, rk_ref, lg_ref, lb_ref,
                    y_ref, s_out_ref, s_scr, yraw_scr):
    t = pl.program_id(2)
    cn = _wkv_consts(1)
    n_pairs = s_scr.shape[0]

    @pl.when(t == 0)
    def _():
        s_scr[...] = jnp.zeros_like(s_scr)
        yraw_scr[...] = jnp.zeros_like(yraw_scr)

    n_chunks = r_ref.shape[1] // CHUNK

    def readout_stages(p, rows):
        ln = pl.ds(p * PAIR, PAIR)
        out = yield from _wkv_readout(
            yraw_scr[:, ln], r_ref[0, rows, ln], k_ref[0, rows, ln], v_ref[0, rows, ln],
            g_ref[0, rows, ln], rk_ref[:, ln], lg_ref[:, ln], lb_ref[:, ln], cn["lane_lo"])
        y_ref[0, rows, ln] = out

    def body(i, states):
        sl = pl.ds(pl.multiple_of(i * CHUNK, CHUNK), CHUNK)
        prev = pl.ds(pl.multiple_of(jnp.maximum(i - 1, 0) * CHUNK, CHUNK), CHUNK)

        def pair_stages(p):
            ln = pl.ds(p * PAIR, PAIR)
            y, (state,) = yield from _wkv_chunk(
                r_ref[0, sl, ln], lw_ref[0, sl, ln], k_ref[0, sl, ln], v_ref[0, sl, ln],
                al_ref[0, sl, ln], be_ref[0, sl, ln], [states[p]], cn, 1)
            yraw_scr[:, ln] = y
            return state

        results = _run_interleaved([readout_stages(p, prev) for p in range(n_pairs)]
                                   + [pair_stages(p) for p in range(n_pairs)])
        return tuple(results[n_pairs:])

    states = lax.fori_loop(0, n_chunks, body, tuple(s_scr[p] for p in range(n_pairs)))
    last = pl.ds((n_chunks - 1) * CHUNK, CHUNK)
    _run_interleaved([readout_stages(p, last) for p in range(n_pairs)])
    for p in range(n_pairs):
        s_scr[p] = states[p]
        s_out_ref[0, p] = states[p][0:HEAD] + states[p][HEAD:]


def _wkv_step_kernel(r_ref, lw_ref, k_ref, v_ref, al_ref, be_ref, g_ref, rk_ref, lg_ref, lb_ref,
                     s_in_ref, y_ref, s_out_ref, *, nseq):
    cn = _wkv_consts(nseq)
    half = lax.broadcasted_iota(jnp.int32, (HEAD, PAIR), 1) < HEAD
    states = [_to_blockdiag(s_in_ref[s, 0], half) for s in range(nseq)]
    r, k, v = r_ref[...], k_ref[...], v_ref[...]
    (y, states), = _run_interleaved(
        [_wkv_chunk(r, lw_ref[...], k, v, al_ref[...], be_ref[...], states, cn, nseq)])
    (ya,) = _run_interleaved(
        [_wkv_readout(y, r, k, v, g_ref[...], rk_ref[...], lg_ref[...], lb_ref[...],
                      cn["lane_lo"])])
    y_ref[...] = ya
    for s in range(nseq):
        s_out_ref[s, 0] = states[s][0:HEAD] + states[s][HEAD:]


def _wkv_seq(acts, wp, tt, n_pairs):
    b, t, _ = acts[0].shape
    width = n_pairs * PAIR
    row_spec = pl.BlockSpec((1, tt, width), lambda i, p, j: (i, j, p))
    vec = pl.BlockSpec((1, width), lambda i, p, j: (0, p))
    return pl.pallas_call(
        _wkv_seq_kernel,
        grid=(b, N_PAIRS // n_pairs, t // tt),
        in_specs=[row_spec] * 7 + [vec] * 3,
        out_specs=[row_spec,
                   pl.BlockSpec((1, n_pairs, HEAD, PAIR), lambda i, p, j: (i, p, 0, 0))],
        out_shape=[jax.ShapeDtypeStruct((b, t, D_MODEL), F32),
                   jax.ShapeDtypeStruct((b, N_PAIRS, HEAD, PAIR), F32)],
        scratch_shapes=[pltpu.VMEM((n_pairs, PAIR, PAIR), F32), pltpu.VMEM((CHUNK, width), F32)],
        compiler_params=pltpu.CompilerParams(
            dimension_semantics=("arbitrary", "arbitrary", "arbitrary"),
            vmem_limit_bytes=VMEM_LIMIT),
        name="wkv_seq",
    )(*acts, wp["r_k"], wp["lnx_g"], wp["lnx_b"])


def _wkv_step(acts, s_pairs, wp, steps):
    n = acts[0].shape[0]
    nseq = CHUNK // steps
    row_spec = pl.BlockSpec((CHUNK, PAIR), lambda i, p: (i, p))
    vec = pl.BlockSpec((1, PAIR), lambda i, p: (0, p))
    st_spec = pl.BlockSpec((nseq, 1, HEAD, PAIR), lambda i, p: (i, p, 0, 0))
    return pl.pallas_call(
        functools.partial(_wkv_step_kernel, nseq=nseq),
        grid=(n // CHUNK, N_PAIRS),
        in_specs=[row_spec] * 7 + [vec] * 3 + [st_spec],
        out_specs=[row_spec, st_spec],
        out_shape=[jax.ShapeDtypeStruct((n, D_MODEL), F32),
                   jax.ShapeDtypeStruct(s_pairs.shape, F32)],
        compiler_params=pltpu.CompilerParams(
            dimension_semantics=("arbitrary", "arbitrary"), vmem_limit_bytes=VMEM_LIMIT),
        name="wkv_step",
    )(*acts, wp["r_k"], wp["lnx_g"], wp["lnx_b"], s_pairs)


def _post_kernel(x_ref, ya_ref, ga_ref, gyb_ref, wo_ref, l1g_ref, l1b_ref, wg_ref, wu_ref, wd_ref,
                 l2g_ref, l2b_ref, y_ref):
    x = x_ref[...]
    merged = ga_ref[...] * ya_ref[...] + gyb_ref[...]
    mix = _dot(merged, wo_ref[...])
    h1 = _layer_norm(ALPHA * x + mix, l1g_ref[...], l1b_ref[...])
    h1b = h1.astype(BF16)
    gate = jnp.dot(h1b, wg_ref[...], preferred_element_type=F32)
    up = jnp.dot(h1b, wu_ref[...], preferred_element_type=F32)
    act = (gate * jax.nn.sigmoid(gate)) * up
    ffn = _dot(act, wd_ref[...])
    y_ref[...] = _layer_norm(ALPHA * h1 + ffn, l2g_ref[...], l2b_ref[...])


def _post(x, ya, ga, gyb, wp, tm):
    n = x.shape[0]
    row_spec = pl.BlockSpec((tm, D_MODEL), lambda i: (i, 0))
    vec = _const_spec((1, D_MODEL))
    return pl.pallas_call(
        _post_kernel,
        grid=(n // tm,),
        in_specs=[row_spec] * 4 + [_const_spec((D_MODEL, D_MODEL)), vec, vec,
                                   _const_spec((D_MODEL, D_FF)), _const_spec((D_MODEL, D_FF)),
                                   _const_spec((D_FF, D_MODEL)), vec, vec],
        out_specs=row_spec,
        out_shape=jax.ShapeDtypeStruct((n, D_MODEL), F32),
        compiler_params=pltpu.CompilerParams(
            dimension_semantics=("arbitrary",), vmem_limit_bytes=VMEM_LIMIT),
        name="post",
    )(x, ya, ga, gyb, wp["w_o"], wp["ln1_g"], wp["ln1_b"], wp["w_gate"], wp["w_up"], wp["w_down"],
      wp["ln2_g"], wp["ln2_b"])


def _prep_weights(w_in, tmix_mu, w0, w2_decay, a0, a2_iclr, g2_gate, k_k, k_a, r_k, lnx_g, lnx_b,
                  conv_w, conv_b, lru_wa, lru_ba, lru_wi, lru_bi, lru_lambda, w_o,
                  ln1_g, ln1_b, w_ffn_gate, w_ffn_up, w_ffn_down, ln2_g, ln2_b):
    row = lambda v: v.reshape(1, -1).astype(F32)
    zeros = lambda r: jnp.zeros((r, D_MODEL), F32)
    w2ext = jnp.concatenate([
        jnp.concatenate([w2_decay, zeros(DECAY_LORA), zeros(DECAY_LORA)], axis=1),
        jnp.concatenate([zeros(ICLR_LORA), a2_iclr, zeros(ICLR_LORA)], axis=1),
        jnp.concatenate([zeros(GATE_LORA), zeros(GATE_LORA), g2_gate], axis=1)], axis=0)

    def gate_blockdiag(w):
        per = LRU_GROUP // LRU_BS
        w4 = w.reshape(LRU_BLOCKS // per, per, LRU_BS, LRU_BS)
        eye = jnp.eye(per, dtype=w.dtype)
        bd = jnp.einsum("gpcd,pq->gpcqd", w4, eye)
        return bd.reshape(LRU_BLOCKS // per, LRU_GROUP, LRU_GROUP).astype(BF16)

    return dict(
        w_rw=w_in[:, :RWKV_COLS].astype(BF16), w_lru=w_in[:, RWKV_COLS:].astype(BF16),
        mu=row(tmix_mu), w2ext=w2ext.astype(BF16), w0=row(w0), a0=row(a0), k_k=row(k_k),
        k_a=row(k_a), r_k=row(r_k), lnx_g=row(lnx_g), lnx_b=row(lnx_b),
        conv_w=conv_w.astype(F32), conv_b=row(conv_b), wa_bd=gate_blockdiag(lru_wa),
        ba=row(lru_ba), wi_bd=gate_blockdiag(lru_wi), bi=row(lru_bi), lam=row(lru_lambda),
        w_o=w_o.astype(BF16), ln1_g=row(ln1_g), ln1_b=row(ln1_b), w_gate=w_ffn_gate.astype(BF16),
        w_up=w_ffn_up.astype(BF16), w_down=w_ffn_down.astype(BF16), ln2_g=row(ln2_g),
        ln2_b=row(ln2_b))


def _pairs_from_heads(s):
    b = s.shape[0]
    return s.reshape(b, N_PAIRS, 2, HEAD, HEAD).transpose(0, 1, 3, 2, 4).reshape(
        b, N_PAIRS, HEAD, PAIR)


def _heads_from_pairs(s):
    b = s.shape[0]
    return s.reshape(b, N_PAIRS, HEAD, 2, HEAD).transpose(0, 1, 3, 2, 4).reshape(
        b, N_HEADS, HEAD, HEAD)


def _prompt_layer(x, wp, tm, tt, tm_post, n_pairs):
    b, t, _ = x.shape
    acts = _rwkv_proj_seq(x, wp, tm)
    ya, s_pairs = _wkv_seq(acts, wp, tt, n_pairs)
    ga, gyb, conv_tail, h_tail = _lru_seq(x, wp, tm)
    y = _post(x.reshape(b * t, D_MODEL), ya.reshape(b * t, D_MODEL), ga.reshape(b * t, D_MODEL),
              gyb.reshape(b * t, D_MODEL), wp, tm_post).reshape(b, t, D_MODEL)
    return (y, x[:, -1], _heads_from_pairs(s_pairs), conv_tail[:, SUBLANES - (CONV_W - 1):],
            h_tail[:, SUBLANES - 1])


def _sample_layer(x, shift_buf, wkv0, conv_buf, h0, wp):
    b, t, _ = x.shape
    n = b * t
    xf = x.reshape(n, D_MODEL)
    xprev = jnp.concatenate([shift_buf[:, None], x[:, :-1]], axis=1).reshape(n, D_MODEL)
    acts = _rwkv_proj_step(xf, xprev, wp)
    ya, s_pairs = _wkv_step(acts, _pairs_from_heads(wkv0), wp, t)
    x_tm = x.transpose(1, 0, 2).reshape(n, D_MODEL)
    conv_tm = conv_buf.transpose(1, 0, 2).reshape((CONV_W - 1) * b, D_LRU)
    ga_tm, gyb_tm, conv_new_tm, h_last = _lru_step(x_tm, conv_tm, h0, wp, t, b)
    to_seq_major = lambda v: v.reshape(t, b, D_MODEL).transpose(1, 0, 2).reshape(n, D_MODEL)
    y = _post(xf, ya, to_seq_major(ga_tm), to_seq_major(gyb_tm), wp, min(n, 256)).reshape(
        b, t, D_MODEL)
    conv_new = conv_new_tm.reshape(CONV_W - 1, b, D_LRU).transpose(1, 0, 2)
    return y, x[:, -1], _heads_from_pairs(s_pairs), conv_new, h_last


def kernel(x_prompt, x_sample, state_shift, state_wkv, state_conv, state_lru, w_in, tmix_mu, w0, w2_decay, a0, a2_iclr, g2_gate, k_k, k_a, r_k, lnx_g, lnx_b, conv_w, conv_b, lru_wa, lru_ba, lru_wi, lru_bi, lru_lambda, w_o, ln1_g, ln1_b, w_ffn_gate, w_ffn_up, w_ffn_down, ln2_g, ln2_b):
    params = (w_in, tmix_mu, w0, w2_decay, a0, a2_iclr, g2_gate, k_k, k_a, r_k, lnx_g, lnx_b,
              conv_w, conv_b, lru_wa, lru_ba, lru_wi, lru_bi, lru_lambda, w_o,
              ln1_g, ln1_b, w_ffn_gate, w_ffn_up, w_ffn_down, ln2_g, ln2_b)
    wp = _prep_weights(*[p[0] for p in params])
    yp, sh_p, wkv_p, conv_p, lru_p = _prompt_layer(x_prompt, wp, 256, 512, 256, 8)
    ys, sh_s, wkv_s, conv_s, lru_s = _sample_layer(
        x_sample, state_shift[0], state_wkv[0], state_conv[0], state_lru[0], wp)
    return (yp, ys, sh_p[None], wkv_p[None], conv_p[None], lru_p[None],
            sh_s[None], wkv_s[None], conv_s[None], lru_s[None])
```

```python
import functools
import math

import jax
import jax.numpy as jnp
from jax import lax
from jax.experimental import pallas as pl
from jax.experimental.pallas import tpu as pltpu

F32 = jnp.float32
BF16 = jnp.bfloat16

D_MODEL = 1024
HEAD = 64
N_HEADS = D_MODEL // HEAD
PAIR = 2 * HEAD
N_PAIRS = N_HEADS // 2
DECAY_LORA = 64
ICLR_LORA = 64
GATE_LORA = 128
LORA = DECAY_LORA + ICLR_LORA + GATE_LORA
RWKV_COLS = 3 * D_MODEL + LORA
GN_EPS = HEAD * 1e-5
D_LRU = D_MODEL
LRU_BLOCKS = 16
LRU_BS = D_LRU // LRU_BLOCKS
LRU_GROUP = 256
CONV_W = 4
LRU_C = 8.0
LRU_COLS = 2 * D_LRU + 2 * D_MODEL
N_IN = RWKV_COLS + LRU_COLS
D_FF = 2816
ALPHA = 2.0 ** 0.25
LN_EPS = 1e-5

SUBLANES = 8
CHUNK = 64
VMEM_LIMIT = 56 * 1024 * 1024


def _softplus(x):
    return jnp.maximum(x, 0.0) + jnp.log1p(jnp.exp(-jnp.abs(x)))


def _sigmoid(x):
    return 0.5 * jnp.tanh(0.5 * x) + 0.5


def _gelu_tanh(x):
    c = math.sqrt(2.0 / math.pi)
    return x * (0.5 * (1.0 + jnp.tanh(c * (x + 0.044715 * (x * x * x)))))


def _layer_norm(x, g, b):
    mu = jnp.mean(x, axis=-1, keepdims=True)
    xc = x - mu
    var = jnp.mean(xc * xc, axis=-1, keepdims=True)
    return xc * lax.rsqrt(var + LN_EPS) * g + b


def _dot(a, b):
    return jnp.dot(a.astype(BF16), b.astype(BF16), preferred_element_type=F32)


def _dot_nt(a, b):
    return lax.dot_general(a.astype(BF16), b.astype(BF16), (((1,), (1,)), ((), ())),
                           preferred_element_type=F32)


def _dot_tn(a, b):
    return lax.dot_general(a.astype(BF16), b.astype(BF16), (((0,), (0,)), ((), ())),
                           preferred_element_type=F32)


def _seg_sum(x, lane_lo):
    s0 = jnp.sum(jnp.where(lane_lo, x, 0.0), axis=-1, keepdims=True)
    s1 = jnp.sum(jnp.where(lane_lo, 0.0, x), axis=-1, keepdims=True)
    return jnp.where(lane_lo, s0, s1)


def _head_sums(x):
    lane_lo = lax.broadcasted_iota(jnp.int32, (x.shape[0], PAIR), 1) < HEAD
    return jnp.concatenate([_seg_sum(x[:, p * PAIR:(p + 1) * PAIR], lane_lo)
                            for p in range(N_PAIRS)], axis=1)


def _rwkv_prep(z, zprev, mu, w2ext, w0, a0, k_k, k_a):
    zm = z + mu * (zprev - z)
    r = zm[:, 0:D_MODEL]
    k = zm[:, D_MODEL:2 * D_MODEL]
    v = zm[:, 2 * D_MODEL:3 * D_MODEL]
    lo = zm[:, 3 * D_MODEL:RWKV_COLS]
    lane = lax.broadcasted_iota(jnp.int32, lo.shape, 1)
    act = jnp.where(lane < DECAY_LORA, jnp.tanh(lo),
                    jnp.where(lane < DECAY_LORA + ICLR_LORA, lo, _sigmoid(lo)))
    lora = _dot(act, w2ext)
    w = -_softplus(-(w0 + lora[:, 0:D_MODEL])) - 0.5
    lw = -jnp.exp(w)
    a = _sigmoid(a0 + lora[:, D_MODEL:2 * D_MODEL])
    g = lora[:, 2 * D_MODEL:3 * D_MODEL]
    kkraw = k * k_k
    kk = kkraw * lax.rsqrt(jnp.maximum(_head_sums(kkraw * kkraw), 1e-24))
    k2 = k * (1.0 + (a - 1.0) * k_a)
    return r, lw, k2, v, -kk, kk * a, g


def _rwkv_proj_seq_kernel(x_ref, w_ref, mu_ref, w2_ref, w0_ref, a0_ref, kk_ref, ka_ref,
                          r_ref, lw_ref, k_ref, v_ref, kkraw_ref, a_ref, g_ref, hist_ref):
    @pl.when(pl.program_id(1) == 0)
    def _():
        hist_ref[...] = jnp.zeros_like(hist_ref)

    z = _dot(x_ref[0], w_ref[...])
    tm = z.shape[0]
    rolled = pltpu.roll(z, 1, 0)
    row = lax.broadcasted_iota(jnp.int32, (SUBLANES, 1), 0)
    first = jnp.where(row == 0, hist_ref[SUBLANES - 1:SUBLANES, :], rolled[0:SUBLANES])
    zprev = jnp.concatenate([first, rolled[SUBLANES:]], axis=0)
    hist_ref[...] = z[tm - SUBLANES:tm]
    outs = _rwkv_prep(z, zprev, mu_ref[...], w2_ref[...], w0_ref[...], a0_ref[...],
                      kk_ref[...], ka_ref[...])
    for o_ref, o in zip((r_ref, lw_ref, k_ref, v_ref, kkraw_ref, a_ref, g_ref), outs):
        o_ref[0] = o


def _rwkv_proj_step_kernel(x_ref, xp_ref, w_ref, mu_ref, w2_ref, w0_ref, a0_ref, kk_ref, ka_ref,
                           r_ref, lw_ref, k_ref, v_ref, kkraw_ref, a_ref, g_ref):
    z = _dot(x_ref[...], w_ref[...])
    zprev = _dot(xp_ref[...], w_ref[...])
    outs = _rwkv_prep(z, zprev, mu_ref[...], w2_ref[...], w0_ref[...], a0_ref[...],
                      kk_ref[...], ka_ref[...])
    for o_ref, o in zip((r_ref, lw_ref, k_ref, v_ref, kkraw_ref, a_ref, g_ref), outs):
        o_ref[...] = o


def _const_spec(shape):
    nd = len(shape)
    return pl.BlockSpec(shape, lambda *_: (0,) * nd, pipeline_mode=pl.Buffered(1))


def _rwkv_proj_seq(x, wp, tm):
    b, t, _ = x.shape
    row_spec = pl.BlockSpec((1, tm, D_MODEL), lambda i, j: (i, j, 0))
    vec = _const_spec((1, D_MODEL))
    return pl.pallas_call(
        _rwkv_proj_seq_kernel,
        grid=(b, t // tm),
        in_specs=[row_spec, _const_spec((D_MODEL, RWKV_COLS)), _const_spec((1, RWKV_COLS)),
                  _const_spec((LORA, 3 * D_MODEL)), vec, vec, vec, vec],
        out_specs=[row_spec] * 7,
        out_shape=[jax.ShapeDtypeStruct((b, t, D_MODEL), F32)] * 7,
        scratch_shapes=[pltpu.VMEM((SUBLANES, RWKV_COLS), F32)],
        compiler_params=pltpu.CompilerParams(
            dimension_semantics=("arbitrary", "arbitrary"), vmem_limit_bytes=VMEM_LIMIT),
        name="rwkv_proj_seq",
    )(x, wp["w_in"], wp["mu"], wp["w2ext"], wp["w0"], wp["a0"], wp["k_k"], wp["k_a"])


def _rwkv_proj_step(x, xprev, wp):
    n = x.shape[0]
    full = pl.BlockSpec((n, D_MODEL), lambda i: (0, 0))
    vec = _const_spec((1, D_MODEL))
    return pl.pallas_call(
        _rwkv_proj_step_kernel,
        grid=(1,),
        in_specs=[full, full, _const_spec((D_MODEL, RWKV_COLS)), _const_spec((1, RWKV_COLS)),
                  _const_spec((LORA, 3 * D_MODEL)), vec, vec, vec, vec],
        out_specs=[full] * 7,
        out_shape=[jax.ShapeDtypeStruct((n, D_MODEL), F32)] * 7,
        compiler_params=pltpu.CompilerParams(
            dimension_semantics=("arbitrary",), vmem_limit_bytes=VMEM_LIMIT),
        name="rwkv_proj_step",
    )(x, xprev, wp["w_in"], wp["mu"], wp["w2ext"], wp["w0"], wp["a0"], wp["k_k"], wp["k_a"])


def _lru_gates(u, wa_ref, ba, wi_ref, bi, sp):
    ub = u.astype(BF16)
    ra, ia = [], []
    for q in range(D_LRU // LRU_GROUP):
        uq = ub[:, q * LRU_GROUP:(q + 1) * LRU_GROUP]
        ra.append(jnp.dot(uq, wa_ref[q], preferred_element_type=F32))
        ia.append(jnp.dot(uq, wi_ref[q], preferred_element_type=F32))
    rg = _sigmoid(jnp.concatenate(ra, axis=1) + ba)
    ig = _sigmoid(jnp.concatenate(ia, axis=1) + bi)
    log_a = -LRU_C * rg * sp
    a = jnp.exp(log_a)
    m2 = -jnp.tanh(log_a) * (a * a + 1.0)
    mult = jnp.where(m2 > 0.0, m2 * lax.rsqrt(m2), 0.0)
    return a, mult, ig


def _lru_seq_kernel(x_ref, w_ref, cw_ref, cb_ref, wa_ref, ba_ref, wi_ref, bi_ref, lam_ref,
                    ga_ref, gyb_ref, conv_ref, hlast_ref,
                    xhist_ref, hc_ref, a_s, bx_s, h_s):
    t = pl.program_id(1)

    @pl.when(t == 0)
    def _():
        xhist_ref[...] = jnp.zeros_like(xhist_ref)
        hc_ref[...] = jnp.zeros_like(hc_ref)

    z = _dot(x_ref[0], w_ref[:, RWKV_COLS:])
    tm = z.shape[0]
    xb = z[:, 0:D_LRU]
    gb = z[:, D_LRU:2 * D_LRU]
    zga = z[:, 2 * D_LRU:2 * D_LRU + D_MODEL]
    zgb = z[:, 2 * D_LRU + D_MODEL:]
    row8 = lax.broadcasted_iota(jnp.int32, (SUBLANES, 1), 0)
    hist = xhist_ref[...]

    def shifted(k):
        rolled = pltpu.roll(xb, k, 0)
        first = jnp.where(row8 < k, pltpu.roll(hist, k, 0), rolled[0:SUBLANES])
        return jnp.concatenate([first, rolled[SUBLANES:]], axis=0)

    cw = cw_ref[...]
    u = (cb_ref[...] + cw[0:1] * shifted(3) + cw[1:2] * shifted(2) + cw[2:3] * shifted(1)
         + cw[3:4] * xb)
    last8 = xb[tm - SUBLANES:tm]
    xhist_ref[...] = last8
    conv_ref[0] = last8

    sp = _softplus(-lam_ref[...])
    a, mult, ig = _lru_gates(u, wa_ref, ba_ref[...], wi_ref, bi_ref[...], sp)
    row = lax.broadcasted_iota(jnp.int32, (tm, 1), 0)
    mult = jnp.where(jnp.logical_and(row == 0, t == 0), 1.0, mult)
    a_s[...] = a
    bx_s[...] = mult * ig * u

    def blk(i, hc):
        o = pl.multiple_of(i * SUBLANES, SUBLANES)
        av = a_s[pl.ds(o, SUBLANES), :]
        bv = bx_s[pl.ds(o, SUBLANES), :]
        for d in (1, 2, 4):
            a_sh = jnp.where(row8 < d, 1.0, pltpu.roll(av, d, 0))
            b_sh = jnp.where(row8 < d, 0.0, pltpu.roll(bv, d, 0))
            bv = av * b_sh + bv
            av = av * a_sh
        h = bv + av * hc
        h_s[pl.ds(o, SUBLANES), :] = h
        return h[SUBLANES - 1:SUBLANES, :]

    hc = lax.fori_loop(0, tm // SUBLANES, blk, hc_ref[0:1, :])
    hc_ref[0:1, :] = hc
    h = h_s[...]
    hlast_ref[0] = h[tm - SUBLANES:tm]
    ga_ref[0] = _sigmoid(zga)
    gyb_ref[0] = _sigmoid(zgb) * (h * _gelu_tanh(gb))


def _lru_step_kernel(x_ref, conv0_ref, h0_ref, w_ref, cw_ref, cb_ref, wa_ref, ba_ref, wi_ref,
                     bi_ref, lam_ref, ga_ref, gyb_ref, conv_ref, hlast_ref, *, steps, nseq):
    z = _dot(x_ref[...], w_ref[:, RWKV_COLS:])
    xb = z[:, 0:D_LRU]
    gb = z[:, D_LRU:2 * D_LRU]
    zga = z[:, 2 * D_LRU:2 * D_LRU + D_MODEL]
    zgb = z[:, 2 * D_LRU + D_MODEL:]
    n = steps * nseq
    hist = (CONV_W - 1) * nseq
    xext = jnp.concatenate([conv0_ref[...], xb], axis=0)

    def shifted(k):
        return xext[hist - k * nseq:hist - k * nseq + n]

    cw = cw_ref[...]
    u = (cb_ref[...] + cw[0:1] * shifted(3) + cw[1:2] * shifted(2) + cw[2:3] * shifted(1)
         + cw[3:4] * xb)
    conv_ref[...] = xext[n:n + hist]
    sp = _softplus(-lam_ref[...])
    a, mult, ig = _lru_gates(u, wa_ref, ba_ref[...], wi_ref, bi_ref[...], sp)
    bx = mult * ig * u
    h = h0_ref[...]
    hs = []
    for s in range(steps):
        h = a[s * nseq:(s + 1) * nseq] * h + bx[s * nseq:(s + 1) * nseq]
        hs.append(h)
    hlast_ref[...] = h
    hall = jnp.concatenate(hs, axis=0)
    ga_ref[...] = _sigmoid(zga)
    gyb_ref[...] = _sigmoid(zgb) * (hall * _gelu_tanh(gb))


def _lru_weight_specs():
    vec = _const_spec((1, D_LRU))
    gate_w = _const_spec((D_LRU // LRU_GROUP, LRU_GROUP, LRU_GROUP))
    return [_const_spec((D_MODEL, N_IN)), _const_spec((CONV_W, D_LRU)), vec,
            gate_w, vec, gate_w, vec, vec]


def _lru_weights(wp):
    return (wp["w_in"], wp["conv_w"], wp["conv_b"], wp["wa_bd"], wp["ba"], wp["wi_bd"],
            wp["bi"], wp["lam"])


def _lru_seq(x, wp, tm):
    b, t, _ = x.shape
    row_spec = pl.BlockSpec((1, tm, D_MODEL), lambda i, j: (i, j, 0))
    tail_spec = pl.BlockSpec((1, SUBLANES, D_LRU), lambda i, j: (i, 0, 0))
    return pl.pallas_call(
        _lru_seq_kernel,
        grid=(b, t // tm),
        in_specs=[row_spec] + _lru_weight_specs(),
        out_specs=[row_spec, row_spec, tail_spec, tail_spec],
        out_shape=[jax.ShapeDtypeStruct((b, t, D_MODEL), F32)] * 2
        + [jax.ShapeDtypeStruct((b, SUBLANES, D_LRU), F32)] * 2,
        scratch_shapes=[pltpu.VMEM((SUBLANES, D_LRU), F32), pltpu.VMEM((SUBLANES, D_LRU), F32),
                        pltpu.VMEM((tm, D_LRU), F32), pltpu.VMEM((tm, D_LRU), F32),
                        pltpu.VMEM((tm, D_LRU), F32)],
        compiler_params=pltpu.CompilerParams(
            dimension_semantics=("arbitrary", "arbitrary"), vmem_limit_bytes=VMEM_LIMIT),
        name="lru_seq",
    )(x, *_lru_weights(wp))


def _lru_step(x_tm, conv_tm, h0, wp, steps, nseq):
    n = steps * nseq
    hist = (CONV_W - 1) * nseq

    def full(r):
        return pl.BlockSpec((r, D_MODEL), lambda i: (0, 0))

    return pl.pallas_call(
        functools.partial(_lru_step_kernel, steps=steps, nseq=nseq),
        grid=(1,),
        in_specs=[full(n), full(hist), full(nseq)] + _lru_weight_specs(),
        out_specs=[full(n), full(n), full(hist), full(nseq)],
        out_shape=[jax.ShapeDtypeStruct((n, D_MODEL), F32)] * 2
        + [jax.ShapeDtypeStruct((hist, D_LRU), F32), jax.ShapeDtypeStruct((nseq, D_LRU), F32)],
        compiler_params=pltpu.CompilerParams(
            dimension_semantics=("arbitrary",), vmem_limit_bytes=VMEM_LIMIT),
        name="lru_step",
    )(x_tm, conv_tm, h0, *_lru_weights(wp))


def _wkv_consts(nseq):
    c = CHUNK
    seq_len = c // nseq
    lane = lax.broadcasted_iota(jnp.int32, (c, PAIR), 1)
    row = lax.broadcasted_iota(jnp.int32, (c, PAIR), 0)
    lane_lo = lane < HEAD
    col = lane % c
    same = (row // seq_len) == (col // seq_len)
    strict = jnp.logical_and(same, col < row)
    incl = jnp.logical_and(same, col <= row)
    eye = jnp.where(col == row, 1.0, 0.0).astype(F32)
    r3 = lax.broadcasted_iota(jnp.int32, (c, 3 * c), 0)
    c3 = lax.broadcasted_iota(jnp.int32, (c, 3 * c), 1) % c
    same3 = (r3 // seq_len) == (c3 // seq_len)
    tri3 = jnp.where(jnp.logical_and(same3, c3 <= r3), 1.0, 0.0).astype(BF16)
    tot3 = jnp.where(same3, 1.0, 0.0).astype(BF16)
    rp = lax.broadcasted_iota(jnp.int32, (PAIR, PAIR), 0)
    cp = lax.broadcasted_iota(jnp.int32, (PAIR, PAIR), 1)
    blockdiag = (rp // HEAD) == (cp // HEAD)
    return dict(lane_lo=lane_lo, tri3=tri3, tot3=tot3, strict=strict, incl=incl, eye=eye,
                blockdiag=blockdiag, seq_len=seq_len)


def _split3(x):
    hi = x.astype(BF16)
    rest = x - hi.astype(F32)
    mid = rest.astype(BF16)
    lo = (rest - mid.astype(F32)).astype(BF16)
    return jnp.concatenate([hi, mid, lo], axis=0)


def _split_heads(x, lane_lo):
    zero = jnp.zeros_like(x)
    return jnp.concatenate([jnp.where(lane_lo, x, zero), jnp.where(lane_lo, zero, x)], axis=0)


def _run_interleaved(stage_gens):
    results = [None] * len(stage_gens)
    live = list(range(len(stage_gens)))
    while live:
        still = []
        for idx in live:
            try:
                next(stage_gens[idx])
                still.append(idx)
            except StopIteration as done:
                results[idx] = done.value
        live = still
    return results


def _wkv_chunk(r, lw, k, v, al, be, states, cn, nseq):
    c = CHUNK
    lane_lo = cn["lane_lo"]
    lw3 = _split3(lw)
    cum = jnp.dot(cn["tri3"], lw3, preferred_element_type=F32)
    if nseq == 1:
        end = cum[c - 1:c, :]
    else:
        end = jnp.dot(cn["tot3"], lw3, preferred_element_type=F32)
    yield
    e_cum = jnp.exp(cum)
    e_neg = jnp.exp(-cum)
    e_end = jnp.exp(end)
    rt = r * e_cum
    at = al * jnp.exp(cum - lw)
    bt = be * e_neg
    kt = k * e_neg
    lhs2 = jnp.concatenate([at, rt], axis=0).astype(BF16)
    keys = jnp.concatenate([_split_heads(bt, lane_lo), _split_heads(kt, lane_lo)], axis=0)
    gram = _dot_nt(lhs2, keys)
    yield
    a_ab = jnp.where(cn["strict"], gram[0:c, 0:PAIR], 0.0)
    a_ak = jnp.where(cn["strict"], gram[0:c, PAIR:], 0.0)
    a_rb = jnp.where(cn["incl"], gram[c:, 0:PAIR], 0.0)
    a_rk = jnp.where(cn["incl"], gram[c:, PAIR:], 0.0)
    inv = cn["eye"] + a_ab
    n_iter = int(math.log2(cn["seq_len"])) - 1
    pw = _dot(a_ab, _split_heads(a_ab, lane_lo))
    yield
    for it in range(n_iter):
        pw_heads = _split_heads(pw, lane_lo)
        if it < n_iter - 1:
            both = _dot(jnp.concatenate([inv, pw], axis=0), pw_heads)
            inv = inv + both[0:c]
            pw = both[c:]
        else:
            inv = inv + _dot(inv, pw_heads)
        yield
    av = _dot(jnp.concatenate([a_ak, a_rk], axis=0), _split_heads(v, lane_lo))
    if nseq == 1:
        from_state = _dot_nt(lhs2, states[0])
    else:
        seq_of_row = (lax.broadcasted_iota(jnp.int32, (2 * c, 1), 0) % c) // cn["seq_len"]
        from_state = jnp.zeros((2 * c, PAIR), F32)
        for s in range(nseq):
            from_state = jnp.where(seq_of_row == s, _dot_nt(lhs2, states[s]), from_state)
    yield
    u = _dot(inv, _split_heads(from_state[0:c] + av[0:c], lane_lo))
    yield
    y = from_state[c:] + av[c:] + _dot(a_rb, _split_heads(u, lane_lo))
    yield
    uv = jnp.concatenate([u, v], axis=0)
    bk = jnp.concatenate([bt * e_end, kt * e_end], axis=0)
    new_states = []
    if nseq == 1:
        inc = _dot_tn(uv, bk)
        new_states.append(states[0] * e_end + jnp.where(cn["blockdiag"], inc, 0.0))
    else:
        seq_of_uv = (lax.broadcasted_iota(jnp.int32, (2 * c, 1), 0) % c) // cn["seq_len"]
        for s in range(nseq):
            inc = _dot_tn(jnp.where(seq_of_uv == s, uv, 0.0), bk)
            decay = e_end[s * cn["seq_len"]:s * cn["seq_len"] + 1, :]
            new_states.append(states[s] * decay + jnp.where(cn["blockdiag"], inc, 0.0))
    return y, new_states


def _wkv_readout(y, r, k, v, g, rk, lnx_g, lnx_b, lane_lo):
    mean = _seg_sum(y, lane_lo) * (1.0 / HEAD)
    bonus = _seg_sum(r * k * rk, lane_lo) * v
    yield
    yc = y - mean
    var = _seg_sum(yc * yc, lane_lo) * (1.0 / HEAD)
    yield
    yn = yc * lax.rsqrt(var + GN_EPS) * lnx_g + lnx_b
    return (yn + bonus) * g


def _to_blockdiag(s2, lane_lo_half):
    zero = jnp.zeros_like(s2)
    return jnp.concatenate([jnp.where(lane_lo_half, s2, zero), jnp.where(lane_lo_half, zero, s2)],
                           axis=0)


def _wkv_seq_kernel(r_ref, lw_ref, k_ref, v_ref, al_ref, be_ref, g_ref, rk_ref, lg_ref, lb_ref,
                    y_ref, s_out_ref, s_scr, yraw_scr):
    t = pl.program_id(2)
    cn = _wkv_consts(1)
    n_pairs = s_scr.shape[0]

    @pl.when(t == 0)
    def _():
        s_scr[...] = jnp.zeros_like(s_scr)
        yraw_scr[...] = jnp.zeros_like(yraw_scr)

    n_chunks = r_ref.shape[1] // CHUNK

    def readout_stages(p, rows):
        ln = pl.ds(p * PAIR, PAIR)
        out = yield from _wkv_readout(
            yraw_scr[:, ln], r_ref[0, rows, ln], k_ref[0, rows, ln], v_ref[0, rows, ln],
            g_ref[0, rows, ln], rk_ref[:, ln], lg_ref[:, ln], lb_ref[:, ln], cn["lane_lo"])
        y_ref[0, rows, ln] = out

    def body(i, states):
        sl = pl.ds(pl.multiple_of(i * CHUNK, CHUNK), CHUNK)
        prev = pl.ds(pl.multiple_of(jnp.maximum(i - 1, 0) * CHUNK, CHUNK), CHUNK)

        def pair_stages(p):
            ln = pl.ds(p * PAIR, PAIR)
            y, (state,) = yield from _wkv_chunk(
                r_ref[0, sl, ln], lw_ref[0, sl, ln], k_ref[0, sl, ln], v_ref[0, sl, ln],
                al_ref[0, sl, ln], be_ref[0, sl, ln], [states[p]], cn, 1)
            yraw_scr[:, ln] = y
            return state

        results = _run_interleaved([readout_stages(p, prev) for p in range(n_pairs)]
                                   + [pair_stages(p) for p in range(n_pairs)])
        return tuple(results[n_pairs:])

    states = lax.fori_loop(0, n_chunks, body, tuple(s_scr[p] for p in range(n_pairs)))
    last = pl.ds((n_chunks - 1) * CHUNK, CHUNK)
    _run_interleaved([readout_stages(p, last) for p in range(n_pairs)])
    for p in range(n_pairs):
        s_scr[p] = states[p]
        s_out_ref[0, p] = states[p][0:HEAD] + states[p][HEAD:]


def _wkv_step_kernel(r_ref, lw_ref, k_ref, v_ref, al_ref, be_ref, g_ref, rk_ref, lg_ref, lb_ref,
                     s_in_ref, y_ref, s_out_ref, *, nseq):
    cn = _wkv_consts(nseq)
    half = lax.broadcasted_iota(jnp.int32, (HEAD, PAIR), 1) < HEAD

    def pair_stages(p):
        ln = pl.ds(p * PAIR, PAIR)
        states = [_to_blockdiag(s_in_ref[s, p], half) for s in range(nseq)]
        r, k, v = r_ref[:, ln], k_ref[:, ln], v_ref[:, ln]
        y, states = yield from _wkv_chunk(r, lw_ref[:, ln], k, v, al_ref[:, ln], be_ref[:, ln],
                                          states, cn, nseq)
        ya = yield from _wkv_readout(y, r, k, v, g_ref[:, ln], rk_ref[:, ln], lg_ref[:, ln],
                                     lb_ref[:, ln], cn["lane_lo"])
        y_ref[:, ln] = ya
        for s in range(nseq):
            s_out_ref[s, p] = states[s][0:HEAD] + states[s][HEAD:]

    _run_interleaved([pair_stages(p) for p in range(s_in_ref.shape[1])])


def _wkv_seq(acts, wp, tt, n_pairs):
    b, t, _ = acts[0].shape
    width = n_pairs * PAIR
    row_spec = pl.BlockSpec((1, tt, width), lambda i, p, j: (i, j, p))
    vec = pl.BlockSpec((1, width), lambda i, p, j: (0, p))
    return pl.pallas_call(
        _wkv_seq_kernel,
        grid=(b, N_PAIRS // n_pairs, t // tt),
        in_specs=[row_spec] * 7 + [vec] * 3,
        out_specs=[row_spec,
                   pl.BlockSpec((1, n_pairs, HEAD, PAIR), lambda i, p, j: (i, p, 0, 0))],
        out_shape=[jax.ShapeDtypeStruct((b, t, D_MODEL), F32),
                   jax.ShapeDtypeStruct((b, N_PAIRS, HEAD, PAIR), F32)],
        scratch_shapes=[pltpu.VMEM((n_pairs, PAIR, PAIR), F32), pltpu.VMEM((CHUNK, width), F32)],
        compiler_params=pltpu.CompilerParams(
            dimension_semantics=("arbitrary", "arbitrary", "arbitrary"),
            vmem_limit_bytes=VMEM_LIMIT),
        name="wkv_seq",
    )(*acts, wp["r_k"], wp["lnx_g"], wp["lnx_b"])


def _wkv_step(acts, s_pairs, wp, steps, n_pairs):
    n = acts[0].shape[0]
    nseq = CHUNK // steps
    width = n_pairs * PAIR
    row_spec = pl.BlockSpec((CHUNK, width), lambda i, p: (i, p))
    vec = pl.BlockSpec((1, width), lambda i, p: (0, p))
    st_spec = pl.BlockSpec((nseq, n_pairs, HEAD, PAIR), lambda i, p: (i, p, 0, 0))
    return pl.pallas_call(
        functools.partial(_wkv_step_kernel, nseq=nseq),
        grid=(n // CHUNK, N_PAIRS // n_pairs),
        in_specs=[row_spec] * 7 + [vec] * 3 + [st_spec],
        out_specs=[row_spec, st_spec],
        out_shape=[jax.ShapeDtypeStruct((n, D_MODEL), F32),
                   jax.ShapeDtypeStruct(s_pairs.shape, F32)],
        compiler_params=pltpu.CompilerParams(
            dimension_semantics=("arbitrary", "arbitrary"), vmem_limit_bytes=VMEM_LIMIT),
        name="wkv_step",
    )(*acts, wp["r_k"], wp["lnx_g"], wp["lnx_b"], s_pairs)


def _post_kernel(x_ref, ya_ref, ga_ref, gyb_ref, wo_ref, l1g_ref, l1b_ref, wg_ref, wu_ref, wd_ref,
                 l2g_ref, l2b_ref, y_ref):
    x = x_ref[...]
    merged = ga_ref[...] * ya_ref[...] + gyb_ref[...]
    mix = _dot(merged, wo_ref[...])
    h1 = _layer_norm(ALPHA * x + mix, l1g_ref[...], l1b_ref[...])
    h1b = h1.astype(BF16)
    gate = jnp.dot(h1b, wg_ref[...], preferred_element_type=F32)
    up = jnp.dot(h1b, wu_ref[...], preferred_element_type=F32)
    act = (gate * _sigmoid(gate)) * up
    ffn = _dot(act, wd_ref[...])
    y_ref[...] = _layer_norm(ALPHA * h1 + ffn, l2g_ref[...], l2b_ref[...])


def _post(x, ya, ga, gyb, wp, tm):
    n = x.shape[0]
    row_spec = pl.BlockSpec((tm, D_MODEL), lambda i: (i, 0))
    vec = _const_spec((1, D_MODEL))
    return pl.pallas_call(
        _post_kernel,
        grid=(n // tm,),
        in_specs=[row_spec] * 4 + [_const_spec((D_MODEL, D_MODEL)), vec, vec,
                                   _const_spec((D_MODEL, D_FF)), _const_spec((D_MODEL, D_FF)),
                                   _const_spec((D_FF, D_MODEL)), vec, vec],
        out_specs=row_spec,
        out_shape=jax.ShapeDtypeStruct((n, D_MODEL), F32),
        compiler_params=pltpu.CompilerParams(
            dimension_semantics=("arbitrary",), vmem_limit_bytes=VMEM_LIMIT),
        name="post",
    )(x, ya, ga, gyb, wp["w_o"], wp["ln1_g"], wp["ln1_b"], wp["w_gate"], wp["w_up"], wp["w_down"],
      wp["ln2_g"], wp["ln2_b"])


def _prep_weights(w_in, tmix_mu, w0, w2_decay, a0, a2_iclr, g2_gate, k_k, k_a, r_k, lnx_g, lnx_b,
                  conv_w, conv_b, lru_wa, lru_ba, lru_wi, lru_bi, lru_lambda, w_o,
                  ln1_g, ln1_b, w_ffn_gate, w_ffn_up, w_ffn_down, ln2_g, ln2_b):
    row = lambda v: v.reshape(1, -1).astype(F32)
    zeros = lambda r: jnp.zeros((r, D_MODEL), F32)
    w2ext = jnp.concatenate([
        jnp.concatenate([w2_decay, zeros(DECAY_LORA), zeros(DECAY_LORA)], axis=1),
        jnp.concatenate([zeros(ICLR_LORA), a2_iclr, zeros(ICLR_LORA)], axis=1),
        jnp.concatenate([zeros(GATE_LORA), zeros(GATE_LORA), g2_gate], axis=1)], axis=0)

    def gate_blockdiag(w):
        per = LRU_GROUP // LRU_BS
        w4 = w.reshape(LRU_BLOCKS // per, per, LRU_BS, LRU_BS)
        eye = jnp.eye(per, dtype=w.dtype)
        bd = jnp.einsum("gpcd,pq->gpcqd", w4, eye)
        return bd.reshape(LRU_BLOCKS // per, LRU_GROUP, LRU_GROUP).astype(BF16)

    return dict(
        w_in=w_in.astype(BF16),
        mu=row(tmix_mu), w2ext=w2ext.astype(BF16), w0=row(w0), a0=row(a0), k_k=row(k_k),
        k_a=row(k_a), r_k=row(r_k), lnx_g=row(lnx_g), lnx_b=row(lnx_b),
        conv_w=conv_w.astype(F32), conv_b=row(conv_b), wa_bd=gate_blockdiag(lru_wa),
        ba=row(lru_ba), wi_bd=gate_blockdiag(lru_wi), bi=row(lru_bi), lam=row(lru_lambda),
        w_o=w_o.astype(BF16), ln1_g=row(ln1_g), ln1_b=row(ln1_b), w_gate=w_ffn_gate.astype(BF16),
        w_up=w_ffn_up.astype(BF16), w_down=w_ffn_down.astype(BF16), ln2_g=row(ln2_g),
        ln2_b=row(ln2_b))


def _pairs_from_heads(s):
    b = s.shape[0]
    return s.reshape(b, N_PAIRS, 2, HEAD, HEAD).transpose(0, 1, 3, 2, 4).reshape(
        b, N_PAIRS, HEAD, PAIR)


def _heads_from_pairs(s):
    b = s.shape[0]
    return s.reshape(b, N_PAIRS, HEAD, 2, HEAD).transpose(0, 1, 3, 2, 4).reshape(
        b, N_HEADS, HEAD, HEAD)


def _prompt_layer(x, wp, tm, tt, tm_post, n_pairs):
    b, t, _ = x.shape
    acts = _rwkv_proj_seq(x, wp, tm)
    ya, s_pairs = _wkv_seq(acts, wp, tt, n_pairs)
    ga, gyb, conv_tail, h_tail = _lru_seq(x, wp, tm)
    y = _post(x.reshape(b * t, D_MODEL), ya.reshape(b * t, D_MODEL), ga.reshape(b * t, D_MODEL),
              gyb.reshape(b * t, D_MODEL), wp, tm_post).reshape(b, t, D_MODEL)
    return (y, x[:, -1], _heads_from_pairs(s_pairs), conv_tail[:, SUBLANES - (CONV_W - 1):],
            h_tail[:, SUBLANES - 1])


def _sample_layer(x, shift_buf, wkv0, conv_buf, h0, wp):
    b, t, _ = x.shape
    n = b * t
    xf = x.reshape(n, D_MODEL)
    xprev = jnp.concatenate([shift_buf[:, None], x[:, :-1]], axis=1).reshape(n, D_MODEL)
    acts = _rwkv_proj_step(xf, xprev, wp)
    ya, s_pairs = _wkv_step(acts, _pairs_from_heads(wkv0), wp, t, 4)
    x_tm = x.transpose(1, 0, 2).reshape(n, D_MODEL)
    conv_tm = conv_buf.transpose(1, 0, 2).reshape((CONV_W - 1) * b, D_LRU)
    ga_tm, gyb_tm, conv_new_tm, h_last = _lru_step(x_tm, conv_tm, h0, wp, t, b)
    to_seq_major = lambda v: v.reshape(t, b, D_MODEL).transpose(1, 0, 2).reshape(n, D_MODEL)
    y = _post(xf, ya, to_seq_major(ga_tm), to_seq_major(gyb_tm), wp, min(n, 256)).reshape(
        b, t, D_MODEL)
    conv_new = conv_new_tm.reshape(CONV_W - 1, b, D_LRU).transpose(1, 0, 2)
    return y, x[:, -1], _heads_from_pairs(s_pairs), conv_new, h_last


def kernel(x_prompt, x_sample, state_shift, state_wkv, state_conv, state_lru, w_in, tmix_mu, w0, w2_decay, a0, a2_iclr, g2_gate, k_k, k_a, r_k, lnx_g, lnx_b, conv_w, conv_b, lru_wa, lru_ba, lru_wi, lru_bi, lru_lambda, w_o, ln1_g, ln1_b, w_ffn_gate, w_ffn_up, w_ffn_down, ln2_g, ln2_b):
    params = (w_in, tmix_mu, w0, w2_decay, a0, a2_iclr, g2_gate, k_k, k_a, r_k, lnx_g, lnx_b,
              conv_w, conv_b, lru_wa, lru_ba, lru_wi, lru_bi, lru_lambda, w_o,
              ln1_g, ln1_b, w_ffn_gate, w_ffn_up, w_ffn_down, ln2_g, ln2_b)
    wp = _prep_weights(*[p[0] for p in params])
    yp, sh_p, wkv_p, conv_p, lru_p = _prompt_layer(x_prompt, wp, 256, 512, 256, 8)
    ys, sh_s, wkv_s, conv_s, lru_s = _sample_layer(
        x_sample, state_shift[0], state_wkv[0], state_conv[0], state_lru[0], wp)
    return (yp, ys, sh_p[None], wkv_p[None], conv_p[None], lru_p[None],
            sh_s[None], wkv_s[None], conv_s[None], lru_s[None])
```

```python
import functools
import math

import jax
import jax.numpy as jnp
from jax import lax
from jax.experimental import pallas as pl
from jax.experimental.pallas import tpu as pltpu

F32 = jnp.float32
BF16 = jnp.bfloat16

D_MODEL = 1024
HEAD = 64
N_HEADS = D_MODEL // HEAD
PAIR = 2 * HEAD
N_PAIRS = N_HEADS // 2
DECAY_LORA = 64
ICLR_LORA = 64
GATE_LORA = 128
LORA = DECAY_LORA + ICLR_LORA + GATE_LORA
RWKV_COLS = 3 * D_MODEL + LORA
GN_EPS = HEAD * 1e-5
D_LRU = D_MODEL
LRU_BLOCKS = 16
LRU_BS = D_LRU // LRU_BLOCKS
LRU_GROUP = 256
CONV_W = 4
LRU_C = 8.0
LRU_COLS = 2 * D_LRU + 2 * D_MODEL
N_IN = RWKV_COLS + LRU_COLS
D_FF = 2816
ALPHA = 2.0 ** 0.25
LN_EPS = 1e-5

SUBLANES = 8
CHUNK = 64
CHUNKS_IN_FLIGHT = 2
VMEM_LIMIT = 56 * 1024 * 1024


def _softplus(x):
    return jnp.maximum(x, 0.0) + jnp.log1p(jnp.exp(-jnp.abs(x)))


def _sigmoid(x):
    return 0.5 * jnp.tanh(0.5 * x) + 0.5


def _gelu_tanh(x):
    c = math.sqrt(2.0 / math.pi)
    return x * (0.5 * (1.0 + jnp.tanh(c * (x + 0.044715 * (x * x * x)))))


def _layer_norm(x, g, b):
    mu = jnp.mean(x, axis=-1, keepdims=True)
    xc = x - mu
    var = jnp.mean(xc * xc, axis=-1, keepdims=True)
    return xc * lax.rsqrt(var + LN_EPS) * g + b


def _dot(a, b):
    return jnp.dot(a.astype(BF16), b.astype(BF16), preferred_element_type=F32)


def _dot_nt(a, b):
    return lax.dot_general(a.astype(BF16), b.astype(BF16), (((1,), (1,)), ((), ())),
                           preferred_element_type=F32)


def _dot_tn(a, b):
    return lax.dot_general(a.astype(BF16), b.astype(BF16), (((0,), (0,)), ((), ())),
                           preferred_element_type=F32)


def _seg_sum(x, lane_lo):
    s0 = jnp.sum(jnp.where(lane_lo, x, 0.0), axis=-1, keepdims=True)
    s1 = jnp.sum(jnp.where(lane_lo, 0.0, x), axis=-1, keepdims=True)
    return jnp.where(lane_lo, s0, s1)


def _head_sums(x):
    lane_lo = lax.broadcasted_iota(jnp.int32, (x.shape[0], PAIR), 1) < HEAD
    return jnp.concatenate([_seg_sum(x[:, p * PAIR:(p + 1) * PAIR], lane_lo)
                            for p in range(N_PAIRS)], axis=1)


def _rwkv_prep(z, zprev, mu, w2ext, w0, a0, k_k, k_a):
    zm = z + mu * (zprev - z)
    r = zm[:, 0:D_MODEL]
    k = zm[:, D_MODEL:2 * D_MODEL]
    v = zm[:, 2 * D_MODEL:3 * D_MODEL]
    lo = zm[:, 3 * D_MODEL:RWKV_COLS]
    lane = lax.broadcasted_iota(jnp.int32, lo.shape, 1)
    act = jnp.where(lane < DECAY_LORA, jnp.tanh(lo),
                    jnp.where(lane < DECAY_LORA + ICLR_LORA, lo, _sigmoid(lo)))
    lora = _dot(act, w2ext)
    w = -_softplus(-(w0 + lora[:, 0:D_MODEL])) - 0.5
    lw = -jnp.exp(w)
    a = _sigmoid(a0 + lora[:, D_MODEL:2 * D_MODEL])
    g = lora[:, 2 * D_MODEL:3 * D_MODEL]
    kkraw = k * k_k
    kk = kkraw * lax.rsqrt(jnp.maximum(_head_sums(kkraw * kkraw), 1e-24))
    k2 = k * (1.0 + (a - 1.0) * k_a)
    return r, lw, k2, v, -kk, kk * a, g


def _rwkv_proj_seq_kernel(x_ref, w_ref, mu_ref, w2_ref, w0_ref, a0_ref, kk_ref, ka_ref,
                          r_ref, lw_ref, k_ref, v_ref, kkraw_ref, a_ref, g_ref, hist_ref):
    @pl.when(pl.program_id(1) == 0)
    def _():
        hist_ref[...] = jnp.zeros_like(hist_ref)

    z = _dot(x_ref[0], w_ref[...])
    tm = z.shape[0]
    rolled = pltpu.roll(z, 1, 0)
    row = lax.broadcasted_iota(jnp.int32, (SUBLANES, 1), 0)
    first = jnp.where(row == 0, hist_ref[SUBLANES - 1:SUBLANES, :], rolled[0:SUBLANES])
    zprev = jnp.concatenate([first, rolled[SUBLANES:]], axis=0)
    hist_ref[...] = z[tm - SUBLANES:tm]
    outs = _rwkv_prep(z, zprev, mu_ref[...], w2_ref[...], w0_ref[...], a0_ref[...],
                      kk_ref[...], ka_ref[...])
    for o_ref, o in zip((r_ref, lw_ref, k_ref, v_ref, kkraw_ref, a_ref, g_ref), outs):
        o_ref[0] = o


def _rwkv_proj_step_kernel(x_ref, xp_ref, w_ref, mu_ref, w2_ref, w0_ref, a0_ref, kk_ref, ka_ref,
                           r_ref, lw_ref, k_ref, v_ref, kkraw_ref, a_ref, g_ref):
    z = _dot(x_ref[...], w_ref[...])
    zprev = _dot(xp_ref[...], w_ref[...])
    outs = _rwkv_prep(z, zprev, mu_ref[...], w2_ref[...], w0_ref[...], a0_ref[...],
                      kk_ref[...], ka_ref[...])
    for o_ref, o in zip((r_ref, lw_ref, k_ref, v_ref, kkraw_ref, a_ref, g_ref), outs):
        o_ref[...] = o


def _const_spec(shape):
    nd = len(shape)
    return pl.BlockSpec(shape, lambda *_: (0,) * nd, pipeline_mode=pl.Buffered(1))


def _rwkv_proj_seq(x, wp, tm):
    b, t, _ = x.shape
    row_spec = pl.BlockSpec((1, tm, D_MODEL), lambda i, j: (i, j, 0))
    vec = _const_spec((1, D_MODEL))
    return pl.pallas_call(
        _rwkv_proj_seq_kernel,
        grid=(b, t // tm),
        in_specs=[row_spec, _const_spec((D_MODEL, RWKV_COLS)), _const_spec((1, RWKV_COLS)),
                  _const_spec((LORA, 3 * D_MODEL)), vec, vec, vec, vec],
        out_specs=[row_spec] * 7,
        out_shape=[jax.ShapeDtypeStruct((b, t, D_MODEL), F32)] * 7,
        scratch_shapes=[pltpu.VMEM((SUBLANES, RWKV_COLS), F32)],
        compiler_params=pltpu.CompilerParams(
            dimension_semantics=("arbitrary", "arbitrary"), vmem_limit_bytes=VMEM_LIMIT),
        name="rwkv_proj_seq",
    )(x, wp["w_in"], wp["mu"], wp["w2ext"], wp["w0"], wp["a0"], wp["k_k"], wp["k_a"])


def _rwkv_proj_step(x, xprev, wp):
    n = x.shape[0]
    full = pl.BlockSpec((n, D_MODEL), lambda i: (0, 0))
    vec = _const_spec((1, D_MODEL))
    return pl.pallas_call(
        _rwkv_proj_step_kernel,
        grid=(1,),
        in_specs=[full, full, _const_spec((D_MODEL, RWKV_COLS)), _const_spec((1, RWKV_COLS)),
                  _const_spec((LORA, 3 * D_MODEL)), vec, vec, vec, vec],
        out_specs=[full] * 7,
        out_shape=[jax.ShapeDtypeStruct((n, D_MODEL), F32)] * 7,
        compiler_params=pltpu.CompilerParams(
            dimension_semantics=("arbitrary",), vmem_limit_bytes=VMEM_LIMIT),
        name="rwkv_proj_step",
    )(x, xprev, wp["w_in"], wp["mu"], wp["w2ext"], wp["w0"], wp["a0"], wp["k_k"], wp["k_a"])


def _lru_gates(u, wa_ref, ba, wi_ref, bi, sp):
    ub = u.astype(BF16)
    ra, ia = [], []
    for q in range(D_LRU // LRU_GROUP):
        uq = ub[:, q * LRU_GROUP:(q + 1) * LRU_GROUP]
        ra.append(jnp.dot(uq, wa_ref[q], preferred_element_type=F32))
        ia.append(jnp.dot(uq, wi_ref[q], preferred_element_type=F32))
    rg = _sigmoid(jnp.concatenate(ra, axis=1) + ba)
    ig = _sigmoid(jnp.concatenate(ia, axis=1) + bi)
    log_a = -LRU_C * rg * sp
    a = jnp.exp(log_a)
    m2 = -jnp.tanh(log_a) * (a * a + 1.0)
    mult = jnp.where(m2 > 0.0, m2 * lax.rsqrt(m2), 0.0)
    return a, mult, ig


def _lru_seq_kernel(x_ref, w_ref, cw_ref, cb_ref, wa_ref, ba_ref, wi_ref, bi_ref, lam_ref,
                    ga_ref, gyb_ref, conv_ref, hlast_ref,
                    xhist_ref, hc_ref, a_s, bx_s, h_s):
    t = pl.program_id(1)

    @pl.when(t == 0)
    def _():
        xhist_ref[...] = jnp.zeros_like(xhist_ref)
        hc_ref[...] = jnp.zeros_like(hc_ref)

    z = _dot(x_ref[0], w_ref[:, RWKV_COLS:])
    tm = z.shape[0]
    xb = z[:, 0:D_LRU]
    gb = z[:, D_LRU:2 * D_LRU]
    zga = z[:, 2 * D_LRU:2 * D_LRU + D_MODEL]
    zgb = z[:, 2 * D_LRU + D_MODEL:]
    row8 = lax.broadcasted_iota(jnp.int32, (SUBLANES, 1), 0)
    hist = xhist_ref[...]

    def shifted(k):
        rolled = pltpu.roll(xb, k, 0)
        first = jnp.where(row8 < k, pltpu.roll(hist, k, 0), rolled[0:SUBLANES])
        return jnp.concatenate([first, rolled[SUBLANES:]], axis=0)

    cw = cw_ref[...]
    u = (cb_ref[...] + cw[0:1] * shifted(3) + cw[1:2] * shifted(2) + cw[2:3] * shifted(1)
         + cw[3:4] * xb)
    last8 = xb[tm - SUBLANES:tm]
    xhist_ref[...] = last8
    conv_ref[0] = last8

    sp = _softplus(-lam_ref[...])
    a, mult, ig = _lru_gates(u, wa_ref, ba_ref[...], wi_ref, bi_ref[...], sp)
    row = lax.broadcasted_iota(jnp.int32, (tm, 1), 0)
    mult = jnp.where(jnp.logical_and(row == 0, t == 0), 1.0, mult)
    a_s[...] = a
    bx_s[...] = mult * ig * u

    def blk(i, hc):
        o = pl.multiple_of(i * SUBLANES, SUBLANES)
        av = a_s[pl.ds(o, SUBLANES), :]
        bv = bx_s[pl.ds(o, SUBLANES), :]
        for d in (1, 2, 4):
            a_sh = jnp.where(row8 < d, 1.0, pltpu.roll(av, d, 0))
            b_sh = jnp.where(row8 < d, 0.0, pltpu.roll(bv, d, 0))
            bv = av * b_sh + bv
            av = av * a_sh
        h = bv + av * hc
        h_s[pl.ds(o, SUBLANES), :] = h
        return h[SUBLANES - 1:SUBLANES, :]

    hc = lax.fori_loop(0, tm // SUBLANES, blk, hc_ref[0:1, :])
    hc_ref[0:1, :] = hc
    h = h_s[...]
    hlast_ref[0] = h[tm - SUBLANES:tm]
    ga_ref[0] = _sigmoid(zga)
    gyb_ref[0] = _sigmoid(zgb) * (h * _gelu_tanh(gb))


def _lru_step_kernel(x_ref, conv0_ref, h0_ref, w_ref, cw_ref, cb_ref, wa_ref, ba_ref, wi_ref,
                     bi_ref, lam_ref, ga_ref, gyb_ref, conv_ref, hlast_ref, *, steps, nseq):
    z = _dot(x_ref[...], w_ref[:, RWKV_COLS:])
    xb = z[:, 0:D_LRU]
    gb = z[:, D_LRU:2 * D_LRU]
    zga = z[:, 2 * D_LRU:2 * D_LRU + D_MODEL]
    zgb = z[:, 2 * D_LRU + D_MODEL:]
    n = steps * nseq
    hist = (CONV_W - 1) * nseq
    xext = jnp.concatenate([conv0_ref[...], xb], axis=0)

    def shifted(k):
        return xext[hist - k * nseq:hist - k * nseq + n]

    cw = cw_ref[...]
    u = (cb_ref[...] + cw[0:1] * shifted(3) + cw[1:2] * shifted(2) + cw[2:3] * shifted(1)
         + cw[3:4] * xb)
    conv_ref[...] = xext[n:n + hist]
    sp = _softplus(-lam_ref[...])
    a, mult, ig = _lru_gates(u, wa_ref, ba_ref[...], wi_ref, bi_ref[...], sp)
    bx = mult * ig * u
    h = h0_ref[...]
    hs = []
    for s in range(steps):
        h = a[s * nseq:(s + 1) * nseq] * h + bx[s * nseq:(s + 1) * nseq]
        hs.append(h)
    hlast_ref[...] = h
    hall = jnp.concatenate(hs, axis=0)
    ga_ref[...] = _sigmoid(zga)
    gyb_ref[...] = _sigmoid(zgb) * (hall * _gelu_tanh(gb))


def _lru_weight_specs():
    vec = _const_spec((1, D_LRU))
    gate_w = _const_spec((D_LRU // LRU_GROUP, LRU_GROUP, LRU_GROUP))
    return [_const_spec((D_MODEL, N_IN)), _const_spec((CONV_W, D_LRU)), vec,
            gate_w, vec, gate_w, vec, vec]


def _lru_weights(wp):
    return (wp["w_in"], wp["conv_w"], wp["conv_b"], wp["wa_bd"], wp["ba"], wp["wi_bd"],
            wp["bi"], wp["lam"])


def _lru_seq(x, wp, tm):
    b, t, _ = x.shape
    row_spec = pl.BlockSpec((1, tm, D_MODEL), lambda i, j: (i, j, 0))
    tail_spec = pl.BlockSpec((1, SUBLANES, D_LRU), lambda i, j: (i, 0, 0))
    return pl.pallas_call(
        _lru_seq_kernel,
        grid=(b, t // tm),
        in_specs=[row_spec] + _lru_weight_specs(),
        out_specs=[row_spec, row_spec, tail_spec, tail_spec],
        out_shape=[jax.ShapeDtypeStruct((b, t, D_MODEL), F32)] * 2
        + [jax.ShapeDtypeStruct((b, SUBLANES, D_LRU), F32)] * 2,
        scratch_shapes=[pltpu.VMEM((SUBLANES, D_LRU), F32), pltpu.VMEM((SUBLANES, D_LRU), F32),
                        pltpu.VMEM((tm, D_LRU), F32), pltpu.VMEM((tm, D_LRU), F32),
                        pltpu.VMEM((tm, D_LRU), F32)],
        compiler_params=pltpu.CompilerParams(
            dimension_semantics=("arbitrary", "arbitrary"), vmem_limit_bytes=VMEM_LIMIT),
        name="lru_seq",
    )(x, *_lru_weights(wp))


def _lru_step(x_tm, conv_tm, h0, wp, steps, nseq):
    n = steps * nseq
    hist = (CONV_W - 1) * nseq

    def full(r):
        return pl.BlockSpec((r, D_MODEL), lambda i: (0, 0))

    return pl.pallas_call(
        functools.partial(_lru_step_kernel, steps=steps, nseq=nseq),
        grid=(1,),
        in_specs=[full(n), full(hist), full(nseq)] + _lru_weight_specs(),
        out_specs=[full(n), full(n), full(hist), full(nseq)],
        out_shape=[jax.ShapeDtypeStruct((n, D_MODEL), F32)] * 2
        + [jax.ShapeDtypeStruct((hist, D_LRU), F32), jax.ShapeDtypeStruct((nseq, D_LRU), F32)],
        compiler_params=pltpu.CompilerParams(
            dimension_semantics=("arbitrary",), vmem_limit_bytes=VMEM_LIMIT),
        name="lru_step",
    )(x_tm, conv_tm, h0, *_lru_weights(wp))


def _wkv_consts(nseq):
    c = CHUNK
    seq_len = c // nseq
    lane = lax.broadcasted_iota(jnp.int32, (c, PAIR), 1)
    row = lax.broadcasted_iota(jnp.int32, (c, PAIR), 0)
    lane_lo = lane < HEAD
    col = lane % c
    same = (row // seq_len) == (col // seq_len)
    strict = jnp.logical_and(same, col < row)
    incl = jnp.logical_and(same, col <= row)
    eye = jnp.where(col == row, 1.0, 0.0).astype(F32)
    r3 = lax.broadcasted_iota(jnp.int32, (c, 3 * c), 0)
    c3 = lax.broadcasted_iota(jnp.int32, (c, 3 * c), 1) % c
    same3 = (r3 // seq_len) == (c3 // seq_len)
    tri3 = jnp.where(jnp.logical_and(same3, c3 <= r3), 1.0, 0.0).astype(BF16)
    tot3 = jnp.where(same3, 1.0, 0.0).astype(BF16)
    rp = lax.broadcasted_iota(jnp.int32, (PAIR, PAIR), 0)
    cp = lax.broadcasted_iota(jnp.int32, (PAIR, PAIR), 1)
    blockdiag = (rp // HEAD) == (cp // HEAD)
    return dict(lane_lo=lane_lo, tri3=tri3, tot3=tot3, strict=strict, incl=incl, eye=eye,
                blockdiag=blockdiag, seq_len=seq_len)


def _split3(x):
    hi = x.astype(BF16)
    rest = x - hi.astype(F32)
    mid = rest.astype(BF16)
    lo = (rest - mid.astype(F32)).astype(BF16)
    return jnp.concatenate([hi, mid, lo], axis=0)


def _split_heads(x, lane_lo):
    zero = jnp.zeros_like(x)
    return jnp.concatenate([jnp.where(lane_lo, x, zero), jnp.where(lane_lo, zero, x)], axis=0)


def _run_interleaved(stage_gens):
    results = [None] * len(stage_gens)
    live = list(range(len(stage_gens)))
    while live:
        still = []
        for idx in live:
            try:
                next(stage_gens[idx])
                still.append(idx)
            except StopIteration as done:
                results[idx] = done.value
        live = still
    return results


def _wkv_chunk(r, lw, k, v, al, be, get_states, cn, nseq):
    c = CHUNK
    lane_lo = cn["lane_lo"]
    lw3 = _split3(lw)
    cum = jnp.dot(cn["tri3"], lw3, preferred_element_type=F32)
    if nseq == 1:
        end = cum[c - 1:c, :]
    else:
        end = jnp.dot(cn["tot3"], lw3, preferred_element_type=F32)
    yield
    e_cum = jnp.exp(cum)
    e_neg = jnp.exp(-cum)
    e_end = jnp.exp(end)
    rt = r * e_cum
    at = al * jnp.exp(cum - lw)
    bt = be * e_neg
    kt = k * e_neg
    lhs2 = jnp.concatenate([at, rt], axis=0).astype(BF16)
    keys = jnp.concatenate([_split_heads(bt, lane_lo), _split_heads(kt, lane_lo)], axis=0)
    gram = _dot_nt(lhs2, keys)
    yield
    a_ab = jnp.where(cn["strict"], gram[0:c, 0:PAIR], 0.0)
    a_ak = jnp.where(cn["strict"], gram[0:c, PAIR:], 0.0)
    a_rb = jnp.where(cn["incl"], gram[c:, 0:PAIR], 0.0)
    a_rk = jnp.where(cn["incl"], gram[c:, PAIR:], 0.0)
    inv = cn["eye"] + a_ab
    n_iter = int(math.log2(cn["seq_len"])) - 1
    pw = _dot(a_ab, _split_heads(a_ab, lane_lo))
    yield
    for it in range(n_iter):
        pw_heads = _split_heads(pw, lane_lo)
        if it < n_iter - 1:
            both = _dot(jnp.concatenate([inv, pw], axis=0), pw_heads)
            inv = inv + both[0:c]
            pw = both[c:]
        else:
            inv = inv + _dot(inv, pw_heads)
        yield
    av = _dot(jnp.concatenate([a_ak, a_rk], axis=0), _split_heads(v, lane_lo))
    states = get_states()
    while states is None:
        yield
        states = get_states()
    if nseq == 1:
        from_state = _dot_nt(lhs2, states[0])
    else:
        seq_of_row = (lax.broadcasted_iota(jnp.int32, (2 * c, 1), 0) % c) // cn["seq_len"]
        from_state = jnp.zeros((2 * c, PAIR), F32)
        for s in range(nseq):
            from_state = jnp.where(seq_of_row == s, _dot_nt(lhs2, states[s]), from_state)
    yield
    u = _dot(inv, _split_heads(from_state[0:c] + av[0:c], lane_lo))
    yield
    y = from_state[c:] + av[c:] + _dot(a_rb, _split_heads(u, lane_lo))
    yield
    uv = jnp.concatenate([u, v], axis=0)
    bk = jnp.concatenate([bt * e_end, kt * e_end], axis=0)
    new_states = []
    if nseq == 1:
        inc = _dot_tn(uv, bk)
        new_states.append(states[0] * e_end + jnp.where(cn["blockdiag"], inc, 0.0))
    else:
        seq_of_uv = (lax.broadcasted_iota(jnp.int32, (2 * c, 1), 0) % c) // cn["seq_len"]
        for s in range(nseq):
            inc = _dot_tn(jnp.where(seq_of_uv == s, uv, 0.0), bk)
            decay = e_end[s * cn["seq_len"]:s * cn["seq_len"] + 1, :]
            new_states.append(states[s] * decay + jnp.where(cn["blockdiag"], inc, 0.0))
    return y, new_states


def _wkv_readout(y, r, k, v, g, rk, lnx_g, lnx_b, lane_lo):
    mean = _seg_sum(y, lane_lo) * (1.0 / HEAD)
    bonus = _seg_sum(r * k * rk, lane_lo) * v
    yield
    yc = y - mean
    var = _seg_sum(yc * yc, lane_lo) * (1.0 / HEAD)
    yield
    yn = yc * lax.rsqrt(var + GN_EPS) * lnx_g + lnx_b
    return (yn + bonus) * g


def _to_blockdiag(s0, s1):
    pad = jnp.zeros((HEAD, HEAD), F32)
    top = jnp.concatenate([s0, pad], axis=1)
    bottom = pltpu.roll(jnp.concatenate([s1, pad], axis=1), HEAD, 1)
    return jnp.concatenate([top, bottom], axis=0)


def _from_blockdiag(state):
    return state[0:HEAD, 0:HEAD], pltpu.roll(state[HEAD:], HEAD, 1)[:, 0:HEAD]


def _wkv_seq_kernel(r_ref, lw_ref, k_ref, v_ref, al_ref, be_ref, g_ref, rk_ref, lg_ref, lb_ref,
                    y_ref, s_out_ref, s_scr, yraw_scr):
    t = pl.program_id(2)
    cn = _wkv_consts(1)
    n_pairs = s_scr.shape[0]

    @pl.when(t == 0)
    def _():
        s_scr[...] = jnp.zeros_like(s_scr)
        yraw_scr[...] = jnp.zeros_like(yraw_scr)

    n_chunks = r_ref.shape[1] // CHUNK

    def readout_stages(p, rows):
        ln = pl.ds(p * PAIR, PAIR)
        out = yield from _wkv_readout(
            yraw_scr[:, ln], r_ref[0, rows, ln], k_ref[0, rows, ln], v_ref[0, rows, ln],
            g_ref[0, rows, ln], rk_ref[:, ln], lg_ref[:, ln], lb_ref[:, ln], cn["lane_lo"])
        y_ref[0, rows, ln] = out

    def chunk_stages(p, rows, get_states):
        ln = pl.ds(p * PAIR, PAIR)
        return _wkv_chunk(r_ref[0, rows, ln], lw_ref[0, rows, ln], k_ref[0, rows, ln],
                          v_ref[0, rows, ln], al_ref[0, rows, ln], be_ref[0, rows, ln],
                          get_states, cn, 1)

    def body(i, states):
        def rows_of(j):
            return pl.ds(pl.multiple_of((CHUNKS_IN_FLIGHT * i + j) * CHUNK, CHUNK), CHUNK)

        prev = pl.ds(pl.multiple_of(jnp.maximum(CHUNKS_IN_FLIGHT * i - 1, 0) * CHUNK, CHUNK),
                     CHUNK)
        end_states = [[None] * n_pairs for _ in range(CHUNKS_IN_FLIGHT)]

        def start_states(j, p):
            if j == 0:
                return [states[p]]
            return None if end_states[j - 1][p] is None else [end_states[j - 1][p]]

        def stages(j, p):
            ln = pl.ds(p * PAIR, PAIR)
            rows = rows_of(j)
            y, (state,) = yield from chunk_stages(p, rows, functools.partial(start_states, j, p))
            end_states[j][p] = state
            if j == CHUNKS_IN_FLIGHT - 1:
                yraw_scr[:, ln] = y
                return
            out = yield from _wkv_readout(
                y, r_ref[0, rows, ln], k_ref[0, rows, ln], v_ref[0, rows, ln],
                g_ref[0, rows, ln], rk_ref[:, ln], lg_ref[:, ln], lb_ref[:, ln], cn["lane_lo"])
            y_ref[0, rows, ln] = out

        _run_interleaved([readout_stages(p, prev) for p in range(n_pairs)]
                         + [stages(j, p) for j in range(CHUNKS_IN_FLIGHT) for p in range(n_pairs)])
        return tuple(end_states[CHUNKS_IN_FLIGHT - 1])

    assert n_chunks % CHUNKS_IN_FLIGHT == 0
    states = lax.fori_loop(0, n_chunks // CHUNKS_IN_FLIGHT, body,
                           tuple(s_scr[p] for p in range(n_pairs)))
    last = pl.ds((n_chunks - 1) * CHUNK, CHUNK)
    _run_interleaved([readout_stages(p, last) for p in range(n_pairs)])
    for p in range(n_pairs):
        s_scr[p] = states[p]
        s_out_ref[0, 2 * p], s_out_ref[0, 2 * p + 1] = _from_blockdiag(states[p])


def _wkv_step_kernel(r_ref, lw_ref, k_ref, v_ref, al_ref, be_ref, g_ref, rk_ref, lg_ref, lb_ref,
                     s_in_ref, y_ref, s_out_ref, *, nseq):
    cn = _wkv_consts(nseq)

    def pair_stages(p):
        ln = pl.ds(p * PAIR, PAIR)
        states = [_to_blockdiag(s_in_ref[s, 2 * p], s_in_ref[s, 2 * p + 1]) for s in range(nseq)]
        r, k, v = r_ref[:, ln], k_ref[:, ln], v_ref[:, ln]
        y, states = yield from _wkv_chunk(r, lw_ref[:, ln], k, v, al_ref[:, ln], be_ref[:, ln],
                                          lambda: states, cn, nseq)
        ya = yield from _wkv_readout(y, r, k, v, g_ref[:, ln], rk_ref[:, ln], lg_ref[:, ln],
                                     lb_ref[:, ln], cn["lane_lo"])
        y_ref[:, ln] = ya
        for s in range(nseq):
            s_out_ref[s, 2 * p], s_out_ref[s, 2 * p + 1] = _from_blockdiag(states[s])

    _run_interleaved([pair_stages(p) for p in range(s_in_ref.shape[1] // 2)])


def _wkv_seq(acts, wp, tt, n_pairs):
    b, t, _ = acts[0].shape
    width = n_pairs * PAIR
    row_spec = pl.BlockSpec((1, tt, width), lambda i, p, j: (i, j, p))
    vec = pl.BlockSpec((1, width), lambda i, p, j: (0, p))
    return pl.pallas_call(
        _wkv_seq_kernel,
        grid=(b, N_PAIRS // n_pairs, t // tt),
        in_specs=[row_spec] * 7 + [vec] * 3,
        out_specs=[row_spec,
                   pl.BlockSpec((1, 2 * n_pairs, HEAD, HEAD), lambda i, p, j: (i, p, 0, 0))],
        out_shape=[jax.ShapeDtypeStruct((b, t, D_MODEL), F32),
                   jax.ShapeDtypeStruct((b, N_HEADS, HEAD, HEAD), F32)],
        scratch_shapes=[pltpu.VMEM((n_pairs, PAIR, PAIR), F32), pltpu.VMEM((CHUNK, width), F32)],
        compiler_params=pltpu.CompilerParams(
            dimension_semantics=("arbitrary", "arbitrary", "arbitrary"),
            vmem_limit_bytes=VMEM_LIMIT),
        name="wkv_seq",
    )(*acts, wp["r_k"], wp["lnx_g"], wp["lnx_b"])


def _wkv_step(acts, s_heads, wp, steps, n_pairs):
    n = acts[0].shape[0]
    nseq = CHUNK // steps
    width = n_pairs * PAIR
    row_spec = pl.BlockSpec((CHUNK, width), lambda i, p: (i, p))
    vec = pl.BlockSpec((1, width), lambda i, p: (0, p))
    st_spec = pl.BlockSpec((nseq, 2 * n_pairs, HEAD, HEAD), lambda i, p: (i, p, 0, 0))
    return pl.pallas_call(
        functools.partial(_wkv_step_kernel, nseq=nseq),
        grid=(n // CHUNK, N_PAIRS // n_pairs),
        in_specs=[row_spec] * 7 + [vec] * 3 + [st_spec],
        out_specs=[row_spec, st_spec],
        out_shape=[jax.ShapeDtypeStruct((n, D_MODEL), F32),
                   jax.ShapeDtypeStruct(s_heads.shape, F32)],
        compiler_params=pltpu.CompilerParams(
            dimension_semantics=("arbitrary", "arbitrary"), vmem_limit_bytes=VMEM_LIMIT),
        name="wkv_step",
    )(*acts, wp["r_k"], wp["lnx_g"], wp["lnx_b"], s_heads)


def _post_kernel(x_ref, ya_ref, ga_ref, gyb_ref, wo_ref, l1g_ref, l1b_ref, wg_ref, wu_ref, wd_ref,
                 l2g_ref, l2b_ref, y_ref):
    x = x_ref[...]
    merged = ga_ref[...] * ya_ref[...] + gyb_ref[...]
    mix = _dot(merged, wo_ref[...])
    h1 = _layer_norm(ALPHA * x + mix, l1g_ref[...], l1b_ref[...])
    h1b = h1.astype(BF16)
    gate = jnp.dot(h1b, wg_ref[...], preferred_element_type=F32)
    up = jnp.dot(h1b, wu_ref[...], preferred_element_type=F32)
    act = (gate * _sigmoid(gate)) * up
    ffn = _dot(act, wd_ref[...])
    y_ref[...] = _layer_norm(ALPHA * h1 + ffn, l2g_ref[...], l2b_ref[...])


def _post(x, ya, ga, gyb, wp, tm):
    n = x.shape[0]
    row_spec = pl.BlockSpec((tm, D_MODEL), lambda i: (i, 0))
    vec = _const_spec((1, D_MODEL))
    return pl.pallas_call(
        _post_kernel,
        grid=(n // tm,),
        in_specs=[row_spec] * 4 + [_const_spec((D_MODEL, D_MODEL)), vec, vec,
                                   _const_spec((D_MODEL, D_FF)), _const_spec((D_MODEL, D_FF)),
                                   _const_spec((D_FF, D_MODEL)), vec, vec],
        out_specs=row_spec,
        out_shape=jax.ShapeDtypeStruct((n, D_MODEL), F32),
        compiler_params=pltpu.CompilerParams(
            dimension_semantics=("arbitrary",), vmem_limit_bytes=VMEM_LIMIT),
        name="post",
    )(x, ya, ga, gyb, wp["w_o"], wp["ln1_g"], wp["ln1_b"], wp["w_gate"], wp["w_up"], wp["w_down"],
      wp["ln2_g"], wp["ln2_b"])


def _prep_weights(w_in, tmix_mu, w0, w2_decay, a0, a2_iclr, g2_gate, k_k, k_a, r_k, lnx_g, lnx_b,
                  conv_w, conv_b, lru_wa, lru_ba, lru_wi, lru_bi, lru_lambda, w_o,
                  ln1_g, ln1_b, w_ffn_gate, w_ffn_up, w_ffn_down, ln2_g, ln2_b):
    row = lambda v: v.reshape(1, -1).astype(F32)
    zeros = lambda r: jnp.zeros((r, D_MODEL), F32)
    w2ext = jnp.concatenate([
        jnp.concatenate([w2_decay, zeros(DECAY_LORA), zeros(DECAY_LORA)], axis=1),
        jnp.concatenate([zeros(ICLR_LORA), a2_iclr, zeros(ICLR_LORA)], axis=1),
        jnp.concatenate([zeros(GATE_LORA), zeros(GATE_LORA), g2_gate], axis=1)], axis=0)

    def gate_blockdiag(w):
        per = LRU_GROUP // LRU_BS
        w4 = w.reshape(LRU_BLOCKS // per, per, LRU_BS, LRU_BS)
        eye = jnp.eye(per, dtype=w.dtype)
        bd = jnp.einsum("gpcd,pq->gpcqd", w4, eye)
        return bd.reshape(LRU_BLOCKS // per, LRU_GROUP, LRU_GROUP).astype(BF16)

    return dict(
        w_in=w_in.astype(BF16),
        mu=row(tmix_mu), w2ext=w2ext.astype(BF16), w0=row(w0), a0=row(a0), k_k=row(k_k),
        k_a=row(k_a), r_k=row(r_k), lnx_g=row(lnx_g), lnx_b=row(lnx_b),
        conv_w=conv_w.astype(F32), conv_b=row(conv_b), wa_bd=gate_blockdiag(lru_wa),
        ba=row(lru_ba), wi_bd=gate_blockdiag(lru_wi), bi=row(lru_bi), lam=row(lru_lambda),
        w_o=w_o.astype(BF16), ln1_g=row(ln1_g), ln1_b=row(ln1_b), w_gate=w_ffn_gate.astype(BF16),
        w_up=w_ffn_up.astype(BF16), w_down=w_ffn_down.astype(BF16), ln2_g=row(ln2_g),
        ln2_b=row(ln2_b))


def _prompt_layer(x, wp, tm, tt, tm_post, n_pairs):
    b, t, _ = x.shape
    acts = _rwkv_proj_seq(x, wp, tm)
    ya, s_heads = _wkv_seq(acts, wp, tt, n_pairs)
    ga, gyb, conv_tail, h_tail = _lru_seq(x, wp, tm)
    y = _post(x.reshape(b * t, D_MODEL), ya.reshape(b * t, D_MODEL), ga.reshape(b * t, D_MODEL),
              gyb.reshape(b * t, D_MODEL), wp, tm_post).reshape(b, t, D_MODEL)
    return (y, x[:, -1], s_heads, conv_tail[:, SUBLANES - (CONV_W - 1):],
            h_tail[:, SUBLANES - 1])


def _sample_layer(x, shift_buf, wkv0, conv_buf, h0, wp):
    b, t, _ = x.shape
    n = b * t
    xf = x.reshape(n, D_MODEL)
    xprev = jnp.concatenate([shift_buf[:, None], x[:, :-1]], axis=1).reshape(n, D_MODEL)
    acts = _rwkv_proj_step(xf, xprev, wp)
    ya, s_heads = _wkv_step(acts, wkv0, wp, t, 4)
    x_tm = x.transpose(1, 0, 2).reshape(n, D_MODEL)
    conv_tm = conv_buf.transpose(1, 0, 2).reshape((CONV_W - 1) * b, D_LRU)
    ga_tm, gyb_tm, conv_new_tm, h_last = _lru_step(x_tm, conv_tm, h0, wp, t, b)
    to_seq_major = lambda v: v.reshape(t, b, D_MODEL).transpose(1, 0, 2).reshape(n, D_MODEL)
    y = _post(xf, ya, to_seq_major(ga_tm), to_seq_major(gyb_tm), wp, min(n, 256)).reshape(
        b, t, D_MODEL)
    conv_new = conv_new_tm.reshape(CONV_W - 1, b, D_LRU).transpose(1, 0, 2)
    return y, x[:, -1], s_heads, conv_new, h_last


def kernel(x_prompt, x_sample, state_shift, state_wkv, state_conv, state_lru, w_in, tmix_mu, w0, w2_decay, a0, a2_iclr, g2_gate, k_k, k_a, r_k, lnx_g, lnx_b, conv_w, conv_b, lru_wa, lru_ba, lru_wi, lru_bi, lru_lambda, w_o, ln1_g, ln1_b, w_ffn_gate, w_ffn_up, w_ffn_down, ln2_g, ln2_b):
    params = (w_in, tmix_mu, w0, w2_decay, a0, a2_iclr, g2_gate, k_k, k_a, r_k, lnx_g, lnx_b,
              conv_w, conv_b, lru_wa, lru_ba, lru_wi, lru_bi, lru_lambda, w_o,
              ln1_g, ln1_b, w_ffn_gate, w_ffn_up, w_ffn_down, ln2_g, ln2_b)
    wp = _prep_weights(*[p[0] for p in params])
    yp, sh_p, wkv_p, conv_p, lru_p = _prompt_layer(x_prompt, wp, 256, 512, 256, 8)
    ys, sh_s, wkv_s, conv_s, lru_s = _sample_layer(
        x_sample, state_shift[0], state_wkv[0], state_conv[0], state_lru[0], wp)
    return (yp, ys, sh_p[None], wkv_p[None], conv_p[None], lru_p[None],
            sh_s[None], wkv_s[None], conv_s[None], lru_s[None])
```

```python
import functools
import math

import jax
import jax.numpy as jnp
from jax import lax
from jax.experimental import pallas as pl
from jax.experimental.pallas import tpu as pltpu

F32 = jnp.float32
BF16 = jnp.bfloat16

D_MODEL = 1024
HEAD = 64
N_HEADS = D_MODEL // HEAD
PAIR = 2 * HEAD
N_PAIRS = N_HEADS // 2
DECAY_LORA = 64
ICLR_LORA = 64
GATE_LORA = 128
LORA = DECAY_LORA + ICLR_LORA + GATE_LORA
RWKV_COLS = 3 * D_MODEL + LORA
GN_EPS = HEAD * 1e-5
D_LRU = D_MODEL
LRU_BLOCKS = 16
LRU_BS = D_LRU // LRU_BLOCKS
LRU_GROUP = 256
CONV_W = 4
LRU_C = 8.0
LRU_COLS = 2 * D_LRU + 2 * D_MODEL
N_IN = RWKV_COLS + LRU_COLS
D_FF = 2816
ALPHA = 2.0 ** 0.25
LN_EPS = 1e-5

SUBLANES = 8
CHUNK = 64
CHUNKS_IN_FLIGHT = 2
VMEM_LIMIT = 56 * 1024 * 1024


def _softplus(x):
    return jnp.maximum(x, 0.0) + jnp.log1p(jnp.exp(-jnp.abs(x)))


def _softplus_plain(x):
    return jnp.maximum(x, 0.0) + jnp.log(1.0 + jnp.exp(-jnp.abs(x)))


def _sigmoid(x):
    return 0.5 * jnp.tanh(0.5 * x) + 0.5


def _gelu_tanh(x):
    c = math.sqrt(2.0 / math.pi)
    return x * (0.5 * (1.0 + jnp.tanh(c * (x + 0.044715 * (x * x * x)))))


def _layer_norm(x, g, b):
    mu = jnp.mean(x, axis=-1, keepdims=True)
    xc = x - mu
    var = jnp.mean(xc * xc, axis=-1, keepdims=True)
    return xc * lax.rsqrt(var + LN_EPS) * g + b


def _dot(a, b):
    return jnp.dot(a.astype(BF16), b.astype(BF16), preferred_element_type=F32)


def _dot_nt(a, b):
    return lax.dot_general(a.astype(BF16), b.astype(BF16), (((1,), (1,)), ((), ())),
                           preferred_element_type=F32)


def _dot_tn(a, b):
    return lax.dot_general(a.astype(BF16), b.astype(BF16), (((0,), (0,)), ((), ())),
                           preferred_element_type=F32)


def _seg_sum(x, lane_lo):
    s0 = jnp.sum(jnp.where(lane_lo, x, 0.0), axis=-1, keepdims=True)
    s1 = jnp.sum(jnp.where(lane_lo, 0.0, x), axis=-1, keepdims=True)
    return jnp.where(lane_lo, s0, s1)


def _head_sums(x):
    lane_lo = lax.broadcasted_iota(jnp.int32, (x.shape[0], PAIR), 1) < HEAD
    return jnp.concatenate([_seg_sum(x[:, p * PAIR:(p + 1) * PAIR], lane_lo)
                            for p in range(N_PAIRS)], axis=1)


def _rwkv_prep(mixed, w2ext, w0, a0, k_k, k_a):
    lo = mixed(3 * D_MODEL, LORA)
    lane = lax.broadcasted_iota(jnp.int32, lo.shape, 1)
    act = jnp.where(lane < DECAY_LORA, jnp.tanh(lo),
                    jnp.where(lane < DECAY_LORA + ICLR_LORA, lo, _sigmoid(lo)))
    r = mixed(0, D_MODEL)
    lora = _dot(act, w2ext)
    k = mixed(D_MODEL, D_MODEL)
    w = -_softplus_plain(-(w0 + lora[:, 0:D_MODEL])) - 0.5
    lw = -jnp.exp(w)
    a = _sigmoid(a0 + lora[:, D_MODEL:2 * D_MODEL])
    g = lora[:, 2 * D_MODEL:3 * D_MODEL]
    v = mixed(2 * D_MODEL, D_MODEL)
    kkraw = k * k_k
    kk = kkraw * lax.rsqrt(jnp.maximum(_head_sums(kkraw * kkraw), 1e-24))
    k2 = k * (1.0 + (a - 1.0) * k_a)
    return r, lw, k2, v, -kk, kk * a, g


def _rwkv_proj_seq_kernel(x_ref, w_ref, mu_ref, w2_ref, w0_ref, a0_ref, kk_ref, ka_ref,
                          r_ref, lw_ref, k_ref, v_ref, al_ref, be_ref, g_ref, hist_ref):
    @pl.when(pl.program_id(1) == 0)
    def _():
        hist_ref[...] = jnp.zeros_like(hist_ref)

    xbf = x_ref[0].astype(BF16)
    tm = xbf.shape[0]
    row = lax.broadcasted_iota(jnp.int32, (SUBLANES, 1), 0)

    def mixed(first_col, n_cols):
        cols = slice(first_col, first_col + n_cols)
        z = jnp.dot(xbf, w_ref[:, cols], preferred_element_type=F32)
        rolled = pltpu.roll(z, 1, 0)
        first = jnp.where(row == 0, hist_ref[SUBLANES - 1:SUBLANES, cols], rolled[0:SUBLANES])
        zprev = jnp.concatenate([first, rolled[SUBLANES:]], axis=0)
        hist_ref[:, cols] = z[tm - SUBLANES:tm]
        return z + mu_ref[:, cols] * (zprev - z)

    outs = _rwkv_prep(mixed, w2_ref[...], w0_ref[...], a0_ref[...], kk_ref[...], ka_ref[...])
    for o_ref, o in zip((r_ref, lw_ref, k_ref, v_ref, al_ref, be_ref, g_ref), outs):
        o_ref[0] = o


def _rwkv_proj_step_kernel(x_ref, xp_ref, w_ref, mu_ref, w2_ref, w0_ref, a0_ref, kk_ref, ka_ref,
                           r_ref, lw_ref, k_ref, v_ref, al_ref, be_ref, g_ref):
    xbf = x_ref[...].astype(BF16)
    xpbf = xp_ref[...].astype(BF16)

    def mixed(first_col, n_cols):
        cols = slice(first_col, first_col + n_cols)
        z = jnp.dot(xbf, w_ref[:, cols], preferred_element_type=F32)
        zprev = jnp.dot(xpbf, w_ref[:, cols], preferred_element_type=F32)
        return z + mu_ref[:, cols] * (zprev - z)

    outs = _rwkv_prep(mixed, w2_ref[...], w0_ref[...], a0_ref[...], kk_ref[...], ka_ref[...])
    for o_ref, o in zip((r_ref, lw_ref, k_ref, v_ref, al_ref, be_ref, g_ref), outs):
        o_ref[...] = o


def _const_spec(shape):
    nd = len(shape)
    return pl.BlockSpec(shape, lambda *_: (0,) * nd, pipeline_mode=pl.Buffered(1))


def _rwkv_proj_seq(x, wp, tm):
    b, t, _ = x.shape
    row_spec = pl.BlockSpec((1, tm, D_MODEL), lambda i, j: (i, j, 0))
    vec = _const_spec((1, D_MODEL))
    return pl.pallas_call(
        _rwkv_proj_seq_kernel,
        grid=(b, t // tm),
        in_specs=[row_spec, _const_spec((D_MODEL, RWKV_COLS)), _const_spec((1, RWKV_COLS)),
                  _const_spec((LORA, 3 * D_MODEL)), vec, vec, vec, vec],
        out_specs=[row_spec] * 7,
        out_shape=[jax.ShapeDtypeStruct((b, t, D_MODEL), F32)] * 7,
        scratch_shapes=[pltpu.VMEM((SUBLANES, RWKV_COLS), F32)],
        compiler_params=pltpu.CompilerParams(
            dimension_semantics=("arbitrary", "arbitrary"), vmem_limit_bytes=VMEM_LIMIT),
        name="rwkv_proj_seq",
    )(x, wp["w_in"], wp["mu"], wp["w2ext"], wp["w0"], wp["a0"], wp["k_k"], wp["k_a"])


def _rwkv_proj_step(x, xprev, wp):
    n = x.shape[0]
    full = pl.BlockSpec((n, D_MODEL), lambda i: (0, 0))
    vec = _const_spec((1, D_MODEL))
    return pl.pallas_call(
        _rwkv_proj_step_kernel,
        grid=(1,),
        in_specs=[full, full, _const_spec((D_MODEL, RWKV_COLS)), _const_spec((1, RWKV_COLS)),
                  _const_spec((LORA, 3 * D_MODEL)), vec, vec, vec, vec],
        out_specs=[full] * 7,
        out_shape=[jax.ShapeDtypeStruct((n, D_MODEL), F32)] * 7,
        compiler_params=pltpu.CompilerParams(
            dimension_semantics=("arbitrary",), vmem_limit_bytes=VMEM_LIMIT),
        name="rwkv_proj_step",
    )(x, xprev, wp["w_in"], wp["mu"], wp["w2ext"], wp["w0"], wp["a0"], wp["k_k"], wp["k_a"])


def _lru_gate_logits(u, wa_ref, wi_ref):
    ub = u.astype(BF16)
    ra, ia = [], []
    for q in range(D_LRU // LRU_GROUP):
        uq = ub[:, q * LRU_GROUP:(q + 1) * LRU_GROUP]
        ra.append(jnp.dot(uq, wa_ref[q], preferred_element_type=F32))
        ia.append(jnp.dot(uq, wi_ref[q], preferred_element_type=F32))
    return jnp.concatenate(ra, axis=1), jnp.concatenate(ia, axis=1)


def _lru_coeffs(ra, ia, ba, bi, sp):
    rg = _sigmoid(ra + ba)
    ig = _sigmoid(ia + bi)
    log_a = -LRU_C * rg * sp
    a = jnp.exp(log_a)
    m2 = -jnp.tanh(log_a) * (a * a + 1.0)
    mult = jnp.where(m2 > 0.0, m2 * lax.rsqrt(m2), 0.0)
    return a, mult, ig


def _lru_seq_kernel(x_ref, w_ref, cw_ref, cb_ref, wa_ref, ba_ref, wi_ref, bi_ref, lam_ref,
                    ga_ref, gyb_ref, conv_ref, hlast_ref,
                    xhist_ref, hc_ref, a_s, bx_s, h_s):
    t = pl.program_id(1)

    @pl.when(t == 0)
    def _():
        xhist_ref[...] = jnp.zeros_like(xhist_ref)
        hc_ref[...] = jnp.zeros_like(hc_ref)

    xbf = x_ref[0].astype(BF16)
    tm = xbf.shape[0]

    def proj(first_col, n_cols):
        cols = slice(RWKV_COLS + first_col, RWKV_COLS + first_col + n_cols)
        return jnp.dot(xbf, w_ref[:, cols], preferred_element_type=F32)

    xb = proj(0, D_LRU)
    gb = proj(D_LRU, D_LRU)
    row8 = lax.broadcasted_iota(jnp.int32, (SUBLANES, 1), 0)
    hist = xhist_ref[...]

    def shifted(k):
        rolled = pltpu.roll(xb, k, 0)
        first = jnp.where(row8 < k, pltpu.roll(hist, k, 0), rolled[0:SUBLANES])
        return jnp.concatenate([first, rolled[SUBLANES:]], axis=0)

    cw = cw_ref[...]
    u = (cb_ref[...] + cw[0:1] * shifted(3) + cw[1:2] * shifted(2) + cw[2:3] * shifted(1)
         + cw[3:4] * xb)
    last8 = xb[tm - SUBLANES:tm]
    xhist_ref[...] = last8
    conv_ref[0] = last8

    ra, ia = _lru_gate_logits(u, wa_ref, wi_ref)
    gelu_gb = _gelu_tanh(gb)
    zg = proj(2 * D_LRU, 2 * D_MODEL)
    sp = _softplus(-lam_ref[...])
    a, mult, ig = _lru_coeffs(ra, ia, ba_ref[...], bi_ref[...], sp)
    row = lax.broadcasted_iota(jnp.int32, (tm, 1), 0)
    mult = jnp.where(jnp.logical_and(row == 0, t == 0), 1.0, mult)
    a_s[...] = a
    bx_s[...] = mult * ig * u
    ga_ref[0] = _sigmoid(zg[:, 0:D_MODEL])
    gate_b = _sigmoid(zg[:, D_MODEL:]) * gelu_gb

    def blk(i, hc):
        o = pl.multiple_of(i * SUBLANES, SUBLANES)
        av = a_s[pl.ds(o, SUBLANES), :]
        bv = bx_s[pl.ds(o, SUBLANES), :]
        for d in (1, 2, 4):
            a_sh = jnp.where(row8 < d, 1.0, pltpu.roll(av, d, 0))
            b_sh = jnp.where(row8 < d, 0.0, pltpu.roll(bv, d, 0))
            bv = av * b_sh + bv
            av = av * a_sh
        h = bv + av * hc
        h_s[pl.ds(o, SUBLANES), :] = h
        return h[SUBLANES - 1:SUBLANES, :]

    hc = lax.fori_loop(0, tm // SUBLANES, blk, hc_ref[0:1, :])
    hc_ref[0:1, :] = hc
    h = h_s[...]
    hlast_ref[0] = h[tm - SUBLANES:tm]
    gyb_ref[0] = gate_b * h


def _lru_step_kernel(x_ref, conv0_ref, h0_ref, w_ref, cw_ref, cb_ref, wa_ref, ba_ref, wi_ref,
                     bi_ref, lam_ref, ga_ref, gyb_ref, conv_ref, hlast_ref, *, steps, nseq):
    z = _dot(x_ref[...], w_ref[:, RWKV_COLS:])
    xb = z[:, 0:D_LRU]
    gb = z[:, D_LRU:2 * D_LRU]
    zga = z[:, 2 * D_LRU:2 * D_LRU + D_MODEL]
    zgb = z[:, 2 * D_LRU + D_MODEL:]
    n = steps * nseq
    hist = (CONV_W - 1) * nseq
    xext = jnp.concatenate([conv0_ref[...], xb], axis=0)

    def shifted(k):
        return xext[hist - k * nseq:hist - k * nseq + n]

    cw = cw_ref[...]
    u = (cb_ref[...] + cw[0:1] * shifted(3) + cw[1:2] * shifted(2) + cw[2:3] * shifted(1)
         + cw[3:4] * xb)
    conv_ref[...] = xext[n:n + hist]
    sp = _softplus(-lam_ref[...])
    ra, ia = _lru_gate_logits(u, wa_ref, wi_ref)
    a, mult, ig = _lru_coeffs(ra, ia, ba_ref[...], bi_ref[...], sp)
    bx = mult * ig * u
    h = h0_ref[...]
    hs = []
    for s in range(steps):
        h = a[s * nseq:(s + 1) * nseq] * h + bx[s * nseq:(s + 1) * nseq]
        hs.append(h)
    hlast_ref[...] = h
    hall = jnp.concatenate(hs, axis=0)
    ga_ref[...] = _sigmoid(zga)
    gyb_ref[...] = _sigmoid(zgb) * (hall * _gelu_tanh(gb))


def _lru_weight_specs():
    vec = _const_spec((1, D_LRU))
    gate_w = _const_spec((D_LRU // LRU_GROUP, LRU_GROUP, LRU_GROUP))
    return [_const_spec((D_MODEL, N_IN)), _const_spec((CONV_W, D_LRU)), vec,
            gate_w, vec, gate_w, vec, vec]


def _lru_weights(wp):
    return (wp["w_in"], wp["conv_w"], wp["conv_b"], wp["wa_bd"], wp["ba"], wp["wi_bd"],
            wp["bi"], wp["lam"])


def _lru_seq(x, wp, tm):
    b, t, _ = x.shape
    row_spec = pl.BlockSpec((1, tm, D_MODEL), lambda i, j: (i, j, 0))
    tail_spec = pl.BlockSpec((1, SUBLANES, D_LRU), lambda i, j: (i, 0, 0))
    return pl.pallas_call(
        _lru_seq_kernel,
        grid=(b, t // tm),
        in_specs=[row_spec] + _lru_weight_specs(),
        out_specs=[row_spec, row_spec, tail_spec, tail_spec],
        out_shape=[jax.ShapeDtypeStruct((b, t, D_MODEL), F32)] * 2
        + [jax.ShapeDtypeStruct((b, SUBLANES, D_LRU), F32)] * 2,
        scratch_shapes=[pltpu.VMEM((SUBLANES, D_LRU), F32), pltpu.VMEM((SUBLANES, D_LRU), F32),
                        pltpu.VMEM((tm, D_LRU), F32), pltpu.VMEM((tm, D_LRU), F32),
                        pltpu.VMEM((tm, D_LRU), F32)],
        compiler_params=pltpu.CompilerParams(
            dimension_semantics=("arbitrary", "arbitrary"), vmem_limit_bytes=VMEM_LIMIT),
        name="lru_seq",
    )(x, *_lru_weights(wp))


def _lru_step(x_tm, conv_tm, h0, wp, steps, nseq):
    n = steps * nseq
    hist = (CONV_W - 1) * nseq

    def full(r):
        return pl.BlockSpec((r, D_MODEL), lambda i: (0, 0))

    return pl.pallas_call(
        functools.partial(_lru_step_kernel, steps=steps, nseq=nseq),
        grid=(1,),
        in_specs=[full(n), full(hist), full(nseq)] + _lru_weight_specs(),
        out_specs=[full(n), full(n), full(hist), full(nseq)],
        out_shape=[jax.ShapeDtypeStruct((n, D_MODEL), F32)] * 2
        + [jax.ShapeDtypeStruct((hist, D_LRU), F32), jax.ShapeDtypeStruct((nseq, D_LRU), F32)],
        compiler_params=pltpu.CompilerParams(
            dimension_semantics=("arbitrary",), vmem_limit_bytes=VMEM_LIMIT),
        name="lru_step",
    )(x_tm, conv_tm, h0, *_lru_weights(wp))


def _wkv_consts(nseq):
    c = CHUNK
    seq_len = c // nseq
    lane = lax.broadcasted_iota(jnp.int32, (c, PAIR), 1)
    row = lax.broadcasted_iota(jnp.int32, (c, PAIR), 0)
    lane_lo = lane < HEAD
    col = lane % c
    same = (row // seq_len) == (col // seq_len)
    strict = jnp.logical_and(same, col < row)
    incl = jnp.logical_and(same, col <= row)
    eye = jnp.where(col == row, 1.0, 0.0).astype(F32)
    r3 = lax.broadcasted_iota(jnp.int32, (c, 3 * c), 0)
    c3 = lax.broadcasted_iota(jnp.int32, (c, 3 * c), 1) % c
    same3 = (r3 // seq_len) == (c3 // seq_len)
    tri3 = jnp.where(jnp.logical_and(same3, c3 <= r3), 1.0, 0.0).astype(BF16)
    tot3 = jnp.where(same3, 1.0, 0.0).astype(BF16)
    rp = lax.broadcasted_iota(jnp.int32, (PAIR, PAIR), 0)
    cp = lax.broadcasted_iota(jnp.int32, (PAIR, PAIR), 1)
    blockdiag = (rp // HEAD) == (cp // HEAD)
    return dict(lane_lo=lane_lo, tri3=tri3, tot3=tot3, strict=strict, incl=incl, eye=eye,
                blockdiag=blockdiag, seq_len=seq_len)


def _split3(x):
    hi = x.astype(BF16)
    rest = x - hi.astype(F32)
    mid = rest.astype(BF16)
    lo = (rest - mid.astype(F32)).astype(BF16)
    return jnp.concatenate([hi, mid, lo], axis=0)


def _split_heads(x, lane_lo):
    zero = jnp.zeros_like(x)
    return jnp.concatenate([jnp.where(lane_lo, x, zero), jnp.where(lane_lo, zero, x)], axis=0)


def _run_interleaved(stage_gens):
    results = [None] * len(stage_gens)
    live = list(range(len(stage_gens)))
    while live:
        still = []
        for idx in live:
            try:
                next(stage_gens[idx])
                still.append(idx)
            except StopIteration as done:
                results[idx] = done.value
        live = still
    return results


def _wkv_chunk(r, lw, k, v, al, be, get_states, cn, nseq):
    c = CHUNK
    lane_lo = cn["lane_lo"]
    lw3 = _split3(lw)
    cum = jnp.dot(cn["tri3"], lw3, preferred_element_type=F32)
    if nseq == 1:
        end = cum[c - 1:c, :]
    else:
        end = jnp.dot(cn["tot3"], lw3, preferred_element_type=F32)
    yield
    e_cum = jnp.exp(cum)
    e_neg = jnp.exp(-cum)
    e_end = jnp.exp(end)
    rt = r * e_cum
    at = al * jnp.exp(cum - lw)
    bt = be * e_neg
    kt = k * e_neg
    lhs2 = jnp.concatenate([at, rt], axis=0).astype(BF16)
    keys = jnp.concatenate([_split_heads(bt, lane_lo), _split_heads(kt, lane_lo)], axis=0)
    gram = _dot_nt(lhs2, keys)
    yield
    a_ab = jnp.where(cn["strict"], gram[0:c, 0:PAIR], 0.0)
    a_ak = jnp.where(cn["strict"], gram[0:c, PAIR:], 0.0)
    a_rb = jnp.where(cn["incl"], gram[c:, 0:PAIR], 0.0)
    a_rk = jnp.where(cn["incl"], gram[c:, PAIR:], 0.0)
    inv = cn["eye"] + a_ab
    n_iter = int(math.log2(cn["seq_len"])) - 1
    pw = _dot(a_ab, _split_heads(a_ab, lane_lo))
    yield
    for it in range(n_iter):
        pw_heads = _split_heads(pw, lane_lo)
        if it < n_iter - 1:
            both = _dot(jnp.concatenate([inv, pw], axis=0), pw_heads)
            inv = inv + both[0:c]
            pw = both[c:]
        else:
            inv = inv + _dot(inv, pw_heads)
        yield
    av = _dot(jnp.concatenate([a_ak, a_rk], axis=0), _split_heads(v, lane_lo))
    states = get_states()
    while states is None:
        yield
        states = get_states()
    if nseq == 1:
        from_state = _dot_nt(lhs2, states[0])
    else:
        seq_of_row = (lax.broadcasted_iota(jnp.int32, (2 * c, 1), 0) % c) // cn["seq_len"]
        from_state = jnp.zeros((2 * c, PAIR), F32)
        for s in range(nseq):
            from_state = jnp.where(seq_of_row == s, _dot_nt(lhs2, states[s]), from_state)
    yield
    u = _dot(inv, _split_heads(from_state[0:c] + av[0:c], lane_lo))
    yield
    y = from_state[c:] + av[c:] + _dot(a_rb, _split_heads(u, lane_lo))
    yield
    uv = jnp.concatenate([u, v], axis=0)
    bk = jnp.concatenate([bt * e_end, kt * e_end], axis=0)
    new_states = []
    if nseq == 1:
        inc = _dot_tn(uv, bk)
        new_states.append(states[0] * e_end + jnp.where(cn["blockdiag"], inc, 0.0))
    else:
        seq_of_uv = (lax.broadcasted_iota(jnp.int32, (2 * c, 1), 0) % c) // cn["seq_len"]
        for s in range(nseq):
            inc = _dot_tn(jnp.where(seq_of_uv == s, uv, 0.0), bk)
            decay = e_end[s * cn["seq_len"]:s * cn["seq_len"] + 1, :]
            new_states.append(states[s] * decay + jnp.where(cn["blockdiag"], inc, 0.0))
    return y, new_states


def _wkv_readout(y, r, k, v, g, rk, lnx_g, lnx_b, lane_lo):
    mean = _seg_sum(y, lane_lo) * (1.0 / HEAD)
    bonus = _seg_sum(r * k * rk, lane_lo) * v
    yield
    yc = y - mean
    var = _seg_sum(yc * yc, lane_lo) * (1.0 / HEAD)
    yield
    yn = yc * lax.rsqrt(var + GN_EPS) * lnx_g + lnx_b
    return (yn + bonus) * g


def _to_blockdiag(s0, s1):
    pad = jnp.zeros((HEAD, HEAD), F32)
    top = jnp.concatenate([s0, pad], axis=1)
    bottom = pltpu.roll(jnp.concatenate([s1, pad], axis=1), HEAD, 1)
    return jnp.concatenate([top, bottom], axis=0)


def _from_blockdiag(state):
    return state[0:HEAD, 0:HEAD], pltpu.roll(state[HEAD:], HEAD, 1)[:, 0:HEAD]


def _wkv_seq_kernel(r_ref, lw_ref, k_ref, v_ref, al_ref, be_ref, g_ref, rk_ref, lg_ref, lb_ref,
                    y_ref, s_out_ref, s_scr, yraw_scr):
    t = pl.program_id(2)
    cn = _wkv_consts(1)
    n_pairs = s_scr.shape[0]

    @pl.when(t == 0)
    def _():
        s_scr[...] = jnp.zeros_like(s_scr)
        yraw_scr[...] = jnp.zeros_like(yraw_scr)

    n_chunks = r_ref.shape[1] // CHUNK

    def readout_stages(p, rows):
        ln = pl.ds(p * PAIR, PAIR)
        out = yield from _wkv_readout(
            yraw_scr[:, ln], r_ref[0, rows, ln], k_ref[0, rows, ln], v_ref[0, rows, ln],
            g_ref[0, rows, ln], rk_ref[:, ln], lg_ref[:, ln], lb_ref[:, ln], cn["lane_lo"])
        y_ref[0, rows, ln] = out

    def chunk_stages(p, rows, get_states):
        ln = pl.ds(p * PAIR, PAIR)
        return _wkv_chunk(r_ref[0, rows, ln], lw_ref[0, rows, ln], k_ref[0, rows, ln],
                          v_ref[0, rows, ln], al_ref[0, rows, ln], be_ref[0, rows, ln],
                          get_states, cn, 1)

    def body(i, states):
        def rows_of(j):
            return pl.ds(pl.multiple_of((CHUNKS_IN_FLIGHT * i + j) * CHUNK, CHUNK), CHUNK)

        prev = pl.ds(pl.multiple_of(jnp.maximum(CHUNKS_IN_FLIGHT * i - 1, 0) * CHUNK, CHUNK),
                     CHUNK)
        end_states = [[None] * n_pairs for _ in range(CHUNKS_IN_FLIGHT)]

        def start_states(j, p):
            if j == 0:
                return [states[p]]
            return None if end_states[j - 1][p] is None else [end_states[j - 1][p]]

        def stages(j, p):
            ln = pl.ds(p * PAIR, PAIR)
            rows = rows_of(j)
            y, (state,) = yield from chunk_stages(p, rows, functools.partial(start_states, j, p))
            end_states[j][p] = state
            if j == CHUNKS_IN_FLIGHT - 1:
                yraw_scr[:, ln] = y
                return
            out = yield from _wkv_readout(
                y, r_ref[0, rows, ln], k_ref[0, rows, ln], v_ref[0, rows, ln],
                g_ref[0, rows, ln], rk_ref[:, ln], lg_ref[:, ln], lb_ref[:, ln], cn["lane_lo"])
            y_ref[0, rows, ln] = out

        _run_interleaved([readout_stages(p, prev) for p in range(n_pairs)]
                         + [stages(j, p) for j in range(CHUNKS_IN_FLIGHT) for p in range(n_pairs)])
        return tuple(end_states[CHUNKS_IN_FLIGHT - 1])

    assert n_chunks % CHUNKS_IN_FLIGHT == 0
    states = lax.fori_loop(0, n_chunks // CHUNKS_IN_FLIGHT, body,
                           tuple(s_scr[p] for p in range(n_pairs)))
    last = pl.ds((n_chunks - 1) * CHUNK, CHUNK)
    _run_interleaved([readout_stages(p, last) for p in range(n_pairs)])
    for p in range(n_pairs):
        s_scr[p] = states[p]
        s_out_ref[0, 2 * p], s_out_ref[0, 2 * p + 1] = _from_blockdiag(states[p])


def _wkv_step_kernel(r_ref, lw_ref, k_ref, v_ref, al_ref, be_ref, g_ref, rk_ref, lg_ref, lb_ref,
                     s_in_ref, y_ref, s_out_ref, *, nseq):
    cn = _wkv_consts(nseq)

    def pair_stages(p):
        ln = pl.ds(p * PAIR, PAIR)
        states = [_to_blockdiag(s_in_ref[s, 2 * p], s_in_ref[s, 2 * p + 1]) for s in range(nseq)]
        r, k, v = r_ref[:, ln], k_ref[:, ln], v_ref[:, ln]
        y, states = yield from _wkv_chunk(r, lw_ref[:, ln], k, v, al_ref[:, ln], be_ref[:, ln],
                                          lambda: states, cn, nseq)
        ya = yield from _wkv_readout(y, r, k, v, g_ref[:, ln], rk_ref[:, ln], lg_ref[:, ln],
                                     lb_ref[:, ln], cn["lane_lo"])
        y_ref[:, ln] = ya
        for s in range(nseq):
            s_out_ref[s, 2 * p], s_out_ref[s, 2 * p + 1] = _from_blockdiag(states[s])

    _run_interleaved([pair_stages(p) for p in range(s_in_ref.shape[1] // 2)])


def _wkv_seq(acts, wp, tt, n_pairs):
    b, t, _ = acts[0].shape
    width = n_pairs * PAIR
    row_spec = pl.BlockSpec((1, tt, width), lambda i, p, j: (i, j, p))
    vec = pl.BlockSpec((1, width), lambda i, p, j: (0, p))
    return pl.pallas_call(
        _wkv_seq_kernel,
        grid=(b, N_PAIRS // n_pairs, t // tt),
        in_specs=[row_spec] * 7 + [vec] * 3,
        out_specs=[row_spec,
                   pl.BlockSpec((1, 2 * n_pairs, HEAD, HEAD), lambda i, p, j: (i, p, 0, 0))],
        out_shape=[jax.ShapeDtypeStruct((b, t, D_MODEL), F32),
                   jax.ShapeDtypeStruct((b, N_HEADS, HEAD, HEAD), F32)],
        scratch_shapes=[pltpu.VMEM((n_pairs, PAIR, PAIR), F32), pltpu.VMEM((CHUNK, width), F32)],
        compiler_params=pltpu.CompilerParams(
            dimension_semantics=("arbitrary", "arbitrary", "arbitrary"),
            vmem_limit_bytes=VMEM_LIMIT),
        name="wkv_seq",
    )(*acts, wp["r_k"], wp["lnx_g"], wp["lnx_b"])


def _wkv_step(acts, s_heads, wp, steps, n_pairs):
    n = acts[0].shape[0]
    nseq = CHUNK // steps
    width = n_pairs * PAIR
    row_spec = pl.BlockSpec((CHUNK, width), lambda i, p: (i, p))
    vec = pl.BlockSpec((1, width), lambda i, p: (0, p))
    st_spec = pl.BlockSpec((nseq, 2 * n_pairs, HEAD, HEAD), lambda i, p: (i, p, 0, 0))
    return pl.pallas_call(
        functools.partial(_wkv_step_kernel, nseq=nseq),
        grid=(n // CHUNK, N_PAIRS // n_pairs),
        in_specs=[row_spec] * 7 + [vec] * 3 + [st_spec],
        out_specs=[row_spec, st_spec],
        out_shape=[jax.ShapeDtypeStruct((n, D_MODEL), F32),
                   jax.ShapeDtypeStruct(s_heads.shape, F32)],
        compiler_params=pltpu.CompilerParams(
            dimension_semantics=("arbitrary", "arbitrary"), vmem_limit_bytes=VMEM_LIMIT),
        name="wkv_step",
    )(*acts, wp["r_k"], wp["lnx_g"], wp["lnx_b"], s_heads)


def _post_kernel(x_ref, ya_ref, ga_ref, gyb_ref, wo_ref, l1g_ref, l1b_ref, wg_ref, wu_ref, wd_ref,
                 l2g_ref, l2b_ref, y_ref):
    x = x_ref[...]
    merged = ga_ref[...] * ya_ref[...] + gyb_ref[...]
    mix = _dot(merged, wo_ref[...])
    h1 = _layer_norm(ALPHA * x + mix, l1g_ref[...], l1b_ref[...])
    h1b = h1.astype(BF16)
    gate = jnp.dot(h1b, wg_ref[...], preferred_element_type=F32)
    up = jnp.dot(h1b, wu_ref[...], preferred_element_type=F32)
    act = (gate * _sigmoid(gate)) * up
    ffn = _dot(act, wd_ref[...])
    y_ref[...] = _layer_norm(ALPHA * h1 + ffn, l2g_ref[...], l2b_ref[...])


def _post(x, ya, ga, gyb, wp, tm):
    n = x.shape[0]
    row_spec = pl.BlockSpec((tm, D_MODEL), lambda i: (i, 0))
    vec = _const_spec((1, D_MODEL))
    return pl.pallas_call(
        _post_kernel,
        grid=(n // tm,),
        in_specs=[row_spec] * 4 + [_const_spec((D_MODEL, D_MODEL)), vec, vec,
                                   _const_spec((D_MODEL, D_FF)), _const_spec((D_MODEL, D_FF)),
                                   _const_spec((D_FF, D_MODEL)), vec, vec],
        out_specs=row_spec,
        out_shape=jax.ShapeDtypeStruct((n, D_MODEL), F32),
        compiler_params=pltpu.CompilerParams(
            dimension_semantics=("arbitrary",), vmem_limit_bytes=VMEM_LIMIT),
        name="post",
    )(x, ya, ga, gyb, wp["w_o"], wp["ln1_g"], wp["ln1_b"], wp["w_gate"], wp["w_up"], wp["w_down"],
      wp["ln2_g"], wp["ln2_b"])


def _prep_weights(w_in, tmix_mu, w0, w2_decay, a0, a2_iclr, g2_gate, k_k, k_a, r_k, lnx_g, lnx_b,
                  conv_w, conv_b, lru_wa, lru_ba, lru_wi, lru_bi, lru_lambda, w_o,
                  ln1_g, ln1_b, w_ffn_gate, w_ffn_up, w_ffn_down, ln2_g, ln2_b):
    row = lambda v: v.reshape(1, -1).astype(F32)
    zeros = lambda r: jnp.zeros((r, D_MODEL), F32)
    w2ext = jnp.concatenate([
        jnp.concatenate([w2_decay, zeros(DECAY_LORA), zeros(DECAY_LORA)], axis=1),
        jnp.concatenate([zeros(ICLR_LORA), a2_iclr, zeros(ICLR_LORA)], axis=1),
        jnp.concatenate([zeros(GATE_LORA), zeros(GATE_LORA), g2_gate], axis=1)], axis=0)

    def gate_blockdiag(w):
        per = LRU_GROUP // LRU_BS
        w4 = w.reshape(LRU_BLOCKS // per, per, LRU_BS, LRU_BS)
        eye = jnp.eye(per, dtype=w.dtype)
        bd = jnp.einsum("gpcd,pq->gpcqd", w4, eye)
        return bd.reshape(LRU_BLOCKS // per, LRU_GROUP, LRU_GROUP).astype(BF16)

    return dict(
        w_in=w_in.astype(BF16),
        mu=row(tmix_mu), w2ext=w2ext.astype(BF16), w0=row(w0), a0=row(a0), k_k=row(k_k),
        k_a=row(k_a), r_k=row(r_k), lnx_g=row(lnx_g), lnx_b=row(lnx_b),
        conv_w=conv_w.astype(F32), conv_b=row(conv_b), wa_bd=gate_blockdiag(lru_wa),
        ba=row(lru_ba), wi_bd=gate_blockdiag(lru_wi), bi=row(lru_bi), lam=row(lru_lambda),
        w_o=w_o.astype(BF16), ln1_g=row(ln1_g), ln1_b=row(ln1_b), w_gate=w_ffn_gate.astype(BF16),
        w_up=w_ffn_up.astype(BF16), w_down=w_ffn_down.astype(BF16), ln2_g=row(ln2_g),
        ln2_b=row(ln2_b))


def _prompt_layer(x, wp, tm, tt, tm_post, n_pairs):
    b, t, _ = x.shape
    acts = _rwkv_proj_seq(x, wp, tm)
    ya, s_heads = _wkv_seq(acts, wp, tt, n_pairs)
    ga, gyb, conv_tail, h_tail = _lru_seq(x, wp, tm)
    y = _post(x.reshape(b * t, D_MODEL), ya.reshape(b * t, D_MODEL), ga.reshape(b * t, D_MODEL),
              gyb.reshape(b * t, D_MODEL), wp, tm_post).reshape(b, t, D_MODEL)
    return (y, x[:, -1], s_heads, conv_tail[:, SUBLANES - (CONV_W - 1):],
            h_tail[:, SUBLANES - 1])


def _sample_layer(x, shift_buf, wkv0, conv_buf, h0, wp):
    b, t, _ = x.shape
    n = b * t
    xf = x.reshape(n, D_MODEL)
    xprev = jnp.concatenate([shift_buf[:, None], x[:, :-1]], axis=1).reshape(n, D_MODEL)
    acts = _rwkv_proj_step(xf, xprev, wp)
    ya, s_heads = _wkv_step(acts, wkv0, wp, t, 4)
    x_tm = x.transpose(1, 0, 2).reshape(n, D_MODEL)
    conv_tm = conv_buf.transpose(1, 0, 2).reshape((CONV_W - 1) * b, D_LRU)
    ga_tm, gyb_tm, conv_new_tm, h_last = _lru_step(x_tm, conv_tm, h0, wp, t, b)
    to_seq_major = lambda v: v.reshape(t, b, D_MODEL).transpose(1, 0, 2).reshape(n, D_MODEL)
    y = _post(xf, ya, to_seq_major(ga_tm), to_seq_major(gyb_tm), wp, min(n, 256)).reshape(
        b, t, D_MODEL)
    conv_new = conv_new_tm.reshape(CONV_W - 1, b, D_LRU).transpose(1, 0, 2)
    return y, x[:, -1], s_heads, conv_new, h_last


def kernel(x_prompt, x_sample, state_shift, state_wkv, state_conv, state_lru, w_in, tmix_mu, w0, w2_decay, a0, a2_iclr, g2_gate, k_k, k_a, r_k, lnx_g, lnx_b, conv_w, conv_b, lru_wa, lru_ba, lru_wi, lru_bi, lru_lambda, w_o, ln1_g, ln1_b, w_ffn_gate, w_ffn_up, w_ffn_down, ln2_g, ln2_b):
    params = (w_in, tmix_mu, w0, w2_decay, a0, a2_iclr, g2_gate, k_k, k_a, r_k, lnx_g, lnx_b,
              conv_w, conv_b, lru_wa, lru_ba, lru_wi, lru_bi, lru_lambda, w_o,
              ln1_g, ln1_b, w_ffn_gate, w_ffn_up, w_ffn_down, ln2_g, ln2_b)
    wp = _prep_weights(*[p[0] for p in params])
    yp, sh_p, wkv_p, conv_p, lru_p = _prompt_layer(x_prompt, wp, 256, 512, 256, 8)
    ys, sh_s, wkv_s, conv_s, lru_s = _sample_layer(
        x_sample, state_shift[0], state_wkv[0], state_conv[0], state_lru[0], wp)
    return (yp, ys, sh_p[None], wkv_p[None], conv_p[None], lru_p[None],
            sh_s[None], wkv_s[None], conv_s[None], lru_s[None])
```

```python
import functools
import math

import jax
import jax.numpy as jnp
from jax import lax
from jax.experimental import pallas as pl
from jax.experimental.pallas import tpu as pltpu

F32 = jnp.float32
BF16 = jnp.bfloat16

D_MODEL = 1024
HEAD = 64
N_HEADS = D_MODEL // HEAD
PAIR = 2 * HEAD
N_PAIRS = N_HEADS // 2
DECAY_LORA = 64
ICLR_LORA = 64
GATE_LORA = 128
LORA = DECAY_LORA + ICLR_LORA + GATE_LORA
RWKV_COLS = 3 * D_MODEL + LORA
GN_EPS = HEAD * 1e-5
D_LRU = D_MODEL
LRU_BLOCKS = 16
LRU_BS = D_LRU // LRU_BLOCKS
LRU_GROUP = 256
CONV_W = 4
LRU_C = 8.0
LRU_COLS = 2 * D_LRU + 2 * D_MODEL
N_IN = RWKV_COLS + LRU_COLS
D_FF = 2816
ALPHA = 2.0 ** 0.25
LN_EPS = 1e-5

SUBLANES = 8
CHUNK = 64
CHUNKS_IN_FLIGHT = 2
VMEM_LIMIT = 56 * 1024 * 1024


def _softplus(x):
    return jnp.maximum(x, 0.0) + jnp.log1p(jnp.exp(-jnp.abs(x)))


def _softplus_plain(x):
    return jnp.maximum(x, 0.0) + jnp.log(1.0 + jnp.exp(-jnp.abs(x)))


def _sigmoid(x):
    return 0.5 * jnp.tanh(0.5 * x) + 0.5


def _gelu_tanh(x):
    c = math.sqrt(2.0 / math.pi)
    return x * (0.5 * (1.0 + jnp.tanh(c * (x + 0.044715 * (x * x * x)))))


def _layer_norm(x, g, b):
    mu = jnp.mean(x, axis=-1, keepdims=True)
    xc = x - mu
    var = jnp.mean(xc * xc, axis=-1, keepdims=True)
    return xc * lax.rsqrt(var + LN_EPS) * g + b


def _dot(a, b):
    return jnp.dot(a.astype(BF16), b.astype(BF16), preferred_element_type=F32)


def _dot_nt(a, b):
    return lax.dot_general(a.astype(BF16), b.astype(BF16), (((1,), (1,)), ((), ())),
                           preferred_element_type=F32)


def _dot_tn(a, b):
    return lax.dot_general(a.astype(BF16), b.astype(BF16), (((0,), (0,)), ((), ())),
                           preferred_element_type=F32)


def _seg_sum(x, lane_lo):
    s0 = jnp.sum(jnp.where(lane_lo, x, 0.0), axis=-1, keepdims=True)
    s1 = jnp.sum(jnp.where(lane_lo, 0.0, x), axis=-1, keepdims=True)
    return jnp.where(lane_lo, s0, s1)


def _head_sums(x):
    lane_lo = lax.broadcasted_iota(jnp.int32, (x.shape[0], PAIR), 1) < HEAD
    return jnp.concatenate([_seg_sum(x[:, p * PAIR:(p + 1) * PAIR], lane_lo)
                            for p in range(N_PAIRS)], axis=1)


def _rwkv_prep(mixed, w2ext, w0, a0, k_k, k_a):
    lo = mixed(3 * D_MODEL, LORA)
    lane = lax.broadcasted_iota(jnp.int32, lo.shape, 1)
    act = jnp.where(lane < DECAY_LORA, jnp.tanh(lo),
                    jnp.where(lane < DECAY_LORA + ICLR_LORA, lo, _sigmoid(lo)))
    r = mixed(0, D_MODEL)
    lora = _dot(act, w2ext)
    k = mixed(D_MODEL, D_MODEL)
    w = -_softplus_plain(-(w0 + lora[:, 0:D_MODEL])) - 0.5
    lw = -jnp.exp(w)
    a = _sigmoid(a0 + lora[:, D_MODEL:2 * D_MODEL])
    g = lora[:, 2 * D_MODEL:3 * D_MODEL]
    v = mixed(2 * D_MODEL, D_MODEL)
    kkraw = k * k_k
    kk = kkraw * lax.rsqrt(jnp.maximum(_head_sums(kkraw * kkraw), 1e-24))
    k2 = k * (1.0 + (a - 1.0) * k_a)
    return r, lw, k2, v, -kk, kk * a, g


def _rwkv_project_rows(x, w_ref, mu_ref, w2_ref, w0_ref, a0_ref, kk_ref, ka_ref, hist_ref):
    xbf = x.astype(BF16)
    tm = xbf.shape[0]
    row = lax.broadcasted_iota(jnp.int32, (SUBLANES, 1), 0)

    def mixed(first_col, n_cols):
        cols = slice(first_col, first_col + n_cols)
        z = jnp.dot(xbf, w_ref[:, cols], preferred_element_type=F32)
        rolled = pltpu.roll(z, 1, 0)
        first = jnp.where(row == 0, hist_ref[SUBLANES - 1:SUBLANES, cols], rolled[0:SUBLANES])
        zprev = jnp.concatenate([first, rolled[SUBLANES:]], axis=0)
        hist_ref[:, cols] = z[tm - SUBLANES:tm]
        return z + mu_ref[:, cols] * (zprev - z)

    return _rwkv_prep(mixed, w2_ref[...], w0_ref[...], a0_ref[...], kk_ref[...], ka_ref[...])


def _rwkv_proj_step_kernel(x_ref, xp_ref, w_ref, mu_ref, w2_ref, w0_ref, a0_ref, kk_ref, ka_ref,
                           r_ref, lw_ref, k_ref, v_ref, al_ref, be_ref, g_ref):
    xbf = x_ref[...].astype(BF16)
    xpbf = xp_ref[...].astype(BF16)

    def mixed(first_col, n_cols):
        cols = slice(first_col, first_col + n_cols)
        z = jnp.dot(xbf, w_ref[:, cols], preferred_element_type=F32)
        zprev = jnp.dot(xpbf, w_ref[:, cols], preferred_element_type=F32)
        return z + mu_ref[:, cols] * (zprev - z)

    outs = _rwkv_prep(mixed, w2_ref[...], w0_ref[...], a0_ref[...], kk_ref[...], ka_ref[...])
    for o_ref, o in zip((r_ref, lw_ref, k_ref, v_ref, al_ref, be_ref, g_ref), outs):
        o_ref[...] = o


def _const_spec(shape):
    nd = len(shape)
    return pl.BlockSpec(shape, lambda *_: (0,) * nd, pipeline_mode=pl.Buffered(1))


def _rwkv_proj_step(x, xprev, wp):
    n = x.shape[0]
    full = pl.BlockSpec((n, D_MODEL), lambda i: (0, 0))
    vec = _const_spec((1, D_MODEL))
    return pl.pallas_call(
        _rwkv_proj_step_kernel,
        grid=(1,),
        in_specs=[full, full, _const_spec((D_MODEL, RWKV_COLS)), _const_spec((1, RWKV_COLS)),
                  _const_spec((LORA, 3 * D_MODEL)), vec, vec, vec, vec],
        out_specs=[full] * 7,
        out_shape=[jax.ShapeDtypeStruct((n, D_MODEL), F32)] * 7,
        compiler_params=pltpu.CompilerParams(
            dimension_semantics=("arbitrary",), vmem_limit_bytes=VMEM_LIMIT),
        name="rwkv_proj_step",
    )(x, xprev, wp["w_in"], wp["mu"], wp["w2ext"], wp["w0"], wp["a0"], wp["k_k"], wp["k_a"])


def _lru_gate_logits(u, wa_ref, wi_ref):
    ub = u.astype(BF16)
    ra, ia = [], []
    for q in range(D_LRU // LRU_GROUP):
        uq = ub[:, q * LRU_GROUP:(q + 1) * LRU_GROUP]
        ra.append(jnp.dot(uq, wa_ref[q], preferred_element_type=F32))
        ia.append(jnp.dot(uq, wi_ref[q], preferred_element_type=F32))
    return jnp.concatenate(ra, axis=1), jnp.concatenate(ia, axis=1)


def _lru_coeffs(ra, ia, ba, bi, sp):
    rg = _sigmoid(ra + ba)
    ig = _sigmoid(ia + bi)
    log_a = -LRU_C * rg * sp
    a = jnp.exp(log_a)
    m2 = -jnp.tanh(log_a) * (a * a + 1.0)
    mult = jnp.where(m2 > 0.0, m2 * lax.rsqrt(m2), 0.0)
    return a, mult, ig


def _lru_seq_kernel(x_ref, w_ref, cw_ref, cb_ref, wa_ref, ba_ref, wi_ref, bi_ref, lam_ref,
                    ga_ref, gyb_ref, conv_ref, hlast_ref,
                    xhist_ref, hc_ref, a_s, bx_s, h_s):
    t = pl.program_id(1)

    @pl.when(t == 0)
    def _():
        xhist_ref[...] = jnp.zeros_like(xhist_ref)
        hc_ref[...] = jnp.zeros_like(hc_ref)

    xbf = x_ref[0].astype(BF16)
    tm = xbf.shape[0]

    def proj(first_col, n_cols):
        cols = slice(RWKV_COLS + first_col, RWKV_COLS + first_col + n_cols)
        return jnp.dot(xbf, w_ref[:, cols], preferred_element_type=F32)

    xb = proj(0, D_LRU)
    gb = proj(D_LRU, D_LRU)
    row8 = lax.broadcasted_iota(jnp.int32, (SUBLANES, 1), 0)
    hist = xhist_ref[...]

    def shifted(k):
        rolled = pltpu.roll(xb, k, 0)
        first = jnp.where(row8 < k, pltpu.roll(hist, k, 0), rolled[0:SUBLANES])
        return jnp.concatenate([first, rolled[SUBLANES:]], axis=0)

    cw = cw_ref[...]
    u = (cb_ref[...] + cw[0:1] * shifted(3) + cw[1:2] * shifted(2) + cw[2:3] * shifted(1)
         + cw[3:4] * xb)
    last8 = xb[tm - SUBLANES:tm]
    xhist_ref[...] = last8
    conv_ref[0] = last8

    ra, ia = _lru_gate_logits(u, wa_ref, wi_ref)
    gelu_gb = _gelu_tanh(gb)
    zga = proj(2 * D_LRU, D_MODEL)
    sp = _softplus(-lam_ref[...])
    a, mult, ig = _lru_coeffs(ra, ia, ba_ref[...], bi_ref[...], sp)
    row = lax.broadcasted_iota(jnp.int32, (tm, 1), 0)
    mult = jnp.where(jnp.logical_and(row == 0, t == 0), 1.0, mult)
    a_s[...] = a
    bx_s[...] = mult * ig * u
    ga_ref[0] = _sigmoid(zga)
    zgb = proj(2 * D_LRU + D_MODEL, D_MODEL)

    hc = hc_ref[0:1, :]
    for blk in range(tm // SUBLANES):
        rows = slice(blk * SUBLANES, (blk + 1) * SUBLANES)
        av = a_s[rows, :]
        bv = bx_s[rows, :]
        for d in (1, 2, 4):
            a_sh = jnp.where(row8 < d, 1.0, pltpu.roll(av, d, 0))
            b_sh = jnp.where(row8 < d, 0.0, pltpu.roll(bv, d, 0))
            bv = av * b_sh + bv
            av = av * a_sh
        h_blk = bv + av * hc
        h_s[rows, :] = h_blk
        hc = h_blk[SUBLANES - 1:SUBLANES, :]
    hc_ref[0:1, :] = hc
    h = h_s[...]
    hlast_ref[0] = h[tm - SUBLANES:tm]
    gyb_ref[0] = (_sigmoid(zgb) * gelu_gb) * h


def _lru_step_kernel(x_ref, conv0_ref, h0_ref, w_ref, cw_ref, cb_ref, wa_ref, ba_ref, wi_ref,
                     bi_ref, lam_ref, ga_ref, gyb_ref, conv_ref, hlast_ref, *, steps, nseq):
    z = _dot(x_ref[...], w_ref[:, RWKV_COLS:])
    xb = z[:, 0:D_LRU]
    gb = z[:, D_LRU:2 * D_LRU]
    zga = z[:, 2 * D_LRU:2 * D_LRU + D_MODEL]
    zgb = z[:, 2 * D_LRU + D_MODEL:]
    n = steps * nseq
    hist = (CONV_W - 1) * nseq
    xext = jnp.concatenate([conv0_ref[...], xb], axis=0)

    def shifted(k):
        return xext[hist - k * nseq:hist - k * nseq + n]

    cw = cw_ref[...]
    u = (cb_ref[...] + cw[0:1] * shifted(3) + cw[1:2] * shifted(2) + cw[2:3] * shifted(1)
         + cw[3:4] * xb)
    conv_ref[...] = xext[n:n + hist]
    sp = _softplus(-lam_ref[...])
    ra, ia = _lru_gate_logits(u, wa_ref, wi_ref)
    a, mult, ig = _lru_coeffs(ra, ia, ba_ref[...], bi_ref[...], sp)
    bx = mult * ig * u
    h = h0_ref[...]
    hs = []
    for s in range(steps):
        h = a[s * nseq:(s + 1) * nseq] * h + bx[s * nseq:(s + 1) * nseq]
        hs.append(h)
    hlast_ref[...] = h
    hall = jnp.concatenate(hs, axis=0)
    ga_ref[...] = _sigmoid(zga)
    gyb_ref[...] = _sigmoid(zgb) * (hall * _gelu_tanh(gb))


def _lru_weight_specs():
    vec = _const_spec((1, D_LRU))
    gate_w = _const_spec((D_LRU // LRU_GROUP, LRU_GROUP, LRU_GROUP))
    return [_const_spec((D_MODEL, N_IN)), _const_spec((CONV_W, D_LRU)), vec,
            gate_w, vec, gate_w, vec, vec]


def _lru_weights(wp):
    return (wp["w_in"], wp["conv_w"], wp["conv_b"], wp["wa_bd"], wp["ba"], wp["wi_bd"],
            wp["bi"], wp["lam"])


def _lru_seq(x, wp, tm):
    b, t, _ = x.shape
    row_spec = pl.BlockSpec((1, tm, D_MODEL), lambda i, j: (i, j, 0))
    tail_spec = pl.BlockSpec((1, SUBLANES, D_LRU), lambda i, j: (i, 0, 0))
    return pl.pallas_call(
        _lru_seq_kernel,
        grid=(b, t // tm),
        in_specs=[row_spec] + _lru_weight_specs(),
        out_specs=[row_spec, row_spec, tail_spec, tail_spec],
        out_shape=[jax.ShapeDtypeStruct((b, t, D_MODEL), F32)] * 2
        + [jax.ShapeDtypeStruct((b, SUBLANES, D_LRU), F32)] * 2,
        scratch_shapes=[pltpu.VMEM((SUBLANES, D_LRU), F32), pltpu.VMEM((SUBLANES, D_LRU), F32),
                        pltpu.VMEM((tm, D_LRU), F32), pltpu.VMEM((tm, D_LRU), F32),
                        pltpu.VMEM((tm, D_LRU), F32)],
        compiler_params=pltpu.CompilerParams(
            dimension_semantics=("arbitrary", "arbitrary"), vmem_limit_bytes=VMEM_LIMIT),
        name="lru_seq",
    )(x, *_lru_weights(wp))


def _lru_step(x_tm, conv_tm, h0, wp, steps, nseq):
    n = steps * nseq
    hist = (CONV_W - 1) * nseq

    def full(r):
        return pl.BlockSpec((r, D_MODEL), lambda i: (0, 0))

    return pl.pallas_call(
        functools.partial(_lru_step_kernel, steps=steps, nseq=nseq),
        grid=(1,),
        in_specs=[full(n), full(hist), full(nseq)] + _lru_weight_specs(),
        out_specs=[full(n), full(n), full(hist), full(nseq)],
        out_shape=[jax.ShapeDtypeStruct((n, D_MODEL), F32)] * 2
        + [jax.ShapeDtypeStruct((hist, D_LRU), F32), jax.ShapeDtypeStruct((nseq, D_LRU), F32)],
        compiler_params=pltpu.CompilerParams(
            dimension_semantics=("arbitrary",), vmem_limit_bytes=VMEM_LIMIT),
        name="lru_step",
    )(x_tm, conv_tm, h0, *_lru_weights(wp))


def _wkv_consts(nseq):
    c = CHUNK
    seq_len = c // nseq
    lane = lax.broadcasted_iota(jnp.int32, (c, PAIR), 1)
    row = lax.broadcasted_iota(jnp.int32, (c, PAIR), 0)
    lane_lo = lane < HEAD
    col = lane % c
    same = (row // seq_len) == (col // seq_len)
    strict = jnp.logical_and(same, col < row)
    incl = jnp.logical_and(same, col <= row)
    eye = jnp.where(col == row, 1.0, 0.0).astype(F32)
    r3 = lax.broadcasted_iota(jnp.int32, (c, 3 * c), 0)
    c3 = lax.broadcasted_iota(jnp.int32, (c, 3 * c), 1) % c
    same3 = (r3 // seq_len) == (c3 // seq_len)
    tri3 = jnp.where(jnp.logical_and(same3, c3 <= r3), 1.0, 0.0).astype(BF16)
    tot3 = jnp.where(same3, 1.0, 0.0).astype(BF16)
    rp = lax.broadcasted_iota(jnp.int32, (PAIR, PAIR), 0)
    cp = lax.broadcasted_iota(jnp.int32, (PAIR, PAIR), 1)
    blockdiag = (rp // HEAD) == (cp // HEAD)
    return dict(lane_lo=lane_lo, tri3=tri3, tot3=tot3, strict=strict, incl=incl, eye=eye,
                blockdiag=blockdiag, seq_len=seq_len)


def _split3(x):
    hi = x.astype(BF16)
    rest = x - hi.astype(F32)
    mid = rest.astype(BF16)
    lo = (rest - mid.astype(F32)).astype(BF16)
    return jnp.concatenate([hi, mid, lo], axis=0)


def _split_heads(x, lane_lo):
    zero = jnp.zeros_like(x)
    return jnp.concatenate([jnp.where(lane_lo, x, zero), jnp.where(lane_lo, zero, x)], axis=0)


def _run_interleaved(stage_gens):
    results = [None] * len(stage_gens)
    live = list(range(len(stage_gens)))
    while live:
        still = []
        for idx in live:
            try:
                next(stage_gens[idx])
                still.append(idx)
            except StopIteration as done:
                results[idx] = done.value
        live = still
    return results


def _wkv_chunk(r, lw, k, v, al, be, get_states, cn, nseq):
    c = CHUNK
    lane_lo = cn["lane_lo"]
    lw3 = _split3(lw)
    cum = jnp.dot(cn["tri3"], lw3, preferred_element_type=F32)
    if nseq == 1:
        end = cum[c - 1:c, :]
    else:
        end = jnp.dot(cn["tot3"], lw3, preferred_element_type=F32)
    yield
    e_cum = jnp.exp(cum)
    e_neg = jnp.exp(-cum)
    e_end = jnp.exp(end)
    rt = r * e_cum
    at = al * jnp.exp(cum - lw)
    bt = be * e_neg
    kt = k * e_neg
    lhs2 = jnp.concatenate([at, rt], axis=0).astype(BF16)
    keys = jnp.concatenate([_split_heads(bt, lane_lo), _split_heads(kt, lane_lo)], axis=0)
    gram = _dot_nt(lhs2, keys)
    yield
    a_ab = jnp.where(cn["strict"], gram[0:c, 0:PAIR], 0.0)
    a_ak = jnp.where(cn["strict"], gram[0:c, PAIR:], 0.0)
    a_rb = jnp.where(cn["incl"], gram[c:, 0:PAIR], 0.0)
    a_rk = jnp.where(cn["incl"], gram[c:, PAIR:], 0.0)
    inv = cn["eye"] + a_ab
    n_iter = int(math.log2(cn["seq_len"])) - 1
    pw = _dot(a_ab, _split_heads(a_ab, lane_lo))
    yield
    for it in range(n_iter):
        pw_heads = _split_heads(pw, lane_lo)
        if it < n_iter - 1:
            both = _dot(jnp.concatenate([inv, pw], axis=0), pw_heads)
            inv = inv + both[0:c]
            pw = both[c:]
        else:
            inv = inv + _dot(inv, pw_heads)
        yield
    av = _dot(jnp.concatenate([a_ak, a_rk], axis=0), _split_heads(v, lane_lo))
    states = get_states()
    while states is None:
        yield
        states = get_states()
    if nseq == 1:
        from_state = _dot_nt(lhs2, states[0])
    else:
        seq_of_row = (lax.broadcasted_iota(jnp.int32, (2 * c, 1), 0) % c) // cn["seq_len"]
        from_state = jnp.zeros((2 * c, PAIR), F32)
        for s in range(nseq):
            from_state = jnp.where(seq_of_row == s, _dot_nt(lhs2, states[s]), from_state)
    yield
    u = _dot(inv, _split_heads(from_state[0:c] + av[0:c], lane_lo))
    yield
    y = from_state[c:] + av[c:] + _dot(a_rb, _split_heads(u, lane_lo))
    yield
    uv = jnp.concatenate([u, v], axis=0)
    bk = jnp.concatenate([bt * e_end, kt * e_end], axis=0)
    new_states = []
    if nseq == 1:
        inc = _dot_tn(uv, bk)
        new_states.append(states[0] * e_end + jnp.where(cn["blockdiag"], inc, 0.0))
    else:
        seq_of_uv = (lax.broadcasted_iota(jnp.int32, (2 * c, 1), 0) % c) // cn["seq_len"]
        for s in range(nseq):
            inc = _dot_tn(jnp.where(seq_of_uv == s, uv, 0.0), bk)
            decay = e_end[s * cn["seq_len"]:s * cn["seq_len"] + 1, :]
            new_states.append(states[s] * decay + jnp.where(cn["blockdiag"], inc, 0.0))
    return y, new_states


def _wkv_readout(y, r, k, v, g, rk, lnx_g, lnx_b, lane_lo):
    mean = _seg_sum(y, lane_lo) * (1.0 / HEAD)
    bonus = _seg_sum(r * k * rk, lane_lo) * v
    yield
    yc = y - mean
    var = _seg_sum(yc * yc, lane_lo) * (1.0 / HEAD)
    yield
    yn = yc * lax.rsqrt(var + GN_EPS) * lnx_g + lnx_b
    return (yn + bonus) * g


def _to_blockdiag(s0, s1):
    pad = jnp.zeros((HEAD, HEAD), F32)
    top = jnp.concatenate([s0, pad], axis=1)
    bottom = pltpu.roll(jnp.concatenate([s1, pad], axis=1), HEAD, 1)
    return jnp.concatenate([top, bottom], axis=0)


def _from_blockdiag(state):
    return state[0:HEAD, 0:HEAD], pltpu.roll(state[HEAD:], HEAD, 1)[:, 0:HEAD]


def _rwkv_seq_kernel(x_ref, w_ref, mu_ref, w2_ref, w0_ref, a0_ref, kk_ref, ka_ref, rk_ref, lg_ref,
                     lb_ref, y_ref, s_out_ref, hist_ref, acts_scr, s_scr, yraw_scr, *, tm):
    t = pl.program_id(1)
    cn = _wkv_consts(1)
    n_pairs = s_scr.shape[0]
    tt = x_ref.shape[1]

    @pl.when(t == 0)
    def _():
        hist_ref[...] = jnp.zeros_like(hist_ref)
        s_scr[...] = jnp.zeros_like(s_scr)
        yraw_scr[...] = jnp.zeros_like(yraw_scr)

    for sub in range(tt // tm):
        rows = slice(sub * tm, (sub + 1) * tm)
        outs = _rwkv_project_rows(x_ref[0, rows, :], w_ref, mu_ref, w2_ref, w0_ref, a0_ref,
                                  kk_ref, ka_ref, hist_ref)
        for idx, o in enumerate(outs):
            acts_scr[idx, rows, :] = o

    r_ref, lw_ref, k_ref, v_ref, al_ref, be_ref, g_ref = (acts_scr.at[i] for i in range(7))
    n_chunks = tt // CHUNK

    def readout_stages(p, rows):
        ln = pl.ds(p * PAIR, PAIR)
        out = yield from _wkv_readout(
            yraw_scr[:, ln], r_ref[rows, ln], k_ref[rows, ln], v_ref[rows, ln],
            g_ref[rows, ln], rk_ref[:, ln], lg_ref[:, ln], lb_ref[:, ln], cn["lane_lo"])
        y_ref[0, rows, ln] = out

    def chunk_stages(p, rows, get_states):
        ln = pl.ds(p * PAIR, PAIR)
        return _wkv_chunk(r_ref[rows, ln], lw_ref[rows, ln], k_ref[rows, ln],
                          v_ref[rows, ln], al_ref[rows, ln], be_ref[rows, ln],
                          get_states, cn, 1)

    def body(i, states):
        def rows_of(j):
            return pl.ds(pl.multiple_of((CHUNKS_IN_FLIGHT * i + j) * CHUNK, CHUNK), CHUNK)

        prev = pl.ds(pl.multiple_of(jnp.maximum(CHUNKS_IN_FLIGHT * i - 1, 0) * CHUNK, CHUNK),
                     CHUNK)
        end_states = [[None] * n_pairs for _ in range(CHUNKS_IN_FLIGHT)]

        def start_states(j, p):
            if j == 0:
                return [states[p]]
            return None if end_states[j - 1][p] is None else [end_states[j - 1][p]]

        def stages(j, p):
            ln = pl.ds(p * PAIR, PAIR)
            rows = rows_of(j)
            y, (state,) = yield from chunk_stages(p, rows, functools.partial(start_states, j, p))
            end_states[j][p] = state
            if j == CHUNKS_IN_FLIGHT - 1:
                yraw_scr[:, ln] = y
                return
            out = yield from _wkv_readout(
                y, r_ref[rows, ln], k_ref[rows, ln], v_ref[rows, ln],
                g_ref[rows, ln], rk_ref[:, ln], lg_ref[:, ln], lb_ref[:, ln], cn["lane_lo"])
            y_ref[0, rows, ln] = out

        _run_interleaved([readout_stages(p, prev) for p in range(n_pairs)]
                         + [stages(j, p) for j in range(CHUNKS_IN_FLIGHT) for p in range(n_pairs)])
        return tuple(end_states[CHUNKS_IN_FLIGHT - 1])

    assert n_chunks % CHUNKS_IN_FLIGHT == 0
    states = lax.fori_loop(0, n_chunks // CHUNKS_IN_FLIGHT, body,
                           tuple(s_scr[p] for p in range(n_pairs)))
    last = pl.ds((n_chunks - 1) * CHUNK, CHUNK)
    _run_interleaved([readout_stages(p, last) for p in range(n_pairs)])
    for p in range(n_pairs):
        s_scr[p] = states[p]
        s_out_ref[0, 2 * p], s_out_ref[0, 2 * p + 1] = _from_blockdiag(states[p])


def _wkv_step_kernel(r_ref, lw_ref, k_ref, v_ref, al_ref, be_ref, g_ref, rk_ref, lg_ref, lb_ref,
                     s_in_ref, y_ref, s_out_ref, *, nseq):
    cn = _wkv_consts(nseq)

    def pair_stages(p):
        ln = pl.ds(p * PAIR, PAIR)
        states = [_to_blockdiag(s_in_ref[s, 2 * p], s_in_ref[s, 2 * p + 1]) for s in range(nseq)]
        r, k, v = r_ref[:, ln], k_ref[:, ln], v_ref[:, ln]
        y, states = yield from _wkv_chunk(r, lw_ref[:, ln], k, v, al_ref[:, ln], be_ref[:, ln],
                                          lambda: states, cn, nseq)
        ya = yield from _wkv_readout(y, r, k, v, g_ref[:, ln], rk_ref[:, ln], lg_ref[:, ln],
                                     lb_ref[:, ln], cn["lane_lo"])
        y_ref[:, ln] = ya
        for s in range(nseq):
            s_out_ref[s, 2 * p], s_out_ref[s, 2 * p + 1] = _from_blockdiag(states[s])

    _run_interleaved([pair_stages(p) for p in range(s_in_ref.shape[1] // 2)])


def _rwkv_seq(x, wp, tt, tm):
    b, t, _ = x.shape
    row_spec = pl.BlockSpec((1, tt, D_MODEL), lambda i, j: (i, j, 0))
    vec = _const_spec((1, D_MODEL))
    return pl.pallas_call(
        functools.partial(_rwkv_seq_kernel, tm=tm),
        grid=(b, t // tt),
        in_specs=[row_spec, _const_spec((D_MODEL, RWKV_COLS)), _const_spec((1, RWKV_COLS)),
                  _const_spec((LORA, 3 * D_MODEL))] + [vec] * 7,
        out_specs=[row_spec,
                   pl.BlockSpec((1, N_HEADS, HEAD, HEAD), lambda i, j: (i, 0, 0, 0))],
        out_shape=[jax.ShapeDtypeStruct((b, t, D_MODEL), F32),
                   jax.ShapeDtypeStruct((b, N_HEADS, HEAD, HEAD), F32)],
        scratch_shapes=[pltpu.VMEM((SUBLANES, RWKV_COLS), F32),
                        pltpu.VMEM((7, tt, D_MODEL), F32),
                        pltpu.VMEM((N_PAIRS, PAIR, PAIR), F32),
                        pltpu.VMEM((CHUNK, D_MODEL), F32)],
        compiler_params=pltpu.CompilerParams(
            dimension_semantics=("arbitrary", "arbitrary"), vmem_limit_bytes=VMEM_LIMIT),
        name="rwkv_seq",
    )(x, wp["w_in"], wp["mu"], wp["w2ext"], wp["w0"], wp["a0"], wp["k_k"], wp["k_a"],
      wp["r_k"], wp["lnx_g"], wp["lnx_b"])


def _wkv_step(acts, s_heads, wp, steps, n_pairs):
    n = acts[0].shape[0]
    nseq = CHUNK // steps
    width = n_pairs * PAIR
    row_spec = pl.BlockSpec((CHUNK, width), lambda i, p: (i, p))
    vec = pl.BlockSpec((1, width), lambda i, p: (0, p))
    st_spec = pl.BlockSpec((nseq, 2 * n_pairs, HEAD, HEAD), lambda i, p: (i, p, 0, 0))
    return pl.pallas_call(
        functools.partial(_wkv_step_kernel, nseq=nseq),
        grid=(n // CHUNK, N_PAIRS // n_pairs),
        in_specs=[row_spec] * 7 + [vec] * 3 + [st_spec],
        out_specs=[row_spec, st_spec],
        out_shape=[jax.ShapeDtypeStruct((n, D_MODEL), F32),
                   jax.ShapeDtypeStruct(s_heads.shape, F32)],
        compiler_params=pltpu.CompilerParams(
            dimension_semantics=("arbitrary", "arbitrary"), vmem_limit_bytes=VMEM_LIMIT),
        name="wkv_step",
    )(*acts, wp["r_k"], wp["lnx_g"], wp["lnx_b"], s_heads)


def _post_kernel(x_ref, ya_ref, ga_ref, gyb_ref, wo_ref, l1g_ref, l1b_ref, wg_ref, wu_ref, wd_ref,
                 l2g_ref, l2b_ref, y_ref):
    x = x_ref[...]
    merged = ga_ref[...] * ya_ref[...] + gyb_ref[...]
    mix = _dot(merged, wo_ref[...])
    h1 = _layer_norm(ALPHA * x + mix, l1g_ref[...], l1b_ref[...])
    h1b = h1.astype(BF16)
    gate = jnp.dot(h1b, wg_ref[...], preferred_element_type=F32)
    up = jnp.dot(h1b, wu_ref[...], preferred_element_type=F32)
    act = (gate * _sigmoid(gate)) * up
    ffn = _dot(act, wd_ref[...])
    y_ref[...] = _layer_norm(ALPHA * h1 + ffn, l2g_ref[...], l2b_ref[...])


def _post(x, ya, ga, gyb, wp, tm):
    n = x.shape[0]
    row_spec = pl.BlockSpec((tm, D_MODEL), lambda i: (i, 0))
    vec = _const_spec((1, D_MODEL))
    return pl.pallas_call(
        _post_kernel,
        grid=(n // tm,),
        in_specs=[row_spec] * 4 + [_const_spec((D_MODEL, D_MODEL)), vec, vec,
                                   _const_spec((D_MODEL, D_FF)), _const_spec((D_MODEL, D_FF)),
                                   _const_spec((D_FF, D_MODEL)), vec, vec],
        out_specs=row_spec,
        out_shape=jax.ShapeDtypeStruct((n, D_MODEL), F32),
        compiler_params=pltpu.CompilerParams(
            dimension_semantics=("arbitrary",), vmem_limit_bytes=VMEM_LIMIT),
        name="post",
    )(x, ya, ga, gyb, wp["w_o"], wp["ln1_g"], wp["ln1_b"], wp["w_gate"], wp["w_up"], wp["w_down"],
      wp["ln2_g"], wp["ln2_b"])


def _prep_weights(w_in, tmix_mu, w0, w2_decay, a0, a2_iclr, g2_gate, k_k, k_a, r_k, lnx_g, lnx_b,
                  conv_w, conv_b, lru_wa, lru_ba, lru_wi, lru_bi, lru_lambda, w_o,
                  ln1_g, ln1_b, w_ffn_gate, w_ffn_up, w_ffn_down, ln2_g, ln2_b):
    row = lambda v: v.reshape(1, -1).astype(F32)
    zeros = lambda r: jnp.zeros((r, D_MODEL), F32)
    w2ext = jnp.concatenate([
        jnp.concatenate([w2_decay, zeros(DECAY_LORA), zeros(DECAY_LORA)], axis=1),
        jnp.concatenate([zeros(ICLR_LORA), a2_iclr, zeros(ICLR_LORA)], axis=1),
        jnp.concatenate([zeros(GATE_LORA), zeros(GATE_LORA), g2_gate], axis=1)], axis=0)

    def gate_blockdiag(w):
        per = LRU_GROUP // LRU_BS
        w4 = w.reshape(LRU_BLOCKS // per, per, LRU_BS, LRU_BS)
        eye = jnp.eye(per, dtype=w.dtype)
        bd = jnp.einsum("gpcd,pq->gpcqd", w4, eye)
        return bd.reshape(LRU_BLOCKS // per, LRU_GROUP, LRU_GROUP).astype(BF16)

    return dict(
        w_in=w_in.astype(BF16),
        mu=row(tmix_mu), w2ext=w2ext.astype(BF16), w0=row(w0), a0=row(a0), k_k=row(k_k),
        k_a=row(k_a), r_k=row(r_k), lnx_g=row(lnx_g), lnx_b=row(lnx_b),
        conv_w=conv_w.astype(F32), conv_b=row(conv_b), wa_bd=gate_blockdiag(lru_wa),
        ba=row(lru_ba), wi_bd=gate_blockdiag(lru_wi), bi=row(lru_bi), lam=row(lru_lambda),
        w_o=w_o.astype(BF16), ln1_g=row(ln1_g), ln1_b=row(ln1_b), w_gate=w_ffn_gate.astype(BF16),
        w_up=w_ffn_up.astype(BF16), w_down=w_ffn_down.astype(BF16), ln2_g=row(ln2_g),
        ln2_b=row(ln2_b))


def _prompt_layer(x, wp, tm, tt, tm_post):
    b, t, _ = x.shape
    ya, s_heads = _rwkv_seq(x, wp, tt, tm)
    ga, gyb, conv_tail, h_tail = _lru_seq(x, wp, tm)
    y = _post(x.reshape(b * t, D_MODEL), ya.reshape(b * t, D_MODEL), ga.reshape(b * t, D_MODEL),
              gyb.reshape(b * t, D_MODEL), wp, tm_post).reshape(b, t, D_MODEL)
    return (y, x[:, -1], s_heads, conv_tail[:, SUBLANES - (CONV_W - 1):],
            h_tail[:, SUBLANES - 1])


def _sample_layer(x, shift_buf, wkv0, conv_buf, h0, wp):
    b, t, _ = x.shape
    n = b * t
    xf = x.reshape(n, D_MODEL)
    xprev = jnp.concatenate([shift_buf[:, None], x[:, :-1]], axis=1).reshape(n, D_MODEL)
    acts = _rwkv_proj_step(xf, xprev, wp)
    ya, s_heads = _wkv_step(acts, wkv0, wp, t, 4)
    x_tm = x.transpose(1, 0, 2).reshape(n, D_MODEL)
    conv_tm = conv_buf.transpose(1, 0, 2).reshape((CONV_W - 1) * b, D_LRU)
    ga_tm, gyb_tm, conv_new_tm, h_last = _lru_step(x_tm, conv_tm, h0, wp, t, b)
    to_seq_major = lambda v: v.reshape(t, b, D_MODEL).transpose(1, 0, 2).reshape(n, D_MODEL)
    y = _post(xf, ya, to_seq_major(ga_tm), to_seq_major(gyb_tm), wp, min(n, 256)).reshape(
        b, t, D_MODEL)
    conv_new = conv_new_tm.reshape(CONV_W - 1, b, D_LRU).transpose(1, 0, 2)
    return y, x[:, -1], s_heads, conv_new, h_last


def kernel(x_prompt, x_sample, state_shift, state_wkv, state_conv, state_lru, w_in, tmix_mu, w0, w2_decay, a0, a2_iclr, g2_gate, k_k, k_a, r_k, lnx_g, lnx_b, conv_w, conv_b, lru_wa, lru_ba, lru_wi, lru_bi, lru_lambda, w_o, ln1_g, ln1_b, w_ffn_gate, w_ffn_up, w_ffn_down, ln2_g, ln2_b):
    params = (w_in, tmix_mu, w0, w2_decay, a0, a2_iclr, g2_gate, k_k, k_a, r_k, lnx_g, lnx_b,
              conv_w, conv_b, lru_wa, lru_ba, lru_wi, lru_bi, lru_lambda, w_o,
              ln1_g, ln1_b, w_ffn_gate, w_ffn_up, w_ffn_down, ln2_g, ln2_b)
    wp = _prep_weights(*[p[0] for p in params])
    yp, sh_p, wkv_p, conv_p, lru_p = _prompt_layer(x_prompt, wp, 256, 512, 256)
    ys, sh_s, wkv_s, conv_s, lru_s = _sample_layer(
        x_sample, state_shift[0], state_wkv[0], state_conv[0], state_lru[0], wp)
    return (yp, ys, sh_p[None], wkv_p[None], conv_p[None], lru_p[None],
            sh_s[None], wkv_s[None], conv_s[None], lru_s[None])
```

```python
import functools
import math

import jax
import jax.numpy as jnp
from jax import lax
from jax.experimental import pallas as pl
from jax.experimental.pallas import tpu as pltpu

F32 = jnp.float32
BF16 = jnp.bfloat16

D_MODEL = 1024
HEAD = 64
N_HEADS = D_MODEL // HEAD
PAIR = 2 * HEAD
N_PAIRS = N_HEADS // 2
DECAY_LORA = 64
ICLR_LORA = 64
GATE_LORA = 128
LORA = DECAY_LORA + ICLR_LORA + GATE_LORA
RWKV_COLS = 3 * D_MODEL + LORA
GN_EPS = HEAD * 1e-5
D_LRU = D_MODEL
LRU_BLOCKS = 16
LRU_BS = D_LRU // LRU_BLOCKS
LRU_GROUP = 256
CONV_W = 4
LRU_C = 8.0
LRU_COLS = 2 * D_LRU + 2 * D_MODEL
N_IN = RWKV_COLS + LRU_COLS
D_FF = 2816
ALPHA = 2.0 ** 0.25
LN_EPS = 1e-5

SUBLANES = 8
CHUNK = 64
CHUNKS_IN_FLIGHT = 2
VMEM_LIMIT = 56 * 1024 * 1024


def _softplus(x):
    return jnp.maximum(x, 0.0) + jnp.log1p(jnp.exp(-jnp.abs(x)))


def _softplus_plain(x):
    return jnp.maximum(x, 0.0) + jnp.log(1.0 + jnp.exp(-jnp.abs(x)))


def _sigmoid(x):
    return 0.5 * jnp.tanh(0.5 * x) + 0.5


def _gelu_tanh(x):
    c = math.sqrt(2.0 / math.pi)
    return x * (0.5 * (1.0 + jnp.tanh(c * (x + 0.044715 * (x * x * x)))))


def _layer_norm(x, g, b):
    mu = jnp.mean(x, axis=-1, keepdims=True)
    xc = x - mu
    var = jnp.mean(xc * xc, axis=-1, keepdims=True)
    return xc * lax.rsqrt(var + LN_EPS) * g + b


def _dot(a, b):
    return jnp.dot(a.astype(BF16), b.astype(BF16), preferred_element_type=F32)


def _dot_nt(a, b):
    return lax.dot_general(a.astype(BF16), b.astype(BF16), (((1,), (1,)), ((), ())),
                           preferred_element_type=F32)


def _dot_tn(a, b):
    return lax.dot_general(a.astype(BF16), b.astype(BF16), (((0,), (0,)), ((), ())),
                           preferred_element_type=F32)


def _seg_sum(x, lane_lo):
    s0 = jnp.sum(jnp.where(lane_lo, x, 0.0), axis=-1, keepdims=True)
    s1 = jnp.sum(jnp.where(lane_lo, 0.0, x), axis=-1, keepdims=True)
    return jnp.where(lane_lo, s0, s1)


def _head_sums(x):
    lane_lo = lax.broadcasted_iota(jnp.int32, (x.shape[0], PAIR), 1) < HEAD
    return jnp.concatenate([_seg_sum(x[:, p * PAIR:(p + 1) * PAIR], lane_lo)
                            for p in range(N_PAIRS)], axis=1)


def _rwkv_prep(mixed, w2ext, w0, a0, k_k, k_a):
    lo = mixed(3 * D_MODEL, LORA)
    lane = lax.broadcasted_iota(jnp.int32, lo.shape, 1)
    act = jnp.where(lane < DECAY_LORA, jnp.tanh(lo),
                    jnp.where(lane < DECAY_LORA + ICLR_LORA, lo, _sigmoid(lo)))
    r = mixed(0, D_MODEL)
    lora = _dot(act, w2ext)
    k = mixed(D_MODEL, D_MODEL)
    w = -_softplus_plain(-(w0 + lora[:, 0:D_MODEL])) - 0.5
    lw = -jnp.exp(w)
    a = _sigmoid(a0 + lora[:, D_MODEL:2 * D_MODEL])
    g = lora[:, 2 * D_MODEL:3 * D_MODEL]
    v = mixed(2 * D_MODEL, D_MODEL)
    kkraw = k * k_k
    kk = kkraw * lax.rsqrt(jnp.maximum(_head_sums(kkraw * kkraw), 1e-24))
    k2 = k * (1.0 + (a - 1.0) * k_a)
    return r, lw, k2, v, -kk, kk * a, g


def _rwkv_project_rows(x, w_ref, mu_ref, w2_ref, w0_ref, a0_ref, kk_ref, ka_ref, hist_ref):
    xbf = x.astype(BF16)
    tm = xbf.shape[0]
    row = lax.broadcasted_iota(jnp.int32, (SUBLANES, 1), 0)

    def mixed(first_col, n_cols):
        cols = slice(first_col, first_col + n_cols)
        z = jnp.dot(xbf, w_ref[:, cols], preferred_element_type=F32)
        rolled = pltpu.roll(z, 1, 0)
        first = jnp.where(row == 0, hist_ref[SUBLANES - 1:SUBLANES, cols], rolled[0:SUBLANES])
        zprev = jnp.concatenate([first, rolled[SUBLANES:]], axis=0)
        hist_ref[:, cols] = z[tm - SUBLANES:tm]
        return z + mu_ref[:, cols] * (zprev - z)

    return _rwkv_prep(mixed, w2_ref[...], w0_ref[...], a0_ref[...], kk_ref[...], ka_ref[...])


def _rwkv_proj_step_kernel(x_ref, xp_ref, w_ref, mu_ref, w2_ref, w0_ref, a0_ref, kk_ref, ka_ref,
                           r_ref, lw_ref, k_ref, v_ref, al_ref, be_ref, g_ref):
    xbf = x_ref[...].astype(BF16)
    xpbf = xp_ref[...].astype(BF16)

    def mixed(first_col, n_cols):
        cols = slice(first_col, first_col + n_cols)
        z = jnp.dot(xbf, w_ref[:, cols], preferred_element_type=F32)
        zprev = jnp.dot(xpbf, w_ref[:, cols], preferred_element_type=F32)
        return z + mu_ref[:, cols] * (zprev - z)

    outs = _rwkv_prep(mixed, w2_ref[...], w0_ref[...], a0_ref[...], kk_ref[...], ka_ref[...])
    for o_ref, o in zip((r_ref, lw_ref, k_ref, v_ref, al_ref, be_ref, g_ref), outs):
        o_ref[...] = o


def _const_spec(shape):
    nd = len(shape)
    return pl.BlockSpec(shape, lambda *_: (0,) * nd, pipeline_mode=pl.Buffered(1))


def _rwkv_proj_step(x, xprev, wp):
    n = x.shape[0]
    full = pl.BlockSpec((n, D_MODEL), lambda i: (0, 0))
    vec = _const_spec((1, D_MODEL))
    return pl.pallas_call(
        _rwkv_proj_step_kernel,
        grid=(1,),
        in_specs=[full, full, _const_spec((D_MODEL, RWKV_COLS)), _const_spec((1, RWKV_COLS)),
                  _const_spec((LORA, 3 * D_MODEL)), vec, vec, vec, vec],
        out_specs=[full] * 7,
        out_shape=[jax.ShapeDtypeStruct((n, D_MODEL), F32)] * 7,
        compiler_params=pltpu.CompilerParams(
            dimension_semantics=("arbitrary",), vmem_limit_bytes=VMEM_LIMIT),
        name="rwkv_proj_step",
    )(x, xprev, wp["w_in"], wp["mu"], wp["w2ext"], wp["w0"], wp["a0"], wp["k_k"], wp["k_a"])


def _lru_gate_logits(u, wa_ref, wi_ref):
    ub = u.astype(BF16)
    ra, ia = [], []
    for q in range(D_LRU // LRU_GROUP):
        uq = ub[:, q * LRU_GROUP:(q + 1) * LRU_GROUP]
        ra.append(jnp.dot(uq, wa_ref[q], preferred_element_type=F32))
        ia.append(jnp.dot(uq, wi_ref[q], preferred_element_type=F32))
    return jnp.concatenate(ra, axis=1), jnp.concatenate(ia, axis=1)


def _lru_coeffs(ra, ia, ba, bi, sp):
    rg = _sigmoid(ra + ba)
    ig = _sigmoid(ia + bi)
    log_a = -LRU_C * rg * sp
    a = jnp.exp(log_a)
    m2 = -jnp.tanh(log_a) * (a * a + 1.0)
    mult = jnp.where(m2 > 0.0, m2 * lax.rsqrt(m2), 0.0)
    return a, mult, ig


def _lru_seq_kernel(x_ref, w_ref, cw_ref, cb_ref, wa_ref, ba_ref, wi_ref, bi_ref, lam_ref,
                    ga_ref, gyb_ref, conv_ref, hlast_ref,
                    xhist_ref, hc_ref, a_s, bx_s, h_s):
    t = pl.program_id(1)

    @pl.when(t == 0)
    def _():
        xhist_ref[...] = jnp.zeros_like(xhist_ref)
        hc_ref[...] = jnp.zeros_like(hc_ref)

    xbf = x_ref[0].astype(BF16)
    tm = xbf.shape[0]

    def proj(first_col, n_cols):
        cols = slice(RWKV_COLS + first_col, RWKV_COLS + first_col + n_cols)
        return jnp.dot(xbf, w_ref[:, cols], preferred_element_type=F32)

    xb = proj(0, D_LRU)
    gb = proj(D_LRU, D_LRU)
    row8 = lax.broadcasted_iota(jnp.int32, (SUBLANES, 1), 0)
    hist = xhist_ref[...]

    def shifted(k):
        rolled = pltpu.roll(xb, k, 0)
        first = jnp.where(row8 < k, pltpu.roll(hist, k, 0), rolled[0:SUBLANES])
        return jnp.concatenate([first, rolled[SUBLANES:]], axis=0)

    cw = cw_ref[...]
    u = (cb_ref[...] + cw[0:1] * shifted(3) + cw[1:2] * shifted(2) + cw[2:3] * shifted(1)
         + cw[3:4] * xb)
    last8 = xb[tm - SUBLANES:tm]
    xhist_ref[...] = last8
    conv_ref[0] = last8

    ra, ia = _lru_gate_logits(u, wa_ref, wi_ref)
    gelu_gb = _gelu_tanh(gb)
    zga = proj(2 * D_LRU, D_MODEL)
    sp = _softplus(-lam_ref[...])
    a, mult, ig = _lru_coeffs(ra, ia, ba_ref[...], bi_ref[...], sp)
    row = lax.broadcasted_iota(jnp.int32, (tm, 1), 0)
    mult = jnp.where(jnp.logical_and(row == 0, t == 0), 1.0, mult)
    a_s[...] = a
    bx_s[...] = mult * ig * u
    ga_ref[0] = _sigmoid(zga)
    zgb = proj(2 * D_LRU + D_MODEL, D_MODEL)

    hc = hc_ref[0:1, :]
    for blk in range(tm // SUBLANES):
        rows = slice(blk * SUBLANES, (blk + 1) * SUBLANES)
        av = a_s[rows, :]
        bv = bx_s[rows, :]
        for d in (1, 2, 4):
            a_sh = jnp.where(row8 < d, 1.0, pltpu.roll(av, d, 0))
            b_sh = jnp.where(row8 < d, 0.0, pltpu.roll(bv, d, 0))
            bv = av * b_sh + bv
            av = av * a_sh
        h_blk = bv + av * hc
        h_s[rows, :] = h_blk
        hc = h_blk[SUBLANES - 1:SUBLANES, :]
    hc_ref[0:1, :] = hc
    h = h_s[...]
    hlast_ref[0] = h[tm - SUBLANES:tm]
    gyb_ref[0] = (_sigmoid(zgb) * gelu_gb) * h


def _lru_step_kernel(x_ref, conv0_ref, h0_ref, w_ref, cw_ref, cb_ref, wa_ref, ba_ref, wi_ref,
                     bi_ref, lam_ref, ga_ref, gyb_ref, conv_ref, hlast_ref, *, steps, nseq):
    z = _dot(x_ref[...], w_ref[:, RWKV_COLS:])
    xb = z[:, 0:D_LRU]
    gb = z[:, D_LRU:2 * D_LRU]
    zga = z[:, 2 * D_LRU:2 * D_LRU + D_MODEL]
    zgb = z[:, 2 * D_LRU + D_MODEL:]
    n = steps * nseq
    hist = (CONV_W - 1) * nseq
    xext = jnp.concatenate([conv0_ref[...], xb], axis=0)

    def shifted(k):
        return xext[hist - k * nseq:hist - k * nseq + n]

    cw = cw_ref[...]
    u = (cb_ref[...] + cw[0:1] * shifted(3) + cw[1:2] * shifted(2) + cw[2:3] * shifted(1)
         + cw[3:4] * xb)
    conv_ref[...] = xext[n:n + hist]
    sp = _softplus(-lam_ref[...])
    ra, ia = _lru_gate_logits(u, wa_ref, wi_ref)
    a, mult, ig = _lru_coeffs(ra, ia, ba_ref[...], bi_ref[...], sp)
    bx = mult * ig * u
    h = h0_ref[...]
    hs = []
    for s in range(steps):
        h = a[s * nseq:(s + 1) * nseq] * h + bx[s * nseq:(s + 1) * nseq]
        hs.append(h)
    hlast_ref[...] = h
    hall = jnp.concatenate(hs, axis=0)
    ga_ref[...] = _sigmoid(zga)
    gyb_ref[...] = _sigmoid(zgb) * (hall * _gelu_tanh(gb))


def _lru_weight_specs():
    vec = _const_spec((1, D_LRU))
    gate_w = _const_spec((D_LRU // LRU_GROUP, LRU_GROUP, LRU_GROUP))
    return [_const_spec((D_MODEL, N_IN)), _const_spec((CONV_W, D_LRU)), vec,
            gate_w, vec, gate_w, vec, vec]


def _lru_weights(wp):
    return (wp["w_in"], wp["conv_w"], wp["conv_b"], wp["wa_bd"], wp["ba"], wp["wi_bd"],
            wp["bi"], wp["lam"])


def _lru_seq(x, wp, tm):
    b, t, _ = x.shape
    row_spec = pl.BlockSpec((1, tm, D_MODEL), lambda i, j: (i, j, 0))
    tail_spec = pl.BlockSpec((1, SUBLANES, D_LRU), lambda i, j: (i, 0, 0))
    return pl.pallas_call(
        _lru_seq_kernel,
        grid=(b, t // tm),
        in_specs=[row_spec] + _lru_weight_specs(),
        out_specs=[row_spec, row_spec, tail_spec, tail_spec],
        out_shape=[jax.ShapeDtypeStruct((b, t, D_MODEL), F32)] * 2
        + [jax.ShapeDtypeStruct((b, SUBLANES, D_LRU), F32)] * 2,
        scratch_shapes=[pltpu.VMEM((SUBLANES, D_LRU), F32), pltpu.VMEM((SUBLANES, D_LRU), F32),
                        pltpu.VMEM((tm, D_LRU), F32), pltpu.VMEM((tm, D_LRU), F32),
                        pltpu.VMEM((tm, D_LRU), F32)],
        compiler_params=pltpu.CompilerParams(
            dimension_semantics=("arbitrary", "arbitrary"), vmem_limit_bytes=VMEM_LIMIT),
        name="lru_seq",
    )(x, *_lru_weights(wp))


def _lru_step(x_tm, conv_tm, h0, wp, steps, nseq):
    n = steps * nseq
    hist = (CONV_W - 1) * nseq

    def full(r):
        return pl.BlockSpec((r, D_MODEL), lambda i: (0, 0))

    return pl.pallas_call(
        functools.partial(_lru_step_kernel, steps=steps, nseq=nseq),
        grid=(1,),
        in_specs=[full(n), full(hist), full(nseq)] + _lru_weight_specs(),
        out_specs=[full(n), full(n), full(hist), full(nseq)],
        out_shape=[jax.ShapeDtypeStruct((n, D_MODEL), F32)] * 2
        + [jax.ShapeDtypeStruct((hist, D_LRU), F32), jax.ShapeDtypeStruct((nseq, D_LRU), F32)],
        compiler_params=pltpu.CompilerParams(
            dimension_semantics=("arbitrary",), vmem_limit_bytes=VMEM_LIMIT),
        name="lru_step",
    )(x_tm, conv_tm, h0, *_lru_weights(wp))


def _wkv_consts(nseq):
    c = CHUNK
    seq_len = c // nseq
    lane = lax.broadcasted_iota(jnp.int32, (c, PAIR), 1)
    row = lax.broadcasted_iota(jnp.int32, (c, PAIR), 0)
    lane_lo = lane < HEAD
    col = lane % c
    same = (row // seq_len) == (col // seq_len)
    strict = jnp.logical_and(same, col < row)
    incl = jnp.logical_and(same, col <= row)
    eye = jnp.where(col == row, 1.0, 0.0).astype(F32)
    r3 = lax.broadcasted_iota(jnp.int32, (c, 3 * c), 0)
    c3 = lax.broadcasted_iota(jnp.int32, (c, 3 * c), 1) % c
    same3 = (r3 // seq_len) == (c3 // seq_len)
    tri3 = jnp.where(jnp.logical_and(same3, c3 <= r3), 1.0, 0.0).astype(BF16)
    tot3 = jnp.where(same3, 1.0, 0.0).astype(BF16)
    rp = lax.broadcasted_iota(jnp.int32, (PAIR, PAIR), 0)
    cp = lax.broadcasted_iota(jnp.int32, (PAIR, PAIR), 1)
    blockdiag = (rp // HEAD) == (cp // HEAD)
    return dict(lane_lo=lane_lo, tri3=tri3, tot3=tot3, strict=strict, incl=incl, eye=eye,
                blockdiag=blockdiag, seq_len=seq_len)


def _split3(x):
    hi = x.astype(BF16)
    rest = x - hi.astype(F32)
    mid = rest.astype(BF16)
    lo = (rest - mid.astype(F32)).astype(BF16)
    return jnp.concatenate([hi, mid, lo], axis=0)


def _split_heads(x, lane_lo):
    zero = jnp.zeros_like(x)
    return jnp.concatenate([jnp.where(lane_lo, x, zero), jnp.where(lane_lo, zero, x)], axis=0)


def _run_interleaved(stage_gens):
    results = [None] * len(stage_gens)
    live = list(range(len(stage_gens)))
    while live:
        still = []
        for idx in live:
            try:
                next(stage_gens[idx])
                still.append(idx)
            except StopIteration as done:
                results[idx] = done.value
        live = still
    return results


def _wkv_chunk(r, lw, k, v, al, be, get_states, cn, nseq):
    c = CHUNK
    lane_lo = cn["lane_lo"]
    lw3 = _split3(lw)
    cum = jnp.dot(cn["tri3"], lw3, preferred_element_type=F32)
    if nseq == 1:
        end = cum[c - 1:c, :]
    else:
        end = jnp.dot(cn["tot3"], lw3, preferred_element_type=F32)
    yield
    e_cum = jnp.exp(cum)
    e_neg = jnp.exp(-cum)
    e_end = jnp.exp(end)
    rt = r * e_cum
    at = al * jnp.exp(cum - lw)
    bt = be * e_neg
    kt = k * e_neg
    lhs2 = jnp.concatenate([at, rt], axis=0).astype(BF16)
    keys = jnp.concatenate([_split_heads(bt, lane_lo), _split_heads(kt, lane_lo)], axis=0)
    gram = _dot_nt(lhs2, keys)
    yield
    a_ab = jnp.where(cn["strict"], gram[0:c, 0:PAIR], 0.0)
    a_ak = jnp.where(cn["strict"], gram[0:c, PAIR:], 0.0)
    a_rb = jnp.where(cn["incl"], gram[c:, 0:PAIR], 0.0)
    a_rk = jnp.where(cn["incl"], gram[c:, PAIR:], 0.0)
    inv = cn["eye"] + a_ab
    n_iter = int(math.log2(cn["seq_len"])) - 1
    pw = _dot(a_ab, _split_heads(a_ab, lane_lo))
    yield
    for it in range(n_iter):
        pw_heads = _split_heads(pw, lane_lo)
        if it < n_iter - 1:
            both = _dot(jnp.concatenate([inv, pw], axis=0), pw_heads)
            inv = inv + both[0:c]
            pw = both[c:]
        else:
            inv = inv + _dot(inv, pw_heads)
        yield
    av = _dot(jnp.concatenate([a_ak, a_rk], axis=0), _split_heads(v, lane_lo))
    states = get_states()
    while states is None:
        yield
        states = get_states()
    if nseq == 1:
        from_state = _dot_nt(lhs2, states[0])
    else:
        seq_of_row = (lax.broadcasted_iota(jnp.int32, (2 * c, 1), 0) % c) // cn["seq_len"]
        from_state = jnp.zeros((2 * c, PAIR), F32)
        for s in range(nseq):
            from_state = jnp.where(seq_of_row == s, _dot_nt(lhs2, states[s]), from_state)
    yield
    u = _dot(inv, _split_heads(from_state[0:c] + av[0:c], lane_lo))
    yield
    y = from_state[c:] + av[c:] + _dot(a_rb, _split_heads(u, lane_lo))
    yield
    uv = jnp.concatenate([u, v], axis=0)
    bk = jnp.concatenate([bt * e_end, kt * e_end], axis=0)
    new_states = []
    if nseq == 1:
        inc = _dot_tn(uv, bk)
        new_states.append(states[0] * e_end + jnp.where(cn["blockdiag"], inc, 0.0))
    else:
        seq_of_uv = (lax.broadcasted_iota(jnp.int32, (2 * c, 1), 0) % c) // cn["seq_len"]
        for s in range(nseq):
            inc = _dot_tn(jnp.where(seq_of_uv == s, uv, 0.0), bk)
            decay = e_end[s * cn["seq_len"]:s * cn["seq_len"] + 1, :]
            new_states.append(states[s] * decay + jnp.where(cn["blockdiag"], inc, 0.0))
    return y, new_states


def _wkv_readout(y, r, k, v, g, rk, lnx_g, lnx_b, lane_lo):
    mean = _seg_sum(y, lane_lo) * (1.0 / HEAD)
    bonus = _seg_sum(r * k * rk, lane_lo) * v
    yield
    yc = y - mean
    var = _seg_sum(yc * yc, lane_lo) * (1.0 / HEAD)
    yield
    yn = yc * lax.rsqrt(var + GN_EPS) * lnx_g + lnx_b
    return (yn + bonus) * g


def _to_blockdiag(s0, s1):
    pad = jnp.zeros((HEAD, HEAD), F32)
    top = jnp.concatenate([s0, pad], axis=1)
    bottom = pltpu.roll(jnp.concatenate([s1, pad], axis=1), HEAD, 1)
    return jnp.concatenate([top, bottom], axis=0)


def _from_blockdiag(state):
    return state[0:HEAD, 0:HEAD], pltpu.roll(state[HEAD:], HEAD, 1)[:, 0:HEAD]


def _rwkv_seq_kernel(x_ref, w_ref, mu_ref, w2_ref, w0_ref, a0_ref, kk_ref, ka_ref, rk_ref, lg_ref,
                     lb_ref, y_ref, s_out_ref, hist_ref, acts_scr, s_scr, yraw_scr, *, tm):
    t = pl.program_id(1)
    cn = _wkv_consts(1)
    n_pairs = s_scr.shape[0]
    tt = x_ref.shape[1]

    @pl.when(t == 0)
    def _():
        hist_ref[...] = jnp.zeros_like(hist_ref)
        s_scr[...] = jnp.zeros_like(s_scr)
        yraw_scr[...] = jnp.zeros_like(yraw_scr)

    @pl.loop(0, tt // tm)
    def _(sub):
        rows = pl.ds(pl.multiple_of(sub * tm, tm), tm)
        outs = _rwkv_project_rows(x_ref[0, rows, :], w_ref, mu_ref, w2_ref, w0_ref, a0_ref,
                                  kk_ref, ka_ref, hist_ref)
        for idx, o in enumerate(outs):
            acts_scr[idx, rows, :] = o

    r_ref, lw_ref, k_ref, v_ref, al_ref, be_ref, g_ref = (acts_scr.at[i] for i in range(7))
    n_chunks = tt // CHUNK

    def readout_stages(p, rows):
        ln = pl.ds(p * PAIR, PAIR)
        out = yield from _wkv_readout(
            yraw_scr[:, ln], r_ref[rows, ln], k_ref[rows, ln], v_ref[rows, ln],
            g_ref[rows, ln], rk_ref[:, ln], lg_ref[:, ln], lb_ref[:, ln], cn["lane_lo"])
        y_ref[0, rows, ln] = out

    def chunk_stages(p, rows, get_states):
        ln = pl.ds(p * PAIR, PAIR)
        return _wkv_chunk(r_ref[rows, ln], lw_ref[rows, ln], k_ref[rows, ln],
                          v_ref[rows, ln], al_ref[rows, ln], be_ref[rows, ln],
                          get_states, cn, 1)

    def body(i, states):
        def rows_of(j):
            return pl.ds(pl.multiple_of((CHUNKS_IN_FLIGHT * i + j) * CHUNK, CHUNK), CHUNK)

        prev = pl.ds(pl.multiple_of(jnp.maximum(CHUNKS_IN_FLIGHT * i - 1, 0) * CHUNK, CHUNK),
                     CHUNK)
        end_states = [[None] * n_pairs for _ in range(CHUNKS_IN_FLIGHT)]

        def start_states(j, p):
            if j == 0:
                return [states[p]]
            return None if end_states[j - 1][p] is None else [end_states[j - 1][p]]

        def stages(j, p):
            ln = pl.ds(p * PAIR, PAIR)
            rows = rows_of(j)
            y, (state,) = yield from chunk_stages(p, rows, functools.partial(start_states, j, p))
            end_states[j][p] = state
            if j == CHUNKS_IN_FLIGHT - 1:
                yraw_scr[:, ln] = y
                return
            out = yield from _wkv_readout(
                y, r_ref[rows, ln], k_ref[rows, ln], v_ref[rows, ln],
                g_ref[rows, ln], rk_ref[:, ln], lg_ref[:, ln], lb_ref[:, ln], cn["lane_lo"])
            y_ref[0, rows, ln] = out

        _run_interleaved([readout_stages(p, prev) for p in range(n_pairs)]
                         + [stages(j, p) for j in range(CHUNKS_IN_FLIGHT) for p in range(n_pairs)])
        return tuple(end_states[CHUNKS_IN_FLIGHT - 1])

    assert n_chunks % CHUNKS_IN_FLIGHT == 0
    states = lax.fori_loop(0, n_chunks // CHUNKS_IN_FLIGHT, body,
                           tuple(s_scr[p] for p in range(n_pairs)))
    last = pl.ds((n_chunks - 1) * CHUNK, CHUNK)
    _run_interleaved([readout_stages(p, last) for p in range(n_pairs)])
    for p in range(n_pairs):
        s_scr[p] = states[p]
        s_out_ref[0, 2 * p], s_out_ref[0, 2 * p + 1] = _from_blockdiag(states[p])


def _wkv_step_kernel(r_ref, lw_ref, k_ref, v_ref, al_ref, be_ref, g_ref, rk_ref, lg_ref, lb_ref,
                     s_in_ref, y_ref, s_out_ref, *, nseq):
    cn = _wkv_consts(nseq)

    def pair_stages(p):
        ln = pl.ds(p * PAIR, PAIR)
        states = [_to_blockdiag(s_in_ref[s, 2 * p], s_in_ref[s, 2 * p + 1]) for s in range(nseq)]
        r, k, v = r_ref[:, ln], k_ref[:, ln], v_ref[:, ln]
        y, states = yield from _wkv_chunk(r, lw_ref[:, ln], k, v, al_ref[:, ln], be_ref[:, ln],
                                          lambda: states, cn, nseq)
        ya = yield from _wkv_readout(y, r, k, v, g_ref[:, ln], rk_ref[:, ln], lg_ref[:, ln],
                                     lb_ref[:, ln], cn["lane_lo"])
        y_ref[:, ln] = ya
        for s in range(nseq):
            s_out_ref[s, 2 * p], s_out_ref[s, 2 * p + 1] = _from_blockdiag(states[s])

    _run_interleaved([pair_stages(p) for p in range(s_in_ref.shape[1] // 2)])


def _rwkv_seq(x, wp, tt, tm):
    b, t, _ = x.shape
    row_spec = pl.BlockSpec((1, tt, D_MODEL), lambda i, j: (i, j, 0))
    vec = _const_spec((1, D_MODEL))
    return pl.pallas_call(
        functools.partial(_rwkv_seq_kernel, tm=tm),
        grid=(b, t // tt),
        in_specs=[row_spec, _const_spec((D_MODEL, RWKV_COLS)), _const_spec((1, RWKV_COLS)),
                  _const_spec((LORA, 3 * D_MODEL))] + [vec] * 7,
        out_specs=[row_spec,
                   pl.BlockSpec((1, N_HEADS, HEAD, HEAD), lambda i, j: (i, 0, 0, 0))],
        out_shape=[jax.ShapeDtypeStruct((b, t, D_MODEL), F32),
                   jax.ShapeDtypeStruct((b, N_HEADS, HEAD, HEAD), F32)],
        scratch_shapes=[pltpu.VMEM((SUBLANES, RWKV_COLS), F32),
                        pltpu.VMEM((7, tt, D_MODEL), F32),
                        pltpu.VMEM((N_PAIRS, PAIR, PAIR), F32),
                        pltpu.VMEM((CHUNK, D_MODEL), F32)],
        compiler_params=pltpu.CompilerParams(
            dimension_semantics=("arbitrary", "arbitrary"), vmem_limit_bytes=VMEM_LIMIT),
        name="rwkv_seq",
    )(x, wp["w_in"], wp["mu"], wp["w2ext"], wp["w0"], wp["a0"], wp["k_k"], wp["k_a"],
      wp["r_k"], wp["lnx_g"], wp["lnx_b"])


def _wkv_step(acts, s_heads, wp, steps, n_pairs):
    n = acts[0].shape[0]
    nseq = CHUNK // steps
    width = n_pairs * PAIR
    row_spec = pl.BlockSpec((CHUNK, width), lambda i, p: (i, p))
    vec = pl.BlockSpec((1, width), lambda i, p: (0, p))
    st_spec = pl.BlockSpec((nseq, 2 * n_pairs, HEAD, HEAD), lambda i, p: (i, p, 0, 0))
    return pl.pallas_call(
        functools.partial(_wkv_step_kernel, nseq=nseq),
        grid=(n // CHUNK, N_PAIRS // n_pairs),
        in_specs=[row_spec] * 7 + [vec] * 3 + [st_spec],
        out_specs=[row_spec, st_spec],
        out_shape=[jax.ShapeDtypeStruct((n, D_MODEL), F32),
                   jax.ShapeDtypeStruct(s_heads.shape, F32)],
        compiler_params=pltpu.CompilerParams(
            dimension_semantics=("arbitrary", "arbitrary"), vmem_limit_bytes=VMEM_LIMIT),
        name="wkv_step",
    )(*acts, wp["r_k"], wp["lnx_g"], wp["lnx_b"], s_heads)


def _post_kernel(x_ref, ya_ref, ga_ref, gyb_ref, wo_ref, l1g_ref, l1b_ref, wg_ref, wu_ref, wd_ref,
                 l2g_ref, l2b_ref, y_ref):
    x = x_ref[...]
    merged = ga_ref[...] * ya_ref[...] + gyb_ref[...]
    mix = _dot(merged, wo_ref[...])
    h1 = _layer_norm(ALPHA * x + mix, l1g_ref[...], l1b_ref[...])
    h1b = h1.astype(BF16)
    gate = jnp.dot(h1b, wg_ref[...], preferred_element_type=F32)
    up = jnp.dot(h1b, wu_ref[...], preferred_element_type=F32)
    act = (gate * _sigmoid(gate)) * up
    ffn = _dot(act, wd_ref[...])
    y_ref[...] = _layer_norm(ALPHA * h1 + ffn, l2g_ref[...], l2b_ref[...])


def _post(x, ya, ga, gyb, wp, tm):
    n = x.shape[0]
    row_spec = pl.BlockSpec((tm, D_MODEL), lambda i: (i, 0))
    vec = _const_spec((1, D_MODEL))
    return pl.pallas_call(
        _post_kernel,
        grid=(n // tm,),
        in_specs=[row_spec] * 4 + [_const_spec((D_MODEL, D_MODEL)), vec, vec,
                                   _const_spec((D_MODEL, D_FF)), _const_spec((D_MODEL, D_FF)),
                                   _const_spec((D_FF, D_MODEL)), vec, vec],
        out_specs=row_spec,
        out_shape=jax.ShapeDtypeStruct((n, D_MODEL), F32),
        compiler_params=pltpu.CompilerParams(
            dimension_semantics=("arbitrary",), vmem_limit_bytes=VMEM_LIMIT),
        name="post",
    )(x, ya, ga, gyb, wp["w_o"], wp["ln1_g"], wp["ln1_b"], wp["w_gate"], wp["w_up"], wp["w_down"],
      wp["ln2_g"], wp["ln2_b"])


def _prep_weights(w_in, tmix_mu, w0, w2_decay, a0, a2_iclr, g2_gate, k_k, k_a, r_k, lnx_g, lnx_b,
                  conv_w, conv_b, lru_wa, lru_ba, lru_wi, lru_bi, lru_lambda, w_o,
                  ln1_g, ln1_b, w_ffn_gate, w_ffn_up, w_ffn_down, ln2_g, ln2_b):
    row = lambda v: v.reshape(1, -1).astype(F32)
    zeros = lambda r: jnp.zeros((r, D_MODEL), F32)
    w2ext = jnp.concatenate([
        jnp.concatenate([w2_decay, zeros(DECAY_LORA), zeros(DECAY_LORA)], axis=1),
        jnp.concatenate([zeros(ICLR_LORA), a2_iclr, zeros(ICLR_LORA)], axis=1),
        jnp.concatenate([zeros(GATE_LORA), zeros(GATE_LORA), g2_gate], axis=1)], axis=0)

    def gate_blockdiag(w):
        per = LRU_GROUP // LRU_BS
        w4 = w.reshape(LRU_BLOCKS // per, per, LRU_BS, LRU_BS)
        eye = jnp.eye(per, dtype=w.dtype)
        bd = jnp.einsum("gpcd,pq->gpcqd", w4, eye)
        return bd.reshape(LRU_BLOCKS // per, LRU_GROUP, LRU_GROUP).astype(BF16)

    return dict(
        w_in=w_in.astype(BF16),
        mu=row(tmix_mu), w2ext=w2ext.astype(BF16), w0=row(w0), a0=row(a0), k_k=row(k_k),
        k_a=row(k_a), r_k=row(r_k), lnx_g=row(lnx_g), lnx_b=row(lnx_b),
        conv_w=conv_w.astype(F32), conv_b=row(conv_b), wa_bd=gate_blockdiag(lru_wa),
        ba=row(lru_ba), wi_bd=gate_blockdiag(lru_wi), bi=row(lru_bi), lam=row(lru_lambda),
        w_o=w_o.astype(BF16), ln1_g=row(ln1_g), ln1_b=row(ln1_b), w_gate=w_ffn_gate.astype(BF16),
        w_up=w_ffn_up.astype(BF16), w_down=w_ffn_down.astype(BF16), ln2_g=row(ln2_g),
        ln2_b=row(ln2_b))


def _prompt_layer(x, wp, tm, tt, tm_post):
    b, t, _ = x.shape
    ya, s_heads = _rwkv_seq(x, wp, tt, tm)
    ga, gyb, conv_tail, h_tail = _lru_seq(x, wp, tm)
    y = _post(x.reshape(b * t, D_MODEL), ya.reshape(b * t, D_MODEL), ga.reshape(b * t, D_MODEL),
              gyb.reshape(b * t, D_MODEL), wp, tm_post).reshape(b, t, D_MODEL)
    return (y, x[:, -1], s_heads, conv_tail[:, SUBLANES - (CONV_W - 1):],
            h_tail[:, SUBLANES - 1])


def _sample_layer(x, shift_buf, wkv0, conv_buf, h0, wp):
    b, t, _ = x.shape
    n = b * t
    xf = x.reshape(n, D_MODEL)
    xprev = jnp.concatenate([shift_buf[:, None], x[:, :-1]], axis=1).reshape(n, D_MODEL)
    acts = _rwkv_proj_step(xf, xprev, wp)
    ya, s_heads = _wkv_step(acts, wkv0, wp, t, 4)
    x_tm = x.transpose(1, 0, 2).reshape(n, D_MODEL)
    conv_tm = conv_buf.transpose(1, 0, 2).reshape((CONV_W - 1) * b, D_LRU)
    ga_tm, gyb_tm, conv_new_tm, h_last = _lru_step(x_tm, conv_tm, h0, wp, t, b)
    to_seq_major = lambda v: v.reshape(t, b, D_MODEL).transpose(1, 0, 2).reshape(n, D_MODEL)
    y = _post(xf, ya, to_seq_major(ga_tm), to_seq_major(gyb_tm), wp, min(n, 256)).reshape(
        b, t, D_MODEL)
    conv_new = conv_new_tm.reshape(CONV_W - 1, b, D_LRU).transpose(1, 0, 2)
    return y, x[:, -1], s_heads, conv_new, h_last


def kernel(x_prompt, x_sample, state_shift, state_wkv, state_conv, state_lru, w_in, tmix_mu, w0, w2_decay, a0, a2_iclr, g2_gate, k_k, k_a, r_k, lnx_g, lnx_b, conv_w, conv_b, lru_wa, lru_ba, lru_wi, lru_bi, lru_lambda, w_o, ln1_g, ln1_b, w_ffn_gate, w_ffn_up, w_ffn_down, ln2_g, ln2_b):
    params = (w_in, tmix_mu, w0, w2_decay, a0, a2_iclr, g2_gate, k_k, k_a, r_k, lnx_g, lnx_b,
              conv_w, conv_b, lru_wa, lru_ba, lru_wi, lru_bi, lru_lambda, w_o,
              ln1_g, ln1_b, w_ffn_gate, w_ffn_up, w_ffn_down, ln2_g, ln2_b)
    wp = _prep_weights(*[p[0] for p in params])
    yp, sh_p, wkv_p, conv_p, lru_p = _prompt_layer(x_prompt, wp, 256, 512, 256)
    ys, sh_s, wkv_s, conv_s, lru_s = _sample_layer(
        x_sample, state_shift[0], state_wkv[0], state_conv[0], state_lru[0], wp)
    return (yp, ys, sh_p[None], wkv_p[None], conv_p[None], lru_p[None],
            sh_s[None], wkv_s[None], conv_s[None], lru_s[None])
```

```python
import functools
import math

import jax
import jax.numpy as jnp
from jax import lax
from jax.experimental import pallas as pl
from jax.experimental.pallas import tpu as pltpu

F32 = jnp.float32
BF16 = jnp.bfloat16

D_MODEL = 1024
HEAD = 64
N_HEADS = D_MODEL // HEAD
PAIR = 2 * HEAD
N_PAIRS = N_HEADS // 2
DECAY_LORA = 64
ICLR_LORA = 64
GATE_LORA = 128
LORA = DECAY_LORA + ICLR_LORA + GATE_LORA
RWKV_COLS = 3 * D_MODEL + LORA
GN_EPS = HEAD * 1e-5
D_LRU = D_MODEL
LRU_BLOCKS = 16
LRU_BS = D_LRU // LRU_BLOCKS
LRU_GROUP = 256
CONV_W = 4
LRU_C = 8.0
LRU_COLS = 2 * D_LRU + 2 * D_MODEL
N_IN = RWKV_COLS + LRU_COLS
D_FF = 2816
ALPHA = 2.0 ** 0.25
LN_EPS = 1e-5

SUBLANES = 8
CHUNK = 64
CHUNKS_IN_FLIGHT = 2
STEP_ROWS_IN_FLIGHT = 4
VMEM_LIMIT = 56 * 1024 * 1024


def _softplus(x):
    return jnp.maximum(x, 0.0) + jnp.log1p(jnp.exp(-jnp.abs(x)))


def _softplus_plain(x):
    return jnp.maximum(x, 0.0) + jnp.log(1.0 + jnp.exp(-jnp.abs(x)))


def _sigmoid(x):
    return 0.5 * jnp.tanh(0.5 * x) + 0.5


def _gelu_tanh(x):
    c = math.sqrt(2.0 / math.pi)
    return x * (0.5 * (1.0 + jnp.tanh(c * (x + 0.044715 * (x * x * x)))))


def _layer_norm(x, g, b):
    mu = jnp.mean(x, axis=-1, keepdims=True)
    xc = x - mu
    var = jnp.mean(xc * xc, axis=-1, keepdims=True)
    return xc * lax.rsqrt(var + LN_EPS) * g + b


def _dot(a, b):
    return jnp.dot(a.astype(BF16), b.astype(BF16), preferred_element_type=F32)


def _dot_nt(a, b):
    return lax.dot_general(a.astype(BF16), b.astype(BF16), (((1,), (1,)), ((), ())),
                           preferred_element_type=F32)


def _dot_tn(a, b):
    return lax.dot_general(a.astype(BF16), b.astype(BF16), (((0,), (0,)), ((), ())),
                           preferred_element_type=F32)


def _seg_sum(x, lane_lo):
    s0 = jnp.sum(jnp.where(lane_lo, x, 0.0), axis=-1, keepdims=True)
    s1 = jnp.sum(jnp.where(lane_lo, 0.0, x), axis=-1, keepdims=True)
    return jnp.where(lane_lo, s0, s1)


def _head_sums(x):
    lane_lo = lax.broadcasted_iota(jnp.int32, (x.shape[0], PAIR), 1) < HEAD
    return jnp.concatenate([_seg_sum(x[:, p * PAIR:(p + 1) * PAIR], lane_lo)
                            for p in range(N_PAIRS)], axis=1)


def _rwkv_prep(mixed, w2ext, w0, a0, k_k, k_a):
    lo = mixed(3 * D_MODEL, LORA)
    lane = lax.broadcasted_iota(jnp.int32, lo.shape, 1)
    act = jnp.where(lane < DECAY_LORA, jnp.tanh(lo),
                    jnp.where(lane < DECAY_LORA + ICLR_LORA, lo, _sigmoid(lo)))
    r = mixed(0, D_MODEL)
    lora = _dot(act, w2ext)
    k = mixed(D_MODEL, D_MODEL)
    w = -_softplus_plain(-(w0 + lora[:, 0:D_MODEL])) - 0.5
    lw = -jnp.exp(w)
    a = _sigmoid(a0 + lora[:, D_MODEL:2 * D_MODEL])
    g = lora[:, 2 * D_MODEL:3 * D_MODEL]
    v = mixed(2 * D_MODEL, D_MODEL)
    kkraw = k * k_k
    kk = kkraw * lax.rsqrt(jnp.maximum(_head_sums(kkraw * kkraw), 1e-24))
    k2 = k * (1.0 + (a - 1.0) * k_a)
    return r, lw, k2, v, -kk, kk * a, g


def _rwkv_project_rows(x, w_ref, mu_ref, w2_ref, w0_ref, a0_ref, kk_ref, ka_ref, hist_ref):
    xbf = x.astype(BF16)
    tm = xbf.shape[0]
    row = lax.broadcasted_iota(jnp.int32, (SUBLANES, 1), 0)

    def mixed(first_col, n_cols):
        cols = slice(first_col, first_col + n_cols)
        z = jnp.dot(xbf, w_ref[:, cols], preferred_element_type=F32)
        rolled = pltpu.roll(z, 1, 0)
        first = jnp.where(row == 0, hist_ref[SUBLANES - 1:SUBLANES, cols], rolled[0:SUBLANES])
        zprev = jnp.concatenate([first, rolled[SUBLANES:]], axis=0)
        hist_ref[:, cols] = z[tm - SUBLANES:tm]
        return z + mu_ref[:, cols] * (zprev - z)

    return _rwkv_prep(mixed, w2_ref[...], w0_ref[...], a0_ref[...], kk_ref[...], ka_ref[...])


def _rwkv_proj_step_kernel(x_ref, xp_ref, w_ref, mu_ref, w2_ref, w0_ref, a0_ref, kk_ref, ka_ref,
                           r_ref, lw_ref, k_ref, v_ref, al_ref, be_ref, g_ref):
    xbf = x_ref[...].astype(BF16)
    xpbf = xp_ref[...].astype(BF16)

    def mixed(first_col, n_cols):
        cols = slice(first_col, first_col + n_cols)
        z = jnp.dot(xbf, w_ref[:, cols], preferred_element_type=F32)
        zprev = jnp.dot(xpbf, w_ref[:, cols], preferred_element_type=F32)
        return z + mu_ref[:, cols] * (zprev - z)

    outs = _rwkv_prep(mixed, w2_ref[...], w0_ref[...], a0_ref[...], kk_ref[...], ka_ref[...])
    for o_ref, o in zip((r_ref, lw_ref, k_ref, v_ref, al_ref, be_ref, g_ref), outs):
        o_ref[...] = o


def _const_spec(shape):
    nd = len(shape)
    return pl.BlockSpec(shape, lambda *_: (0,) * nd, pipeline_mode=pl.Buffered(1))


def _rwkv_proj_step(x, xprev, wp):
    n = x.shape[0]
    full = pl.BlockSpec((n, D_MODEL), lambda i: (0, 0))
    vec = _const_spec((1, D_MODEL))
    return pl.pallas_call(
        _rwkv_proj_step_kernel,
        grid=(1,),
        in_specs=[full, full, _const_spec((D_MODEL, RWKV_COLS)), _const_spec((1, RWKV_COLS)),
                  _const_spec((LORA, 3 * D_MODEL)), vec, vec, vec, vec],
        out_specs=[full] * 7,
        out_shape=[jax.ShapeDtypeStruct((n, D_MODEL), F32)] * 7,
        compiler_params=pltpu.CompilerParams(
            dimension_semantics=("arbitrary",), vmem_limit_bytes=VMEM_LIMIT),
        name="rwkv_proj_step",
    )(x, xprev, wp["w_in"], wp["mu"], wp["w2ext"], wp["w0"], wp["a0"], wp["k_k"], wp["k_a"])


def _lru_gate_logits(u, wa_ref, wi_ref):
    ub = u.astype(BF16)
    ra, ia = [], []
    for q in range(D_LRU // LRU_GROUP):
        uq = ub[:, q * LRU_GROUP:(q + 1) * LRU_GROUP]
        ra.append(jnp.dot(uq, wa_ref[q], preferred_element_type=F32))
        ia.append(jnp.dot(uq, wi_ref[q], preferred_element_type=F32))
    return jnp.concatenate(ra, axis=1), jnp.concatenate(ia, axis=1)


def _lru_coeffs(ra, ia, ba, bi, sp):
    rg = _sigmoid(ra + ba)
    ig = _sigmoid(ia + bi)
    log_a = -LRU_C * rg * sp
    a = jnp.exp(log_a)
    m2 = -jnp.tanh(log_a) * (a * a + 1.0)
    mult = jnp.where(m2 > 0.0, m2 * lax.rsqrt(m2), 0.0)
    return a, mult, ig


def _lru_seq_kernel(x_ref, w_ref, cw_ref, cb_ref, wa_ref, ba_ref, wi_ref, bi_ref, lam_ref,
                    ga_ref, gyb_ref, conv_ref, hlast_ref,
                    xhist_ref, hc_ref, a_s, bx_s, h_s):
    t = pl.program_id(1)

    @pl.when(t == 0)
    def _():
        xhist_ref[...] = jnp.zeros_like(xhist_ref)
        hc_ref[...] = jnp.zeros_like(hc_ref)

    xbf = x_ref[0].astype(BF16)
    tm = xbf.shape[0]

    def proj(first_col, n_cols):
        cols = slice(RWKV_COLS + first_col, RWKV_COLS + first_col + n_cols)
        return jnp.dot(xbf, w_ref[:, cols], preferred_element_type=F32)

    xb = proj(0, D_LRU)
    gb = proj(D_LRU, D_LRU)
    row8 = lax.broadcasted_iota(jnp.int32, (SUBLANES, 1), 0)
    hist = xhist_ref[...]

    def shifted(k):
        rolled = pltpu.roll(xb, k, 0)
        first = jnp.where(row8 < k, pltpu.roll(hist, k, 0), rolled[0:SUBLANES])
        return jnp.concatenate([first, rolled[SUBLANES:]], axis=0)

    cw = cw_ref[...]
    u = (cb_ref[...] + cw[0:1] * shifted(3) + cw[1:2] * shifted(2) + cw[2:3] * shifted(1)
         + cw[3:4] * xb)
    last8 = xb[tm - SUBLANES:tm]
    xhist_ref[...] = last8
    conv_ref[0] = last8

    ra, ia = _lru_gate_logits(u, wa_ref, wi_ref)
    gelu_gb = _gelu_tanh(gb)
    zga = proj(2 * D_LRU, D_MODEL)
    sp = _softplus(-lam_ref[...])
    a, mult, ig = _lru_coeffs(ra, ia, ba_ref[...], bi_ref[...], sp)
    row = lax.broadcasted_iota(jnp.int32, (tm, 1), 0)
    mult = jnp.where(jnp.logical_and(row == 0, t == 0), 1.0, mult)
    a_s[...] = a
    bx_s[...] = mult * ig * u
    ga_ref[0] = _sigmoid(zga)
    zgb = proj(2 * D_LRU + D_MODEL, D_MODEL)

    hc = hc_ref[0:1, :]
    for blk in range(tm // SUBLANES):
        rows = slice(blk * SUBLANES, (blk + 1) * SUBLANES)
        av = a_s[rows, :]
        bv = bx_s[rows, :]
        for d in (1, 2, 4):
            a_sh = jnp.where(row8 < d, 1.0, pltpu.roll(av, d, 0))
            b_sh = jnp.where(row8 < d, 0.0, pltpu.roll(bv, d, 0))
            bv = av * b_sh + bv
            av = av * a_sh
        h_blk = bv + av * hc
        h_s[rows, :] = h_blk
        hc = h_blk[SUBLANES - 1:SUBLANES, :]
    hc_ref[0:1, :] = hc
    h = h_s[...]
    hlast_ref[0] = h[tm - SUBLANES:tm]
    gyb_ref[0] = (_sigmoid(zgb) * gelu_gb) * h


def _lru_step_kernel(x_ref, conv0_ref, h0_ref, w_ref, cw_ref, cb_ref, wa_ref, ba_ref, wi_ref,
                     bi_ref, lam_ref, ga_ref, gyb_ref, conv_ref, hlast_ref, *, steps, nseq):
    z = _dot(x_ref[...], w_ref[:, RWKV_COLS:])
    xb = z[:, 0:D_LRU]
    gb = z[:, D_LRU:2 * D_LRU]
    zga = z[:, 2 * D_LRU:2 * D_LRU + D_MODEL]
    zgb = z[:, 2 * D_LRU + D_MODEL:]
    n = steps * nseq
    hist = (CONV_W - 1) * nseq
    xext = jnp.concatenate([conv0_ref[...], xb], axis=0)

    def shifted(k):
        return xext[hist - k * nseq:hist - k * nseq + n]

    cw = cw_ref[...]
    u = (cb_ref[...] + cw[0:1] * shifted(3) + cw[1:2] * shifted(2) + cw[2:3] * shifted(1)
         + cw[3:4] * xb)
    conv_ref[...] = xext[n:n + hist]
    sp = _softplus(-lam_ref[...])
    ra, ia = _lru_gate_logits(u, wa_ref, wi_ref)
    a, mult, ig = _lru_coeffs(ra, ia, ba_ref[...], bi_ref[...], sp)
    bx = mult * ig * u
    h = h0_ref[...]
    hs = []
    for s in range(steps):
        h = a[s * nseq:(s + 1) * nseq] * h + bx[s * nseq:(s + 1) * nseq]
        hs.append(h)
    hlast_ref[...] = h
    hall = jnp.concatenate(hs, axis=0)
    ga_ref[...] = _sigmoid(zga)
    gyb_ref[...] = _sigmoid(zgb) * (hall * _gelu_tanh(gb))


def _lru_weight_specs():
    vec = _const_spec((1, D_LRU))
    gate_w = _const_spec((D_LRU // LRU_GROUP, LRU_GROUP, LRU_GROUP))
    return [_const_spec((D_MODEL, N_IN)), _const_spec((CONV_W, D_LRU)), vec,
            gate_w, vec, gate_w, vec, vec]


def _lru_weights(wp):
    return (wp["w_in"], wp["conv_w"], wp["conv_b"], wp["wa_bd"], wp["ba"], wp["wi_bd"],
            wp["bi"], wp["lam"])


def _lru_seq(x, wp, tm):
    b, t, _ = x.shape
    row_spec = pl.BlockSpec((1, tm, D_MODEL), lambda i, j: (i, j, 0))
    tail_spec = pl.BlockSpec((1, SUBLANES, D_LRU), lambda i, j: (i, 0, 0))
    return pl.pallas_call(
        _lru_seq_kernel,
        grid=(b, t // tm),
        in_specs=[row_spec] + _lru_weight_specs(),
        out_specs=[row_spec, row_spec, tail_spec, tail_spec],
        out_shape=[jax.ShapeDtypeStruct((b, t, D_MODEL), F32)] * 2
        + [jax.ShapeDtypeStruct((b, SUBLANES, D_LRU), F32)] * 2,
        scratch_shapes=[pltpu.VMEM((SUBLANES, D_LRU), F32), pltpu.VMEM((SUBLANES, D_LRU), F32),
                        pltpu.VMEM((tm, D_LRU), F32), pltpu.VMEM((tm, D_LRU), F32),
                        pltpu.VMEM((tm, D_LRU), F32)],
        compiler_params=pltpu.CompilerParams(
            dimension_semantics=("arbitrary", "arbitrary"), vmem_limit_bytes=VMEM_LIMIT),
        name="lru_seq",
    )(x, *_lru_weights(wp))


def _lru_step(x_tm, conv_tm, h0, wp, steps, nseq):
    n = steps * nseq
    hist = (CONV_W - 1) * nseq

    def full(r):
        return pl.BlockSpec((r, D_MODEL), lambda i: (0, 0))

    return pl.pallas_call(
        functools.partial(_lru_step_kernel, steps=steps, nseq=nseq),
        grid=(1,),
        in_specs=[full(n), full(hist), full(nseq)] + _lru_weight_specs(),
        out_specs=[full(n), full(n), full(hist), full(nseq)],
        out_shape=[jax.ShapeDtypeStruct((n, D_MODEL), F32)] * 2
        + [jax.ShapeDtypeStruct((hist, D_LRU), F32), jax.ShapeDtypeStruct((nseq, D_LRU), F32)],
        compiler_params=pltpu.CompilerParams(
            dimension_semantics=("arbitrary",), vmem_limit_bytes=VMEM_LIMIT),
        name="lru_step",
    )(x_tm, conv_tm, h0, *_lru_weights(wp))


def _wkv_consts():
    c = CHUNK
    lane = lax.broadcasted_iota(jnp.int32, (c, PAIR), 1)
    row = lax.broadcasted_iota(jnp.int32, (c, PAIR), 0)
    lane_lo = lane < HEAD
    col = lane % c
    strict = col < row
    incl = col <= row
    eye = jnp.where(col == row, 1.0, 0.0).astype(F32)
    r3 = lax.broadcasted_iota(jnp.int32, (c, 3 * c), 0)
    c3 = lax.broadcasted_iota(jnp.int32, (c, 3 * c), 1) % c
    tri3 = jnp.where(c3 <= r3, 1.0, 0.0).astype(BF16)
    rp = lax.broadcasted_iota(jnp.int32, (PAIR, PAIR), 0)
    cp = lax.broadcasted_iota(jnp.int32, (PAIR, PAIR), 1)
    blockdiag = (rp // HEAD) == (cp // HEAD)
    return dict(lane_lo=lane_lo, tri3=tri3, strict=strict, incl=incl, eye=eye,
                blockdiag=blockdiag)


def _split3(x):
    hi = x.astype(BF16)
    rest = x - hi.astype(F32)
    mid = rest.astype(BF16)
    lo = (rest - mid.astype(F32)).astype(BF16)
    return jnp.concatenate([hi, mid, lo], axis=0)


def _split_heads(x, lane_lo):
    zero = jnp.zeros_like(x)
    return jnp.concatenate([jnp.where(lane_lo, x, zero), jnp.where(lane_lo, zero, x)], axis=0)


def _run_interleaved(stage_gens):
    results = [None] * len(stage_gens)
    live = list(range(len(stage_gens)))
    while live:
        still = []
        for idx in live:
            try:
                next(stage_gens[idx])
                still.append(idx)
            except StopIteration as done:
                results[idx] = done.value
        live = still
    return results


def _wkv_chunk(r, lw, k, v, al, be, get_state, cn):
    c = CHUNK
    lane_lo = cn["lane_lo"]
    cum = jnp.dot(cn["tri3"], _split3(lw), preferred_element_type=F32)
    end = cum[c - 1:c, :]
    yield
    e_cum = jnp.exp(cum)
    e_neg = jnp.exp(-cum)
    e_end = jnp.exp(end)
    rt = r * e_cum
    at = al * jnp.exp(cum - lw)
    bt = be * e_neg
    kt = k * e_neg
    lhs2 = jnp.concatenate([at, rt], axis=0).astype(BF16)
    keys = jnp.concatenate([_split_heads(bt, lane_lo), _split_heads(kt, lane_lo)], axis=0)
    gram = _dot_nt(lhs2, keys)
    yield
    a_ab = jnp.where(cn["strict"], gram[0:c, 0:PAIR], 0.0)
    a_ak = jnp.where(cn["strict"], gram[0:c, PAIR:], 0.0)
    a_rb = jnp.where(cn["incl"], gram[c:, 0:PAIR], 0.0)
    a_rk = jnp.where(cn["incl"], gram[c:, PAIR:], 0.0)
    inv = cn["eye"] + a_ab
    n_iter = int(math.log2(c)) - 1
    pw = _dot(a_ab, _split_heads(a_ab, lane_lo))
    yield
    for it in range(n_iter):
        pw_heads = _split_heads(pw, lane_lo)
        if it < n_iter - 1:
            both = _dot(jnp.concatenate([inv, pw], axis=0), pw_heads)
            inv = inv + both[0:c]
            pw = both[c:]
        else:
            inv = inv + _dot(inv, pw_heads)
        yield
    av = _dot(jnp.concatenate([a_ak, a_rk], axis=0), _split_heads(v, lane_lo))
    state = get_state()
    while state is None:
        yield
        state = get_state()
    from_state = _dot_nt(lhs2, state)
    yield
    u = _dot(inv, _split_heads(from_state[0:c] + av[0:c], lane_lo))
    yield
    y = from_state[c:] + av[c:] + _dot(a_rb, _split_heads(u, lane_lo))
    yield
    uv = jnp.concatenate([u, v], axis=0)
    bk = jnp.concatenate([bt * e_end, kt * e_end], axis=0)
    return y, state * e_end + jnp.where(cn["blockdiag"], _dot_tn(uv, bk), 0.0)


def _wkv_readout(y, r, k, v, g, rk, lnx_g, lnx_b, lane_lo):
    mean = _seg_sum(y, lane_lo) * (1.0 / HEAD)
    bonus = _seg_sum(r * k * rk, lane_lo) * v
    yield
    yc = y - mean
    var = _seg_sum(yc * yc, lane_lo) * (1.0 / HEAD)
    yield
    yn = yc * lax.rsqrt(var + GN_EPS) * lnx_g + lnx_b
    return (yn + bonus) * g


def _from_blockdiag(state):
    return state[0:HEAD, 0:HEAD], pltpu.roll(state[HEAD:], HEAD, 1)[:, 0:HEAD]


def _rwkv_seq_kernel(x_ref, w_ref, mu_ref, w2_ref, w0_ref, a0_ref, kk_ref, ka_ref, rk_ref, lg_ref,
                     lb_ref, y_ref, s_out_ref, hist_ref, acts_scr, s_scr, yraw_scr, *, tm):
    t = pl.program_id(1)
    cn = _wkv_consts()
    n_pairs = s_scr.shape[0]
    tt = x_ref.shape[1]

    @pl.when(t == 0)
    def _():
        hist_ref[...] = jnp.zeros_like(hist_ref)
        s_scr[...] = jnp.zeros_like(s_scr)
        yraw_scr[...] = jnp.zeros_like(yraw_scr)

    for sub in range(tt // tm):
        rows = slice(sub * tm, (sub + 1) * tm)
        outs = _rwkv_project_rows(x_ref[0, rows, :], w_ref, mu_ref, w2_ref, w0_ref, a0_ref,
                                  kk_ref, ka_ref, hist_ref)
        for idx, o in enumerate(outs):
            acts_scr[idx, rows, :] = o

    r_ref, lw_ref, k_ref, v_ref, al_ref, be_ref, g_ref = (acts_scr.at[i] for i in range(7))
    n_chunks = tt // CHUNK

    def readout_stages(p, rows):
        ln = pl.ds(p * PAIR, PAIR)
        out = yield from _wkv_readout(
            yraw_scr[:, ln], r_ref[rows, ln], k_ref[rows, ln], v_ref[rows, ln],
            g_ref[rows, ln], rk_ref[:, ln], lg_ref[:, ln], lb_ref[:, ln], cn["lane_lo"])
        y_ref[0, rows, ln] = out

    def chunk_stages(p, rows, get_state):
        ln = pl.ds(p * PAIR, PAIR)
        return _wkv_chunk(r_ref[rows, ln], lw_ref[rows, ln], k_ref[rows, ln],
                          v_ref[rows, ln], al_ref[rows, ln], be_ref[rows, ln],
                          get_state, cn)

    def body(i, states):
        def rows_of(j):
            return pl.ds(pl.multiple_of((CHUNKS_IN_FLIGHT * i + j) * CHUNK, CHUNK), CHUNK)

        prev = pl.ds(pl.multiple_of(jnp.maximum(CHUNKS_IN_FLIGHT * i - 1, 0) * CHUNK, CHUNK),
                     CHUNK)
        end_states = [[None] * n_pairs for _ in range(CHUNKS_IN_FLIGHT)]

        def start_state(j, p):
            return states[p] if j == 0 else end_states[j - 1][p]

        def stages(j, p):
            ln = pl.ds(p * PAIR, PAIR)
            rows = rows_of(j)
            y, state = yield from chunk_stages(p, rows, functools.partial(start_state, j, p))
            end_states[j][p] = state
            if j == CHUNKS_IN_FLIGHT - 1:
                yraw_scr[:, ln] = y
                return
            out = yield from _wkv_readout(
                y, r_ref[rows, ln], k_ref[rows, ln], v_ref[rows, ln],
                g_ref[rows, ln], rk_ref[:, ln], lg_ref[:, ln], lb_ref[:, ln], cn["lane_lo"])
            y_ref[0, rows, ln] = out

        _run_interleaved([readout_stages(p, prev) for p in range(n_pairs)]
                         + [stages(j, p) for j in range(CHUNKS_IN_FLIGHT) for p in range(n_pairs)])
        return tuple(end_states[CHUNKS_IN_FLIGHT - 1])

    assert n_chunks % CHUNKS_IN_FLIGHT == 0
    states = lax.fori_loop(0, n_chunks // CHUNKS_IN_FLIGHT, body,
                           tuple(s_scr[p] for p in range(n_pairs)))
    last = pl.ds((n_chunks - 1) * CHUNK, CHUNK)
    _run_interleaved([readout_stages(p, last) for p in range(n_pairs)])
    for p in range(n_pairs):
        s_scr[p] = states[p]
        s_out_ref[0, 2 * p], s_out_ref[0, 2 * p + 1] = _from_blockdiag(states[p])


def _wkv_step_kernel(r_ref, lw_ref, k_ref, v_ref, al_ref, be_ref, g_ref, rk_ref, lg_ref, lb_ref,
                     s_in_ref, y_ref, s_out_ref, tr_scr, y_scr, *, steps):
    nseq = s_in_ref.shape[3]
    idx_r, idx_dec, idx_k, idx_v, idx_al, idx_be, idx_g = range(7)
    for idx, a_ref in enumerate((r_ref, lw_ref, k_ref, v_ref, al_ref, be_ref, g_ref)):
        for t in range(steps):
            tile = a_ref[t * nseq:(t + 1) * nseq, :].T
            tr_scr[idx, t] = jnp.exp(tile) if idx == idx_dec else tile

    for h in range(2):
        ch = slice(h * HEAD, (h + 1) * HEAD)

        def row_stages(i, h=h, ch=ch):
            s = s_in_ref[h, i]
            for t in range(steps):
                sa = jnp.sum(s * tr_scr[idx_al, t, ch, :], axis=0, keepdims=True)
                yield
                v_i = tr_scr[idx_v, t, pl.ds(h * HEAD + i, 1), :]
                s = (s * tr_scr[idx_dec, t, ch, :] + sa * tr_scr[idx_be, t, ch, :]
                     + v_i * tr_scr[idx_k, t, ch, :])
                y_scr[t, pl.ds(h * HEAD + i, 1), :] = jnp.sum(
                    s * tr_scr[idx_r, t, ch, :], axis=0, keepdims=True)
                yield
            s_out_ref[h, i] = s

        @pl.loop(0, HEAD // STEP_ROWS_IN_FLIGHT)
        def _(blk):
            _run_interleaved([row_stages(blk * STEP_ROWS_IN_FLIGHT + ii)
                              for ii in range(STEP_ROWS_IN_FLIGHT)])

    for t in range(steps):
        outs = []
        for h in range(2):
            ch = slice(h * HEAD, (h + 1) * HEAD)
            y = y_scr[t, ch, :]
            yc = y - jnp.mean(y, axis=0, keepdims=True)
            var = jnp.mean(yc * yc, axis=0, keepdims=True)
            yn = yc * lax.rsqrt(var + GN_EPS) * lg_ref[ch, :] + lb_ref[ch, :]
            rk_sum = jnp.sum(tr_scr[idx_r, t, ch, :] * tr_scr[idx_k, t, ch, :] * rk_ref[ch, :],
                             axis=0, keepdims=True)
            outs.append((yn + rk_sum * tr_scr[idx_v, t, ch, :]) * tr_scr[idx_g, t, ch, :])
        y_ref[t * nseq:(t + 1) * nseq, :] = jnp.concatenate(outs, axis=0).T


def _rwkv_seq(x, wp, tt, tm):
    b, t, _ = x.shape
    row_spec = pl.BlockSpec((1, tt, D_MODEL), lambda i, j: (i, j, 0))
    vec = _const_spec((1, D_MODEL))
    return pl.pallas_call(
        functools.partial(_rwkv_seq_kernel, tm=tm),
        grid=(b, t // tt),
        in_specs=[row_spec, _const_spec((D_MODEL, RWKV_COLS)), _const_spec((1, RWKV_COLS)),
                  _const_spec((LORA, 3 * D_MODEL))] + [vec] * 7,
        out_specs=[row_spec,
                   pl.BlockSpec((1, N_HEADS, HEAD, HEAD), lambda i, j: (i, 0, 0, 0))],
        out_shape=[jax.ShapeDtypeStruct((b, t, D_MODEL), F32),
                   jax.ShapeDtypeStruct((b, N_HEADS, HEAD, HEAD), F32)],
        scratch_shapes=[pltpu.VMEM((SUBLANES, RWKV_COLS), F32),
                        pltpu.VMEM((7, tt, D_MODEL), F32),
                        pltpu.VMEM((N_PAIRS, PAIR, PAIR), F32),
                        pltpu.VMEM((CHUNK, D_MODEL), F32)],
        compiler_params=pltpu.CompilerParams(
            dimension_semantics=("arbitrary", "arbitrary"), vmem_limit_bytes=VMEM_LIMIT),
        name="rwkv_seq",
    )(x, wp["w_in"], wp["mu"], wp["w2ext"], wp["w0"], wp["a0"], wp["k_k"], wp["k_a"],
      wp["r_k"], wp["lnx_g"], wp["lnx_b"])


def _wkv_step(acts, s_last, wp, steps):
    n = acts[0].shape[0]
    nseq = s_last.shape[3]
    row_spec = pl.BlockSpec((n, PAIR), lambda p: (0, p))
    col_spec = pl.BlockSpec((PAIR, nseq), lambda p: (p, 0))
    st_spec = pl.BlockSpec((2, HEAD, HEAD, nseq), lambda p: (p, 0, 0, 0))
    cols = [jnp.broadcast_to(wp[name].reshape(D_MODEL, 1), (D_MODEL, nseq))
            for name in ("r_k", "lnx_g", "lnx_b")]
    return pl.pallas_call(
        functools.partial(_wkv_step_kernel, steps=steps),
        grid=(N_PAIRS,),
        in_specs=[row_spec] * 7 + [col_spec] * 3 + [st_spec],
        out_specs=[row_spec, st_spec],
        out_shape=[jax.ShapeDtypeStruct((n, D_MODEL), F32),
                   jax.ShapeDtypeStruct(s_last.shape, F32)],
        scratch_shapes=[pltpu.VMEM((7, steps, PAIR, nseq), F32),
                        pltpu.VMEM((steps, PAIR, nseq), F32)],
        compiler_params=pltpu.CompilerParams(
            dimension_semantics=("arbitrary",), vmem_limit_bytes=VMEM_LIMIT),
        name="wkv_step",
    )(*acts, *cols, s_last)


def _post_kernel(x_ref, ya_ref, ga_ref, gyb_ref, wo_ref, l1g_ref, l1b_ref, wg_ref, wu_ref, wd_ref,
                 l2g_ref, l2b_ref, y_ref):
    x = x_ref[...]
    merged = ga_ref[...] * ya_ref[...] + gyb_ref[...]
    mix = _dot(merged, wo_ref[...])
    h1 = _layer_norm(ALPHA * x + mix, l1g_ref[...], l1b_ref[...])
    h1b = h1.astype(BF16)
    gate = jnp.dot(h1b, wg_ref[...], preferred_element_type=F32)
    up = jnp.dot(h1b, wu_ref[...], preferred_element_type=F32)
    act = (gate * _sigmoid(gate)) * up
    ffn = _dot(act, wd_ref[...])
    y_ref[...] = _layer_norm(ALPHA * h1 + ffn, l2g_ref[...], l2b_ref[...])


def _post(x, ya, ga, gyb, wp, tm):
    n = x.shape[0]
    row_spec = pl.BlockSpec((tm, D_MODEL), lambda i: (i, 0))
    vec = _const_spec((1, D_MODEL))
    return pl.pallas_call(
        _post_kernel,
        grid=(n // tm,),
        in_specs=[row_spec] * 4 + [_const_spec((D_MODEL, D_MODEL)), vec, vec,
                                   _const_spec((D_MODEL, D_FF)), _const_spec((D_MODEL, D_FF)),
                                   _const_spec((D_FF, D_MODEL)), vec, vec],
        out_specs=row_spec,
        out_shape=jax.ShapeDtypeStruct((n, D_MODEL), F32),
        compiler_params=pltpu.CompilerParams(
            dimension_semantics=("arbitrary",), vmem_limit_bytes=VMEM_LIMIT),
        name="post",
    )(x, ya, ga, gyb, wp["w_o"], wp["ln1_g"], wp["ln1_b"], wp["w_gate"], wp["w_up"], wp["w_down"],
      wp["ln2_g"], wp["ln2_b"])


def _prep_weights(w_in, tmix_mu, w0, w2_decay, a0, a2_iclr, g2_gate, k_k, k_a, r_k, lnx_g, lnx_b,
                  conv_w, conv_b, lru_wa, lru_ba, lru_wi, lru_bi, lru_lambda, w_o,
                  ln1_g, ln1_b, w_ffn_gate, w_ffn_up, w_ffn_down, ln2_g, ln2_b):
    row = lambda v: v.reshape(1, -1).astype(F32)
    zeros = lambda r: jnp.zeros((r, D_MODEL), F32)
    w2ext = jnp.concatenate([
        jnp.concatenate([w2_decay, zeros(DECAY_LORA), zeros(DECAY_LORA)], axis=1),
        jnp.concatenate([zeros(ICLR_LORA), a2_iclr, zeros(ICLR_LORA)], axis=1),
        jnp.concatenate([zeros(GATE_LORA), zeros(GATE_LORA), g2_gate], axis=1)], axis=0)

    def gate_blockdiag(w):
        per = LRU_GROUP // LRU_BS
        w4 = w.reshape(LRU_BLOCKS // per, per, LRU_BS, LRU_BS)
        eye = jnp.eye(per, dtype=w.dtype)
        bd = jnp.einsum("gpcd,pq->gpcqd", w4, eye)
        return bd.reshape(LRU_BLOCKS // per, LRU_GROUP, LRU_GROUP).astype(BF16)

    return dict(
        w_in=w_in.astype(BF16),
        mu=row(tmix_mu), w2ext=w2ext.astype(BF16), w0=row(w0), a0=row(a0), k_k=row(k_k),
        k_a=row(k_a), r_k=row(r_k), lnx_g=row(lnx_g), lnx_b=row(lnx_b),
        conv_w=conv_w.astype(F32), conv_b=row(conv_b), wa_bd=gate_blockdiag(lru_wa),
        ba=row(lru_ba), wi_bd=gate_blockdiag(lru_wi), bi=row(lru_bi), lam=row(lru_lambda),
        w_o=w_o.astype(BF16), ln1_g=row(ln1_g), ln1_b=row(ln1_b), w_gate=w_ffn_gate.astype(BF16),
        w_up=w_ffn_up.astype(BF16), w_down=w_ffn_down.astype(BF16), ln2_g=row(ln2_g),
        ln2_b=row(ln2_b))


def _prompt_layer(x, wp, tm, tt, tm_post):
    b, t, _ = x.shape
    ya, s_heads = _rwkv_seq(x, wp, tt, tm)
    ga, gyb, conv_tail, h_tail = _lru_seq(x, wp, tm)
    y = _post(x.reshape(b * t, D_MODEL), ya.reshape(b * t, D_MODEL), ga.reshape(b * t, D_MODEL),
              gyb.reshape(b * t, D_MODEL), wp, tm_post).reshape(b, t, D_MODEL)
    return (y, x[:, -1], s_heads, conv_tail[:, SUBLANES - (CONV_W - 1):],
            h_tail[:, SUBLANES - 1])


def _sample_layer(x, shift_buf, wkv0, conv_buf, h0, wp):
    b, t, _ = x.shape
    n = b * t
    x_tm = x.transpose(1, 0, 2).reshape(n, D_MODEL)
    xprev_tm = jnp.concatenate([shift_buf, x_tm[:n - b]], axis=0)
    acts = _rwkv_proj_step(x_tm, xprev_tm, wp)
    ya_tm, s_last = _wkv_step(acts, wkv0.transpose(1, 2, 3, 0), wp, t)
    conv_tm = conv_buf.transpose(1, 0, 2).reshape((CONV_W - 1) * b, D_LRU)
    ga_tm, gyb_tm, conv_new_tm, h_last = _lru_step(x_tm, conv_tm, h0, wp, t, b)
    y_tm = _post(x_tm, ya_tm, ga_tm, gyb_tm, wp, min(n, 256))
    y = y_tm.reshape(t, b, D_MODEL).transpose(1, 0, 2)
    conv_new = conv_new_tm.reshape(CONV_W - 1, b, D_LRU).transpose(1, 0, 2)
    return y, x[:, -1], s_last.transpose(3, 0, 1, 2), conv_new, h_last


def kernel(x_prompt, x_sample, state_shift, state_wkv, state_conv, state_lru, w_in, tmix_mu, w0, w2_decay, a0, a2_iclr, g2_gate, k_k, k_a, r_k, lnx_g, lnx_b, conv_w, conv_b, lru_wa, lru_ba, lru_wi, lru_bi, lru_lambda, w_o, ln1_g, ln1_b, w_ffn_gate, w_ffn_up, w_ffn_down, ln2_g, ln2_b):
    params = (w_in, tmix_mu, w0, w2_decay, a0, a2_iclr, g2_gate, k_k, k_a, r_k, lnx_g, lnx_b,
              conv_w, conv_b, lru_wa, lru_ba, lru_wi, lru_bi, lru_lambda, w_o,
              ln1_g, ln1_b, w_ffn_gate, w_ffn_up, w_ffn_down, ln2_g, ln2_b)
    wp = _prep_weights(*[p[0] for p in params])
    yp, sh_p, wkv_p, conv_p, lru_p = _prompt_layer(x_prompt, wp, 256, 512, 256)
    ys, sh_s, wkv_s, conv_s, lru_s = _sample_layer(
        x_sample, state_shift[0], state_wkv[0], state_conv[0], state_lru[0], wp)
    return (yp, ys, sh_p[None], wkv_p[None], conv_p[None], lru_p[None],
            sh_s[None], wkv_s[None], conv_s[None], lru_s[None])
```

```python
import functools
import math

import jax
import jax.numpy as jnp
from jax import lax
from jax.experimental import pallas as pl
from jax.experimental.pallas import tpu as pltpu

F32 = jnp.float32
BF16 = jnp.bfloat16

D_MODEL = 1024
HEAD = 64
N_HEADS = D_MODEL // HEAD
PAIR = 2 * HEAD
N_PAIRS = N_HEADS // 2
DECAY_LORA = 64
ICLR_LORA = 64
GATE_LORA = 128
LORA = DECAY_LORA + ICLR_LORA + GATE_LORA
RWKV_COLS = 3 * D_MODEL + LORA
GN_EPS = HEAD * 1e-5
D_LRU = D_MODEL
LRU_BLOCKS = 16
LRU_BS = D_LRU // LRU_BLOCKS
LRU_GROUP = 256
CONV_W = 4
LRU_C = 8.0
LRU_COLS = 2 * D_LRU + 2 * D_MODEL
N_IN = RWKV_COLS + LRU_COLS
D_FF = 2816
ALPHA = 2.0 ** 0.25
LN_EPS = 1e-5

SUBLANES = 8
CHUNK = 64
CHUNKS_IN_FLIGHT = 4
STEP_ROWS_IN_FLIGHT = 8
VMEM_LIMIT = 56 * 1024 * 1024


def _softplus(x):
    return jnp.maximum(x, 0.0) + jnp.log1p(jnp.exp(-jnp.abs(x)))


def _softplus_plain(x):
    return jnp.maximum(x, 0.0) + jnp.log(1.0 + jnp.exp(-jnp.abs(x)))


def _sigmoid(x):
    return 0.5 * jnp.tanh(0.5 * x) + 0.5


def _gelu_tanh(x):
    c = math.sqrt(2.0 / math.pi)
    return x * (0.5 * (1.0 + jnp.tanh(c * (x + 0.044715 * (x * x * x)))))


def _layer_norm(x, g, b):
    mu = jnp.mean(x, axis=-1, keepdims=True)
    xc = x - mu
    var = jnp.mean(xc * xc, axis=-1, keepdims=True)
    return xc * lax.rsqrt(var + LN_EPS) * g + b


def _dot(a, b):
    return jnp.dot(a.astype(BF16), b.astype(BF16), preferred_element_type=F32)


def _dot_nt(a, b):
    return lax.dot_general(a.astype(BF16), b.astype(BF16), (((1,), (1,)), ((), ())),
                           preferred_element_type=F32)


def _dot_tn(a, b):
    return lax.dot_general(a.astype(BF16), b.astype(BF16), (((0,), (0,)), ((), ())),
                           preferred_element_type=F32)


def _seg_sum(x, lane_lo):
    s0 = jnp.sum(jnp.where(lane_lo, x, 0.0), axis=-1, keepdims=True)
    s1 = jnp.sum(jnp.where(lane_lo, 0.0, x), axis=-1, keepdims=True)
    return jnp.where(lane_lo, s0, s1)


def _head_sums(x):
    lane_lo = lax.broadcasted_iota(jnp.int32, (x.shape[0], PAIR), 1) < HEAD
    return jnp.concatenate([_seg_sum(x[:, p * PAIR:(p + 1) * PAIR], lane_lo)
                            for p in range(N_PAIRS)], axis=1)


def _rwkv_prep(mixed, w2ext, w0, a0, k_k, k_a):
    lo = mixed(3 * D_MODEL, LORA)
    lane = lax.broadcasted_iota(jnp.int32, lo.shape, 1)
    act = jnp.where(lane < DECAY_LORA, jnp.tanh(lo),
                    jnp.where(lane < DECAY_LORA + ICLR_LORA, lo, _sigmoid(lo)))
    r = mixed(0, D_MODEL)
    lora = _dot(act, w2ext)
    k = mixed(D_MODEL, D_MODEL)
    w = -_softplus_plain(-(w0 + lora[:, 0:D_MODEL])) - 0.5
    lw = -jnp.exp(w)
    a = _sigmoid(a0 + lora[:, D_MODEL:2 * D_MODEL])
    g = lora[:, 2 * D_MODEL:3 * D_MODEL]
    v = mixed(2 * D_MODEL, D_MODEL)
    kkraw = k * k_k
    kk = kkraw * lax.rsqrt(jnp.maximum(_head_sums(kkraw * kkraw), 1e-24))
    k2 = k * (1.0 + (a - 1.0) * k_a)
    return r, lw, k2, v, -kk, kk * a, g


def _rwkv_project_rows(x, w_ref, mu_ref, w2_ref, w0_ref, a0_ref, kk_ref, ka_ref, hist_ref):
    xbf = x.astype(BF16)
    tm = xbf.shape[0]
    row = lax.broadcasted_iota(jnp.int32, (SUBLANES, 1), 0)

    def mixed(first_col, n_cols):
        cols = slice(first_col, first_col + n_cols)
        z = jnp.dot(xbf, w_ref[:, cols], preferred_element_type=F32)
        rolled = pltpu.roll(z, 1, 0)
        first = jnp.where(row == 0, hist_ref[SUBLANES - 1:SUBLANES, cols], rolled[0:SUBLANES])
        zprev = jnp.concatenate([first, rolled[SUBLANES:]], axis=0)
        hist_ref[:, cols] = z[tm - SUBLANES:tm]
        return z + mu_ref[:, cols] * (zprev - z)

    return _rwkv_prep(mixed, w2_ref[...], w0_ref[...], a0_ref[...], kk_ref[...], ka_ref[...])


def _rwkv_proj_step_kernel(x_ref, xp_ref, w_ref, mu_ref, w2_ref, w0_ref, a0_ref, kk_ref, ka_ref,
                           r_ref, lw_ref, k_ref, v_ref, al_ref, be_ref, g_ref):
    xbf = x_ref[...].astype(BF16)
    xpbf = xp_ref[...].astype(BF16)

    def mixed(first_col, n_cols):
        cols = slice(first_col, first_col + n_cols)
        z = jnp.dot(xbf, w_ref[:, cols], preferred_element_type=F32)
        zprev = jnp.dot(xpbf, w_ref[:, cols], preferred_element_type=F32)
        return z + mu_ref[:, cols] * (zprev - z)

    outs = _rwkv_prep(mixed, w2_ref[...], w0_ref[...], a0_ref[...], kk_ref[...], ka_ref[...])
    for o_ref, o in zip((r_ref, lw_ref, k_ref, v_ref, al_ref, be_ref, g_ref), outs):
        o_ref[...] = o


def _const_spec(shape):
    nd = len(shape)
    return pl.BlockSpec(shape, lambda *_: (0,) * nd, pipeline_mode=pl.Buffered(1))


def _rwkv_proj_step(x, xprev, wp):
    n = x.shape[0]
    full = pl.BlockSpec((n, D_MODEL), lambda i: (0, 0))
    vec = _const_spec((1, D_MODEL))
    return pl.pallas_call(
        _rwkv_proj_step_kernel,
        grid=(1,),
        in_specs=[full, full, _const_spec((D_MODEL, RWKV_COLS)), _const_spec((1, RWKV_COLS)),
                  _const_spec((LORA, 3 * D_MODEL)), vec, vec, vec, vec],
        out_specs=[full] * 7,
        out_shape=[jax.ShapeDtypeStruct((n, D_MODEL), F32)] * 7,
        compiler_params=pltpu.CompilerParams(
            dimension_semantics=("arbitrary",), vmem_limit_bytes=VMEM_LIMIT),
        name="rwkv_proj_step",
    )(x, xprev, wp["w_in"], wp["mu"], wp["w2ext"], wp["w0"], wp["a0"], wp["k_k"], wp["k_a"])


def _lru_gate_logits(u, wa_ref, wi_ref):
    ub = u.astype(BF16)
    ra, ia = [], []
    for q in range(D_LRU // LRU_GROUP):
        uq = ub[:, q * LRU_GROUP:(q + 1) * LRU_GROUP]
        ra.append(jnp.dot(uq, wa_ref[q], preferred_element_type=F32))
        ia.append(jnp.dot(uq, wi_ref[q], preferred_element_type=F32))
    return jnp.concatenate(ra, axis=1), jnp.concatenate(ia, axis=1)


def _lru_coeffs(ra, ia, ba, bi, sp):
    rg = _sigmoid(ra + ba)
    ig = _sigmoid(ia + bi)
    log_a = -LRU_C * rg * sp
    a = jnp.exp(log_a)
    m2 = -jnp.tanh(log_a) * (a * a + 1.0)
    mult = jnp.where(m2 > 0.0, m2 * lax.rsqrt(m2), 0.0)
    return a, mult, ig


def _lru_seq_kernel(x_ref, w_ref, cw_ref, cb_ref, wa_ref, ba_ref, wi_ref, bi_ref, lam_ref,
                    ga_ref, gyb_ref, conv_ref, hlast_ref,
                    xhist_ref, hc_ref, a_s, bx_s, h_s):
    t = pl.program_id(1)

    @pl.when(t == 0)
    def _():
        xhist_ref[...] = jnp.zeros_like(xhist_ref)
        hc_ref[...] = jnp.zeros_like(hc_ref)

    xbf = x_ref[0].astype(BF16)
    tm = xbf.shape[0]

    def proj(first_col, n_cols):
        cols = slice(RWKV_COLS + first_col, RWKV_COLS + first_col + n_cols)
        return jnp.dot(xbf, w_ref[:, cols], preferred_element_type=F32)

    xb = proj(0, D_LRU)
    gb = proj(D_LRU, D_LRU)
    row8 = lax.broadcasted_iota(jnp.int32, (SUBLANES, 1), 0)
    hist = xhist_ref[...]

    def shifted(k):
        rolled = pltpu.roll(xb, k, 0)
        first = jnp.where(row8 < k, pltpu.roll(hist, k, 0), rolled[0:SUBLANES])
        return jnp.concatenate([first, rolled[SUBLANES:]], axis=0)

    cw = cw_ref[...]
    u = (cb_ref[...] + cw[0:1] * shifted(3) + cw[1:2] * shifted(2) + cw[2:3] * shifted(1)
         + cw[3:4] * xb)
    last8 = xb[tm - SUBLANES:tm]
    xhist_ref[...] = last8
    conv_ref[0] = last8

    ra, ia = _lru_gate_logits(u, wa_ref, wi_ref)
    gelu_gb = _gelu_tanh(gb)
    zga = proj(2 * D_LRU, D_MODEL)
    sp = _softplus(-lam_ref[...])
    a, mult, ig = _lru_coeffs(ra, ia, ba_ref[...], bi_ref[...], sp)
    row = lax.broadcasted_iota(jnp.int32, (tm, 1), 0)
    mult = jnp.where(jnp.logical_and(row == 0, t == 0), 1.0, mult)
    a_s[...] = a
    bx_s[...] = mult * ig * u
    ga_ref[0] = _sigmoid(zga)
    zgb = proj(2 * D_LRU + D_MODEL, D_MODEL)

    hc = hc_ref[0:1, :]
    for blk in range(tm // SUBLANES):
        rows = slice(blk * SUBLANES, (blk + 1) * SUBLANES)
        av = a_s[rows, :]
        bv = bx_s[rows, :]
        for d in (1, 2, 4):
            a_sh = jnp.where(row8 < d, 1.0, pltpu.roll(av, d, 0))
            b_sh = jnp.where(row8 < d, 0.0, pltpu.roll(bv, d, 0))
            bv = av * b_sh + bv
            av = av * a_sh
        h_blk = bv + av * hc
        h_s[rows, :] = h_blk
        hc = h_blk[SUBLANES - 1:SUBLANES, :]
    hc_ref[0:1, :] = hc
    h = h_s[...]
    hlast_ref[0] = h[tm - SUBLANES:tm]
    gyb_ref[0] = (_sigmoid(zgb) * gelu_gb) * h


def _lru_step_kernel(x_ref, conv0_ref, h0_ref, w_ref, cw_ref, cb_ref, wa_ref, ba_ref, wi_ref,
                     bi_ref, lam_ref, ga_ref, gyb_ref, conv_ref, hlast_ref, *, steps, nseq):
    z = _dot(x_ref[...], w_ref[:, RWKV_COLS:])
    xb = z[:, 0:D_LRU]
    gb = z[:, D_LRU:2 * D_LRU]
    zga = z[:, 2 * D_LRU:2 * D_LRU + D_MODEL]
    zgb = z[:, 2 * D_LRU + D_MODEL:]
    n = steps * nseq
    hist = (CONV_W - 1) * nseq
    xext = jnp.concatenate([conv0_ref[...], xb], axis=0)

    def shifted(k):
        return xext[hist - k * nseq:hist - k * nseq + n]

    cw = cw_ref[...]
    u = (cb_ref[...] + cw[0:1] * shifted(3) + cw[1:2] * shifted(2) + cw[2:3] * shifted(1)
         + cw[3:4] * xb)
    conv_ref[...] = xext[n:n + hist]
    sp = _softplus(-lam_ref[...])
    ra, ia = _lru_gate_logits(u, wa_ref, wi_ref)
    a, mult, ig = _lru_coeffs(ra, ia, ba_ref[...], bi_ref[...], sp)
    bx = mult * ig * u
    h = h0_ref[...]
    hs = []
    for s in range(steps):
        h = a[s * nseq:(s + 1) * nseq] * h + bx[s * nseq:(s + 1) * nseq]
        hs.append(h)
    hlast_ref[...] = h
    hall = jnp.concatenate(hs, axis=0)
    ga_ref[...] = _sigmoid(zga)
    gyb_ref[...] = _sigmoid(zgb) * (hall * _gelu_tanh(gb))


def _lru_weight_specs():
    vec = _const_spec((1, D_LRU))
    gate_w = _const_spec((D_LRU // LRU_GROUP, LRU_GROUP, LRU_GROUP))
    return [_const_spec((D_MODEL, N_IN)), _const_spec((CONV_W, D_LRU)), vec,
            gate_w, vec, gate_w, vec, vec]


def _lru_weights(wp):
    return (wp["w_in"], wp["conv_w"], wp["conv_b"], wp["wa_bd"], wp["ba"], wp["wi_bd"],
            wp["bi"], wp["lam"])


def _lru_seq(x, wp, tm):
    b, t, _ = x.shape
    row_spec = pl.BlockSpec((1, tm, D_MODEL), lambda i, j: (i, j, 0))
    tail_spec = pl.BlockSpec((1, SUBLANES, D_LRU), lambda i, j: (i, 0, 0))
    return pl.pallas_call(
        _lru_seq_kernel,
        grid=(b, t // tm),
        in_specs=[row_spec] + _lru_weight_specs(),
        out_specs=[row_spec, row_spec, tail_spec, tail_spec],
        out_shape=[jax.ShapeDtypeStruct((b, t, D_MODEL), F32)] * 2
        + [jax.ShapeDtypeStruct((b, SUBLANES, D_LRU), F32)] * 2,
        scratch_shapes=[pltpu.VMEM((SUBLANES, D_LRU), F32), pltpu.VMEM((SUBLANES, D_LRU), F32),
                        pltpu.VMEM((tm, D_LRU), F32), pltpu.VMEM((tm, D_LRU), F32),
                        pltpu.VMEM((tm, D_LRU), F32)],
        compiler_params=pltpu.CompilerParams(
            dimension_semantics=("arbitrary", "arbitrary"), vmem_limit_bytes=VMEM_LIMIT),
        name="lru_seq",
    )(x, *_lru_weights(wp))


def _lru_step(x_tm, conv_tm, h0, wp, steps, nseq):
    n = steps * nseq
    hist = (CONV_W - 1) * nseq

    def full(r):
        return pl.BlockSpec((r, D_MODEL), lambda i: (0, 0))

    return pl.pallas_call(
        functools.partial(_lru_step_kernel, steps=steps, nseq=nseq),
        grid=(1,),
        in_specs=[full(n), full(hist), full(nseq)] + _lru_weight_specs(),
        out_specs=[full(n), full(n), full(hist), full(nseq)],
        out_shape=[jax.ShapeDtypeStruct((n, D_MODEL), F32)] * 2
        + [jax.ShapeDtypeStruct((hist, D_LRU), F32), jax.ShapeDtypeStruct((nseq, D_LRU), F32)],
        compiler_params=pltpu.CompilerParams(
            dimension_semantics=("arbitrary",), vmem_limit_bytes=VMEM_LIMIT),
        name="lru_step",
    )(x_tm, conv_tm, h0, *_lru_weights(wp))


def _wkv_consts():
    c = CHUNK
    lane = lax.broadcasted_iota(jnp.int32, (c, PAIR), 1)
    row = lax.broadcasted_iota(jnp.int32, (c, PAIR), 0)
    lane_lo = lane < HEAD
    col = lane % c
    strict = col < row
    incl = col <= row
    eye = jnp.where(col == row, 1.0, 0.0).astype(F32)
    r3 = lax.broadcasted_iota(jnp.int32, (c, 3 * c), 0)
    c3 = lax.broadcasted_iota(jnp.int32, (c, 3 * c), 1) % c
    tri3 = jnp.where(c3 <= r3, 1.0, 0.0).astype(BF16)
    rp = lax.broadcasted_iota(jnp.int32, (PAIR, PAIR), 0)
    cp = lax.broadcasted_iota(jnp.int32, (PAIR, PAIR), 1)
    blockdiag = (rp // HEAD) == (cp // HEAD)
    return dict(lane_lo=lane_lo, tri3=tri3, strict=strict, incl=incl, eye=eye,
                blockdiag=blockdiag)


def _split3(x):
    hi = x.astype(BF16)
    rest = x - hi.astype(F32)
    mid = rest.astype(BF16)
    lo = (rest - mid.astype(F32)).astype(BF16)
    return jnp.concatenate([hi, mid, lo], axis=0)


def _split_heads(x, lane_lo):
    zero = jnp.zeros_like(x)
    return jnp.concatenate([jnp.where(lane_lo, x, zero), jnp.where(lane_lo, zero, x)], axis=0)


def _run_interleaved(stage_gens):
    results = [None] * len(stage_gens)
    live = list(range(len(stage_gens)))
    while live:
        still = []
        for idx in live:
            try:
                next(stage_gens[idx])
                still.append(idx)
            except StopIteration as done:
                results[idx] = done.value
        live = still
    return results


def _wkv_chunk(r, lw, k, v, al, be, get_state, cn):
    c = CHUNK
    lane_lo = cn["lane_lo"]
    cum = jnp.dot(cn["tri3"], _split3(lw), preferred_element_type=F32)
    end = cum[c - 1:c, :]
    yield
    e_cum = jnp.exp(cum)
    e_neg = jnp.exp(-cum)
    e_end = jnp.exp(end)
    rt = r * e_cum
    at = al * jnp.exp(cum - lw)
    bt = be * e_neg
    kt = k * e_neg
    lhs2 = jnp.concatenate([at, rt], axis=0).astype(BF16)
    keys = jnp.concatenate([_split_heads(bt, lane_lo), _split_heads(kt, lane_lo)], axis=0)
    gram = _dot_nt(lhs2, keys)
    yield
    a_ab = jnp.where(cn["strict"], gram[0:c, 0:PAIR], 0.0)
    a_ak = jnp.where(cn["strict"], gram[0:c, PAIR:], 0.0)
    a_rb = jnp.where(cn["incl"], gram[c:, 0:PAIR], 0.0)
    a_rk = jnp.where(cn["incl"], gram[c:, PAIR:], 0.0)
    inv = cn["eye"] + a_ab
    n_iter = int(math.log2(c)) - 1
    pw = _dot(a_ab, _split_heads(a_ab, lane_lo))
    yield
    for it in range(n_iter):
        pw_heads = _split_heads(pw, lane_lo)
        if it < n_iter - 1:
            both = _dot(jnp.concatenate([inv, pw], axis=0), pw_heads)
            inv = inv + both[0:c]
            pw = both[c:]
        else:
            inv = inv + _dot(inv, pw_heads)
        yield
    av = _dot(jnp.concatenate([a_ak, a_rk], axis=0), _split_heads(v, lane_lo))
    state = get_state()
    while state is None:
        yield
        state = get_state()
    from_state = _dot_nt(lhs2, state)
    yield
    u = _dot(inv, _split_heads(from_state[0:c] + av[0:c], lane_lo))
    yield
    y = from_state[c:] + av[c:] + _dot(a_rb, _split_heads(u, lane_lo))
    yield
    uv = jnp.concatenate([u, v], axis=0)
    bk = jnp.concatenate([bt * e_end, kt * e_end], axis=0)
    return y, state * e_end + jnp.where(cn["blockdiag"], _dot_tn(uv, bk), 0.0)


def _wkv_readout(y, r, k, v, g, rk, lnx_g, lnx_b, lane_lo):
    mean = _seg_sum(y, lane_lo) * (1.0 / HEAD)
    bonus = _seg_sum(r * k * rk, lane_lo) * v
    yield
    yc = y - mean
    var = _seg_sum(yc * yc, lane_lo) * (1.0 / HEAD)
    yield
    yn = yc * lax.rsqrt(var + GN_EPS) * lnx_g + lnx_b
    return (yn + bonus) * g


def _from_blockdiag(state):
    return state[0:HEAD, 0:HEAD], pltpu.roll(state[HEAD:], HEAD, 1)[:, 0:HEAD]


def _rwkv_seq_kernel(x_ref, w_ref, mu_ref, w2_ref, w0_ref, a0_ref, kk_ref, ka_ref, rk_ref, lg_ref,
                     lb_ref, y_ref, s_out_ref, hist_ref, acts_scr, s_scr, yraw_scr, *, tm):
    t = pl.program_id(1)
    cn = _wkv_consts()
    n_pairs = s_scr.shape[0]
    tt = x_ref.shape[1]

    @pl.when(t == 0)
    def _():
        hist_ref[...] = jnp.zeros_like(hist_ref)
        s_scr[...] = jnp.zeros_like(s_scr)
        yraw_scr[...] = jnp.zeros_like(yraw_scr)

    for sub in range(tt // tm):
        rows = slice(sub * tm, (sub + 1) * tm)
        outs = _rwkv_project_rows(x_ref[0, rows, :], w_ref, mu_ref, w2_ref, w0_ref, a0_ref,
                                  kk_ref, ka_ref, hist_ref)
        for idx, o in enumerate(outs):
            acts_scr[idx, rows, :] = o

    r_ref, lw_ref, k_ref, v_ref, al_ref, be_ref, g_ref = (acts_scr.at[i] for i in range(7))
    n_chunks = tt // CHUNK

    def readout_stages(p, rows):
        ln = pl.ds(p * PAIR, PAIR)
        out = yield from _wkv_readout(
            yraw_scr[:, ln], r_ref[rows, ln], k_ref[rows, ln], v_ref[rows, ln],
            g_ref[rows, ln], rk_ref[:, ln], lg_ref[:, ln], lb_ref[:, ln], cn["lane_lo"])
        y_ref[0, rows, ln] = out

    def chunk_stages(p, rows, get_state):
        ln = pl.ds(p * PAIR, PAIR)
        return _wkv_chunk(r_ref[rows, ln], lw_ref[rows, ln], k_ref[rows, ln],
                          v_ref[rows, ln], al_ref[rows, ln], be_ref[rows, ln],
                          get_state, cn)

    def body(i, states):
        def rows_of(j):
            return pl.ds(pl.multiple_of((CHUNKS_IN_FLIGHT * i + j) * CHUNK, CHUNK), CHUNK)

        prev = pl.ds(pl.multiple_of(jnp.maximum(CHUNKS_IN_FLIGHT * i - 1, 0) * CHUNK, CHUNK),
                     CHUNK)
        end_states = [[None] * n_pairs for _ in range(CHUNKS_IN_FLIGHT)]

        def start_state(j, p):
            return states[p] if j == 0 else end_states[j - 1][p]

        def stages(j, p):
            ln = pl.ds(p * PAIR, PAIR)
            rows = rows_of(j)
            y, state = yield from chunk_stages(p, rows, functools.partial(start_state, j, p))
            end_states[j][p] = state
            if j == CHUNKS_IN_FLIGHT - 1:
                yraw_scr[:, ln] = y
                return
            out = yield from _wkv_readout(
                y, r_ref[rows, ln], k_ref[rows, ln], v_ref[rows, ln],
                g_ref[rows, ln], rk_ref[:, ln], lg_ref[:, ln], lb_ref[:, ln], cn["lane_lo"])
            y_ref[0, rows, ln] = out

        _run_interleaved([readout_stages(p, prev) for p in range(n_pairs)]
                         + [stages(j, p) for j in range(CHUNKS_IN_FLIGHT) for p in range(n_pairs)])
        return tuple(end_states[CHUNKS_IN_FLIGHT - 1])

    assert n_chunks % CHUNKS_IN_FLIGHT == 0
    states = lax.fori_loop(0, n_chunks // CHUNKS_IN_FLIGHT, body,
                           tuple(s_scr[p] for p in range(n_pairs)))
    last = pl.ds((n_chunks - 1) * CHUNK, CHUNK)
    _run_interleaved([readout_stages(p, last) for p in range(n_pairs)])
    for p in range(n_pairs):
        s_scr[p] = states[p]
        s_out_ref[0, 2 * p], s_out_ref[0, 2 * p + 1] = _from_blockdiag(states[p])


def _wkv_step_kernel(r_ref, lw_ref, k_ref, v_ref, al_ref, be_ref, g_ref, rk_ref, lg_ref, lb_ref,
                     s_in_ref, y_ref, s_out_ref, tr_scr, y_scr, *, steps):
    nseq = s_in_ref.shape[3]
    idx_r, idx_dec, idx_k, idx_v, idx_al, idx_be, idx_g = range(7)
    for idx, a_ref in enumerate((r_ref, lw_ref, k_ref, v_ref, al_ref, be_ref, g_ref)):
        for t in range(steps):
            tile = a_ref[t * nseq:(t + 1) * nseq, :].T
            tr_scr[idx, t] = jnp.exp(tile) if idx == idx_dec else tile

    for h in range(2):
        ch = slice(h * HEAD, (h + 1) * HEAD)

        def row_stages(i, h=h, ch=ch):
            s = s_in_ref[h, i]
            for t in range(steps):
                sa = jnp.sum(s * tr_scr[idx_al, t, ch, :], axis=0, keepdims=True)
                yield
                v_i = tr_scr[idx_v, t, pl.ds(h * HEAD + i, 1), :]
                s = (s * tr_scr[idx_dec, t, ch, :] + sa * tr_scr[idx_be, t, ch, :]
                     + v_i * tr_scr[idx_k, t, ch, :])
                y_scr[t, pl.ds(h * HEAD + i, 1), :] = jnp.sum(
                    s * tr_scr[idx_r, t, ch, :], axis=0, keepdims=True)
                yield
            s_out_ref[h, i] = s

        @pl.loop(0, HEAD // STEP_ROWS_IN_FLIGHT)
        def _(blk):
            _run_interleaved([row_stages(blk * STEP_ROWS_IN_FLIGHT + ii)
                              for ii in range(STEP_ROWS_IN_FLIGHT)])

    for t in range(steps):
        outs = []
        for h in range(2):
            ch = slice(h * HEAD, (h + 1) * HEAD)
            y = y_scr[t, ch, :]
            yc = y - jnp.mean(y, axis=0, keepdims=True)
            var = jnp.mean(yc * yc, axis=0, keepdims=True)
            yn = yc * lax.rsqrt(var + GN_EPS) * lg_ref[ch, :] + lb_ref[ch, :]
            rk_sum = jnp.sum(tr_scr[idx_r, t, ch, :] * tr_scr[idx_k, t, ch, :] * rk_ref[ch, :],
                             axis=0, keepdims=True)
            outs.append((yn + rk_sum * tr_scr[idx_v, t, ch, :]) * tr_scr[idx_g, t, ch, :])
        y_ref[t * nseq:(t + 1) * nseq, :] = jnp.concatenate(outs, axis=0).T


def _rwkv_seq(x, wp, tt, tm):
    b, t, _ = x.shape
    row_spec = pl.BlockSpec((1, tt, D_MODEL), lambda i, j: (i, j, 0))
    vec = _const_spec((1, D_MODEL))
    return pl.pallas_call(
        functools.partial(_rwkv_seq_kernel, tm=tm),
        grid=(b, t // tt),
        in_specs=[row_spec, _const_spec((D_MODEL, RWKV_COLS)), _const_spec((1, RWKV_COLS)),
                  _const_spec((LORA, 3 * D_MODEL))] + [vec] * 7,
        out_specs=[row_spec,
                   pl.BlockSpec((1, N_HEADS, HEAD, HEAD), lambda i, j: (i, 0, 0, 0))],
        out_shape=[jax.ShapeDtypeStruct((b, t, D_MODEL), F32),
                   jax.ShapeDtypeStruct((b, N_HEADS, HEAD, HEAD), F32)],
        scratch_shapes=[pltpu.VMEM((SUBLANES, RWKV_COLS), F32),
                        pltpu.VMEM((7, tt, D_MODEL), F32),
                        pltpu.VMEM((N_PAIRS, PAIR, PAIR), F32),
                        pltpu.VMEM((CHUNK, D_MODEL), F32)],
        compiler_params=pltpu.CompilerParams(
            dimension_semantics=("arbitrary", "arbitrary"), vmem_limit_bytes=VMEM_LIMIT),
        name="rwkv_seq",
    )(x, wp["w_in"], wp["mu"], wp["w2ext"], wp["w0"], wp["a0"], wp["k_k"], wp["k_a"],
      wp["r_k"], wp["lnx_g"], wp["lnx_b"])


def _wkv_step(acts, s_last, wp, steps):
    n = acts[0].shape[0]
    nseq = s_last.shape[3]
    row_spec = pl.BlockSpec((n, PAIR), lambda p: (0, p))
    col_spec = pl.BlockSpec((PAIR, nseq), lambda p: (p, 0))
    st_spec = pl.BlockSpec((2, HEAD, HEAD, nseq), lambda p: (p, 0, 0, 0))
    cols = [jnp.broadcast_to(wp[name].reshape(D_MODEL, 1), (D_MODEL, nseq))
            for name in ("r_k", "lnx_g", "lnx_b")]
    return pl.pallas_call(
        functools.partial(_wkv_step_kernel, steps=steps),
        grid=(N_PAIRS,),
        in_specs=[row_spec] * 7 + [col_spec] * 3 + [st_spec],
        out_specs=[row_spec, st_spec],
        out_shape=[jax.ShapeDtypeStruct((n, D_MODEL), F32),
                   jax.ShapeDtypeStruct(s_last.shape, F32)],
        scratch_shapes=[pltpu.VMEM((7, steps, PAIR, nseq), F32),
                        pltpu.VMEM((steps, PAIR, nseq), F32)],
        compiler_params=pltpu.CompilerParams(
            dimension_semantics=("arbitrary",), vmem_limit_bytes=VMEM_LIMIT),
        name="wkv_step",
    )(*acts, *cols, s_last)


def _post_kernel(x_ref, ya_ref, ga_ref, gyb_ref, wo_ref, l1g_ref, l1b_ref, wg_ref, wu_ref, wd_ref,
                 l2g_ref, l2b_ref, y_ref):
    x = x_ref[...]
    merged = ga_ref[...] * ya_ref[...] + gyb_ref[...]
    mix = _dot(merged, wo_ref[...])
    h1 = _layer_norm(ALPHA * x + mix, l1g_ref[...], l1b_ref[...])
    h1b = h1.astype(BF16)
    gate = jnp.dot(h1b, wg_ref[...], preferred_element_type=F32)
    up = jnp.dot(h1b, wu_ref[...], preferred_element_type=F32)
    act = (gate * _sigmoid(gate)) * up
    ffn = _dot(act, wd_ref[...])
    y_ref[...] = _layer_norm(ALPHA * h1 + ffn, l2g_ref[...], l2b_ref[...])


def _post(x, ya, ga, gyb, wp, tm):
    n = x.shape[0]
    row_spec = pl.BlockSpec((tm, D_MODEL), lambda i: (i, 0))
    vec = _const_spec((1, D_MODEL))
    return pl.pallas_call(
        _post_kernel,
        grid=(n // tm,),
        in_specs=[row_spec] * 4 + [_const_spec((D_MODEL, D_MODEL)), vec, vec,
                                   _const_spec((D_MODEL, D_FF)), _const_spec((D_MODEL, D_FF)),
                                   _const_spec((D_FF, D_MODEL)), vec, vec],
        out_specs=row_spec,
        out_shape=jax.ShapeDtypeStruct((n, D_MODEL), F32),
        compiler_params=pltpu.CompilerParams(
            dimension_semantics=("arbitrary",), vmem_limit_bytes=VMEM_LIMIT),
        name="post",
    )(x, ya, ga, gyb, wp["w_o"], wp["ln1_g"], wp["ln1_b"], wp["w_gate"], wp["w_up"], wp["w_down"],
      wp["ln2_g"], wp["ln2_b"])


def _prep_weights(w_in, tmix_mu, w0, w2_decay, a0, a2_iclr, g2_gate, k_k, k_a, r_k, lnx_g, lnx_b,
                  conv_w, conv_b, lru_wa, lru_ba, lru_wi, lru_bi, lru_lambda, w_o,
                  ln1_g, ln1_b, w_ffn_gate, w_ffn_up, w_ffn_down, ln2_g, ln2_b):
    row = lambda v: v.reshape(1, -1).astype(F32)
    zeros = lambda r: jnp.zeros((r, D_MODEL), F32)
    w2ext = jnp.concatenate([
        jnp.concatenate([w2_decay, zeros(DECAY_LORA), zeros(DECAY_LORA)], axis=1),
        jnp.concatenate([zeros(ICLR_LORA), a2_iclr, zeros(ICLR_LORA)], axis=1),
        jnp.concatenate([zeros(GATE_LORA), zeros(GATE_LORA), g2_gate], axis=1)], axis=0)

    def gate_blockdiag(w):
        per = LRU_GROUP // LRU_BS
        w4 = w.reshape(LRU_BLOCKS // per, per, LRU_BS, LRU_BS)
        eye = jnp.eye(per, dtype=w.dtype)
        bd = jnp.einsum("gpcd,pq->gpcqd", w4, eye)
        return bd.reshape(LRU_BLOCKS // per, LRU_GROUP, LRU_GROUP).astype(BF16)

    return dict(
        w_in=w_in.astype(BF16),
        mu=row(tmix_mu), w2ext=w2ext.astype(BF16), w0=row(w0), a0=row(a0), k_k=row(k_k),
        k_a=row(k_a), r_k=row(r_k), lnx_g=row(lnx_g), lnx_b=row(lnx_b),
        conv_w=conv_w.astype(F32), conv_b=row(conv_b), wa_bd=gate_blockdiag(lru_wa),
        ba=row(lru_ba), wi_bd=gate_blockdiag(lru_wi), bi=row(lru_bi), lam=row(lru_lambda),
        w_o=w_o.astype(BF16), ln1_g=row(ln1_g), ln1_b=row(ln1_b), w_gate=w_ffn_gate.astype(BF16),
        w_up=w_ffn_up.astype(BF16), w_down=w_ffn_down.astype(BF16), ln2_g=row(ln2_g),
        ln2_b=row(ln2_b))


def _row_tiles(t):
    rwkv_tile = min(t, 8 * CHUNK)
    return rwkv_tile, min(t, 256), min(t, 256), min(t, 256)


def _prompt_layer(x, wp):
    b, t, _ = x.shape
    rwkv_tile, proj_tile, lru_tile, post_tile = _row_tiles(t)
    ya, s_heads = _rwkv_seq(x, wp, rwkv_tile, proj_tile)
    ga, gyb, conv_tail, h_tail = _lru_seq(x, wp, lru_tile)
    y = _post(x.reshape(b * t, D_MODEL), ya.reshape(b * t, D_MODEL), ga.reshape(b * t, D_MODEL),
              gyb.reshape(b * t, D_MODEL), wp, post_tile).reshape(b, t, D_MODEL)
    return (y, x[:, -1], s_heads, conv_tail[:, SUBLANES - (CONV_W - 1):],
            h_tail[:, SUBLANES - 1])


def _sample_layer(x, shift_buf, wkv0, conv_buf, h0, wp):
    b, t, _ = x.shape
    n = b * t
    x_tm = x.transpose(1, 0, 2).reshape(n, D_MODEL)
    xprev_tm = jnp.concatenate([shift_buf, x_tm[:n - b]], axis=0)
    acts = _rwkv_proj_step(x_tm, xprev_tm, wp)
    ya_tm, s_last = _wkv_step(acts, wkv0.transpose(1, 2, 3, 0), wp, t)
    conv_tm = conv_buf.transpose(1, 0, 2).reshape((CONV_W - 1) * b, D_LRU)
    ga_tm, gyb_tm, conv_new_tm, h_last = _lru_step(x_tm, conv_tm, h0, wp, t, b)
    y_tm = _post(x_tm, ya_tm, ga_tm, gyb_tm, wp, min(n, 256))
    y = y_tm.reshape(t, b, D_MODEL).transpose(1, 0, 2)
    conv_new = conv_new_tm.reshape(CONV_W - 1, b, D_LRU).transpose(1, 0, 2)
    return y, x[:, -1], s_last.transpose(3, 0, 1, 2), conv_new, h_last


def kernel(x_prompt, x_sample, state_shift, state_wkv, state_conv, state_lru, w_in, tmix_mu, w0, w2_decay, a0, a2_iclr, g2_gate, k_k, k_a, r_k, lnx_g, lnx_b, conv_w, conv_b, lru_wa, lru_ba, lru_wi, lru_bi, lru_lambda, w_o, ln1_g, ln1_b, w_ffn_gate, w_ffn_up, w_ffn_down, ln2_g, ln2_b):
    params = (w_in, tmix_mu, w0, w2_decay, a0, a2_iclr, g2_gate, k_k, k_a, r_k, lnx_g, lnx_b,
              conv_w, conv_b, lru_wa, lru_ba, lru_wi, lru_bi, lru_lambda, w_o,
              ln1_g, ln1_b, w_ffn_gate, w_ffn_up, w_ffn_down, ln2_g, ln2_b)
    wp = _prep_weights(*[p[0] for p in params])
    yp, sh_p, wkv_p, conv_p, lru_p = _prompt_layer(x_prompt, wp)
    ys, sh_s, wkv_s, conv_s, lru_s = _sample_layer(
        x_sample, state_shift[0], state_wkv[0], state_conv[0], state_lru[0], wp)
    return (yp, ys, sh_p[None], wkv_p[None], conv_p[None], lru_p[None],
            sh_s[None], wkv_s[None], conv_s[None], lru_s[None])
```

```python
import functools
import math

import jax
import jax.numpy as jnp
from jax import lax
from jax.experimental import pallas as pl
from jax.experimental.pallas import tpu as pltpu

F32 = jnp.float32
BF16 = jnp.bfloat16

D_MODEL = 1024
HEAD = 64
N_HEADS = D_MODEL // HEAD
PAIR = 2 * HEAD
N_PAIRS = N_HEADS // 2
DECAY_LORA = 64
ICLR_LORA = 64
GATE_LORA = 128
LORA = DECAY_LORA + ICLR_LORA + GATE_LORA
RWKV_COLS = 3 * D_MODEL + LORA
GN_EPS = HEAD * 1e-5
D_LRU = D_MODEL
LRU_BLOCKS = 16
LRU_BS = D_LRU // LRU_BLOCKS
LRU_GROUP = 256
CONV_W = 4
LRU_C = 8.0
LRU_COLS = 2 * D_LRU + 2 * D_MODEL
N_IN = RWKV_COLS + LRU_COLS
D_FF = 2816
ALPHA = 2.0 ** 0.25
LN_EPS = 1e-5

SUBLANES = 8
BF16_ROWS = 16
CHUNK = 64
CHUNKS_IN_FLIGHT = 4
STEP_ROWS_IN_FLIGHT = 8
VMEM_LIMIT = 56 * 1024 * 1024


def _softplus(x):
    return jnp.maximum(x, 0.0) + jnp.log1p(jnp.exp(-jnp.abs(x)))


def _softplus_plain(x):
    return jnp.maximum(x, 0.0) + jnp.log(1.0 + jnp.exp(-jnp.abs(x)))


def _sigmoid(x):
    return 0.5 * jnp.tanh(0.5 * x) + 0.5


def _gelu_tanh(x):
    c = math.sqrt(2.0 / math.pi)
    return x * (0.5 * (1.0 + jnp.tanh(c * (x + 0.044715 * (x * x * x)))))


def _layer_norm(x, g, b):
    mu = jnp.mean(x, axis=-1, keepdims=True)
    xc = x - mu
    var = jnp.mean(xc * xc, axis=-1, keepdims=True)
    return xc * lax.rsqrt(var + LN_EPS) * g + b


def _dot(a, b):
    return jnp.dot(a.astype(BF16), b.astype(BF16), preferred_element_type=F32)


def _dot_nt(a, b):
    return lax.dot_general(a.astype(BF16), b.astype(BF16), (((1,), (1,)), ((), ())),
                           preferred_element_type=F32)


def _dot_tn(a, b):
    return lax.dot_general(a.astype(BF16), b.astype(BF16), (((0,), (0,)), ((), ())),
                           preferred_element_type=F32)


def _seg_sum(x, lane_lo):
    s0 = jnp.sum(jnp.where(lane_lo, x, 0.0), axis=-1, keepdims=True)
    s1 = jnp.sum(jnp.where(lane_lo, 0.0, x), axis=-1, keepdims=True)
    return jnp.where(lane_lo, s0, s1)


def _head_sums(x):
    lane_lo = lax.broadcasted_iota(jnp.int32, (x.shape[0], PAIR), 1) < HEAD
    return jnp.concatenate([_seg_sum(x[:, p * PAIR:(p + 1) * PAIR], lane_lo)
                            for p in range(N_PAIRS)], axis=1)


def _rwkv_prep(mixed, w2ext, w0, a0, k_k, k_a):
    lo = mixed(3 * D_MODEL, LORA)
    lane = lax.broadcasted_iota(jnp.int32, lo.shape, 1)
    act = jnp.where(lane < DECAY_LORA, jnp.tanh(lo),
                    jnp.where(lane < DECAY_LORA + ICLR_LORA, lo, _sigmoid(lo)))
    r = mixed(0, D_MODEL)
    lora = _dot(act, w2ext)
    k = mixed(D_MODEL, D_MODEL)
    w = -_softplus_plain(-(w0 + lora[:, 0:D_MODEL])) - 0.5
    lw = -jnp.exp(w)
    a = _sigmoid(a0 + lora[:, D_MODEL:2 * D_MODEL])
    g = lora[:, 2 * D_MODEL:3 * D_MODEL]
    v = mixed(2 * D_MODEL, D_MODEL)
    kkraw = k * k_k
    kk = kkraw * lax.rsqrt(jnp.maximum(_head_sums(kkraw * kkraw), 1e-24))
    k2 = k * (1.0 + (a - 1.0) * k_a)
    return r, lw, k2, v, -kk, kk * a, g


def _rwkv_project_rows(x, w_ref, mu_ref, w2_ref, w0_ref, a0_ref, kk_ref, ka_ref, hist_ref):
    xbf = x.astype(BF16)
    tm = xbf.shape[0]
    row = lax.broadcasted_iota(jnp.int32, (SUBLANES, 1), 0)

    def mixed(first_col, n_cols):
        cols = slice(first_col, first_col + n_cols)
        z = jnp.dot(xbf, w_ref[:, cols], preferred_element_type=F32)
        rolled = pltpu.roll(z, 1, 0)
        first = jnp.where(row == 0, hist_ref[SUBLANES - 1:SUBLANES, cols], rolled[0:SUBLANES])
        zprev = jnp.concatenate([first, rolled[SUBLANES:]], axis=0)
        hist_ref[:, cols] = z[tm - SUBLANES:tm]
        return z + mu_ref[:, cols] * (zprev - z)

    return _rwkv_prep(mixed, w2_ref[...], w0_ref[...], a0_ref[...], kk_ref[...], ka_ref[...])


def _rwkv_proj_step_kernel(x_ref, xp_ref, w_ref, mu_ref, w2_ref, w0_ref, a0_ref, kk_ref, ka_ref,
                           r_ref, lw_ref, k_ref, v_ref, al_ref, be_ref, g_ref):
    xbf = x_ref[...].astype(BF16)
    xpbf = xp_ref[...].astype(BF16)

    def mixed(first_col, n_cols):
        cols = slice(first_col, first_col + n_cols)
        z = jnp.dot(xbf, w_ref[:, cols], preferred_element_type=F32)
        zprev = jnp.dot(xpbf, w_ref[:, cols], preferred_element_type=F32)
        return z + mu_ref[:, cols] * (zprev - z)

    outs = _rwkv_prep(mixed, w2_ref[...], w0_ref[...], a0_ref[...], kk_ref[...], ka_ref[...])
    for o_ref, o in zip((r_ref, lw_ref, k_ref, v_ref, al_ref, be_ref, g_ref), outs):
        o_ref[...] = o


def _const_spec(shape):
    nd = len(shape)
    return pl.BlockSpec(shape, lambda *_: (0,) * nd, pipeline_mode=pl.Buffered(1))


def _rwkv_proj_step(x, xprev, wp):
    n = x.shape[0]
    full = pl.BlockSpec((n, D_MODEL), lambda i: (0, 0))
    vec = _const_spec((1, D_MODEL))
    return pl.pallas_call(
        _rwkv_proj_step_kernel,
        grid=(1,),
        in_specs=[full, full, _const_spec((D_MODEL, RWKV_COLS)), _const_spec((1, RWKV_COLS)),
                  _const_spec((LORA, 3 * D_MODEL)), vec, vec, vec, vec],
        out_specs=[full] * 7,
        out_shape=[jax.ShapeDtypeStruct((n, D_MODEL), F32)] * 7,
        compiler_params=pltpu.CompilerParams(
            dimension_semantics=("arbitrary",), vmem_limit_bytes=VMEM_LIMIT),
        name="rwkv_proj_step",
    )(x, xprev, wp["w_in"], wp["mu"], wp["w2ext"], wp["w0"], wp["a0"], wp["k_k"], wp["k_a"])


def _lru_gate_logits(u, wa_ref, wi_ref):
    ub = u.astype(BF16)
    ra, ia = [], []
    for q in range(D_LRU // LRU_GROUP):
        uq = ub[:, q * LRU_GROUP:(q + 1) * LRU_GROUP]
        ra.append(jnp.dot(uq, wa_ref[q], preferred_element_type=F32))
        ia.append(jnp.dot(uq, wi_ref[q], preferred_element_type=F32))
    return jnp.concatenate(ra, axis=1), jnp.concatenate(ia, axis=1)


def _lru_coeffs(ra, ia, ba, bi, sp):
    rg = _sigmoid(ra + ba)
    ig = _sigmoid(ia + bi)
    log_a = -LRU_C * rg * sp
    a = jnp.exp(log_a)
    m2 = -jnp.tanh(log_a) * (a * a + 1.0)
    mult = jnp.where(m2 > 0.0, m2 * lax.rsqrt(m2), 0.0)
    return a, mult, ig


def _lru_seq_kernel(x_ref, w_ref, cw_ref, cb_ref, wa_ref, ba_ref, wi_ref, bi_ref, lam_ref,
                    ga_ref, gyb_ref, conv_ref, hlast_ref,
                    xhist_ref, hc_ref, a_s, bx_s, h_s):
    t = pl.program_id(1)

    @pl.when(t == 0)
    def _():
        xhist_ref[...] = jnp.zeros_like(xhist_ref)
        hc_ref[...] = jnp.zeros_like(hc_ref)

    xbf = x_ref[0].astype(BF16)
    tm = xbf.shape[0]

    def proj(first_col, n_cols):
        cols = slice(RWKV_COLS + first_col, RWKV_COLS + first_col + n_cols)
        return jnp.dot(xbf, w_ref[:, cols], preferred_element_type=F32)

    xb = proj(0, D_LRU)
    gb = proj(D_LRU, D_LRU)
    row8 = lax.broadcasted_iota(jnp.int32, (SUBLANES, 1), 0)
    hist = xhist_ref[...]

    def shifted(k):
        rolled = pltpu.roll(xb, k, 0)
        first = jnp.where(row8 < k, pltpu.roll(hist, k, 0), rolled[0:SUBLANES])
        return jnp.concatenate([first, rolled[SUBLANES:]], axis=0)

    cw = cw_ref[...]
    u = (cb_ref[...] + cw[0:1] * shifted(3) + cw[1:2] * shifted(2) + cw[2:3] * shifted(1)
         + cw[3:4] * xb)
    last8 = xb[tm - SUBLANES:tm]
    xhist_ref[...] = last8
    conv_ref[0] = last8

    ra, ia = _lru_gate_logits(u, wa_ref, wi_ref)
    gelu_gb = _gelu_tanh(gb)
    zga = proj(2 * D_LRU, D_MODEL)
    sp = _softplus(-lam_ref[...])
    a, mult, ig = _lru_coeffs(ra, ia, ba_ref[...], bi_ref[...], sp)
    row = lax.broadcasted_iota(jnp.int32, (tm, 1), 0)
    mult = jnp.where(jnp.logical_and(row == 0, t == 0), 1.0, mult)
    a_s[...] = a
    bx_s[...] = mult * ig * u
    ga_ref[0] = _sigmoid(zga)
    zgb = proj(2 * D_LRU + D_MODEL, D_MODEL)

    hc = hc_ref[0:1, :]
    for blk in range(tm // SUBLANES):
        rows = slice(blk * SUBLANES, (blk + 1) * SUBLANES)
        av = a_s[rows, :]
        bv = bx_s[rows, :]
        for d in (1, 2, 4):
            a_sh = jnp.where(row8 < d, 1.0, pltpu.roll(av, d, 0))
            b_sh = jnp.where(row8 < d, 0.0, pltpu.roll(bv, d, 0))
            bv = av * b_sh + bv
            av = av * a_sh
        h_blk = bv + av * hc
        h_s[rows, :] = h_blk
        hc = h_blk[SUBLANES - 1:SUBLANES, :]
    hc_ref[0:1, :] = hc
    h = h_s[...]
    hlast_ref[0] = h[tm - SUBLANES:tm]
    gyb_ref[0] = (_sigmoid(zgb) * gelu_gb) * h


def _lru_step_kernel(x_ref, conv0_ref, h0_ref, w_ref, cw_ref, cb_ref, wa_ref, ba_ref, wi_ref,
                     bi_ref, lam_ref, ga_ref, gyb_ref, conv_ref, hlast_ref, *, steps, nseq):
    z = _dot(x_ref[...], w_ref[:, RWKV_COLS:])
    xb = z[:, 0:D_LRU]
    gb = z[:, D_LRU:2 * D_LRU]
    zga = z[:, 2 * D_LRU:2 * D_LRU + D_MODEL]
    zgb = z[:, 2 * D_LRU + D_MODEL:]
    n = steps * nseq
    hist = (CONV_W - 1) * nseq
    xext = jnp.concatenate([conv0_ref[...], xb], axis=0)

    def shifted(k):
        return xext[hist - k * nseq:hist - k * nseq + n]

    cw = cw_ref[...]
    u = (cb_ref[...] + cw[0:1] * shifted(3) + cw[1:2] * shifted(2) + cw[2:3] * shifted(1)
         + cw[3:4] * xb)
    conv_ref[...] = xext[n:n + hist]
    sp = _softplus(-lam_ref[...])
    ra, ia = _lru_gate_logits(u, wa_ref, wi_ref)
    a, mult, ig = _lru_coeffs(ra, ia, ba_ref[...], bi_ref[...], sp)
    bx = mult * ig * u
    h = h0_ref[...]
    hs = []
    for s in range(steps):
        h = a[s * nseq:(s + 1) * nseq] * h + bx[s * nseq:(s + 1) * nseq]
        hs.append(h)
    hlast_ref[...] = h
    hall = jnp.concatenate(hs, axis=0)
    ga_ref[...] = _sigmoid(zga)
    gyb_ref[...] = _sigmoid(zgb) * (hall * _gelu_tanh(gb))


def _lru_weight_specs():
    vec = _const_spec((1, D_LRU))
    gate_w = _const_spec((D_LRU // LRU_GROUP, LRU_GROUP, LRU_GROUP))
    return [_const_spec((D_MODEL, N_IN)), _const_spec((CONV_W, D_LRU)), vec,
            gate_w, vec, gate_w, vec, vec]


def _lru_weights(wp):
    return (wp["w_in"], wp["conv_w"], wp["conv_b"], wp["wa_bd"], wp["ba"], wp["wi_bd"],
            wp["bi"], wp["lam"])


def _lru_seq(x, wp, tm):
    b, t, _ = x.shape
    row_spec = pl.BlockSpec((1, tm, D_MODEL), lambda i, j: (i, j, 0))
    tail_spec = pl.BlockSpec((1, SUBLANES, D_LRU), lambda i, j: (i, 0, 0))
    return pl.pallas_call(
        _lru_seq_kernel,
        grid=(b, t // tm),
        in_specs=[row_spec] + _lru_weight_specs(),
        out_specs=[row_spec, row_spec, tail_spec, tail_spec],
        out_shape=[jax.ShapeDtypeStruct((b, t, D_MODEL), F32)] * 2
        + [jax.ShapeDtypeStruct((b, SUBLANES, D_LRU), F32)] * 2,
        scratch_shapes=[pltpu.VMEM((SUBLANES, D_LRU), F32), pltpu.VMEM((SUBLANES, D_LRU), F32),
                        pltpu.VMEM((tm, D_LRU), F32), pltpu.VMEM((tm, D_LRU), F32),
                        pltpu.VMEM((tm, D_LRU), F32)],
        compiler_params=pltpu.CompilerParams(
            dimension_semantics=("arbitrary", "arbitrary"), vmem_limit_bytes=VMEM_LIMIT),
        name="lru_seq",
    )(x, *_lru_weights(wp))


def _lru_step(x_tm, conv_tm, h0, wp, steps, nseq):
    n = steps * nseq
    hist = (CONV_W - 1) * nseq

    def full(r):
        return pl.BlockSpec((r, D_MODEL), lambda i: (0, 0))

    return pl.pallas_call(
        functools.partial(_lru_step_kernel, steps=steps, nseq=nseq),
        grid=(1,),
        in_specs=[full(n), full(hist), full(nseq)] + _lru_weight_specs(),
        out_specs=[full(n), full(n), full(hist), full(nseq)],
        out_shape=[jax.ShapeDtypeStruct((n, D_MODEL), F32)] * 2
        + [jax.ShapeDtypeStruct((hist, D_LRU), F32), jax.ShapeDtypeStruct((nseq, D_LRU), F32)],
        compiler_params=pltpu.CompilerParams(
            dimension_semantics=("arbitrary",), vmem_limit_bytes=VMEM_LIMIT),
        name="lru_step",
    )(x_tm, conv_tm, h0, *_lru_weights(wp))


def _wkv_consts():
    c = CHUNK
    lane = lax.broadcasted_iota(jnp.int32, (c, PAIR), 1)
    row = lax.broadcasted_iota(jnp.int32, (c, PAIR), 0)
    lane_lo = lane < HEAD
    col = lane % c
    strict = col < row
    incl = col <= row
    eye = jnp.where(col == row, 1.0, 0.0).astype(F32)
    r3 = lax.broadcasted_iota(jnp.int32, (c, 3 * c), 0)
    c3 = lax.broadcasted_iota(jnp.int32, (c, 3 * c), 1) % c
    tri3 = jnp.where(c3 <= r3, 1.0, 0.0).astype(BF16)
    rp = lax.broadcasted_iota(jnp.int32, (PAIR, PAIR), 0)
    cp = lax.broadcasted_iota(jnp.int32, (PAIR, PAIR), 1)
    blockdiag = (rp // HEAD) == (cp // HEAD)
    return dict(lane_lo=lane_lo, tri3=tri3, strict=strict, incl=incl, eye=eye,
                blockdiag=blockdiag)


def _split3(x):
    hi = x.astype(BF16)
    rest = x - hi.astype(F32)
    mid = rest.astype(BF16)
    lo = (rest - mid.astype(F32)).astype(BF16)
    return jnp.concatenate([hi, mid, lo], axis=0)


def _split_heads(x, lane_lo):
    zero = jnp.zeros_like(x)
    return jnp.concatenate([jnp.where(lane_lo, x, zero), jnp.where(lane_lo, zero, x)], axis=0)


def _run_interleaved(stage_gens):
    results = [None] * len(stage_gens)
    live = list(range(len(stage_gens)))
    while live:
        still = []
        for idx in live:
            try:
                next(stage_gens[idx])
                still.append(idx)
            except StopIteration as done:
                results[idx] = done.value
        live = still
    return results


def _wkv_chunk(r, lw, k, v, al, be, get_state, cn):
    c = CHUNK
    lane_lo = cn["lane_lo"]
    cum = jnp.dot(cn["tri3"], _split3(lw), preferred_element_type=F32)
    end = cum[c - 1:c, :]
    yield
    e_cum = jnp.exp(cum)
    e_neg = jnp.exp(-cum)
    e_end = jnp.exp(end)
    rt = r * e_cum
    at = al * jnp.exp(cum - lw)
    bt = be * e_neg
    kt = k * e_neg
    lhs2 = jnp.concatenate([at, rt], axis=0).astype(BF16)
    keys = jnp.concatenate([_split_heads(bt, lane_lo), _split_heads(kt, lane_lo)], axis=0)
    gram = _dot_nt(lhs2, keys)
    yield
    a_ab = jnp.where(cn["strict"], gram[0:c, 0:PAIR], 0.0)
    a_ak = jnp.where(cn["strict"], gram[0:c, PAIR:], 0.0)
    a_rb = jnp.where(cn["incl"], gram[c:, 0:PAIR], 0.0)
    a_rk = jnp.where(cn["incl"], gram[c:, PAIR:], 0.0)
    inv = cn["eye"] + a_ab
    n_iter = int(math.log2(c)) - 1
    pw = _dot(a_ab, _split_heads(a_ab, lane_lo))
    yield
    for it in range(n_iter):
        power = 2 ** (it + 1)
        skip_inv = (power // BF16_ROWS) * BF16_ROWS
        skip_pw = min((2 * power // BF16_ROWS) * BF16_ROWS, c)
        pw_heads = _split_heads(pw, lane_lo)
        last = it == n_iter - 1
        lhs = inv[skip_inv:] if last else jnp.concatenate([inv[skip_inv:], pw[skip_pw:]], axis=0)
        prod = _dot(lhs, pw_heads)
        inv_tail = inv[skip_inv:] + prod[0:c - skip_inv]
        inv = inv_tail if skip_inv == 0 else jnp.concatenate([inv[0:skip_inv], inv_tail], axis=0)
        if not last:
            pw = prod[c - skip_inv:]
            if skip_pw:
                pw = jnp.concatenate([jnp.zeros((skip_pw, PAIR), F32), pw], axis=0)
        yield
    av = _dot(jnp.concatenate([a_ak, a_rk], axis=0), _split_heads(v, lane_lo))
    state = get_state()
    while state is None:
        yield
        state = get_state()
    from_state = _dot_nt(lhs2, state)
    yield
    u = _dot(inv, _split_heads(from_state[0:c] + av[0:c], lane_lo))
    yield
    y = from_state[c:] + av[c:] + _dot(a_rb, _split_heads(u, lane_lo))
    yield
    uv = jnp.concatenate([u, v], axis=0)
    bk = jnp.concatenate([bt * e_end, kt * e_end], axis=0)
    return y, state * e_end + jnp.where(cn["blockdiag"], _dot_tn(uv, bk), 0.0)


def _wkv_readout(y, r, k, v, g, rk, lnx_g, lnx_b, lane_lo):
    mean = _seg_sum(y, lane_lo) * (1.0 / HEAD)
    bonus = _seg_sum(r * k * rk, lane_lo) * v
    yield
    yc = y - mean
    var = _seg_sum(yc * yc, lane_lo) * (1.0 / HEAD)
    yield
    yn = yc * lax.rsqrt(var + GN_EPS) * lnx_g + lnx_b
    return (yn + bonus) * g


def _from_blockdiag(state):
    return state[0:HEAD, 0:HEAD], pltpu.roll(state[HEAD:], HEAD, 1)[:, 0:HEAD]


def _rwkv_seq_kernel(x_ref, w_ref, mu_ref, w2_ref, w0_ref, a0_ref, kk_ref, ka_ref, rk_ref, lg_ref,
                     lb_ref, y_ref, s_out_ref, hist_ref, acts_scr, s_scr, yraw_scr, *, tm):
    t = pl.program_id(1)
    cn = _wkv_consts()
    n_pairs = s_scr.shape[0]
    tt = x_ref.shape[1]

    @pl.when(t == 0)
    def _():
        hist_ref[...] = jnp.zeros_like(hist_ref)
        s_scr[...] = jnp.zeros_like(s_scr)
        yraw_scr[...] = jnp.zeros_like(yraw_scr)

    for sub in range(tt // tm):
        rows = slice(sub * tm, (sub + 1) * tm)
        outs = _rwkv_project_rows(x_ref[0, rows, :], w_ref, mu_ref, w2_ref, w0_ref, a0_ref,
                                  kk_ref, ka_ref, hist_ref)
        for idx, o in enumerate(outs):
            acts_scr[idx, rows, :] = o

    r_ref, lw_ref, k_ref, v_ref, al_ref, be_ref, g_ref = (acts_scr.at[i] for i in range(7))
    n_chunks = tt // CHUNK

    def readout_stages(p, rows):
        ln = pl.ds(p * PAIR, PAIR)
        out = yield from _wkv_readout(
            yraw_scr[:, ln], r_ref[rows, ln], k_ref[rows, ln], v_ref[rows, ln],
            g_ref[rows, ln], rk_ref[:, ln], lg_ref[:, ln], lb_ref[:, ln], cn["lane_lo"])
        y_ref[0, rows, ln] = out

    def chunk_stages(p, rows, get_state):
        ln = pl.ds(p * PAIR, PAIR)
        return _wkv_chunk(r_ref[rows, ln], lw_ref[rows, ln], k_ref[rows, ln],
                          v_ref[rows, ln], al_ref[rows, ln], be_ref[rows, ln],
                          get_state, cn)

    def body(i, states):
        def rows_of(j):
            return pl.ds(pl.multiple_of((CHUNKS_IN_FLIGHT * i + j) * CHUNK, CHUNK), CHUNK)

        prev = pl.ds(pl.multiple_of(jnp.maximum(CHUNKS_IN_FLIGHT * i - 1, 0) * CHUNK, CHUNK),
                     CHUNK)
        end_states = [[None] * n_pairs for _ in range(CHUNKS_IN_FLIGHT)]

        def start_state(j, p):
            return states[p] if j == 0 else end_states[j - 1][p]

        def stages(j, p):
            ln = pl.ds(p * PAIR, PAIR)
            rows = rows_of(j)
            y, state = yield from chunk_stages(p, rows, functools.partial(start_state, j, p))
            end_states[j][p] = state
            if j == CHUNKS_IN_FLIGHT - 1:
                yraw_scr[:, ln] = y
                return
            out = yield from _wkv_readout(
                y, r_ref[rows, ln], k_ref[rows, ln], v_ref[rows, ln],
                g_ref[rows, ln], rk_ref[:, ln], lg_ref[:, ln], lb_ref[:, ln], cn["lane_lo"])
            y_ref[0, rows, ln] = out

        _run_interleaved([readout_stages(p, prev) for p in range(n_pairs)]
                         + [stages(j, p) for j in range(CHUNKS_IN_FLIGHT) for p in range(n_pairs)])
        return tuple(end_states[CHUNKS_IN_FLIGHT - 1])

    assert n_chunks % CHUNKS_IN_FLIGHT == 0
    states = lax.fori_loop(0, n_chunks // CHUNKS_IN_FLIGHT, body,
                           tuple(s_scr[p] for p in range(n_pairs)))
    last = pl.ds((n_chunks - 1) * CHUNK, CHUNK)
    _run_interleaved([readout_stages(p, last) for p in range(n_pairs)])
    for p in range(n_pairs):
        s_scr[p] = states[p]
        s_out_ref[0, 2 * p], s_out_ref[0, 2 * p + 1] = _from_blockdiag(states[p])


def _wkv_step_kernel(r_ref, lw_ref, k_ref, v_ref, al_ref, be_ref, g_ref, rk_ref, lg_ref, lb_ref,
                     s_in_ref, y_ref, s_out_ref, tr_scr, y_scr, *, steps):
    nseq = s_in_ref.shape[3]
    idx_r, idx_dec, idx_k, idx_v, idx_al, idx_be, idx_g = range(7)
    for idx, a_ref in enumerate((r_ref, lw_ref, k_ref, v_ref, al_ref, be_ref, g_ref)):
        for t in range(steps):
            tile = a_ref[t * nseq:(t + 1) * nseq, :].T
            tr_scr[idx, t] = jnp.exp(tile) if idx == idx_dec else tile

    for h in range(2):
        ch = slice(h * HEAD, (h + 1) * HEAD)

        def row_stages(i, h=h, ch=ch):
            s = s_in_ref[h, i]
            for t in range(steps):
                sa = jnp.sum(s * tr_scr[idx_al, t, ch, :], axis=0, keepdims=True)
                yield
                v_i = tr_scr[idx_v, t, pl.ds(h * HEAD + i, 1), :]
                s = (s * tr_scr[idx_dec, t, ch, :] + sa * tr_scr[idx_be, t, ch, :]
                     + v_i * tr_scr[idx_k, t, ch, :])
                y_scr[t, pl.ds(h * HEAD + i, 1), :] = jnp.sum(
                    s * tr_scr[idx_r, t, ch, :], axis=0, keepdims=True)
                yield
            s_out_ref[h, i] = s

        @pl.loop(0, HEAD // STEP_ROWS_IN_FLIGHT)
        def _(blk):
            _run_interleaved([row_stages(blk * STEP_ROWS_IN_FLIGHT + ii)
                              for ii in range(STEP_ROWS_IN_FLIGHT)])

    for t in range(steps):
        outs = []
        for h in range(2):
            ch = slice(h * HEAD, (h + 1) * HEAD)
            y = y_scr[t, ch, :]
            yc = y - jnp.mean(y, axis=0, keepdims=True)
            var = jnp.mean(yc * yc, axis=0, keepdims=True)
            yn = yc * lax.rsqrt(var + GN_EPS) * lg_ref[ch, :] + lb_ref[ch, :]
            rk_sum = jnp.sum(tr_scr[idx_r, t, ch, :] * tr_scr[idx_k, t, ch, :] * rk_ref[ch, :],
                             axis=0, keepdims=True)
            outs.append((yn + rk_sum * tr_scr[idx_v, t, ch, :]) * tr_scr[idx_g, t, ch, :])
        y_ref[t * nseq:(t + 1) * nseq, :] = jnp.concatenate(outs, axis=0).T


def _rwkv_seq(x, wp, tt, tm):
    b, t, _ = x.shape
    row_spec = pl.BlockSpec((1, tt, D_MODEL), lambda i, j: (i, j, 0))
    vec = _const_spec((1, D_MODEL))
    return pl.pallas_call(
        functools.partial(_rwkv_seq_kernel, tm=tm),
        grid=(b, t // tt),
        in_specs=[row_spec, _const_spec((D_MODEL, RWKV_COLS)), _const_spec((1, RWKV_COLS)),
                  _const_spec((LORA, 3 * D_MODEL))] + [vec] * 7,
        out_specs=[row_spec,
                   pl.BlockSpec((1, N_HEADS, HEAD, HEAD), lambda i, j: (i, 0, 0, 0))],
        out_shape=[jax.ShapeDtypeStruct((b, t, D_MODEL), F32),
                   jax.ShapeDtypeStruct((b, N_HEADS, HEAD, HEAD), F32)],
        scratch_shapes=[pltpu.VMEM((SUBLANES, RWKV_COLS), F32),
                        pltpu.VMEM((7, tt, D_MODEL), F32),
                        pltpu.VMEM((N_PAIRS, PAIR, PAIR), F32),
                        pltpu.VMEM((CHUNK, D_MODEL), F32)],
        compiler_params=pltpu.CompilerParams(
            dimension_semantics=("arbitrary", "arbitrary"), vmem_limit_bytes=VMEM_LIMIT),
        name="rwkv_seq",
    )(x, wp["w_in"], wp["mu"], wp["w2ext"], wp["w0"], wp["a0"], wp["k_k"], wp["k_a"],
      wp["r_k"], wp["lnx_g"], wp["lnx_b"])


def _wkv_step(acts, s_last, wp, steps):
    n = acts[0].shape[0]
    nseq = s_last.shape[3]
    row_spec = pl.BlockSpec((n, PAIR), lambda p: (0, p))
    col_spec = pl.BlockSpec((PAIR, nseq), lambda p: (p, 0))
    st_spec = pl.BlockSpec((2, HEAD, HEAD, nseq), lambda p: (p, 0, 0, 0))
    cols = [jnp.broadcast_to(wp[name].reshape(D_MODEL, 1), (D_MODEL, nseq))
            for name in ("r_k", "lnx_g", "lnx_b")]
    return pl.pallas_call(
        functools.partial(_wkv_step_kernel, steps=steps),
        grid=(N_PAIRS,),
        in_specs=[row_spec] * 7 + [col_spec] * 3 + [st_spec],
        out_specs=[row_spec, st_spec],
        out_shape=[jax.ShapeDtypeStruct((n, D_MODEL), F32),
                   jax.ShapeDtypeStruct(s_last.shape, F32)],
        scratch_shapes=[pltpu.VMEM((7, steps, PAIR, nseq), F32),
                        pltpu.VMEM((steps, PAIR, nseq), F32)],
        compiler_params=pltpu.CompilerParams(
            dimension_semantics=("arbitrary",), vmem_limit_bytes=VMEM_LIMIT),
        name="wkv_step",
    )(*acts, *cols, s_last)


def _post_kernel(x_ref, ya_ref, ga_ref, gyb_ref, wo_ref, l1g_ref, l1b_ref, wg_ref, wu_ref, wd_ref,
                 l2g_ref, l2b_ref, y_ref):
    x = x_ref[...]
    merged = ga_ref[...] * ya_ref[...] + gyb_ref[...]
    mix = _dot(merged, wo_ref[...])
    h1 = _layer_norm(ALPHA * x + mix, l1g_ref[...], l1b_ref[...])
    h1b = h1.astype(BF16)
    gate = jnp.dot(h1b, wg_ref[...], preferred_element_type=F32)
    up = jnp.dot(h1b, wu_ref[...], preferred_element_type=F32)
    act = (gate * _sigmoid(gate)) * up
    ffn = _dot(act, wd_ref[...])
    y_ref[...] = _layer_norm(ALPHA * h1 + ffn, l2g_ref[...], l2b_ref[...])


def _post(x, ya, ga, gyb, wp, tm):
    n = x.shape[0]
    row_spec = pl.BlockSpec((tm, D_MODEL), lambda i: (i, 0))
    vec = _const_spec((1, D_MODEL))
    return pl.pallas_call(
        _post_kernel,
        grid=(n // tm,),
        in_specs=[row_spec] * 4 + [_const_spec((D_MODEL, D_MODEL)), vec, vec,
                                   _const_spec((D_MODEL, D_FF)), _const_spec((D_MODEL, D_FF)),
                                   _const_spec((D_FF, D_MODEL)), vec, vec],
        out_specs=row_spec,
        out_shape=jax.ShapeDtypeStruct((n, D_MODEL), F32),
        compiler_params=pltpu.CompilerParams(
            dimension_semantics=("arbitrary",), vmem_limit_bytes=VMEM_LIMIT),
        name="post",
    )(x, ya, ga, gyb, wp["w_o"], wp["ln1_g"], wp["ln1_b"], wp["w_gate"], wp["w_up"], wp["w_down"],
      wp["ln2_g"], wp["ln2_b"])


def _prep_weights(w_in, tmix_mu, w0, w2_decay, a0, a2_iclr, g2_gate, k_k, k_a, r_k, lnx_g, lnx_b,
                  conv_w, conv_b, lru_wa, lru_ba, lru_wi, lru_bi, lru_lambda, w_o,
                  ln1_g, ln1_b, w_ffn_gate, w_ffn_up, w_ffn_down, ln2_g, ln2_b):
    row = lambda v: v.reshape(1, -1).astype(F32)
    zeros = lambda r: jnp.zeros((r, D_MODEL), F32)
    w2ext = jnp.concatenate([
        jnp.concatenate([w2_decay, zeros(DECAY_LORA), zeros(DECAY_LORA)], axis=1),
        jnp.concatenate([zeros(ICLR_LORA), a2_iclr, zeros(ICLR_LORA)], axis=1),
        jnp.concatenate([zeros(GATE_LORA), zeros(GATE_LORA), g2_gate], axis=1)], axis=0)

    def gate_blockdiag(w):
        per = LRU_GROUP // LRU_BS
        w4 = w.reshape(LRU_BLOCKS // per, per, LRU_BS, LRU_BS)
        eye = jnp.eye(per, dtype=w.dtype)
        bd = jnp.einsum("gpcd,pq->gpcqd", w4, eye)
        return bd.reshape(LRU_BLOCKS // per, LRU_GROUP, LRU_GROUP).astype(BF16)

    return dict(
        w_in=w_in.astype(BF16),
        mu=row(tmix_mu), w2ext=w2ext.astype(BF16), w0=row(w0), a0=row(a0), k_k=row(k_k),
        k_a=row(k_a), r_k=row(r_k), lnx_g=row(lnx_g), lnx_b=row(lnx_b),
        conv_w=conv_w.astype(F32), conv_b=row(conv_b), wa_bd=gate_blockdiag(lru_wa),
        ba=row(lru_ba), wi_bd=gate_blockdiag(lru_wi), bi=row(lru_bi), lam=row(lru_lambda),
        w_o=w_o.astype(BF16), ln1_g=row(ln1_g), ln1_b=row(ln1_b), w_gate=w_ffn_gate.astype(BF16),
        w_up=w_ffn_up.astype(BF16), w_down=w_ffn_down.astype(BF16), ln2_g=row(ln2_g),
        ln2_b=row(ln2_b))


def _row_tiles(t):
    rwkv_tile = min(t, 8 * CHUNK)
    return rwkv_tile, min(t, 256), min(t, 256), min(t, 256)


def _prompt_layer(x, wp):
    b, t, _ = x.shape
    rwkv_tile, proj_tile, lru_tile, post_tile = _row_tiles(t)
    ya, s_heads = _rwkv_seq(x, wp, rwkv_tile, proj_tile)
    ga, gyb, conv_tail, h_tail = _lru_seq(x, wp, lru_tile)
    y = _post(x.reshape(b * t, D_MODEL), ya.reshape(b * t, D_MODEL), ga.reshape(b * t, D_MODEL),
              gyb.reshape(b * t, D_MODEL), wp, post_tile).reshape(b, t, D_MODEL)
    return (y, x[:, -1], s_heads, conv_tail[:, SUBLANES - (CONV_W - 1):],
            h_tail[:, SUBLANES - 1])


def _sample_layer(x, shift_buf, wkv0, conv_buf, h0, wp):
    b, t, _ = x.shape
    n = b * t
    x_tm = x.transpose(1, 0, 2).reshape(n, D_MODEL)
    xprev_tm = jnp.concatenate([shift_buf, x_tm[:n - b]], axis=0)
    acts = _rwkv_proj_step(x_tm, xprev_tm, wp)
    ya_tm, s_last = _wkv_step(acts, wkv0.transpose(1, 2, 3, 0), wp, t)
    conv_tm = conv_buf.transpose(1, 0, 2).reshape((CONV_W - 1) * b, D_LRU)
    ga_tm, gyb_tm, conv_new_tm, h_last = _lru_step(x_tm, conv_tm, h0, wp, t, b)
    y_tm = _post(x_tm, ya_tm, ga_tm, gyb_tm, wp, min(n, 256))
    y = y_tm.reshape(t, b, D_MODEL).transpose(1, 0, 2)
    conv_new = conv_new_tm.reshape(CONV_W - 1, b, D_LRU).transpose(1, 0, 2)
    return y, x[:, -1], s_last.transpose(3, 0, 1, 2), conv_new, h_last


def kernel(x_prompt, x_sample, state_shift, state_wkv, state_conv, state_lru, w_in, tmix_mu, w0, w2_decay, a0, a2_iclr, g2_gate, k_k, k_a, r_k, lnx_g, lnx_b, conv_w, conv_b, lru_wa, lru_ba, lru_wi, lru_bi, lru_lambda, w_o, ln1_g, ln1_b, w_ffn_gate, w_ffn_up, w_ffn_down, ln2_g, ln2_b):
    params = (w_in, tmix_mu, w0, w2_decay, a0, a2_iclr, g2_gate, k_k, k_a, r_k, lnx_g, lnx_b,
              conv_w, conv_b, lru_wa, lru_ba, lru_wi, lru_bi, lru_lambda, w_o,
              ln1_g, ln1_b, w_ffn_gate, w_ffn_up, w_ffn_down, ln2_g, ln2_b)
    wp = _prep_weights(*[p[0] for p in params])
    yp, sh_p, wkv_p, conv_p, lru_p = _prompt_layer(x_prompt, wp)
    ys, sh_s, wkv_s, conv_s, lru_s = _sample_layer(
        x_sample, state_shift[0], state_wkv[0], state_conv[0], state_lru[0], wp)
    return (yp, ys, sh_p[None], wkv_p[None], conv_p[None], lru_p[None],
            sh_s[None], wkv_s[None], conv_s[None], lru_s[None])
```

```python
import functools
import math

import jax
import jax.numpy as jnp
from jax import lax
from jax.experimental import pallas as pl
from jax.experimental.pallas import tpu as pltpu

F32 = jnp.float32
BF16 = jnp.bfloat16

D_MODEL = 1024
HEAD = 64
N_HEADS = D_MODEL // HEAD
PAIR = 2 * HEAD
N_PAIRS = N_HEADS // 2
DECAY_LORA = 64
ICLR_LORA = 64
GATE_LORA = 128
LORA = DECAY_LORA + ICLR_LORA + GATE_LORA
RWKV_COLS = 3 * D_MODEL + LORA
GN_EPS = HEAD * 1e-5
D_LRU = D_MODEL
LRU_BLOCKS = 16
LRU_BS = D_LRU // LRU_BLOCKS
LRU_GROUP = 256
CONV_W = 4
LRU_C = 8.0
LRU_COLS = 2 * D_LRU + 2 * D_MODEL
N_IN = RWKV_COLS + LRU_COLS
D_FF = 2816
ALPHA = 2.0 ** 0.25
LN_EPS = 1e-5

SUBLANES = 8
CHUNK = 64
CHUNKS_IN_FLIGHT = 4
STEP_ROWS_IN_FLIGHT = 8
VMEM_LIMIT = 56 * 1024 * 1024


def _softplus(x):
    return jnp.maximum(x, 0.0) + jnp.log1p(jnp.exp(-jnp.abs(x)))


def _softplus_plain(x):
    return jnp.maximum(x, 0.0) + jnp.log(1.0 + jnp.exp(-jnp.abs(x)))


def _sigmoid(x):
    return 0.5 * jnp.tanh(0.5 * x) + 0.5


def _gelu_tanh(x):
    c = math.sqrt(2.0 / math.pi)
    return x * (0.5 * (1.0 + jnp.tanh(c * (x + 0.044715 * (x * x * x)))))


def _layer_norm(x, g, b):
    mu = jnp.mean(x, axis=-1, keepdims=True)
    xc = x - mu
    var = jnp.mean(xc * xc, axis=-1, keepdims=True)
    return xc * lax.rsqrt(var + LN_EPS) * g + b


def _dot(a, b):
    return jnp.dot(a.astype(BF16), b.astype(BF16), preferred_element_type=F32)


def _dot_nt(a, b):
    return lax.dot_general(a.astype(BF16), b.astype(BF16), (((1,), (1,)), ((), ())),
                           preferred_element_type=F32)


def _dot_tn(a, b):
    return lax.dot_general(a.astype(BF16), b.astype(BF16), (((0,), (0,)), ((), ())),
                           preferred_element_type=F32)


def _seg_sum(x, lane_lo):
    s0 = jnp.sum(jnp.where(lane_lo, x, 0.0), axis=-1, keepdims=True)
    s1 = jnp.sum(jnp.where(lane_lo, 0.0, x), axis=-1, keepdims=True)
    return jnp.where(lane_lo, s0, s1)


def _head_sums(x):
    lane_lo = lax.broadcasted_iota(jnp.int32, (x.shape[0], PAIR), 1) < HEAD
    return jnp.concatenate([_seg_sum(x[:, p * PAIR:(p + 1) * PAIR], lane_lo)
                            for p in range(N_PAIRS)], axis=1)


def _rwkv_prep(mixed, w2ext, w0, a0, k_k, k_a):
    lo = mixed(3 * D_MODEL, LORA)
    lane = lax.broadcasted_iota(jnp.int32, lo.shape, 1)
    act = jnp.where(lane < DECAY_LORA, jnp.tanh(lo),
                    jnp.where(lane < DECAY_LORA + ICLR_LORA, lo, _sigmoid(lo)))
    r = mixed(0, D_MODEL)
    lora = _dot(act, w2ext)
    k = mixed(D_MODEL, D_MODEL)
    w = -_softplus_plain(-(w0 + lora[:, 0:D_MODEL])) - 0.5
    lw = -jnp.exp(w)
    a = _sigmoid(a0 + lora[:, D_MODEL:2 * D_MODEL])
    g = lora[:, 2 * D_MODEL:3 * D_MODEL]
    v = mixed(2 * D_MODEL, D_MODEL)
    kkraw = k * k_k
    kk = kkraw * lax.rsqrt(jnp.maximum(_head_sums(kkraw * kkraw), 1e-24))
    k2 = k * (1.0 + (a - 1.0) * k_a)
    return r, lw, k2, v, -kk, kk * a, g


def _rwkv_project_rows(x, w_ref, mu_ref, w2_ref, w0_ref, a0_ref, kk_ref, ka_ref, hist_ref):
    xbf = x.astype(BF16)
    tm = xbf.shape[0]
    row = lax.broadcasted_iota(jnp.int32, (SUBLANES, 1), 0)

    def mixed(first_col, n_cols):
        cols = slice(first_col, first_col + n_cols)
        z = jnp.dot(xbf, w_ref[:, cols], preferred_element_type=F32)
        rolled = pltpu.roll(z, 1, 0)
        first = jnp.where(row == 0, hist_ref[SUBLANES - 1:SUBLANES, cols], rolled[0:SUBLANES])
        zprev = jnp.concatenate([first, rolled[SUBLANES:]], axis=0)
        hist_ref[:, cols] = z[tm - SUBLANES:tm]
        return z + mu_ref[:, cols] * (zprev - z)

    return _rwkv_prep(mixed, w2_ref[...], w0_ref[...], a0_ref[...], kk_ref[...], ka_ref[...])


def _rwkv_proj_step_kernel(x_ref, xp_ref, w_ref, mu_ref, w2_ref, w0_ref, a0_ref, kk_ref, ka_ref,
                           r_ref, lw_ref, k_ref, v_ref, al_ref, be_ref, g_ref):
    xbf = x_ref[...].astype(BF16)
    xpbf = xp_ref[...].astype(BF16)

    def mixed(first_col, n_cols):
        cols = slice(first_col, first_col + n_cols)
        z = jnp.dot(xbf, w_ref[:, cols], preferred_element_type=F32)
        zprev = jnp.dot(xpbf, w_ref[:, cols], preferred_element_type=F32)
        return z + mu_ref[:, cols] * (zprev - z)

    outs = _rwkv_prep(mixed, w2_ref[...], w0_ref[...], a0_ref[...], kk_ref[...], ka_ref[...])
    for o_ref, o in zip((r_ref, lw_ref, k_ref, v_ref, al_ref, be_ref, g_ref), outs):
        o_ref[...] = o


def _const_spec(shape):
    nd = len(shape)
    return pl.BlockSpec(shape, lambda *_: (0,) * nd, pipeline_mode=pl.Buffered(1))


def _rwkv_proj_step(x, xprev, wp):
    n = x.shape[0]
    full = pl.BlockSpec((n, D_MODEL), lambda i: (0, 0))
    vec = _const_spec((1, D_MODEL))
    return pl.pallas_call(
        _rwkv_proj_step_kernel,
        grid=(1,),
        in_specs=[full, full, _const_spec((D_MODEL, RWKV_COLS)), _const_spec((1, RWKV_COLS)),
                  _const_spec((LORA, 3 * D_MODEL)), vec, vec, vec, vec],
        out_specs=[full] * 7,
        out_shape=[jax.ShapeDtypeStruct((n, D_MODEL), F32)] * 7,
        compiler_params=pltpu.CompilerParams(
            dimension_semantics=("arbitrary",), vmem_limit_bytes=VMEM_LIMIT),
        name="rwkv_proj_step",
    )(x, xprev, wp["w_in"], wp["mu"], wp["w2ext"], wp["w0"], wp["a0"], wp["k_k"], wp["k_a"])


def _lru_gate_logits(u, wa_ref, wi_ref):
    ub = u.astype(BF16)
    ra, ia = [], []
    for q in range(D_LRU // LRU_GROUP):
        uq = ub[:, q * LRU_GROUP:(q + 1) * LRU_GROUP]
        ra.append(jnp.dot(uq, wa_ref[q], preferred_element_type=F32))
        ia.append(jnp.dot(uq, wi_ref[q], preferred_element_type=F32))
    return jnp.concatenate(ra, axis=1), jnp.concatenate(ia, axis=1)


def _lru_coeffs(ra, ia, ba, bi, sp):
    rg = _sigmoid(ra + ba)
    ig = _sigmoid(ia + bi)
    log_a = -LRU_C * rg * sp
    a = jnp.exp(log_a)
    m2 = -jnp.tanh(log_a) * (a * a + 1.0)
    mult = jnp.where(m2 > 0.0, m2 * lax.rsqrt(m2), 0.0)
    return a, mult, ig


def _lru_seq_kernel(x_ref, w_ref, cw_ref, cb_ref, wa_ref, ba_ref, wi_ref, bi_ref, lam_ref,
                    ga_ref, gyb_ref, conv_ref, hlast_ref,
                    xhist_ref, hc_ref, a_s, bx_s, h_s):
    t = pl.program_id(1)

    @pl.when(t == 0)
    def _():
        xhist_ref[...] = jnp.zeros_like(xhist_ref)
        hc_ref[...] = jnp.zeros_like(hc_ref)

    xbf = x_ref[0].astype(BF16)
    tm = xbf.shape[0]

    def proj(first_col, n_cols):
        cols = slice(RWKV_COLS + first_col, RWKV_COLS + first_col + n_cols)
        return jnp.dot(xbf, w_ref[:, cols], preferred_element_type=F32)

    xb = proj(0, D_LRU)
    gb = proj(D_LRU, D_LRU)
    row8 = lax.broadcasted_iota(jnp.int32, (SUBLANES, 1), 0)
    hist = xhist_ref[...]

    def shifted(k):
        rolled = pltpu.roll(xb, k, 0)
        first = jnp.where(row8 < k, pltpu.roll(hist, k, 0), rolled[0:SUBLANES])
        return jnp.concatenate([first, rolled[SUBLANES:]], axis=0)

    cw = cw_ref[...]
    u = (cb_ref[...] + cw[0:1] * shifted(3) + cw[1:2] * shifted(2) + cw[2:3] * shifted(1)
         + cw[3:4] * xb)
    last8 = xb[tm - SUBLANES:tm]
    xhist_ref[...] = last8
    conv_ref[0] = last8

    ra, ia = _lru_gate_logits(u, wa_ref, wi_ref)
    gelu_gb = _gelu_tanh(gb)
    zga = proj(2 * D_LRU, D_MODEL)
    sp = _softplus(-lam_ref[...])
    a, mult, ig = _lru_coeffs(ra, ia, ba_ref[...], bi_ref[...], sp)
    row = lax.broadcasted_iota(jnp.int32, (tm, 1), 0)
    mult = jnp.where(jnp.logical_and(row == 0, t == 0), 1.0, mult)
    a_s[...] = a
    bx_s[...] = mult * ig * u
    ga_ref[0] = _sigmoid(zga)
    zgb = proj(2 * D_LRU + D_MODEL, D_MODEL)

    hc = hc_ref[0:1, :]
    for blk in range(tm // SUBLANES):
        rows = slice(blk * SUBLANES, (blk + 1) * SUBLANES)
        av = a_s[rows, :]
        bv = bx_s[rows, :]
        for d in (1, 2, 4):
            a_sh = jnp.where(row8 < d, 1.0, pltpu.roll(av, d, 0))
            b_sh = jnp.where(row8 < d, 0.0, pltpu.roll(bv, d, 0))
            bv = av * b_sh + bv
            av = av * a_sh
        h_blk = bv + av * hc
        h_s[rows, :] = h_blk
        hc = h_blk[SUBLANES - 1:SUBLANES, :]
    hc_ref[0:1, :] = hc
    h = h_s[...]
    hlast_ref[0] = h[tm - SUBLANES:tm]
    gyb_ref[0] = (_sigmoid(zgb) * gelu_gb) * h


def _lru_step_kernel(x_ref, conv0_ref, h0_ref, w_ref, cw_ref, cb_ref, wa_ref, ba_ref, wi_ref,
                     bi_ref, lam_ref, ga_ref, gyb_ref, conv_ref, hlast_ref, *, steps, nseq):
    z = _dot(x_ref[...], w_ref[:, RWKV_COLS:])
    xb = z[:, 0:D_LRU]
    gb = z[:, D_LRU:2 * D_LRU]
    zga = z[:, 2 * D_LRU:2 * D_LRU + D_MODEL]
    zgb = z[:, 2 * D_LRU + D_MODEL:]
    n = steps * nseq
    hist = (CONV_W - 1) * nseq
    xext = jnp.concatenate([conv0_ref[...], xb], axis=0)

    def shifted(k):
        return xext[hist - k * nseq:hist - k * nseq + n]

    cw = cw_ref[...]
    u = (cb_ref[...] + cw[0:1] * shifted(3) + cw[1:2] * shifted(2) + cw[2:3] * shifted(1)
         + cw[3:4] * xb)
    conv_ref[...] = xext[n:n + hist]
    sp = _softplus(-lam_ref[...])
    ra, ia = _lru_gate_logits(u, wa_ref, wi_ref)
    a, mult, ig = _lru_coeffs(ra, ia, ba_ref[...], bi_ref[...], sp)
    bx = mult * ig * u
    h = h0_ref[...]
    hs = []
    for s in range(steps):
        h = a[s * nseq:(s + 1) * nseq] * h + bx[s * nseq:(s + 1) * nseq]
        hs.append(h)
    hlast_ref[...] = h
    hall = jnp.concatenate(hs, axis=0)
    ga_ref[...] = _sigmoid(zga)
    gyb_ref[...] = _sigmoid(zgb) * (hall * _gelu_tanh(gb))


def _lru_weight_specs():
    vec = _const_spec((1, D_LRU))
    gate_w = _const_spec((D_LRU // LRU_GROUP, LRU_GROUP, LRU_GROUP))
    return [_const_spec((D_MODEL, N_IN)), _const_spec((CONV_W, D_LRU)), vec,
            gate_w, vec, gate_w, vec, vec]


def _lru_weights(wp):
    return (wp["w_in"], wp["conv_w"], wp["conv_b"], wp["wa_bd"], wp["ba"], wp["wi_bd"],
            wp["bi"], wp["lam"])


def _lru_seq(x, wp, tm):
    b, t, _ = x.shape
    row_spec = pl.BlockSpec((1, tm, D_MODEL), lambda i, j: (i, j, 0))
    tail_spec = pl.BlockSpec((1, SUBLANES, D_LRU), lambda i, j: (i, 0, 0))
    return pl.pallas_call(
        _lru_seq_kernel,
        grid=(b, t // tm),
        in_specs=[row_spec] + _lru_weight_specs(),
        out_specs=[row_spec, row_spec, tail_spec, tail_spec],
        out_shape=[jax.ShapeDtypeStruct((b, t, D_MODEL), F32)] * 2
        + [jax.ShapeDtypeStruct((b, SUBLANES, D_LRU), F32)] * 2,
        scratch_shapes=[pltpu.VMEM((SUBLANES, D_LRU), F32), pltpu.VMEM((SUBLANES, D_LRU), F32),
                        pltpu.VMEM((tm, D_LRU), F32), pltpu.VMEM((tm, D_LRU), F32),
                        pltpu.VMEM((tm, D_LRU), F32)],
        compiler_params=pltpu.CompilerParams(
            dimension_semantics=("arbitrary", "arbitrary"), vmem_limit_bytes=VMEM_LIMIT),
        name="lru_seq",
    )(x, *_lru_weights(wp))


def _lru_step(x_tm, conv_tm, h0, wp, steps, nseq):
    n = steps * nseq
    hist = (CONV_W - 1) * nseq

    def full(r):
        return pl.BlockSpec((r, D_MODEL), lambda i: (0, 0))

    return pl.pallas_call(
        functools.partial(_lru_step_kernel, steps=steps, nseq=nseq),
        grid=(1,),
        in_specs=[full(n), full(hist), full(nseq)] + _lru_weight_specs(),
        out_specs=[full(n), full(n), full(hist), full(nseq)],
        out_shape=[jax.ShapeDtypeStruct((n, D_MODEL), F32)] * 2
        + [jax.ShapeDtypeStruct((hist, D_LRU), F32), jax.ShapeDtypeStruct((nseq, D_LRU), F32)],
        compiler_params=pltpu.CompilerParams(
            dimension_semantics=("arbitrary",), vmem_limit_bytes=VMEM_LIMIT),
        name="lru_step",
    )(x_tm, conv_tm, h0, *_lru_weights(wp))


def _wkv_consts():
    c = CHUNK
    lane = lax.broadcasted_iota(jnp.int32, (c, PAIR), 1)
    row = lax.broadcasted_iota(jnp.int32, (c, PAIR), 0)
    lane_lo = lane < HEAD
    col = lane % c
    strict = col < row
    incl = col <= row
    eye = jnp.where(col == row, 1.0, 0.0).astype(F32)
    r3 = lax.broadcasted_iota(jnp.int32, (c, 3 * c), 0)
    c3 = lax.broadcasted_iota(jnp.int32, (c, 3 * c), 1) % c
    tri3 = jnp.where(c3 <= r3, 1.0, 0.0).astype(BF16)
    rp = lax.broadcasted_iota(jnp.int32, (PAIR, PAIR), 0)
    cp = lax.broadcasted_iota(jnp.int32, (PAIR, PAIR), 1)
    blockdiag = (rp // HEAD) == (cp // HEAD)
    return dict(lane_lo=lane_lo, tri3=tri3, strict=strict, incl=incl, eye=eye,
                blockdiag=blockdiag)


def _split3(x):
    hi = x.astype(BF16)
    rest = x - hi.astype(F32)
    mid = rest.astype(BF16)
    lo = (rest - mid.astype(F32)).astype(BF16)
    return jnp.concatenate([hi, mid, lo], axis=0)


def _split_heads(x, lane_lo):
    zero = jnp.zeros_like(x)
    return jnp.concatenate([jnp.where(lane_lo, x, zero), jnp.where(lane_lo, zero, x)], axis=0)


def _run_interleaved(stage_gens):
    results = [None] * len(stage_gens)
    live = list(range(len(stage_gens)))
    while live:
        still = []
        for idx in live:
            try:
                next(stage_gens[idx])
                still.append(idx)
            except StopIteration as done:
                results[idx] = done.value
        live = still
    return results


def _wkv_chunk(r, lw, k, v, al, be, get_state, cn):
    c = CHUNK
    lane_lo = cn["lane_lo"]
    cum = jnp.dot(cn["tri3"], _split3(lw), preferred_element_type=F32)
    end = cum[c - 1:c, :]
    yield
    e_cum = jnp.exp(cum)
    e_neg = jnp.exp(-cum)
    e_end = jnp.exp(end)
    rt = r * e_cum
    at = al * jnp.exp(cum - lw)
    bt = be * e_neg
    kt = k * e_neg
    lhs2 = jnp.concatenate([at, rt], axis=0).astype(BF16)
    keys = jnp.concatenate([_split_heads(bt, lane_lo), _split_heads(kt, lane_lo)], axis=0)
    gram = _dot_nt(lhs2, keys)
    yield
    a_ab = jnp.where(cn["strict"], gram[0:c, 0:PAIR], 0.0)
    a_ak = jnp.where(cn["strict"], gram[0:c, PAIR:], 0.0)
    a_rb = jnp.where(cn["incl"], gram[c:, 0:PAIR], 0.0)
    a_rk = jnp.where(cn["incl"], gram[c:, PAIR:], 0.0)
    inv = cn["eye"] + a_ab
    n_iter = int(math.log2(c)) - 1
    pw = _dot(a_ab, _split_heads(a_ab, lane_lo))
    yield
    for it in range(n_iter):
        pw_heads = _split_heads(pw, lane_lo)
        if it < n_iter - 1:
            both = _dot(jnp.concatenate([inv, pw], axis=0), pw_heads)
            inv = inv + both[0:c]
            pw = both[c:]
        else:
            inv = inv + _dot(inv, pw_heads)
        yield
    av = _dot(jnp.concatenate([a_ak, a_rk], axis=0), _split_heads(v, lane_lo))
    state = get_state()
    while state is None:
        yield
        state = get_state()
    from_state = _dot_nt(lhs2, state)
    yield
    u = _dot(inv, _split_heads(from_state[0:c] + av[0:c], lane_lo))
    yield
    y = from_state[c:] + av[c:] + _dot(a_rb, _split_heads(u, lane_lo))
    yield
    uv = jnp.concatenate([u, v], axis=0)
    bk = jnp.concatenate([bt * e_end, kt * e_end], axis=0)
    return y, state * e_end + jnp.where(cn["blockdiag"], _dot_tn(uv, bk), 0.0)


def _wkv_readout(y, r, k, v, g, rk, lnx_g, lnx_b, lane_lo):
    mean = _seg_sum(y, lane_lo) * (1.0 / HEAD)
    bonus = _seg_sum(r * k * rk, lane_lo) * v
    yield
    yc = y - mean
    var = _seg_sum(yc * yc, lane_lo) * (1.0 / HEAD)
    yield
    yn = yc * lax.rsqrt(var + GN_EPS) * lnx_g + lnx_b
    return (yn + bonus) * g


def _from_blockdiag(state):
    return state[0:HEAD, 0:HEAD], pltpu.roll(state[HEAD:], HEAD, 1)[:, 0:HEAD]


def _rwkv_seq_kernel(x_ref, w_ref, mu_ref, w2_ref, w0_ref, a0_ref, kk_ref, ka_ref, rk_ref, lg_ref,
                     lb_ref, y_ref, s_out_ref, hist_ref, acts_scr, s_scr, yraw_scr, *, tm):
    t = pl.program_id(1)
    cn = _wkv_consts()
    n_pairs = s_scr.shape[0]
    tt = x_ref.shape[1]

    @pl.when(t == 0)
    def _():
        hist_ref[...] = jnp.zeros_like(hist_ref)
        s_scr[...] = jnp.zeros_like(s_scr)
        yraw_scr[...] = jnp.zeros_like(yraw_scr)

    for sub in range(tt // tm):
        rows = slice(sub * tm, (sub + 1) * tm)
        outs = _rwkv_project_rows(x_ref[0, rows, :], w_ref, mu_ref, w2_ref, w0_ref, a0_ref,
                                  kk_ref, ka_ref, hist_ref)
        for idx, o in enumerate(outs):
            acts_scr[idx, rows, :] = o

    r_ref, lw_ref, k_ref, v_ref, al_ref, be_ref, g_ref = (acts_scr.at[i] for i in range(7))
    n_chunks = tt // CHUNK

    def readout_stages(p, rows):
        ln = pl.ds(p * PAIR, PAIR)
        out = yield from _wkv_readout(
            yraw_scr[:, ln], r_ref[rows, ln], k_ref[rows, ln], v_ref[rows, ln],
            g_ref[rows, ln], rk_ref[:, ln], lg_ref[:, ln], lb_ref[:, ln], cn["lane_lo"])
        y_ref[0, rows, ln] = out

    def chunk_stages(p, rows, get_state):
        ln = pl.ds(p * PAIR, PAIR)
        return _wkv_chunk(r_ref[rows, ln], lw_ref[rows, ln], k_ref[rows, ln],
                          v_ref[rows, ln], al_ref[rows, ln], be_ref[rows, ln],
                          get_state, cn)

    def body(i, states):
        def rows_of(j):
            return pl.ds(pl.multiple_of((CHUNKS_IN_FLIGHT * i + j) * CHUNK, CHUNK), CHUNK)

        prev = pl.ds(pl.multiple_of(jnp.maximum(CHUNKS_IN_FLIGHT * i - 1, 0) * CHUNK, CHUNK),
                     CHUNK)
        end_states = [[None] * n_pairs for _ in range(CHUNKS_IN_FLIGHT)]

        def start_state(j, p):
            return states[p] if j == 0 else end_states[j - 1][p]

        def stages(j, p):
            ln = pl.ds(p * PAIR, PAIR)
            rows = rows_of(j)
            y, state = yield from chunk_stages(p, rows, functools.partial(start_state, j, p))
            end_states[j][p] = state
            if j == CHUNKS_IN_FLIGHT - 1:
                yraw_scr[:, ln] = y
                return
            out = yield from _wkv_readout(
                y, r_ref[rows, ln], k_ref[rows, ln], v_ref[rows, ln],
                g_ref[rows, ln], rk_ref[:, ln], lg_ref[:, ln], lb_ref[:, ln], cn["lane_lo"])
            y_ref[0, rows, ln] = out

        _run_interleaved([readout_stages(p, prev) for p in range(n_pairs)]
                         + [stages(j, p) for j in range(CHUNKS_IN_FLIGHT) for p in range(n_pairs)])
        return tuple(end_states[CHUNKS_IN_FLIGHT - 1])

    assert n_chunks % CHUNKS_IN_FLIGHT == 0
    states = lax.fori_loop(0, n_chunks // CHUNKS_IN_FLIGHT, body,
                           tuple(s_scr[p] for p in range(n_pairs)))
    last = pl.ds((n_chunks - 1) * CHUNK, CHUNK)
    _run_interleaved([readout_stages(p, last) for p in range(n_pairs)])
    for p in range(n_pairs):
        s_scr[p] = states[p]
        s_out_ref[0, 2 * p], s_out_ref[0, 2 * p + 1] = _from_blockdiag(states[p])


def _wkv_step_kernel(r_ref, lw_ref, k_ref, v_ref, al_ref, be_ref, g_ref, rk_ref, lg_ref, lb_ref,
                     s_in_ref, y_ref, s_out_ref, tr_scr, y_scr, *, steps):
    nseq = s_in_ref.shape[3]
    idx_r, idx_dec, idx_k, idx_v, idx_al, idx_be, idx_g = range(7)
    for idx, a_ref in enumerate((r_ref, lw_ref, k_ref, v_ref, al_ref, be_ref, g_ref)):
        for t in range(steps):
            tile = a_ref[t * nseq:(t + 1) * nseq, :].T
            tr_scr[idx, t] = jnp.exp(tile) if idx == idx_dec else tile

    for h in range(2):
        ch = slice(h * HEAD, (h + 1) * HEAD)

        def row_stages(i, h=h, ch=ch):
            s = s_in_ref[h, i]
            for t in range(steps):
                sa = jnp.sum(s * tr_scr[idx_al, t, ch, :], axis=0, keepdims=True)
                yield
                v_i = tr_scr[idx_v, t, pl.ds(h * HEAD + i, 1), :]
                s = (s * tr_scr[idx_dec, t, ch, :] + sa * tr_scr[idx_be, t, ch, :]
                     + v_i * tr_scr[idx_k, t, ch, :])
                y_scr[t, pl.ds(h * HEAD + i, 1), :] = jnp.sum(
                    s * tr_scr[idx_r, t, ch, :], axis=0, keepdims=True)
                yield
            s_out_ref[h, i] = s

        @pl.loop(0, HEAD // STEP_ROWS_IN_FLIGHT)
        def _(blk):
            _run_interleaved([row_stages(blk * STEP_ROWS_IN_FLIGHT + ii)
                              for ii in range(STEP_ROWS_IN_FLIGHT)])

    for t in range(steps):
        outs = []
        for h in range(2):
            ch = slice(h * HEAD, (h + 1) * HEAD)
            y = y_scr[t, ch, :]
            yc = y - jnp.mean(y, axis=0, keepdims=True)
            var = jnp.mean(yc * yc, axis=0, keepdims=True)
            yn = yc * lax.rsqrt(var + GN_EPS) * lg_ref[ch, :] + lb_ref[ch, :]
            rk_sum = jnp.sum(tr_scr[idx_r, t, ch, :] * tr_scr[idx_k, t, ch, :] * rk_ref[ch, :],
                             axis=0, keepdims=True)
            outs.append((yn + rk_sum * tr_scr[idx_v, t, ch, :]) * tr_scr[idx_g, t, ch, :])
        y_ref[t * nseq:(t + 1) * nseq, :] = jnp.concatenate(outs, axis=0).T


def _rwkv_seq(x, wp, tt, tm):
    b, t, _ = x.shape
    row_spec = pl.BlockSpec((1, tt, D_MODEL), lambda i, j: (i, j, 0))
    vec = _const_spec((1, D_MODEL))
    return pl.pallas_call(
        functools.partial(_rwkv_seq_kernel, tm=tm),
        grid=(b, t // tt),
        in_specs=[row_spec, _const_spec((D_MODEL, RWKV_COLS)), _const_spec((1, RWKV_COLS)),
                  _const_spec((LORA, 3 * D_MODEL))] + [vec] * 7,
        out_specs=[row_spec,
                   pl.BlockSpec((1, N_HEADS, HEAD, HEAD), lambda i, j: (i, 0, 0, 0))],
        out_shape=[jax.ShapeDtypeStruct((b, t, D_MODEL), F32),
                   jax.ShapeDtypeStruct((b, N_HEADS, HEAD, HEAD), F32)],
        scratch_shapes=[pltpu.VMEM((SUBLANES, RWKV_COLS), F32),
                        pltpu.VMEM((7, tt, D_MODEL), F32),
                        pltpu.VMEM((N_PAIRS, PAIR, PAIR), F32),
                        pltpu.VMEM((CHUNK, D_MODEL), F32)],
        compiler_params=pltpu.CompilerParams(
            dimension_semantics=("arbitrary", "arbitrary"), vmem_limit_bytes=VMEM_LIMIT),
        name="rwkv_seq",
    )(x, wp["w_in"], wp["mu"], wp["w2ext"], wp["w0"], wp["a0"], wp["k_k"], wp["k_a"],
      wp["r_k"], wp["lnx_g"], wp["lnx_b"])


def _wkv_step(acts, s_last, wp, steps):
    n = acts[0].shape[0]
    nseq = s_last.shape[3]
    row_spec = pl.BlockSpec((n, PAIR), lambda p: (0, p))
    col_spec = pl.BlockSpec((PAIR, nseq), lambda p: (p, 0))
    st_spec = pl.BlockSpec((2, HEAD, HEAD, nseq), lambda p: (p, 0, 0, 0))
    cols = [jnp.broadcast_to(wp[name].reshape(D_MODEL, 1), (D_MODEL, nseq))
            for name in ("r_k", "lnx_g", "lnx_b")]
    return pl.pallas_call(
        functools.partial(_wkv_step_kernel, steps=steps),
        grid=(N_PAIRS,),
        in_specs=[row_spec] * 7 + [col_spec] * 3 + [st_spec],
        out_specs=[row_spec, st_spec],
        out_shape=[jax.ShapeDtypeStruct((n, D_MODEL), F32),
                   jax.ShapeDtypeStruct(s_last.shape, F32)],
        scratch_shapes=[pltpu.VMEM((7, steps, PAIR, nseq), F32),
                        pltpu.VMEM((steps, PAIR, nseq), F32)],
        compiler_params=pltpu.CompilerParams(
            dimension_semantics=("arbitrary",), vmem_limit_bytes=VMEM_LIMIT),
        name="wkv_step",
    )(*acts, *cols, s_last)


def _post_tile(x_ref, ya_ref, ga_ref, gyb_ref, wo_ref, l1g_ref, l1b_ref, wg_ref, wu_ref, wd_ref,
               l2g_ref, l2b_ref, y_ref):
    x = x_ref[...]
    merged = ga_ref[...] * ya_ref[...] + gyb_ref[...]
    mix = _dot(merged, wo_ref[...])
    h1 = _layer_norm(ALPHA * x + mix, l1g_ref[...], l1b_ref[...])
    h1b = h1.astype(BF16)
    gate = jnp.dot(h1b, wg_ref[...], preferred_element_type=F32)
    up = jnp.dot(h1b, wu_ref[...], preferred_element_type=F32)
    act = (gate * _sigmoid(gate)) * up
    ffn = _dot(act, wd_ref[...])
    y_ref[...] = _layer_norm(ALPHA * h1 + ffn, l2g_ref[...], l2b_ref[...])


def _post_kernel(*refs, first_tiles):
    rows_a, rows_b, weights, (ya_out, yb_out) = refs[0:4], refs[4:8], refs[8:16], refs[16:18]
    step = pl.program_id(0)

    @pl.when(step < first_tiles)
    def _():
        _post_tile(*rows_a, *weights, ya_out)

    @pl.when(step >= first_tiles)
    def _():
        _post_tile(*rows_b, *weights, yb_out)


def _post(rows_a, rows_b, wp, tm):
    tiles_a, tiles_b = rows_a[0].shape[0] // tm, rows_b[0].shape[0] // tm
    spec_a = pl.BlockSpec((tm, D_MODEL), lambda i: (jnp.minimum(i, tiles_a - 1), 0))
    spec_b = pl.BlockSpec((tm, D_MODEL), lambda i: (jnp.maximum(i - tiles_a, 0), 0))
    vec = _const_spec((1, D_MODEL))
    return pl.pallas_call(
        functools.partial(_post_kernel, first_tiles=tiles_a),
        grid=(tiles_a + tiles_b,),
        in_specs=[spec_a] * 4 + [spec_b] * 4
        + [_const_spec((D_MODEL, D_MODEL)), vec, vec, _const_spec((D_MODEL, D_FF)),
           _const_spec((D_MODEL, D_FF)), _const_spec((D_FF, D_MODEL)), vec, vec],
        out_specs=[spec_a, spec_b],
        out_shape=[jax.ShapeDtypeStruct(rows_a[0].shape, F32),
                   jax.ShapeDtypeStruct(rows_b[0].shape, F32)],
        compiler_params=pltpu.CompilerParams(
            dimension_semantics=("arbitrary",), vmem_limit_bytes=VMEM_LIMIT),
        name="post",
    )(*rows_a, *rows_b, wp["w_o"], wp["ln1_g"], wp["ln1_b"], wp["w_gate"], wp["w_up"],
      wp["w_down"], wp["ln2_g"], wp["ln2_b"])


def _prep_weights(w_in, tmix_mu, w0, w2_decay, a0, a2_iclr, g2_gate, k_k, k_a, r_k, lnx_g, lnx_b,
                  conv_w, conv_b, lru_wa, lru_ba, lru_wi, lru_bi, lru_lambda, w_o,
                  ln1_g, ln1_b, w_ffn_gate, w_ffn_up, w_ffn_down, ln2_g, ln2_b):
    row = lambda v: v.reshape(1, -1).astype(F32)
    zeros = lambda r: jnp.zeros((r, D_MODEL), F32)
    w2ext = jnp.concatenate([
        jnp.concatenate([w2_decay, zeros(DECAY_LORA), zeros(DECAY_LORA)], axis=1),
        jnp.concatenate([zeros(ICLR_LORA), a2_iclr, zeros(ICLR_LORA)], axis=1),
        jnp.concatenate([zeros(GATE_LORA), zeros(GATE_LORA), g2_gate], axis=1)], axis=0)

    def gate_blockdiag(w):
        per = LRU_GROUP // LRU_BS
        w4 = w.reshape(LRU_BLOCKS // per, per, LRU_BS, LRU_BS)
        eye = jnp.eye(per, dtype=w.dtype)
        bd = jnp.einsum("gpcd,pq->gpcqd", w4, eye)
        return bd.reshape(LRU_BLOCKS // per, LRU_GROUP, LRU_GROUP).astype(BF16)

    return dict(
        w_in=w_in.astype(BF16),
        mu=row(tmix_mu), w2ext=w2ext.astype(BF16), w0=row(w0), a0=row(a0), k_k=row(k_k),
        k_a=row(k_a), r_k=row(r_k), lnx_g=row(lnx_g), lnx_b=row(lnx_b),
        conv_w=conv_w.astype(F32), conv_b=row(conv_b), wa_bd=gate_blockdiag(lru_wa),
        ba=row(lru_ba), wi_bd=gate_blockdiag(lru_wi), bi=row(lru_bi), lam=row(lru_lambda),
        w_o=w_o.astype(BF16), ln1_g=row(ln1_g), ln1_b=row(ln1_b), w_gate=w_ffn_gate.astype(BF16),
        w_up=w_ffn_up.astype(BF16), w_down=w_ffn_down.astype(BF16), ln2_g=row(ln2_g),
        ln2_b=row(ln2_b))


def _row_tiles(t):
    rwkv_tile = min(t, 8 * CHUNK)
    return rwkv_tile, min(t, 256), min(t, 256), min(t, 256)


def _prompt_branches(x, wp):
    b, t, _ = x.shape
    rwkv_tile, proj_tile, lru_tile, _ = _row_tiles(t)
    ya, s_heads = _rwkv_seq(x, wp, rwkv_tile, proj_tile)
    ga, gyb, conv_tail, h_tail = _lru_seq(x, wp, lru_tile)
    rows = tuple(a.reshape(b * t, D_MODEL) for a in (x, ya, ga, gyb))
    states = (x[:, -1], s_heads, conv_tail[:, SUBLANES - (CONV_W - 1):], h_tail[:, SUBLANES - 1])
    return rows, states


def _sample_branches(x, shift_buf, wkv0, conv_buf, h0, wp):
    b, t, _ = x.shape
    n = b * t
    x_tm = x.transpose(1, 0, 2).reshape(n, D_MODEL)
    xprev_tm = jnp.concatenate([shift_buf, x_tm[:n - b]], axis=0)
    acts = _rwkv_proj_step(x_tm, xprev_tm, wp)
    ya_tm, s_last = _wkv_step(acts, wkv0.transpose(1, 2, 3, 0), wp, t)
    conv_tm = conv_buf.transpose(1, 0, 2).reshape((CONV_W - 1) * b, D_LRU)
    ga_tm, gyb_tm, conv_new_tm, h_last = _lru_step(x_tm, conv_tm, h0, wp, t, b)
    conv_new = conv_new_tm.reshape(CONV_W - 1, b, D_LRU).transpose(1, 0, 2)
    states = (x[:, -1], s_last.transpose(3, 0, 1, 2), conv_new, h_last)
    return (x_tm, ya_tm, ga_tm, gyb_tm), states


def _layers(x_prompt, x_sample, shift_buf, wkv0, conv_buf, h0, wp):
    bp, tp, _ = x_prompt.shape
    bs, ts, _ = x_sample.shape
    rows_p, states_p = _prompt_branches(x_prompt, wp)
    rows_s, states_s = _sample_branches(x_sample, shift_buf, wkv0, conv_buf, h0, wp)
    post_tile = min(_row_tiles(tp)[3], bs * ts)
    yp, ys_tm = _post(rows_p, rows_s, wp, post_tile)
    ys = ys_tm.reshape(ts, bs, D_MODEL).transpose(1, 0, 2)
    return yp.reshape(bp, tp, D_MODEL), states_p, ys, states_s


def kernel(x_prompt, x_sample, state_shift, state_wkv, state_conv, state_lru, w_in, tmix_mu, w0, w2_decay, a0, a2_iclr, g2_gate, k_k, k_a, r_k, lnx_g, lnx_b, conv_w, conv_b, lru_wa, lru_ba, lru_wi, lru_bi, lru_lambda, w_o, ln1_g, ln1_b, w_ffn_gate, w_ffn_up, w_ffn_down, ln2_g, ln2_b):
    params = (w_in, tmix_mu, w0, w2_decay, a0, a2_iclr, g2_gate, k_k, k_a, r_k, lnx_g, lnx_b,
              conv_w, conv_b, lru_wa, lru_ba, lru_wi, lru_bi, lru_lambda, w_o,
              ln1_g, ln1_b, w_ffn_gate, w_ffn_up, w_ffn_down, ln2_g, ln2_b)
    wp = _prep_weights(*[p[0] for p in params])
    yp, states_p, ys, states_s = _layers(x_prompt, x_sample, state_shift[0], state_wkv[0],
                                         state_conv[0], state_lru[0], wp)
    return (yp, ys) + tuple(s[None] for s in states_p) + tuple(s[None] for s in states_s)
```

```python
import functools
import math

import jax
import jax.numpy as jnp
from jax import lax
from jax.experimental import pallas as pl
from jax.experimental.pallas import tpu as pltpu

F32 = jnp.float32
BF16 = jnp.bfloat16

D_MODEL = 1024
HEAD = 64
N_HEADS = D_MODEL // HEAD
PAIR = 2 * HEAD
N_PAIRS = N_HEADS // 2
DECAY_LORA = 64
ICLR_LORA = 64
GATE_LORA = 128
LORA = DECAY_LORA + ICLR_LORA + GATE_LORA
RWKV_COLS = 3 * D_MODEL + LORA
GN_EPS = HEAD * 1e-5
D_LRU = D_MODEL
LRU_BLOCKS = 16
LRU_BS = D_LRU // LRU_BLOCKS
LRU_GROUP = 256
CONV_W = 4
LRU_C = 8.0
LRU_COLS = 2 * D_LRU + 2 * D_MODEL
N_IN = RWKV_COLS + LRU_COLS
D_FF = 2816
ALPHA = 2.0 ** 0.25
LN_EPS = 1e-5

SUBLANES = 8
CHUNK = 64
CHUNKS_IN_FLIGHT = 4
STEP_ROWS_IN_FLIGHT = 8
VMEM_LIMIT = 56 * 1024 * 1024


def _softplus(x):
    return jnp.maximum(x, 0.0) + jnp.log1p(jnp.exp(-jnp.abs(x)))


def _sigmoid(x):
    return 0.5 * jnp.tanh(0.5 * x) + 0.5


def _gelu_tanh(x):
    c = math.sqrt(2.0 / math.pi)
    inner = x * (c + (c * 0.044715) * (x * x))
    return (0.5 * x) * (1.0 + jnp.tanh(inner))


def _layer_norm(x, g, b):
    mu = jnp.mean(x, axis=-1, keepdims=True)
    xc = x - mu
    var = jnp.mean(xc * xc, axis=-1, keepdims=True)
    return xc * lax.rsqrt(var + LN_EPS) * g + b


def _dot(a, b):
    return jnp.dot(a.astype(BF16), b.astype(BF16), preferred_element_type=F32)


def _dot_nt(a, b):
    return lax.dot_general(a.astype(BF16), b.astype(BF16), (((1,), (1,)), ((), ())),
                           preferred_element_type=F32)


def _dot_tn(a, b):
    return lax.dot_general(a.astype(BF16), b.astype(BF16), (((0,), (0,)), ((), ())),
                           preferred_element_type=F32)


def _seg_sum(x, lane_lo):
    s0 = jnp.sum(jnp.where(lane_lo, x, 0.0), axis=-1, keepdims=True)
    s1 = jnp.sum(jnp.where(lane_lo, 0.0, x), axis=-1, keepdims=True)
    return jnp.where(lane_lo, s0, s1)


def _head_sums(x):
    lane_lo = lax.broadcasted_iota(jnp.int32, (x.shape[0], PAIR), 1) < HEAD
    return jnp.concatenate([_seg_sum(x[:, p * PAIR:(p + 1) * PAIR], lane_lo)
                            for p in range(N_PAIRS)], axis=1)


def _rwkv_prep(mixed, w2ext, w0, a0, k_k, k_a):
    lo = mixed(3 * D_MODEL, LORA)
    lane = lax.broadcasted_iota(jnp.int32, lo.shape, 1)
    act = jnp.where(lane < DECAY_LORA, jnp.tanh(lo),
                    jnp.where(lane < DECAY_LORA + ICLR_LORA, lo, _sigmoid(lo)))
    r = mixed(0, D_MODEL)
    lora = _dot(act, w2ext)
    k = mixed(D_MODEL, D_MODEL)
    lw = -math.exp(-0.5) * _sigmoid(w0 + lora[:, 0:D_MODEL])
    a = _sigmoid(a0 + lora[:, D_MODEL:2 * D_MODEL])
    g = lora[:, 2 * D_MODEL:3 * D_MODEL]
    v = mixed(2 * D_MODEL, D_MODEL)
    kkraw = k * k_k
    kk = kkraw * lax.rsqrt(jnp.maximum(_head_sums(kkraw * kkraw), 1e-24))
    k2 = k * (1.0 + (a - 1.0) * k_a)
    return r, lw, k2, v, -kk, kk * a, g


def _rwkv_project_rows(x, w_ref, mu_ref, w2_ref, w0_ref, a0_ref, kk_ref, ka_ref, hist_ref):
    xbf = x.astype(BF16)
    tm = xbf.shape[0]
    row = lax.broadcasted_iota(jnp.int32, (SUBLANES, 1), 0)

    def mixed(first_col, n_cols):
        cols = slice(first_col, first_col + n_cols)
        z = jnp.dot(xbf, w_ref[:, cols], preferred_element_type=F32)
        rolled = pltpu.roll(z, 1, 0)
        first = jnp.where(row == 0, hist_ref[SUBLANES - 1:SUBLANES, cols], rolled[0:SUBLANES])
        zprev = jnp.concatenate([first, rolled[SUBLANES:]], axis=0)
        hist_ref[:, cols] = z[tm - SUBLANES:tm]
        return z + mu_ref[:, cols] * (zprev - z)

    return _rwkv_prep(mixed, w2_ref[...], w0_ref[...], a0_ref[...], kk_ref[...], ka_ref[...])


def _rwkv_proj_step_kernel(x_ref, xp_ref, w_ref, mu_ref, w2_ref, w0_ref, a0_ref, kk_ref, ka_ref,
                           r_ref, lw_ref, k_ref, v_ref, al_ref, be_ref, g_ref):
    xbf = x_ref[...].astype(BF16)
    xpbf = xp_ref[...].astype(BF16)

    def mixed(first_col, n_cols):
        cols = slice(first_col, first_col + n_cols)
        z = jnp.dot(xbf, w_ref[:, cols], preferred_element_type=F32)
        zprev = jnp.dot(xpbf, w_ref[:, cols], preferred_element_type=F32)
        return z + mu_ref[:, cols] * (zprev - z)

    outs = _rwkv_prep(mixed, w2_ref[...], w0_ref[...], a0_ref[...], kk_ref[...], ka_ref[...])
    for o_ref, o in zip((r_ref, lw_ref, k_ref, v_ref, al_ref, be_ref, g_ref), outs):
        o_ref[...] = o


def _const_spec(shape):
    nd = len(shape)
    return pl.BlockSpec(shape, lambda *_: (0,) * nd, pipeline_mode=pl.Buffered(1))


def _rwkv_proj_step(x, xprev, wp):
    n = x.shape[0]
    full = pl.BlockSpec((n, D_MODEL), lambda i: (0, 0))
    vec = _const_spec((1, D_MODEL))
    return pl.pallas_call(
        _rwkv_proj_step_kernel,
        grid=(1,),
        in_specs=[full, full, _const_spec((D_MODEL, RWKV_COLS)), _const_spec((1, RWKV_COLS)),
                  _const_spec((LORA, 3 * D_MODEL)), vec, vec, vec, vec],
        out_specs=[full] * 7,
        out_shape=[jax.ShapeDtypeStruct((n, D_MODEL), F32)] * 7,
        compiler_params=pltpu.CompilerParams(
            dimension_semantics=("arbitrary",), vmem_limit_bytes=VMEM_LIMIT),
        name="rwkv_proj_step",
    )(x, xprev, wp["w_in"], wp["mu"], wp["w2ext"], wp["w0"], wp["a0"], wp["k_k"], wp["k_a"])


def _lru_gate_logits(u, wa_ref, wi_ref):
    ub = u.astype(BF16)
    ra, ia = [], []
    for q in range(D_LRU // LRU_GROUP):
        uq = ub[:, q * LRU_GROUP:(q + 1) * LRU_GROUP]
        ra.append(jnp.dot(uq, wa_ref[q], preferred_element_type=F32))
        ia.append(jnp.dot(uq, wi_ref[q], preferred_element_type=F32))
    return jnp.concatenate(ra, axis=1), jnp.concatenate(ia, axis=1)


def _lru_coeffs(ra, ia, ba, bi, sp):
    ig = _sigmoid(ia + bi)
    half = (-0.5 * LRU_C) * sp
    log_a = half * jnp.tanh(0.5 * (ra + ba)) + half
    a = jnp.exp(log_a)
    m2 = -jnp.tanh(log_a) * (a * a + 1.0)
    mult = jnp.where(m2 > 0.0, m2 * lax.rsqrt(m2), 0.0)
    return a, mult, ig


def _lru_seq_kernel(x_ref, w_ref, cw_ref, cb_ref, wa_ref, ba_ref, wi_ref, bi_ref, lam_ref,
                    ga_ref, gyb_ref, conv_ref, hlast_ref,
                    xhist_ref, hc_ref, a_s, bx_s, h_s):
    t = pl.program_id(1)

    @pl.when(t == 0)
    def _():
        xhist_ref[...] = jnp.zeros_like(xhist_ref)
        hc_ref[...] = jnp.zeros_like(hc_ref)

    xbf = x_ref[0].astype(BF16)
    tm = xbf.shape[0]

    def proj(first_col, n_cols):
        cols = slice(RWKV_COLS + first_col, RWKV_COLS + first_col + n_cols)
        return jnp.dot(xbf, w_ref[:, cols], preferred_element_type=F32)

    xb = proj(0, D_LRU)
    gb = proj(D_LRU, D_LRU)
    row8 = lax.broadcasted_iota(jnp.int32, (SUBLANES, 1), 0)
    hist = xhist_ref[...]

    def shifted(k):
        rolled = pltpu.roll(xb, k, 0)
        first = jnp.where(row8 < k, pltpu.roll(hist, k, 0), rolled[0:SUBLANES])
        return jnp.concatenate([first, rolled[SUBLANES:]], axis=0)

    cw = cw_ref[...]
    u = (cb_ref[...] + cw[0:1] * shifted(3) + cw[1:2] * shifted(2) + cw[2:3] * shifted(1)
         + cw[3:4] * xb)
    last8 = xb[tm - SUBLANES:tm]
    xhist_ref[...] = last8
    conv_ref[0] = last8

    ra, ia = _lru_gate_logits(u, wa_ref, wi_ref)
    gelu_gb = _gelu_tanh(gb)
    zga = proj(2 * D_LRU, D_MODEL)
    sp = _softplus(-lam_ref[...])
    a, mult, ig = _lru_coeffs(ra, ia, ba_ref[...], bi_ref[...], sp)
    first_row = jnp.logical_and(row8 == 0, t == 0)
    mult = jnp.concatenate([jnp.where(first_row, 1.0, mult[0:SUBLANES]), mult[SUBLANES:]], axis=0)
    a_s[...] = a
    bx_s[...] = mult * ig * u
    ga_ref[0] = _sigmoid(zga)
    zgb = proj(2 * D_LRU + D_MODEL, D_MODEL)

    hc = hc_ref[0:1, :]
    for blk in range(tm // SUBLANES):
        rows = slice(blk * SUBLANES, (blk + 1) * SUBLANES)
        av = a_s[rows, :]
        bv = bx_s[rows, :]
        for d in (1, 2, 4):
            a_sh = jnp.where(row8 < d, 1.0, pltpu.roll(av, d, 0))
            b_sh = jnp.where(row8 < d, 0.0, pltpu.roll(bv, d, 0))
            bv = av * b_sh + bv
            av = av * a_sh
        h_blk = bv + av * hc
        h_s[rows, :] = h_blk
        hc = h_blk[SUBLANES - 1:SUBLANES, :]
    hc_ref[0:1, :] = hc
    h = h_s[...]
    hlast_ref[0] = h[tm - SUBLANES:tm]
    gyb_ref[0] = (_sigmoid(zgb) * gelu_gb) * h


def _lru_step_kernel(x_ref, conv0_ref, h0_ref, w_ref, cw_ref, cb_ref, wa_ref, ba_ref, wi_ref,
                     bi_ref, lam_ref, ga_ref, gyb_ref, conv_ref, hlast_ref, *, steps, nseq):
    z = _dot(x_ref[...], w_ref[:, RWKV_COLS:])
    xb = z[:, 0:D_LRU]
    gb = z[:, D_LRU:2 * D_LRU]
    zga = z[:, 2 * D_LRU:2 * D_LRU + D_MODEL]
    zgb = z[:, 2 * D_LRU + D_MODEL:]
    n = steps * nseq
    hist = (CONV_W - 1) * nseq
    xext = jnp.concatenate([conv0_ref[...], xb], axis=0)

    def shifted(k):
        return xext[hist - k * nseq:hist - k * nseq + n]

    cw = cw_ref[...]
    u = (cb_ref[...] + cw[0:1] * shifted(3) + cw[1:2] * shifted(2) + cw[2:3] * shifted(1)
         + cw[3:4] * xb)
    conv_ref[...] = xext[n:n + hist]
    sp = _softplus(-lam_ref[...])
    ra, ia = _lru_gate_logits(u, wa_ref, wi_ref)
    a, mult, ig = _lru_coeffs(ra, ia, ba_ref[...], bi_ref[...], sp)
    bx = mult * ig * u
    h = h0_ref[...]
    hs = []
    for s in range(steps):
        h = a[s * nseq:(s + 1) * nseq] * h + bx[s * nseq:(s + 1) * nseq]
        hs.append(h)
    hlast_ref[...] = h
    hall = jnp.concatenate(hs, axis=0)
    ga_ref[...] = _sigmoid(zga)
    gyb_ref[...] = _sigmoid(zgb) * (hall * _gelu_tanh(gb))


def _lru_weight_specs():
    vec = _const_spec((1, D_LRU))
    gate_w = _const_spec((D_LRU // LRU_GROUP, LRU_GROUP, LRU_GROUP))
    return [_const_spec((D_MODEL, N_IN)), _const_spec((CONV_W, D_LRU)), vec,
            gate_w, vec, gate_w, vec, vec]


def _lru_weights(wp):
    return (wp["w_in"], wp["conv_w"], wp["conv_b"], wp["wa_bd"], wp["ba"], wp["wi_bd"],
            wp["bi"], wp["lam"])


def _lru_seq(x, wp, tm):
    b, t, _ = x.shape
    row_spec = pl.BlockSpec((1, tm, D_MODEL), lambda i, j: (i, j, 0))
    tail_spec = pl.BlockSpec((1, SUBLANES, D_LRU), lambda i, j: (i, 0, 0))
    return pl.pallas_call(
        _lru_seq_kernel,
        grid=(b, t // tm),
        in_specs=[row_spec] + _lru_weight_specs(),
        out_specs=[row_spec, row_spec, tail_spec, tail_spec],
        out_shape=[jax.ShapeDtypeStruct((b, t, D_MODEL), F32)] * 2
        + [jax.ShapeDtypeStruct((b, SUBLANES, D_LRU), F32)] * 2,
        scratch_shapes=[pltpu.VMEM((SUBLANES, D_LRU), F32), pltpu.VMEM((SUBLANES, D_LRU), F32),
                        pltpu.VMEM((tm, D_LRU), F32), pltpu.VMEM((tm, D_LRU), F32),
                        pltpu.VMEM((tm, D_LRU), F32)],
        compiler_params=pltpu.CompilerParams(
            dimension_semantics=("arbitrary", "arbitrary"), vmem_limit_bytes=VMEM_LIMIT),
        name="lru_seq",
    )(x, *_lru_weights(wp))


def _lru_step(x_tm, conv_tm, h0, wp, steps, nseq):
    n = steps * nseq
    hist = (CONV_W - 1) * nseq

    def full(r):
        return pl.BlockSpec((r, D_MODEL), lambda i: (0, 0))

    return pl.pallas_call(
        functools.partial(_lru_step_kernel, steps=steps, nseq=nseq),
        grid=(1,),
        in_specs=[full(n), full(hist), full(nseq)] + _lru_weight_specs(),
        out_specs=[full(n), full(n), full(hist), full(nseq)],
        out_shape=[jax.ShapeDtypeStruct((n, D_MODEL), F32)] * 2
        + [jax.ShapeDtypeStruct((hist, D_LRU), F32), jax.ShapeDtypeStruct((nseq, D_LRU), F32)],
        compiler_params=pltpu.CompilerParams(
            dimension_semantics=("arbitrary",), vmem_limit_bytes=VMEM_LIMIT),
        name="lru_step",
    )(x_tm, conv_tm, h0, *_lru_weights(wp))


def _wkv_consts():
    c = CHUNK
    lane = lax.broadcasted_iota(jnp.int32, (c, PAIR), 1)
    row = lax.broadcasted_iota(jnp.int32, (c, PAIR), 0)
    lane_lo = lane < HEAD
    col = lane % c
    strict = col < row
    incl = col <= row
    eye = jnp.where(col == row, 1.0, 0.0).astype(F32)
    r3 = lax.broadcasted_iota(jnp.int32, (c, 3 * c), 0)
    c3 = lax.broadcasted_iota(jnp.int32, (c, 3 * c), 1) % c
    tri3 = jnp.where(c3 <= r3, 1.0, 0.0).astype(BF16)
    rp = lax.broadcasted_iota(jnp.int32, (PAIR, PAIR), 0)
    cp = lax.broadcasted_iota(jnp.int32, (PAIR, PAIR), 1)
    blockdiag = (rp // HEAD) == (cp // HEAD)
    return dict(lane_lo=lane_lo, tri3=tri3, strict=strict, incl=incl, eye=eye,
                blockdiag=blockdiag)


def _split3(x):
    hi = x.astype(BF16)
    rest = x - hi.astype(F32)
    mid = rest.astype(BF16)
    lo = (rest - mid.astype(F32)).astype(BF16)
    return jnp.concatenate([hi, mid, lo], axis=0)


def _split_heads(x, lane_lo):
    zero = jnp.zeros_like(x)
    return jnp.concatenate([jnp.where(lane_lo, x, zero), jnp.where(lane_lo, zero, x)], axis=0)


def _run_interleaved(stage_gens):
    results = [None] * len(stage_gens)
    live = list(range(len(stage_gens)))
    while live:
        still = []
        for idx in live:
            try:
                next(stage_gens[idx])
                still.append(idx)
            except StopIteration as done:
                results[idx] = done.value
        live = still
    return results


def _wkv_chunk(r, lw, k, v, al, be, get_state, cn):
    c = CHUNK
    lane_lo = cn["lane_lo"]
    cum = jnp.dot(cn["tri3"], _split3(lw), preferred_element_type=F32)
    end = cum[c - 1:c, :]
    yield
    e_cum = jnp.exp(cum)
    e_neg = jnp.exp(-cum)
    e_end = jnp.exp(end)
    rt = r * e_cum
    at = al * jnp.exp(cum - lw)
    bt = be * e_neg
    kt = k * e_neg
    lhs2 = jnp.concatenate([at, rt], axis=0).astype(BF16)
    keys = jnp.concatenate([_split_heads(bt, lane_lo), _split_heads(kt, lane_lo)], axis=0)
    gram = _dot_nt(lhs2, keys)
    yield
    a_ab = jnp.where(cn["strict"], gram[0:c, 0:PAIR], 0.0)
    a_ak = jnp.where(cn["strict"], gram[0:c, PAIR:], 0.0)
    a_rb = jnp.where(cn["incl"], gram[c:, 0:PAIR], 0.0)
    a_rk = jnp.where(cn["incl"], gram[c:, PAIR:], 0.0)
    inv = cn["eye"] + a_ab
    n_iter = int(math.log2(c)) - 1
    pw = _dot(a_ab, _split_heads(a_ab, lane_lo))
    yield
    for it in range(n_iter):
        pw_heads = _split_heads(pw, lane_lo)
        if it < n_iter - 1:
            both = _dot(jnp.concatenate([inv, pw], axis=0), pw_heads)
            inv = inv + both[0:c]
            pw = both[c:]
        else:
            inv = inv + _dot(inv, pw_heads)
        yield
    av = _dot(jnp.concatenate([a_ak, a_rk], axis=0), _split_heads(v, lane_lo))
    state = get_state()
    while state is None:
        yield
        state = get_state()
    from_state = _dot_nt(lhs2, state)
    yield
    u = _dot(inv, _split_heads(from_state[0:c] + av[0:c], lane_lo))
    yield
    y = from_state[c:] + av[c:] + _dot(a_rb, _split_heads(u, lane_lo))
    yield
    uv = jnp.concatenate([u, v], axis=0)
    bk = jnp.concatenate([bt * e_end, kt * e_end], axis=0)
    return y, state * e_end + jnp.where(cn["blockdiag"], _dot_tn(uv, bk), 0.0)


def _wkv_readout(y, r, k, v, g, rk, lnx_g, lnx_b, lane_lo):
    mean = _seg_sum(y, lane_lo) * (1.0 / HEAD)
    bonus = _seg_sum(r * k * rk, lane_lo) * v
    yield
    yc = y - mean
    var = _seg_sum(yc * yc, lane_lo) * (1.0 / HEAD)
    yield
    yn = yc * lax.rsqrt(var + GN_EPS) * lnx_g + lnx_b
    return (yn + bonus) * g


def _from_blockdiag(state):
    return state[0:HEAD, 0:HEAD], pltpu.roll(state[HEAD:], HEAD, 1)[:, 0:HEAD]


def _rwkv_seq_kernel(x_ref, w_ref, mu_ref, w2_ref, w0_ref, a0_ref, kk_ref, ka_ref, rk_ref, lg_ref,
                     lb_ref, y_ref, s_out_ref, hist_ref, acts_scr, s_scr, yraw_scr, *, tm):
    t = pl.program_id(1)
    cn = _wkv_consts()
    n_pairs = s_scr.shape[0]
    tt = x_ref.shape[1]

    @pl.when(t == 0)
    def _():
        hist_ref[...] = jnp.zeros_like(hist_ref)
        s_scr[...] = jnp.zeros_like(s_scr)
        yraw_scr[...] = jnp.zeros_like(yraw_scr)

    for sub in range(tt // tm):
        rows = slice(sub * tm, (sub + 1) * tm)
        outs = _rwkv_project_rows(x_ref[0, rows, :], w_ref, mu_ref, w2_ref, w0_ref, a0_ref,
                                  kk_ref, ka_ref, hist_ref)
        for idx, o in enumerate(outs):
            acts_scr[idx, rows, :] = o

    r_ref, lw_ref, k_ref, v_ref, al_ref, be_ref, g_ref = (acts_scr.at[i] for i in range(7))
    n_chunks = tt // CHUNK

    def readout_stages(p, rows):
        ln = pl.ds(p * PAIR, PAIR)
        out = yield from _wkv_readout(
            yraw_scr[:, ln], r_ref[rows, ln], k_ref[rows, ln], v_ref[rows, ln],
            g_ref[rows, ln], rk_ref[:, ln], lg_ref[:, ln], lb_ref[:, ln], cn["lane_lo"])
        y_ref[0, rows, ln] = out

    def chunk_stages(p, rows, get_state):
        ln = pl.ds(p * PAIR, PAIR)
        return _wkv_chunk(r_ref[rows, ln], lw_ref[rows, ln], k_ref[rows, ln],
                          v_ref[rows, ln], al_ref[rows, ln], be_ref[rows, ln],
                          get_state, cn)

    def body(i, states):
        def rows_of(j):
            return pl.ds(pl.multiple_of((CHUNKS_IN_FLIGHT * i + j) * CHUNK, CHUNK), CHUNK)

        prev = pl.ds(pl.multiple_of(jnp.maximum(CHUNKS_IN_FLIGHT * i - 1, 0) * CHUNK, CHUNK),
                     CHUNK)
        end_states = [[None] * n_pairs for _ in range(CHUNKS_IN_FLIGHT)]

        def start_state(j, p):
            return states[p] if j == 0 else end_states[j - 1][p]

        def stages(j, p):
            ln = pl.ds(p * PAIR, PAIR)
            rows = rows_of(j)
            y, state = yield from chunk_stages(p, rows, functools.partial(start_state, j, p))
            end_states[j][p] = state
            if j == CHUNKS_IN_FLIGHT - 1:
                yraw_scr[:, ln] = y
                return
            out = yield from _wkv_readout(
                y, r_ref[rows, ln], k_ref[rows, ln], v_ref[rows, ln],
                g_ref[rows, ln], rk_ref[:, ln], lg_ref[:, ln], lb_ref[:, ln], cn["lane_lo"])
            y_ref[0, rows, ln] = out

        _run_interleaved([readout_stages(p, prev) for p in range(n_pairs)]
                         + [stages(j, p) for j in range(CHUNKS_IN_FLIGHT) for p in range(n_pairs)])
        return tuple(end_states[CHUNKS_IN_FLIGHT - 1])

    assert n_chunks % CHUNKS_IN_FLIGHT == 0
    states = lax.fori_loop(0, n_chunks // CHUNKS_IN_FLIGHT, body,
                           tuple(s_scr[p] for p in range(n_pairs)))
    last = pl.ds((n_chunks - 1) * CHUNK, CHUNK)
    _run_interleaved([readout_stages(p, last) for p in range(n_pairs)])
    for p in range(n_pairs):
        s_scr[p] = states[p]
        s_out_ref[0, 2 * p], s_out_ref[0, 2 * p + 1] = _from_blockdiag(states[p])


def _wkv_step_kernel(r_ref, lw_ref, k_ref, v_ref, al_ref, be_ref, g_ref, rk_ref, lg_ref, lb_ref,
                     s_in_ref, y_ref, s_out_ref, tr_scr, y_scr, *, steps):
    nseq = s_in_ref.shape[3]
    idx_r, idx_dec, idx_k, idx_v, idx_al, idx_be, idx_g = range(7)
    for idx, a_ref in enumerate((r_ref, lw_ref, k_ref, v_ref, al_ref, be_ref, g_ref)):
        for t in range(steps):
            tile = a_ref[t * nseq:(t + 1) * nseq, :].T
            tr_scr[idx, t] = jnp.exp(tile) if idx == idx_dec else tile

    for h in range(2):
        ch = slice(h * HEAD, (h + 1) * HEAD)

        def row_stages(i, h=h, ch=ch):
            s = s_in_ref[h, i]
            for t in range(steps):
                sa = jnp.sum(s * tr_scr[idx_al, t, ch, :], axis=0, keepdims=True)
                yield
                v_i = tr_scr[idx_v, t, pl.ds(h * HEAD + i, 1), :]
                s = (s * tr_scr[idx_dec, t, ch, :] + sa * tr_scr[idx_be, t, ch, :]
                     + v_i * tr_scr[idx_k, t, ch, :])
                y_scr[t, pl.ds(h * HEAD + i, 1), :] = jnp.sum(
                    s * tr_scr[idx_r, t, ch, :], axis=0, keepdims=True)
                yield
            s_out_ref[h, i] = s

        @pl.loop(0, HEAD // STEP_ROWS_IN_FLIGHT)
        def _(blk):
            _run_interleaved([row_stages(blk * STEP_ROWS_IN_FLIGHT + ii)
                              for ii in range(STEP_ROWS_IN_FLIGHT)])

    for t in range(steps):
        outs = []
        for h in range(2):
            ch = slice(h * HEAD, (h + 1) * HEAD)
            y = y_scr[t, ch, :]
            yc = y - jnp.mean(y, axis=0, keepdims=True)
            var = jnp.mean(yc * yc, axis=0, keepdims=True)
            yn = yc * lax.rsqrt(var + GN_EPS) * lg_ref[ch, :] + lb_ref[ch, :]
            rk_sum = jnp.sum(tr_scr[idx_r, t, ch, :] * tr_scr[idx_k, t, ch, :] * rk_ref[ch, :],
                             axis=0, keepdims=True)
            outs.append((yn + rk_sum * tr_scr[idx_v, t, ch, :]) * tr_scr[idx_g, t, ch, :])
        y_ref[t * nseq:(t + 1) * nseq, :] = jnp.concatenate(outs, axis=0).T


def _rwkv_seq(x, wp, tt, tm):
    b, t, _ = x.shape
    row_spec = pl.BlockSpec((1, tt, D_MODEL), lambda i, j: (i, j, 0))
    vec = _const_spec((1, D_MODEL))
    return pl.pallas_call(
        functools.partial(_rwkv_seq_kernel, tm=tm),
        grid=(b, t // tt),
        in_specs=[row_spec, _const_spec((D_MODEL, RWKV_COLS)), _const_spec((1, RWKV_COLS)),
                  _const_spec((LORA, 3 * D_MODEL))] + [vec] * 7,
        out_specs=[row_spec,
                   pl.BlockSpec((1, N_HEADS, HEAD, HEAD), lambda i, j: (i, 0, 0, 0))],
        out_shape=[jax.ShapeDtypeStruct((b, t, D_MODEL), F32),
                   jax.ShapeDtypeStruct((b, N_HEADS, HEAD, HEAD), F32)],
        scratch_shapes=[pltpu.VMEM((SUBLANES, RWKV_COLS), F32),
                        pltpu.VMEM((7, tt, D_MODEL), F32),
                        pltpu.VMEM((N_PAIRS, PAIR, PAIR), F32),
                        pltpu.VMEM((CHUNK, D_MODEL), F32)],
        compiler_params=pltpu.CompilerParams(
            dimension_semantics=("arbitrary", "arbitrary"), vmem_limit_bytes=VMEM_LIMIT),
        name="rwkv_seq",
    )(x, wp["w_in"], wp["mu"], wp["w2ext"], wp["w0"], wp["a0"], wp["k_k"], wp["k_a"],
      wp["r_k"], wp["lnx_g"], wp["lnx_b"])


def _wkv_step(acts, s_last, wp, steps):
    n = acts[0].shape[0]
    nseq = s_last.shape[3]
    row_spec = pl.BlockSpec((n, PAIR), lambda p: (0, p))
    col_spec = pl.BlockSpec((PAIR, nseq), lambda p: (p, 0))
    st_spec = pl.BlockSpec((2, HEAD, HEAD, nseq), lambda p: (p, 0, 0, 0))
    cols = [jnp.broadcast_to(wp[name].reshape(D_MODEL, 1), (D_MODEL, nseq))
            for name in ("r_k", "lnx_g", "lnx_b")]
    return pl.pallas_call(
        functools.partial(_wkv_step_kernel, steps=steps),
        grid=(N_PAIRS,),
        in_specs=[row_spec] * 7 + [col_spec] * 3 + [st_spec],
        out_specs=[row_spec, st_spec],
        out_shape=[jax.ShapeDtypeStruct((n, D_MODEL), F32),
                   jax.ShapeDtypeStruct(s_last.shape, F32)],
        scratch_shapes=[pltpu.VMEM((7, steps, PAIR, nseq), F32),
                        pltpu.VMEM((steps, PAIR, nseq), F32)],
        compiler_params=pltpu.CompilerParams(
            dimension_semantics=("arbitrary",), vmem_limit_bytes=VMEM_LIMIT),
        name="wkv_step",
    )(*acts, *cols, s_last)


def _post_kernel(x_ref, ya_ref, ga_ref, gyb_ref, wo_ref, l1g_ref, l1b_ref, wg_ref, wu_ref, wd_ref,
                 l2g_ref, l2b_ref, y_ref):
    x = x_ref[...]
    merged = ga_ref[...] * ya_ref[...] + gyb_ref[...]
    mix = _dot(merged, wo_ref[...])
    h1 = _layer_norm(ALPHA * x + mix, l1g_ref[...], l1b_ref[...])
    h1b = h1.astype(BF16)
    gate = jnp.dot(h1b, wg_ref[...], preferred_element_type=F32)
    up = jnp.dot(h1b, wu_ref[...], preferred_element_type=F32)
    act = (gate * _sigmoid(gate)) * up
    ffn = _dot(act, wd_ref[...])
    y_ref[...] = _layer_norm(ALPHA * h1 + ffn, l2g_ref[...], l2b_ref[...])


def _post(x, ya, ga, gyb, wp, tm):
    n = x.shape[0]
    row_spec = pl.BlockSpec((tm, D_MODEL), lambda i: (i, 0))
    vec = _const_spec((1, D_MODEL))
    return pl.pallas_call(
        _post_kernel,
        grid=(n // tm,),
        in_specs=[row_spec] * 4 + [_const_spec((D_MODEL, D_MODEL)), vec, vec,
                                   _const_spec((D_MODEL, D_FF)), _const_spec((D_MODEL, D_FF)),
                                   _const_spec((D_FF, D_MODEL)), vec, vec],
        out_specs=row_spec,
        out_shape=jax.ShapeDtypeStruct((n, D_MODEL), F32),
        compiler_params=pltpu.CompilerParams(
            dimension_semantics=("arbitrary",), vmem_limit_bytes=VMEM_LIMIT),
        name="post",
    )(x, ya, ga, gyb, wp["w_o"], wp["ln1_g"], wp["ln1_b"], wp["w_gate"], wp["w_up"], wp["w_down"],
      wp["ln2_g"], wp["ln2_b"])


def _prep_weights(w_in, tmix_mu, w0, w2_decay, a0, a2_iclr, g2_gate, k_k, k_a, r_k, lnx_g, lnx_b,
                  conv_w, conv_b, lru_wa, lru_ba, lru_wi, lru_bi, lru_lambda, w_o,
                  ln1_g, ln1_b, w_ffn_gate, w_ffn_up, w_ffn_down, ln2_g, ln2_b):
    row = lambda v: v.reshape(1, -1).astype(F32)
    zeros = lambda r: jnp.zeros((r, D_MODEL), F32)
    w2ext = jnp.concatenate([
        jnp.concatenate([w2_decay, zeros(DECAY_LORA), zeros(DECAY_LORA)], axis=1),
        jnp.concatenate([zeros(ICLR_LORA), a2_iclr, zeros(ICLR_LORA)], axis=1),
        jnp.concatenate([zeros(GATE_LORA), zeros(GATE_LORA), g2_gate], axis=1)], axis=0)

    def gate_blockdiag(w):
        per = LRU_GROUP // LRU_BS
        w4 = w.reshape(LRU_BLOCKS // per, per, LRU_BS, LRU_BS)
        eye = jnp.eye(per, dtype=w.dtype)
        bd = jnp.einsum("gpcd,pq->gpcqd", w4, eye)
        return bd.reshape(LRU_BLOCKS // per, LRU_GROUP, LRU_GROUP).astype(BF16)

    return dict(
        w_in=w_in.astype(BF16),
        mu=row(tmix_mu), w2ext=w2ext.astype(BF16), w0=row(w0), a0=row(a0), k_k=row(k_k),
        k_a=row(k_a), r_k=row(r_k), lnx_g=row(lnx_g), lnx_b=row(lnx_b),
        conv_w=conv_w.astype(F32), conv_b=row(conv_b), wa_bd=gate_blockdiag(lru_wa),
        ba=row(lru_ba), wi_bd=gate_blockdiag(lru_wi), bi=row(lru_bi), lam=row(lru_lambda),
        w_o=w_o.astype(BF16), ln1_g=row(ln1_g), ln1_b=row(ln1_b), w_gate=w_ffn_gate.astype(BF16),
        w_up=w_ffn_up.astype(BF16), w_down=w_ffn_down.astype(BF16), ln2_g=row(ln2_g),
        ln2_b=row(ln2_b))


def _row_tiles(t):
    rwkv_tile = min(t, 8 * CHUNK)
    return rwkv_tile, min(t, 256), min(t, 256), min(t, 256)


def _prompt_layer(x, wp):
    b, t, _ = x.shape
    rwkv_tile, proj_tile, lru_tile, post_tile = _row_tiles(t)
    ya, s_heads = _rwkv_seq(x, wp, rwkv_tile, proj_tile)
    ga, gyb, conv_tail, h_tail = _lru_seq(x, wp, lru_tile)
    y = _post(x.reshape(b * t, D_MODEL), ya.reshape(b * t, D_MODEL), ga.reshape(b * t, D_MODEL),
              gyb.reshape(b * t, D_MODEL), wp, post_tile).reshape(b, t, D_MODEL)
    return (y, x[:, -1], s_heads, conv_tail[:, SUBLANES - (CONV_W - 1):],
            h_tail[:, SUBLANES - 1])


def _sample_layer(x, shift_buf, wkv0, conv_buf, h0, wp):
    b, t, _ = x.shape
    n = b * t
    x_tm = x.transpose(1, 0, 2).reshape(n, D_MODEL)
    xprev_tm = jnp.concatenate([shift_buf, x_tm[:n - b]], axis=0)
    acts = _rwkv_proj_step(x_tm, xprev_tm, wp)
    ya_tm, s_last = _wkv_step(acts, wkv0.transpose(1, 2, 3, 0), wp, t)
    conv_tm = conv_buf.transpose(1, 0, 2).reshape((CONV_W - 1) * b, D_LRU)
    ga_tm, gyb_tm, conv_new_tm, h_last = _lru_step(x_tm, conv_tm, h0, wp, t, b)
    y_tm = _post(x_tm, ya_tm, ga_tm, gyb_tm, wp, min(n, 256))
    y = y_tm.reshape(t, b, D_MODEL).transpose(1, 0, 2)
    conv_new = conv_new_tm.reshape(CONV_W - 1, b, D_LRU).transpose(1, 0, 2)
    return y, x[:, -1], s_last.transpose(3, 0, 1, 2), conv_new, h_last


def kernel(x_prompt, x_sample, state_shift, state_wkv, state_conv, state_lru, w_in, tmix_mu, w0, w2_decay, a0, a2_iclr, g2_gate, k_k, k_a, r_k, lnx_g, lnx_b, conv_w, conv_b, lru_wa, lru_ba, lru_wi, lru_bi, lru_lambda, w_o, ln1_g, ln1_b, w_ffn_gate, w_ffn_up, w_ffn_down, ln2_g, ln2_b):
    params = (w_in, tmix_mu, w0, w2_decay, a0, a2_iclr, g2_gate, k_k, k_a, r_k, lnx_g, lnx_b,
              conv_w, conv_b, lru_wa, lru_ba, lru_wi, lru_bi, lru_lambda, w_o,
              ln1_g, ln1_b, w_ffn_gate, w_ffn_up, w_ffn_down, ln2_g, ln2_b)
    wp = _prep_weights(*[p[0] for p in params])
    yp, sh_p, wkv_p, conv_p, lru_p = _prompt_layer(x_prompt, wp)
    ys, sh_s, wkv_s, conv_s, lru_s = _sample_layer(
        x_sample, state_shift[0], state_wkv[0], state_conv[0], state_lru[0], wp)
    return (yp, ys, sh_p[None], wkv_p[None], conv_p[None], lru_p[None],
            sh_s[None], wkv_s[None], conv_s[None], lru_s[None])
```

```python
import functools
import math

import jax
import jax.numpy as jnp
from jax import lax
from jax.experimental import pallas as pl
from jax.experimental.pallas import tpu as pltpu

F32 = jnp.float32
BF16 = jnp.bfloat16

D_MODEL = 1024
HEAD = 64
N_HEADS = D_MODEL // HEAD
PAIR = 2 * HEAD
N_PAIRS = N_HEADS // 2
DECAY_LORA = 64
ICLR_LORA = 64
GATE_LORA = 128
LORA = DECAY_LORA + ICLR_LORA + GATE_LORA
RWKV_COLS = 3 * D_MODEL + LORA
GN_EPS = HEAD * 1e-5
D_LRU = D_MODEL
LRU_BLOCKS = 16
LRU_BS = D_LRU // LRU_BLOCKS
LRU_GROUP = 256
CONV_W = 4
LRU_C = 8.0
LRU_COLS = 2 * D_LRU + 2 * D_MODEL
N_IN = RWKV_COLS + LRU_COLS
D_FF = 2816
ALPHA = 2.0 ** 0.25
LN_EPS = 1e-5

SUBLANES = 8
CHUNK = 64
CHUNKS_IN_FLIGHT = 4
STEP_ROWS_IN_FLIGHT = 8
VMEM_LIMIT = 56 * 1024 * 1024


def _softplus(x):
    return jnp.maximum(x, 0.0) + jnp.log1p(jnp.exp(-jnp.abs(x)))


def _sigmoid(x):
    return 0.5 * jnp.tanh(0.5 * x) + 0.5


def _gelu_tanh(x):
    c = math.sqrt(2.0 / math.pi)
    inner = x * (c + (c * 0.044715) * (x * x))
    return (0.5 * x) * (1.0 + jnp.tanh(inner))


def _layer_norm(x, g, b):
    mu = jnp.mean(x, axis=-1, keepdims=True)
    xc = x - mu
    var = jnp.mean(xc * xc, axis=-1, keepdims=True)
    return xc * lax.rsqrt(var + LN_EPS) * g + b


def _dot(a, b):
    return jnp.dot(a.astype(BF16), b.astype(BF16), preferred_element_type=F32)


def _dot_nt(a, b):
    return lax.dot_general(a.astype(BF16), b.astype(BF16), (((1,), (1,)), ((), ())),
                           preferred_element_type=F32)


def _dot_tn(a, b):
    return lax.dot_general(a.astype(BF16), b.astype(BF16), (((0,), (0,)), ((), ())),
                           preferred_element_type=F32)


def _seg_sum(x, lane_lo):
    s0 = jnp.sum(jnp.where(lane_lo, x, 0.0), axis=-1, keepdims=True)
    s1 = jnp.sum(jnp.where(lane_lo, 0.0, x), axis=-1, keepdims=True)
    return jnp.where(lane_lo, s0, s1)


def _head_sums(x):
    lane_lo = lax.broadcasted_iota(jnp.int32, (x.shape[0], PAIR), 1) < HEAD
    return jnp.concatenate([_seg_sum(x[:, p * PAIR:(p + 1) * PAIR], lane_lo)
                            for p in range(N_PAIRS)], axis=1)


def _lora_act(lo):
    lane = lax.broadcasted_iota(jnp.int32, lo.shape, 1)
    return jnp.where(lane < DECAY_LORA, jnp.tanh(lo),
                     jnp.where(lane < DECAY_LORA + ICLR_LORA, lo, _sigmoid(lo)))


def _rwkv_prep(lora_act, mixed, lora_out, w0, a0, k_k, k_a, head_sums):
    act = lora_act()
    r = mixed("r")
    lora_decay, lora_iclr, g = lora_out(act)
    k = mixed("k")
    lw = -math.exp(-0.5) * _sigmoid(w0 + lora_decay)
    a = _sigmoid(a0 + lora_iclr)
    v = mixed("v")
    kkraw = k * k_k
    kk = kkraw * lax.rsqrt(jnp.maximum(head_sums(kkraw * kkraw), 1e-24))
    k2 = k * (1.0 + (a - 1.0) * k_a)
    return r, lw, k2, v, -kk, kk * a, g


def _rwkv_project_rows(x, w_ref, mu_ref, w2_ref, w0_ref, a0_ref, kk_ref, ka_ref, hist_ref):
    xbf = x.astype(BF16)
    tm = xbf.shape[0]
    row = lax.broadcasted_iota(jnp.int32, (SUBLANES, 1), 0)

    def mixed_cols(first_col, n_cols):
        cols = slice(first_col, first_col + n_cols)
        z = jnp.dot(xbf, w_ref[:, cols], preferred_element_type=F32)
        rolled = pltpu.roll(z, 1, 0)
        first = jnp.where(row == 0, hist_ref[SUBLANES - 1:SUBLANES, cols], rolled[0:SUBLANES])
        zprev = jnp.concatenate([first, rolled[SUBLANES:]], axis=0)
        hist_ref[:, cols] = z[tm - SUBLANES:tm]
        return z + mu_ref[:, cols] * (zprev - z)

    def lora_out(act):
        lora = _dot(act, w2_ref[...])
        return tuple(lora[:, i * D_MODEL:(i + 1) * D_MODEL] for i in range(3))

    first_col = {"r": 0, "k": D_MODEL, "v": 2 * D_MODEL}
    return _rwkv_prep(lambda: _lora_act(mixed_cols(3 * D_MODEL, LORA)),
                      lambda group: mixed_cols(first_col[group], D_MODEL), lora_out,
                      w0_ref[...], a0_ref[...], kk_ref[...], ka_ref[...], _head_sums)


def _const_spec(shape):
    nd = len(shape)
    return pl.BlockSpec(shape, lambda *_: (0,) * nd, pipeline_mode=pl.Buffered(1))


def _lru_gate_logits(u, wa_ref, wi_ref):
    ub = u.astype(BF16)
    ra, ia = [], []
    for q in range(D_LRU // LRU_GROUP):
        uq = ub[:, q * LRU_GROUP:(q + 1) * LRU_GROUP]
        ra.append(jnp.dot(uq, wa_ref[q], preferred_element_type=F32))
        ia.append(jnp.dot(uq, wi_ref[q], preferred_element_type=F32))
    return jnp.concatenate(ra, axis=1), jnp.concatenate(ia, axis=1)


def _lru_coeffs(ra, ia, ba, bi, sp):
    ig = _sigmoid(ia + bi)
    half = (-0.5 * LRU_C) * sp
    log_a = half * jnp.tanh(0.5 * (ra + ba)) + half
    a = jnp.exp(log_a)
    m2 = -jnp.tanh(log_a) * (a * a + 1.0)
    mult = jnp.where(m2 > 0.0, m2 * lax.rsqrt(m2), 0.0)
    return a, mult, ig


def _lru_seq_kernel(x_ref, w_ref, cw_ref, cb_ref, wa_ref, ba_ref, wi_ref, bi_ref, lam_ref,
                    ga_ref, gyb_ref, conv_ref, hlast_ref,
                    xhist_ref, hc_ref, a_s, bx_s, h_s):
    t = pl.program_id(1)

    @pl.when(t == 0)
    def _():
        xhist_ref[...] = jnp.zeros_like(xhist_ref)
        hc_ref[...] = jnp.zeros_like(hc_ref)

    xbf = x_ref[0].astype(BF16)
    tm = xbf.shape[0]

    def proj(first_col, n_cols):
        cols = slice(RWKV_COLS + first_col, RWKV_COLS + first_col + n_cols)
        return jnp.dot(xbf, w_ref[:, cols], preferred_element_type=F32)

    xb = proj(0, D_LRU)
    gb = proj(D_LRU, D_LRU)
    row8 = lax.broadcasted_iota(jnp.int32, (SUBLANES, 1), 0)
    hist = xhist_ref[...]

    def shifted(k):
        rolled = pltpu.roll(xb, k, 0)
        first = jnp.where(row8 < k, pltpu.roll(hist, k, 0), rolled[0:SUBLANES])
        return jnp.concatenate([first, rolled[SUBLANES:]], axis=0)

    cw = cw_ref[...]
    u = (cb_ref[...] + cw[0:1] * shifted(3) + cw[1:2] * shifted(2) + cw[2:3] * shifted(1)
         + cw[3:4] * xb)
    last8 = xb[tm - SUBLANES:tm]
    xhist_ref[...] = last8
    conv_ref[0] = last8

    ra, ia = _lru_gate_logits(u, wa_ref, wi_ref)
    gelu_gb = _gelu_tanh(gb)
    zga = proj(2 * D_LRU, D_MODEL)
    sp = _softplus(-lam_ref[...])
    a, mult, ig = _lru_coeffs(ra, ia, ba_ref[...], bi_ref[...], sp)
    first_row = jnp.logical_and(row8 == 0, t == 0)
    mult = jnp.concatenate([jnp.where(first_row, 1.0, mult[0:SUBLANES]), mult[SUBLANES:]], axis=0)
    a_s[...] = a
    bx_s[...] = mult * ig * u
    ga_ref[0] = _sigmoid(zga)
    zgb = proj(2 * D_LRU + D_MODEL, D_MODEL)

    hc = hc_ref[0:1, :]
    for blk in range(tm // SUBLANES):
        rows = slice(blk * SUBLANES, (blk + 1) * SUBLANES)
        av = a_s[rows, :]
        bv = bx_s[rows, :]
        for d in (1, 2, 4):
            a_sh = jnp.where(row8 < d, 1.0, pltpu.roll(av, d, 0))
            b_sh = jnp.where(row8 < d, 0.0, pltpu.roll(bv, d, 0))
            bv = av * b_sh + bv
            av = av * a_sh
        h_blk = bv + av * hc
        h_s[rows, :] = h_blk
        hc = h_blk[SUBLANES - 1:SUBLANES, :]
    hc_ref[0:1, :] = hc
    h = h_s[...]
    hlast_ref[0] = h[tm - SUBLANES:tm]
    gyb_ref[0] = (_sigmoid(zgb) * gelu_gb) * h


def _lru_step_kernel(x_ref, conv0_ref, h0_ref, w_ref, cw_ref, cb_ref, wa_ref, ba_ref, wi_ref,
                     bi_ref, lam_ref, ga_ref, gyb_ref, conv_ref, hlast_ref, *, steps, nseq):
    z = _dot(x_ref[...], w_ref[:, RWKV_COLS:])
    xb = z[:, 0:D_LRU]
    gb = z[:, D_LRU:2 * D_LRU]
    zga = z[:, 2 * D_LRU:2 * D_LRU + D_MODEL]
    zgb = z[:, 2 * D_LRU + D_MODEL:]
    n = steps * nseq
    hist = (CONV_W - 1) * nseq
    xext = jnp.concatenate([conv0_ref[...], xb], axis=0)

    def shifted(k):
        return xext[hist - k * nseq:hist - k * nseq + n]

    cw = cw_ref[...]
    u = (cb_ref[...] + cw[0:1] * shifted(3) + cw[1:2] * shifted(2) + cw[2:3] * shifted(1)
         + cw[3:4] * xb)
    conv_ref[...] = xext[n:n + hist]
    sp = _softplus(-lam_ref[...])
    ra, ia = _lru_gate_logits(u, wa_ref, wi_ref)
    a, mult, ig = _lru_coeffs(ra, ia, ba_ref[...], bi_ref[...], sp)
    bx = mult * ig * u
    h = h0_ref[...]
    hs = []
    for s in range(steps):
        h = a[s * nseq:(s + 1) * nseq] * h + bx[s * nseq:(s + 1) * nseq]
        hs.append(h)
    hlast_ref[...] = h
    hall = jnp.concatenate(hs, axis=0)
    ga_ref[...] = _sigmoid(zga)
    gyb_ref[...] = _sigmoid(zgb) * (hall * _gelu_tanh(gb))


def _lru_weight_specs():
    vec = _const_spec((1, D_LRU))
    gate_w = _const_spec((D_LRU // LRU_GROUP, LRU_GROUP, LRU_GROUP))
    return [_const_spec((D_MODEL, N_IN)), _const_spec((CONV_W, D_LRU)), vec,
            gate_w, vec, gate_w, vec, vec]


def _lru_weights(wp):
    return (wp["w_in"], wp["conv_w"], wp["conv_b"], wp["wa_bd"], wp["ba"], wp["wi_bd"],
            wp["bi"], wp["lam"])


def _lru_seq(x, wp, tm):
    b, t, _ = x.shape
    row_spec = pl.BlockSpec((1, tm, D_MODEL), lambda i, j: (i, j, 0))
    tail_spec = pl.BlockSpec((1, SUBLANES, D_LRU), lambda i, j: (i, 0, 0))
    return pl.pallas_call(
        _lru_seq_kernel,
        grid=(b, t // tm),
        in_specs=[row_spec] + _lru_weight_specs(),
        out_specs=[row_spec, row_spec, tail_spec, tail_spec],
        out_shape=[jax.ShapeDtypeStruct((b, t, D_MODEL), F32)] * 2
        + [jax.ShapeDtypeStruct((b, SUBLANES, D_LRU), F32)] * 2,
        scratch_shapes=[pltpu.VMEM((SUBLANES, D_LRU), F32), pltpu.VMEM((SUBLANES, D_LRU), F32),
                        pltpu.VMEM((tm, D_LRU), F32), pltpu.VMEM((tm, D_LRU), F32),
                        pltpu.VMEM((tm, D_LRU), F32)],
        compiler_params=pltpu.CompilerParams(
            dimension_semantics=("arbitrary", "arbitrary"), vmem_limit_bytes=VMEM_LIMIT),
        name="lru_seq",
    )(x, *_lru_weights(wp))


def _lru_step(x_tm, conv_tm, h0, wp, steps, nseq):
    n = steps * nseq
    hist = (CONV_W - 1) * nseq

    def full(r):
        return pl.BlockSpec((r, D_MODEL), lambda i: (0, 0))

    return pl.pallas_call(
        functools.partial(_lru_step_kernel, steps=steps, nseq=nseq),
        grid=(1,),
        in_specs=[full(n), full(hist), full(nseq)] + _lru_weight_specs(),
        out_specs=[full(n), full(n), full(hist), full(nseq)],
        out_shape=[jax.ShapeDtypeStruct((n, D_MODEL), F32)] * 2
        + [jax.ShapeDtypeStruct((hist, D_LRU), F32), jax.ShapeDtypeStruct((nseq, D_LRU), F32)],
        compiler_params=pltpu.CompilerParams(
            dimension_semantics=("arbitrary",), vmem_limit_bytes=VMEM_LIMIT),
        name="lru_step",
    )(x_tm, conv_tm, h0, *_lru_weights(wp))


def _wkv_consts():
    c = CHUNK
    lane = lax.broadcasted_iota(jnp.int32, (c, PAIR), 1)
    row = lax.broadcasted_iota(jnp.int32, (c, PAIR), 0)
    lane_lo = lane < HEAD
    col = lane % c
    strict = col < row
    incl = col <= row
    eye = jnp.where(col == row, 1.0, 0.0).astype(F32)
    r3 = lax.broadcasted_iota(jnp.int32, (c, 3 * c), 0)
    c3 = lax.broadcasted_iota(jnp.int32, (c, 3 * c), 1) % c
    tri3 = jnp.where(c3 <= r3, 1.0, 0.0).astype(BF16)
    rp = lax.broadcasted_iota(jnp.int32, (PAIR, PAIR), 0)
    cp = lax.broadcasted_iota(jnp.int32, (PAIR, PAIR), 1)
    blockdiag = (rp // HEAD) == (cp // HEAD)
    return dict(lane_lo=lane_lo, tri3=tri3, strict=strict, incl=incl, eye=eye,
                blockdiag=blockdiag)


def _split3(x):
    hi = x.astype(BF16)
    rest = x - hi.astype(F32)
    mid = rest.astype(BF16)
    lo = (rest - mid.astype(F32)).astype(BF16)
    return jnp.concatenate([hi, mid, lo], axis=0)


def _split_heads(x, lane_lo):
    zero = jnp.zeros_like(x)
    return jnp.concatenate([jnp.where(lane_lo, x, zero), jnp.where(lane_lo, zero, x)], axis=0)


def _run_interleaved(stage_gens):
    results = [None] * len(stage_gens)
    live = list(range(len(stage_gens)))
    while live:
        still = []
        for idx in live:
            try:
                next(stage_gens[idx])
                still.append(idx)
            except StopIteration as done:
                results[idx] = done.value
        live = still
    return results


def _wkv_chunk(r, lw, k, v, al, be, get_state, cn):
    c = CHUNK
    lane_lo = cn["lane_lo"]
    cum = jnp.dot(cn["tri3"], _split3(lw), preferred_element_type=F32)
    end = cum[c - 1:c, :]
    yield
    e_cum = jnp.exp(cum)
    e_neg = jnp.exp(-cum)
    e_end = jnp.exp(end)
    rt = r * e_cum
    at = al * jnp.exp(cum - lw)
    bt = be * e_neg
    kt = k * e_neg
    lhs2 = jnp.concatenate([at, rt], axis=0).astype(BF16)
    keys = jnp.concatenate([_split_heads(bt, lane_lo), _split_heads(kt, lane_lo)], axis=0)
    gram = _dot_nt(lhs2, keys)
    yield
    a_ab = jnp.where(cn["strict"], gram[0:c, 0:PAIR], 0.0)
    a_ak = jnp.where(cn["strict"], gram[0:c, PAIR:], 0.0)
    a_rb = jnp.where(cn["incl"], gram[c:, 0:PAIR], 0.0)
    a_rk = jnp.where(cn["incl"], gram[c:, PAIR:], 0.0)
    inv = cn["eye"] + a_ab
    n_iter = int(math.log2(c)) - 1
    pw = _dot(a_ab, _split_heads(a_ab, lane_lo))
    yield
    for it in range(n_iter):
        pw_heads = _split_heads(pw, lane_lo)
        if it < n_iter - 1:
            both = _dot(jnp.concatenate([inv, pw], axis=0), pw_heads)
            inv = inv + both[0:c]
            pw = both[c:]
        else:
            inv = inv + _dot(inv, pw_heads)
        yield
    av = _dot(jnp.concatenate([a_ak, a_rk], axis=0), _split_heads(v, lane_lo))
    state = get_state()
    while state is None:
        yield
        state = get_state()
    from_state = _dot_nt(lhs2, state)
    yield
    u = _dot(inv, _split_heads(from_state[0:c] + av[0:c], lane_lo))
    yield
    y = from_state[c:] + av[c:] + _dot(a_rb, _split_heads(u, lane_lo))
    yield
    uv = jnp.concatenate([u, v], axis=0)
    bk = jnp.concatenate([bt * e_end, kt * e_end], axis=0)
    return y, state * e_end + jnp.where(cn["blockdiag"], _dot_tn(uv, bk), 0.0)


def _wkv_readout(y, r, k, v, g, rk, lnx_g, lnx_b, lane_lo):
    mean = _seg_sum(y, lane_lo) * (1.0 / HEAD)
    bonus = _seg_sum(r * k * rk, lane_lo) * v
    yield
    yc = y - mean
    var = _seg_sum(yc * yc, lane_lo) * (1.0 / HEAD)
    yield
    yn = yc * lax.rsqrt(var + GN_EPS) * lnx_g + lnx_b
    return (yn + bonus) * g


def _from_blockdiag(state):
    return state[0:HEAD, 0:HEAD], pltpu.roll(state[HEAD:], HEAD, 1)[:, 0:HEAD]


def _rwkv_seq_kernel(x_ref, w_ref, mu_ref, w2_ref, w0_ref, a0_ref, kk_ref, ka_ref, rk_ref, lg_ref,
                     lb_ref, y_ref, s_out_ref, hist_ref, acts_scr, s_scr, yraw_scr, *, tm):
    t = pl.program_id(1)
    cn = _wkv_consts()
    n_pairs = s_scr.shape[0]
    tt = x_ref.shape[1]

    @pl.when(t == 0)
    def _():
        hist_ref[...] = jnp.zeros_like(hist_ref)
        s_scr[...] = jnp.zeros_like(s_scr)
        yraw_scr[...] = jnp.zeros_like(yraw_scr)

    for sub in range(tt // tm):
        rows = slice(sub * tm, (sub + 1) * tm)
        outs = _rwkv_project_rows(x_ref[0, rows, :], w_ref, mu_ref, w2_ref, w0_ref, a0_ref,
                                  kk_ref, ka_ref, hist_ref)
        for idx, o in enumerate(outs):
            acts_scr[idx, rows, :] = o

    r_ref, lw_ref, k_ref, v_ref, al_ref, be_ref, g_ref = (acts_scr.at[i] for i in range(7))
    n_chunks = tt // CHUNK

    def readout_stages(p, rows):
        ln = pl.ds(p * PAIR, PAIR)
        out = yield from _wkv_readout(
            yraw_scr[:, ln], r_ref[rows, ln], k_ref[rows, ln], v_ref[rows, ln],
            g_ref[rows, ln], rk_ref[:, ln], lg_ref[:, ln], lb_ref[:, ln], cn["lane_lo"])
        y_ref[0, rows, ln] = out

    def chunk_stages(p, rows, get_state):
        ln = pl.ds(p * PAIR, PAIR)
        return _wkv_chunk(r_ref[rows, ln], lw_ref[rows, ln], k_ref[rows, ln],
                          v_ref[rows, ln], al_ref[rows, ln], be_ref[rows, ln],
                          get_state, cn)

    def body(i, states):
        def rows_of(j):
            return pl.ds(pl.multiple_of((CHUNKS_IN_FLIGHT * i + j) * CHUNK, CHUNK), CHUNK)

        prev = pl.ds(pl.multiple_of(jnp.maximum(CHUNKS_IN_FLIGHT * i - 1, 0) * CHUNK, CHUNK),
                     CHUNK)
        end_states = [[None] * n_pairs for _ in range(CHUNKS_IN_FLIGHT)]

        def start_state(j, p):
            return states[p] if j == 0 else end_states[j - 1][p]

        def stages(j, p):
            ln = pl.ds(p * PAIR, PAIR)
            rows = rows_of(j)
            y, state = yield from chunk_stages(p, rows, functools.partial(start_state, j, p))
            end_states[j][p] = state
            if j == CHUNKS_IN_FLIGHT - 1:
                yraw_scr[:, ln] = y
                return
            out = yield from _wkv_readout(
                y, r_ref[rows, ln], k_ref[rows, ln], v_ref[rows, ln],
                g_ref[rows, ln], rk_ref[:, ln], lg_ref[:, ln], lb_ref[:, ln], cn["lane_lo"])
            y_ref[0, rows, ln] = out

        _run_interleaved([readout_stages(p, prev) for p in range(n_pairs)]
                         + [stages(j, p) for j in range(CHUNKS_IN_FLIGHT) for p in range(n_pairs)])
        return tuple(end_states[CHUNKS_IN_FLIGHT - 1])

    assert n_chunks % CHUNKS_IN_FLIGHT == 0
    states = lax.fori_loop(0, n_chunks // CHUNKS_IN_FLIGHT, body,
                           tuple(s_scr[p] for p in range(n_pairs)))
    last = pl.ds((n_chunks - 1) * CHUNK, CHUNK)
    _run_interleaved([readout_stages(p, last) for p in range(n_pairs)])
    for p in range(n_pairs):
        s_scr[p] = states[p]
        s_out_ref[0, 2 * p], s_out_ref[0, 2 * p + 1] = _from_blockdiag(states[p])


def _rwkv_step_kernel(x_ref, xp_ref, wlo_ref, wr_ref, wk_ref, wv_ref, mulo_ref, mur_ref, muk_ref,
                      muv_ref, w2d_ref, w2a_ref, w2g_ref, w0_ref, a0_ref, kk_ref, ka_ref,
                      rk_ref, lg_ref, lb_ref, s_in_ref, y_ref, s_out_ref,
                      xb_scr, act_scr, tr_scr, y_scr, *, steps):
    nseq = s_in_ref.shape[3]

    def mixed(w_ref, mu_ref):
        z = jnp.dot(xb_scr[0], w_ref[...], preferred_element_type=F32)
        zprev = jnp.dot(xb_scr[1], w_ref[...], preferred_element_type=F32)
        return z + mu_ref[...] * (zprev - z)

    @pl.when(pl.program_id(0) == 0)
    def _():
        xb_scr[0] = x_ref[...].astype(BF16)
        xb_scr[1] = xp_ref[...].astype(BF16)
        act_scr[...] = _lora_act(mixed(wlo_ref, mulo_ref)).astype(BF16)

    group_refs = {"r": (wr_ref, mur_ref), "k": (wk_ref, muk_ref), "v": (wv_ref, muv_ref)}
    lane_lo = lax.broadcasted_iota(jnp.int32, (x_ref.shape[0], PAIR), 1) < HEAD
    acts = _rwkv_prep(
        lambda: act_scr[...], lambda group: mixed(*group_refs[group]),
        lambda act: tuple(jnp.dot(act, w2[...], preferred_element_type=F32)
                          for w2 in (w2d_ref, w2a_ref, w2g_ref)),
        w0_ref[...], a0_ref[...], kk_ref[...], ka_ref[...],
        functools.partial(_seg_sum, lane_lo=lane_lo))
    idx_r, idx_dec, idx_k, idx_v, idx_al, idx_be, idx_g = range(7)
    for idx, act_rows in enumerate(acts):
        for t in range(steps):
            tile = act_rows[t * nseq:(t + 1) * nseq, :].T
            tr_scr[idx, t] = jnp.exp(tile) if idx == idx_dec else tile

    for h in range(2):
        ch = slice(h * HEAD, (h + 1) * HEAD)

        def row_stages(i, h=h, ch=ch):
            s = s_in_ref[h, i]
            for t in range(steps):
                sa = jnp.sum(s * tr_scr[idx_al, t, ch, :], axis=0, keepdims=True)
                yield
                v_i = tr_scr[idx_v, t, pl.ds(h * HEAD + i, 1), :]
                s = (s * tr_scr[idx_dec, t, ch, :] + sa * tr_scr[idx_be, t, ch, :]
                     + v_i * tr_scr[idx_k, t, ch, :])
                y_scr[t, pl.ds(h * HEAD + i, 1), :] = jnp.sum(
                    s * tr_scr[idx_r, t, ch, :], axis=0, keepdims=True)
                yield
            s_out_ref[h, i] = s

        @pl.loop(0, HEAD // STEP_ROWS_IN_FLIGHT)
        def _(blk):
            _run_interleaved([row_stages(blk * STEP_ROWS_IN_FLIGHT + ii)
                              for ii in range(STEP_ROWS_IN_FLIGHT)])

    for t in range(steps):
        outs = []
        for h in range(2):
            ch = slice(h * HEAD, (h + 1) * HEAD)
            y = y_scr[t, ch, :]
            yc = y - jnp.mean(y, axis=0, keepdims=True)
            var = jnp.mean(yc * yc, axis=0, keepdims=True)
            yn = yc * lax.rsqrt(var + GN_EPS) * lg_ref[ch, :] + lb_ref[ch, :]
            rk_sum = jnp.sum(tr_scr[idx_r, t, ch, :] * tr_scr[idx_k, t, ch, :] * rk_ref[ch, :],
                             axis=0, keepdims=True)
            outs.append((yn + rk_sum * tr_scr[idx_v, t, ch, :]) * tr_scr[idx_g, t, ch, :])
        y_ref[t * nseq:(t + 1) * nseq, :] = jnp.concatenate(outs, axis=0).T


def _rwkv_seq(x, wp, tt, tm):
    b, t, _ = x.shape
    row_spec = pl.BlockSpec((1, tt, D_MODEL), lambda i, j: (i, j, 0))
    vec = _const_spec((1, D_MODEL))
    return pl.pallas_call(
        functools.partial(_rwkv_seq_kernel, tm=tm),
        grid=(b, t // tt),
        in_specs=[row_spec, _const_spec((D_MODEL, RWKV_COLS)), _const_spec((1, RWKV_COLS)),
                  _const_spec((LORA, 3 * D_MODEL))] + [vec] * 7,
        out_specs=[row_spec,
                   pl.BlockSpec((1, N_HEADS, HEAD, HEAD), lambda i, j: (i, 0, 0, 0))],
        out_shape=[jax.ShapeDtypeStruct((b, t, D_MODEL), F32),
                   jax.ShapeDtypeStruct((b, N_HEADS, HEAD, HEAD), F32)],
        scratch_shapes=[pltpu.VMEM((SUBLANES, RWKV_COLS), F32),
                        pltpu.VMEM((7, tt, D_MODEL), F32),
                        pltpu.VMEM((N_PAIRS, PAIR, PAIR), F32),
                        pltpu.VMEM((CHUNK, D_MODEL), F32)],
        compiler_params=pltpu.CompilerParams(
            dimension_semantics=("arbitrary", "arbitrary"), vmem_limit_bytes=VMEM_LIMIT),
        name="rwkv_seq",
    )(x, wp["w_in"], wp["mu"], wp["w2ext"], wp["w0"], wp["a0"], wp["k_k"], wp["k_a"],
      wp["r_k"], wp["lnx_g"], wp["lnx_b"])


def _rwkv_step(x_tm, xprev_tm, s_last, wp, steps):
    n = x_tm.shape[0]
    nseq = s_last.shape[3]

    def pair_cols(rows, group):
        return pl.BlockSpec((rows, PAIR), lambda p: (0, group * N_PAIRS + p))

    def lora_cols(rows):
        return pl.BlockSpec((rows, LORA), lambda p: (0, 3 * D_MODEL // LORA))

    full = pl.BlockSpec((n, D_MODEL), lambda p: (0, 0), pipeline_mode=pl.Buffered(1))
    col_spec = pl.BlockSpec((PAIR, nseq), lambda p: (p, 0))
    st_spec = pl.BlockSpec((2, HEAD, HEAD, nseq), lambda p: (p, 0, 0, 0))
    cols = [jnp.broadcast_to(wp[name].reshape(D_MODEL, 1), (D_MODEL, nseq))
            for name in ("r_k", "lnx_g", "lnx_b")]
    return pl.pallas_call(
        functools.partial(_rwkv_step_kernel, steps=steps),
        grid=(N_PAIRS,),
        in_specs=[full, full]
        + [lora_cols(D_MODEL)] + [pair_cols(D_MODEL, g) for g in range(3)]
        + [lora_cols(1)] + [pair_cols(1, g) for g in range(3)]
        + [pair_cols(LORA, g) for g in range(3)]
        + [pair_cols(1, 0)] * 4 + [col_spec] * 3 + [st_spec],
        out_specs=[pl.BlockSpec((n, PAIR), lambda p: (0, p)), st_spec],
        out_shape=[jax.ShapeDtypeStruct((n, D_MODEL), F32),
                   jax.ShapeDtypeStruct(s_last.shape, F32)],
        scratch_shapes=[pltpu.VMEM((2, n, D_MODEL), BF16), pltpu.VMEM((n, LORA), BF16),
                        pltpu.VMEM((7, steps, PAIR, nseq), F32),
                        pltpu.VMEM((steps, PAIR, nseq), F32)],
        compiler_params=pltpu.CompilerParams(
            dimension_semantics=("arbitrary",), vmem_limit_bytes=VMEM_LIMIT),
        name="rwkv_step",
    )(x_tm, xprev_tm, *[wp["w_in"]] * 4, *[wp["mu"]] * 4, *[wp["w2ext"]] * 3,
      wp["w0"], wp["a0"], wp["k_k"], wp["k_a"], *cols, s_last)


def _post_kernel(x_ref, ya_ref, ga_ref, gyb_ref, wo_ref, l1g_ref, l1b_ref, wg_ref, wu_ref, wd_ref,
                 l2g_ref, l2b_ref, y_ref):
    x = x_ref[...]
    merged = ga_ref[...] * ya_ref[...] + gyb_ref[...]
    mix = _dot(merged, wo_ref[...])
    h1 = _layer_norm(ALPHA * x + mix, l1g_ref[...], l1b_ref[...])
    h1b = h1.astype(BF16)
    gate = jnp.dot(h1b, wg_ref[...], preferred_element_type=F32)
    up = jnp.dot(h1b, wu_ref[...], preferred_element_type=F32)
    act = (gate * _sigmoid(gate)) * up
    ffn = _dot(act, wd_ref[...])
    y_ref[...] = _layer_norm(ALPHA * h1 + ffn, l2g_ref[...], l2b_ref[...])


def _post(x, ya, ga, gyb, wp, tm):
    n = x.shape[0]
    row_spec = pl.BlockSpec((tm, D_MODEL), lambda i: (i, 0))
    vec = _const_spec((1, D_MODEL))
    return pl.pallas_call(
        _post_kernel,
        grid=(n // tm,),
        in_specs=[row_spec] * 4 + [_const_spec((D_MODEL, D_MODEL)), vec, vec,
                                   _const_spec((D_MODEL, D_FF)), _const_spec((D_MODEL, D_FF)),
                                   _const_spec((D_FF, D_MODEL)), vec, vec],
        out_specs=row_spec,
        out_shape=jax.ShapeDtypeStruct((n, D_MODEL), F32),
        compiler_params=pltpu.CompilerParams(
            dimension_semantics=("arbitrary",), vmem_limit_bytes=VMEM_LIMIT),
        name="post",
    )(x, ya, ga, gyb, wp["w_o"], wp["ln1_g"], wp["ln1_b"], wp["w_gate"], wp["w_up"], wp["w_down"],
      wp["ln2_g"], wp["ln2_b"])


def _prep_weights(w_in, tmix_mu, w0, w2_decay, a0, a2_iclr, g2_gate, k_k, k_a, r_k, lnx_g, lnx_b,
                  conv_w, conv_b, lru_wa, lru_ba, lru_wi, lru_bi, lru_lambda, w_o,
                  ln1_g, ln1_b, w_ffn_gate, w_ffn_up, w_ffn_down, ln2_g, ln2_b):
    row = lambda v: v.reshape(1, -1).astype(F32)
    zeros = lambda r: jnp.zeros((r, D_MODEL), F32)
    w2ext = jnp.concatenate([
        jnp.concatenate([w2_decay, zeros(DECAY_LORA), zeros(DECAY_LORA)], axis=1),
        jnp.concatenate([zeros(ICLR_LORA), a2_iclr, zeros(ICLR_LORA)], axis=1),
        jnp.concatenate([zeros(GATE_LORA), zeros(GATE_LORA), g2_gate], axis=1)], axis=0)

    def gate_blockdiag(w):
        per = LRU_GROUP // LRU_BS
        w4 = w.reshape(LRU_BLOCKS // per, per, LRU_BS, LRU_BS)
        eye = jnp.eye(per, dtype=w.dtype)
        bd = jnp.einsum("gpcd,pq->gpcqd", w4, eye)
        return bd.reshape(LRU_BLOCKS // per, LRU_GROUP, LRU_GROUP).astype(BF16)

    return dict(
        w_in=w_in.astype(BF16),
        mu=row(tmix_mu), w2ext=w2ext.astype(BF16), w0=row(w0), a0=row(a0), k_k=row(k_k),
        k_a=row(k_a), r_k=row(r_k), lnx_g=row(lnx_g), lnx_b=row(lnx_b),
        conv_w=conv_w.astype(F32), conv_b=row(conv_b), wa_bd=gate_blockdiag(lru_wa),
        ba=row(lru_ba), wi_bd=gate_blockdiag(lru_wi), bi=row(lru_bi), lam=row(lru_lambda),
        w_o=w_o.astype(BF16), ln1_g=row(ln1_g), ln1_b=row(ln1_b), w_gate=w_ffn_gate.astype(BF16),
        w_up=w_ffn_up.astype(BF16), w_down=w_ffn_down.astype(BF16), ln2_g=row(ln2_g),
        ln2_b=row(ln2_b))


def _row_tiles(t):
    rwkv_tile = min(t, 8 * CHUNK)
    return rwkv_tile, min(t, 256), min(t, 256), min(t, 256)


def _prompt_layer(x, wp):
    b, t, _ = x.shape
    rwkv_tile, proj_tile, lru_tile, post_tile = _row_tiles(t)
    ya, s_heads = _rwkv_seq(x, wp, rwkv_tile, proj_tile)
    ga, gyb, conv_tail, h_tail = _lru_seq(x, wp, lru_tile)
    y = _post(x.reshape(b * t, D_MODEL), ya.reshape(b * t, D_MODEL), ga.reshape(b * t, D_MODEL),
              gyb.reshape(b * t, D_MODEL), wp, post_tile).reshape(b, t, D_MODEL)
    return (y, x[:, -1], s_heads, conv_tail[:, SUBLANES - (CONV_W - 1):],
            h_tail[:, SUBLANES - 1])


def _sample_layer(x, shift_buf, wkv0, conv_buf, h0, wp):
    b, t, _ = x.shape
    n = b * t
    x_tm = x.transpose(1, 0, 2).reshape(n, D_MODEL)
    xprev_tm = jnp.concatenate([shift_buf, x_tm[:n - b]], axis=0)
    ya_tm, s_last = _rwkv_step(x_tm, xprev_tm, wkv0.transpose(1, 2, 3, 0), wp, t)
    conv_tm = conv_buf.transpose(1, 0, 2).reshape((CONV_W - 1) * b, D_LRU)
    ga_tm, gyb_tm, conv_new_tm, h_last = _lru_step(x_tm, conv_tm, h0, wp, t, b)
    y_tm = _post(x_tm, ya_tm, ga_tm, gyb_tm, wp, min(n, 256))
    y = y_tm.reshape(t, b, D_MODEL).transpose(1, 0, 2)
    conv_new = conv_new_tm.reshape(CONV_W - 1, b, D_LRU).transpose(1, 0, 2)
    return y, x[:, -1], s_last.transpose(3, 0, 1, 2), conv_new, h_last


def kernel(x_prompt, x_sample, state_shift, state_wkv, state_conv, state_lru, w_in, tmix_mu, w0, w2_decay, a0, a2_iclr, g2_gate, k_k, k_a, r_k, lnx_g, lnx_b, conv_w, conv_b, lru_wa, lru_ba, lru_wi, lru_bi, lru_lambda, w_o, ln1_g, ln1_b, w_ffn_gate, w_ffn_up, w_ffn_down, ln2_g, ln2_b):
    params = (w_in, tmix_mu, w0, w2_decay, a0, a2_iclr, g2_gate, k_k, k_a, r_k, lnx_g, lnx_b,
              conv_w, conv_b, lru_wa, lru_ba, lru_wi, lru_bi, lru_lambda, w_o,
              ln1_g, ln1_b, w_ffn_gate, w_ffn_up, w_ffn_down, ln2_g, ln2_b)
    wp = _prep_weights(*[p[0] for p in params])
    yp, sh_p, wkv_p, conv_p, lru_p = _prompt_layer(x_prompt, wp)
    ys, sh_s, wkv_s, conv_s, lru_s = _sample_layer(
        x_sample, state_shift[0], state_wkv[0], state_conv[0], state_lru[0], wp)
    return (yp, ys, sh_p[None], wkv_p[None], conv_p[None], lru_p[None],
            sh_s[None], wkv_s[None], conv_s[None], lru_s[None])
```

```python
import functools
import math

import jax
import jax.numpy as jnp
from jax import lax
from jax.experimental import pallas as pl
from jax.experimental.pallas import tpu as pltpu

F32 = jnp.float32
BF16 = jnp.bfloat16

D_MODEL = 1024
HEAD = 64
N_HEADS = D_MODEL // HEAD
PAIR = 2 * HEAD
N_PAIRS = N_HEADS // 2
DECAY_LORA = 64
ICLR_LORA = 64
GATE_LORA = 128
LORA = DECAY_LORA + ICLR_LORA + GATE_LORA
RWKV_COLS = 3 * D_MODEL + LORA
GN_EPS = HEAD * 1e-5
D_LRU = D_MODEL
LRU_BLOCKS = 16
LRU_BS = D_LRU // LRU_BLOCKS
LRU_GROUP = 256
CONV_W = 4
LRU_C = 8.0
LRU_COLS = 2 * D_LRU + 2 * D_MODEL
N_IN = RWKV_COLS + LRU_COLS
D_FF = 2816
ALPHA = 2.0 ** 0.25
LN_EPS = 1e-5

SUBLANES = 8
CHUNK = 64
CHUNKS_IN_FLIGHT = 4
STEP_ROWS_IN_FLIGHT = 8
STEP_PAIRS = 2
VMEM_LIMIT = 56 * 1024 * 1024


def _softplus(x):
    return jnp.maximum(x, 0.0) + jnp.log1p(jnp.exp(-jnp.abs(x)))


def _sigmoid(x):
    return 0.5 * jnp.tanh(0.5 * x) + 0.5


def _gelu_tanh(x):
    c = math.sqrt(2.0 / math.pi)
    inner = x * (c + (c * 0.044715) * (x * x))
    return (0.5 * x) * (1.0 + jnp.tanh(inner))


def _layer_norm(x, g, b):
    mu = jnp.mean(x, axis=-1, keepdims=True)
    xc = x - mu
    var = jnp.mean(xc * xc, axis=-1, keepdims=True)
    return xc * lax.rsqrt(var + LN_EPS) * g + b


def _dot(a, b):
    return jnp.dot(a.astype(BF16), b.astype(BF16), preferred_element_type=F32)


def _dot_nt(a, b):
    return lax.dot_general(a.astype(BF16), b.astype(BF16), (((1,), (1,)), ((), ())),
                           preferred_element_type=F32)


def _dot_tn(a, b):
    return lax.dot_general(a.astype(BF16), b.astype(BF16), (((0,), (0,)), ((), ())),
                           preferred_element_type=F32)


def _seg_sum(x, lane_lo):
    s0 = jnp.sum(jnp.where(lane_lo, x, 0.0), axis=-1, keepdims=True)
    s1 = jnp.sum(jnp.where(lane_lo, 0.0, x), axis=-1, keepdims=True)
    return jnp.where(lane_lo, s0, s1)


def _head_sums(x):
    lane_lo = lax.broadcasted_iota(jnp.int32, (x.shape[0], PAIR), 1) < HEAD
    return jnp.concatenate([_seg_sum(x[:, p * PAIR:(p + 1) * PAIR], lane_lo)
                            for p in range(N_PAIRS)], axis=1)


def _lora_act(lo):
    lane = lax.broadcasted_iota(jnp.int32, lo.shape, 1)
    return jnp.where(lane < DECAY_LORA, jnp.tanh(lo),
                     jnp.where(lane < DECAY_LORA + ICLR_LORA, lo, _sigmoid(lo)))


def _rwkv_prep(lora_act, mixed, lora_out, w0, a0, k_k, k_a, head_sums):
    act = lora_act()
    r = mixed("r")
    lora_decay, lora_iclr, g = lora_out(act)
    k = mixed("k")
    lw = -math.exp(-0.5) * _sigmoid(w0 + lora_decay)
    a = _sigmoid(a0 + lora_iclr)
    v = mixed("v")
    kkraw = k * k_k
    kk = kkraw * lax.rsqrt(jnp.maximum(head_sums(kkraw * kkraw), 1e-24))
    k2 = k * (1.0 + (a - 1.0) * k_a)
    return r, lw, k2, v, -kk, kk * a, g


def _rwkv_project_rows(x, w_ref, mu_ref, w2_ref, w0_ref, a0_ref, kk_ref, ka_ref, hist_ref):
    xbf = x.astype(BF16)
    tm = xbf.shape[0]
    row = lax.broadcasted_iota(jnp.int32, (SUBLANES, 1), 0)

    def mixed_cols(first_col, n_cols):
        cols = slice(first_col, first_col + n_cols)
        z = jnp.dot(xbf, w_ref[:, cols], preferred_element_type=F32)
        rolled = pltpu.roll(z, 1, 0)
        first = jnp.where(row == 0, hist_ref[SUBLANES - 1:SUBLANES, cols], rolled[0:SUBLANES])
        zprev = jnp.concatenate([first, rolled[SUBLANES:]], axis=0)
        hist_ref[:, cols] = z[tm - SUBLANES:tm]
        return z + mu_ref[:, cols] * (zprev - z)

    def lora_out(act):
        lora = _dot(act, w2_ref[...])
        return tuple(lora[:, i * D_MODEL:(i + 1) * D_MODEL] for i in range(3))

    first_col = {"r": 0, "k": D_MODEL, "v": 2 * D_MODEL}
    return _rwkv_prep(lambda: _lora_act(mixed_cols(3 * D_MODEL, LORA)),
                      lambda group: mixed_cols(first_col[group], D_MODEL), lora_out,
                      w0_ref[...], a0_ref[...], kk_ref[...], ka_ref[...], _head_sums)


def _const_spec(shape):
    nd = len(shape)
    return pl.BlockSpec(shape, lambda *_: (0,) * nd, pipeline_mode=pl.Buffered(1))


def _lru_gate_logits(u, wa_ref, wi_ref):
    ub = u.astype(BF16)
    ra, ia = [], []
    for q in range(D_LRU // LRU_GROUP):
        uq = ub[:, q * LRU_GROUP:(q + 1) * LRU_GROUP]
        ra.append(jnp.dot(uq, wa_ref[q], preferred_element_type=F32))
        ia.append(jnp.dot(uq, wi_ref[q], preferred_element_type=F32))
    return jnp.concatenate(ra, axis=1), jnp.concatenate(ia, axis=1)


def _lru_coeffs(ra, ia, ba, bi, sp):
    ig = _sigmoid(ia + bi)
    half = (-0.5 * LRU_C) * sp
    log_a = half * jnp.tanh(0.5 * (ra + ba)) + half
    a = jnp.exp(log_a)
    m2 = -jnp.tanh(log_a) * (a * a + 1.0)
    mult = jnp.where(m2 > 0.0, m2 * lax.rsqrt(m2), 0.0)
    return a, mult, ig


def _lru_seq_kernel(x_ref, w_ref, cw_ref, cb_ref, wa_ref, ba_ref, wi_ref, bi_ref, lam_ref,
                    ga_ref, gyb_ref, conv_ref, hlast_ref,
                    xhist_ref, hc_ref, a_s, bx_s, h_s):
    t = pl.program_id(1)

    @pl.when(t == 0)
    def _():
        xhist_ref[...] = jnp.zeros_like(xhist_ref)
        hc_ref[...] = jnp.zeros_like(hc_ref)

    xbf = x_ref[0].astype(BF16)
    tm = xbf.shape[0]

    def proj(first_col, n_cols):
        cols = slice(RWKV_COLS + first_col, RWKV_COLS + first_col + n_cols)
        return jnp.dot(xbf, w_ref[:, cols], preferred_element_type=F32)

    xb = proj(0, D_LRU)
    gb = proj(D_LRU, D_LRU)
    row8 = lax.broadcasted_iota(jnp.int32, (SUBLANES, 1), 0)
    hist = xhist_ref[...]

    def shifted(k):
        rolled = pltpu.roll(xb, k, 0)
        first = jnp.where(row8 < k, pltpu.roll(hist, k, 0), rolled[0:SUBLANES])
        return jnp.concatenate([first, rolled[SUBLANES:]], axis=0)

    cw = cw_ref[...]
    u = (cb_ref[...] + cw[0:1] * shifted(3) + cw[1:2] * shifted(2) + cw[2:3] * shifted(1)
         + cw[3:4] * xb)
    last8 = xb[tm - SUBLANES:tm]
    xhist_ref[...] = last8
    conv_ref[0] = last8

    ra, ia = _lru_gate_logits(u, wa_ref, wi_ref)
    gelu_gb = _gelu_tanh(gb)
    zga = proj(2 * D_LRU, D_MODEL)
    sp = _softplus(-lam_ref[...])
    a, mult, ig = _lru_coeffs(ra, ia, ba_ref[...], bi_ref[...], sp)
    first_row = jnp.logical_and(row8 == 0, t == 0)
    mult = jnp.concatenate([jnp.where(first_row, 1.0, mult[0:SUBLANES]), mult[SUBLANES:]], axis=0)
    a_s[...] = a
    bx_s[...] = mult * ig * u
    ga_ref[0] = _sigmoid(zga)
    zgb = proj(2 * D_LRU + D_MODEL, D_MODEL)

    hc = hc_ref[0:1, :]
    for blk in range(tm // SUBLANES):
        rows = slice(blk * SUBLANES, (blk + 1) * SUBLANES)
        av = a_s[rows, :]
        bv = bx_s[rows, :]
        for d in (1, 2, 4):
            a_sh = jnp.where(row8 < d, 1.0, pltpu.roll(av, d, 0))
            b_sh = jnp.where(row8 < d, 0.0, pltpu.roll(bv, d, 0))
            bv = av * b_sh + bv
            av = av * a_sh
        h_blk = bv + av * hc
        h_s[rows, :] = h_blk
        hc = h_blk[SUBLANES - 1:SUBLANES, :]
    hc_ref[0:1, :] = hc
    h = h_s[...]
    hlast_ref[0] = h[tm - SUBLANES:tm]
    gyb_ref[0] = (_sigmoid(zgb) * gelu_gb) * h


def _lru_step_kernel(x_ref, conv0_ref, h0_ref, w_ref, cw_ref, cb_ref, wa_ref, ba_ref, wi_ref,
                     bi_ref, lam_ref, ga_ref, gyb_ref, conv_ref, hlast_ref, *, steps, nseq):
    z = _dot(x_ref[...], w_ref[:, RWKV_COLS:])
    xb = z[:, 0:D_LRU]
    gb = z[:, D_LRU:2 * D_LRU]
    zga = z[:, 2 * D_LRU:2 * D_LRU + D_MODEL]
    zgb = z[:, 2 * D_LRU + D_MODEL:]
    n = steps * nseq
    hist = (CONV_W - 1) * nseq
    xext = jnp.concatenate([conv0_ref[...], xb], axis=0)

    def shifted(k):
        return xext[hist - k * nseq:hist - k * nseq + n]

    cw = cw_ref[...]
    u = (cb_ref[...] + cw[0:1] * shifted(3) + cw[1:2] * shifted(2) + cw[2:3] * shifted(1)
         + cw[3:4] * xb)
    conv_ref[...] = xext[n:n + hist]
    sp = _softplus(-lam_ref[...])
    ra, ia = _lru_gate_logits(u, wa_ref, wi_ref)
    a, mult, ig = _lru_coeffs(ra, ia, ba_ref[...], bi_ref[...], sp)
    bx = mult * ig * u
    h = h0_ref[...]
    hs = []
    for s in range(steps):
        h = a[s * nseq:(s + 1) * nseq] * h + bx[s * nseq:(s + 1) * nseq]
        hs.append(h)
    hlast_ref[...] = h
    hall = jnp.concatenate(hs, axis=0)
    ga_ref[...] = _sigmoid(zga)
    gyb_ref[...] = _sigmoid(zgb) * (hall * _gelu_tanh(gb))


def _lru_weight_specs():
    vec = _const_spec((1, D_LRU))
    gate_w = _const_spec((D_LRU // LRU_GROUP, LRU_GROUP, LRU_GROUP))
    return [_const_spec((D_MODEL, N_IN)), _const_spec((CONV_W, D_LRU)), vec,
            gate_w, vec, gate_w, vec, vec]


def _lru_weights(wp):
    return (wp["w_in"], wp["conv_w"], wp["conv_b"], wp["wa_bd"], wp["ba"], wp["wi_bd"],
            wp["bi"], wp["lam"])


def _lru_seq(x, wp, tm):
    b, t, _ = x.shape
    row_spec = pl.BlockSpec((1, tm, D_MODEL), lambda i, j: (i, j, 0))
    tail_spec = pl.BlockSpec((1, SUBLANES, D_LRU), lambda i, j: (i, 0, 0))
    return pl.pallas_call(
        _lru_seq_kernel,
        grid=(b, t // tm),
        in_specs=[row_spec] + _lru_weight_specs(),
        out_specs=[row_spec, row_spec, tail_spec, tail_spec],
        out_shape=[jax.ShapeDtypeStruct((b, t, D_MODEL), F32)] * 2
        + [jax.ShapeDtypeStruct((b, SUBLANES, D_LRU), F32)] * 2,
        scratch_shapes=[pltpu.VMEM((SUBLANES, D_LRU), F32), pltpu.VMEM((SUBLANES, D_LRU), F32),
                        pltpu.VMEM((tm, D_LRU), F32), pltpu.VMEM((tm, D_LRU), F32),
                        pltpu.VMEM((tm, D_LRU), F32)],
        compiler_params=pltpu.CompilerParams(
            dimension_semantics=("arbitrary", "arbitrary"), vmem_limit_bytes=VMEM_LIMIT),
        name="lru_seq",
    )(x, *_lru_weights(wp))


def _lru_step(x_tm, conv_tm, h0, wp, steps, nseq):
    n = steps * nseq
    hist = (CONV_W - 1) * nseq

    def full(r):
        return pl.BlockSpec((r, D_MODEL), lambda i: (0, 0))

    return pl.pallas_call(
        functools.partial(_lru_step_kernel, steps=steps, nseq=nseq),
        grid=(1,),
        in_specs=[full(n), full(hist), full(nseq)] + _lru_weight_specs(),
        out_specs=[full(n), full(n), full(hist), full(nseq)],
        out_shape=[jax.ShapeDtypeStruct((n, D_MODEL), F32)] * 2
        + [jax.ShapeDtypeStruct((hist, D_LRU), F32), jax.ShapeDtypeStruct((nseq, D_LRU), F32)],
        compiler_params=pltpu.CompilerParams(
            dimension_semantics=("arbitrary",), vmem_limit_bytes=VMEM_LIMIT),
        name="lru_step",
    )(x_tm, conv_tm, h0, *_lru_weights(wp))


def _wkv_consts():
    c = CHUNK
    lane = lax.broadcasted_iota(jnp.int32, (c, PAIR), 1)
    row = lax.broadcasted_iota(jnp.int32, (c, PAIR), 0)
    lane_lo = lane < HEAD
    col = lane % c
    strict = col < row
    incl = col <= row
    eye = jnp.where(col == row, 1.0, 0.0).astype(F32)
    r3 = lax.broadcasted_iota(jnp.int32, (c, 3 * c), 0)
    c3 = lax.broadcasted_iota(jnp.int32, (c, 3 * c), 1) % c
    tri3 = jnp.where(c3 <= r3, 1.0, 0.0).astype(BF16)
    rp = lax.broadcasted_iota(jnp.int32, (PAIR, PAIR), 0)
    cp = lax.broadcasted_iota(jnp.int32, (PAIR, PAIR), 1)
    blockdiag = (rp // HEAD) == (cp // HEAD)
    return dict(lane_lo=lane_lo, tri3=tri3, strict=strict, incl=incl, eye=eye,
                blockdiag=blockdiag)


def _split3(x):
    hi = x.astype(BF16)
    rest = x - hi.astype(F32)
    mid = rest.astype(BF16)
    lo = (rest - mid.astype(F32)).astype(BF16)
    return jnp.concatenate([hi, mid, lo], axis=0)


def _split_heads(x, lane_lo):
    zero = jnp.zeros_like(x)
    return jnp.concatenate([jnp.where(lane_lo, x, zero), jnp.where(lane_lo, zero, x)], axis=0)


def _run_interleaved(stage_gens):
    results = [None] * len(stage_gens)
    live = list(range(len(stage_gens)))
    while live:
        still = []
        for idx in live:
            try:
                next(stage_gens[idx])
                still.append(idx)
            except StopIteration as done:
                results[idx] = done.value
        live = still
    return results


def _wkv_chunk(r, lw, k, v, al, be, get_state, cn):
    c = CHUNK
    lane_lo = cn["lane_lo"]
    cum = jnp.dot(cn["tri3"], _split3(lw), preferred_element_type=F32)
    end = cum[c - 1:c, :]
    yield
    e_cum = jnp.exp(cum)
    e_neg = jnp.exp(-cum)
    e_end = jnp.exp(end)
    rt = r * e_cum
    at = al * jnp.exp(cum - lw)
    bt = be * e_neg
    kt = k * e_neg
    lhs2 = jnp.concatenate([at, rt], axis=0).astype(BF16)
    keys = jnp.concatenate([_split_heads(bt, lane_lo), _split_heads(kt, lane_lo)], axis=0)
    gram = _dot_nt(lhs2, keys)
    yield
    a_ab = jnp.where(cn["strict"], gram[0:c, 0:PAIR], 0.0)
    a_ak = jnp.where(cn["strict"], gram[0:c, PAIR:], 0.0)
    a_rb = jnp.where(cn["incl"], gram[c:, 0:PAIR], 0.0)
    a_rk = jnp.where(cn["incl"], gram[c:, PAIR:], 0.0)
    inv = cn["eye"] + a_ab
    n_iter = int(math.log2(c)) - 1
    pw = _dot(a_ab, _split_heads(a_ab, lane_lo))
    yield
    for it in range(n_iter):
        pw_heads = _split_heads(pw, lane_lo)
        if it < n_iter - 1:
            both = _dot(jnp.concatenate([inv, pw], axis=0), pw_heads)
            inv = inv + both[0:c]
            pw = both[c:]
        else:
            inv = inv + _dot(inv, pw_heads)
        yield
    av = _dot(jnp.concatenate([a_ak, a_rk], axis=0), _split_heads(v, lane_lo))
    state = get_state()
    while state is None:
        yield
        state = get_state()
    from_state = _dot_nt(lhs2, state)
    yield
    u = _dot(inv, _split_heads(from_state[0:c] + av[0:c], lane_lo))
    yield
    y = from_state[c:] + av[c:] + _dot(a_rb, _split_heads(u, lane_lo))
    yield
    uv = jnp.concatenate([u, v], axis=0)
    bk = jnp.concatenate([bt * e_end, kt * e_end], axis=0)
    return y, state * e_end + jnp.where(cn["blockdiag"], _dot_tn(uv, bk), 0.0)


def _wkv_readout(y, r, k, v, g, rk, lnx_g, lnx_b, lane_lo):
    mean = _seg_sum(y, lane_lo) * (1.0 / HEAD)
    bonus = _seg_sum(r * k * rk, lane_lo) * v
    yield
    yc = y - mean
    var = _seg_sum(yc * yc, lane_lo) * (1.0 / HEAD)
    yield
    yn = yc * lax.rsqrt(var + GN_EPS) * lnx_g + lnx_b
    return (yn + bonus) * g


def _from_blockdiag(state):
    return state[0:HEAD, 0:HEAD], pltpu.roll(state[HEAD:], HEAD, 1)[:, 0:HEAD]


def _rwkv_seq_kernel(x_ref, w_ref, mu_ref, w2_ref, w0_ref, a0_ref, kk_ref, ka_ref, rk_ref, lg_ref,
                     lb_ref, y_ref, s_out_ref, hist_ref, acts_scr, s_scr, yraw_scr, *, tm):
    t = pl.program_id(1)
    cn = _wkv_consts()
    n_pairs = s_scr.shape[0]
    tt = x_ref.shape[1]

    @pl.when(t == 0)
    def _():
        hist_ref[...] = jnp.zeros_like(hist_ref)
        s_scr[...] = jnp.zeros_like(s_scr)
        yraw_scr[...] = jnp.zeros_like(yraw_scr)

    for sub in range(tt // tm):
        rows = slice(sub * tm, (sub + 1) * tm)
        outs = _rwkv_project_rows(x_ref[0, rows, :], w_ref, mu_ref, w2_ref, w0_ref, a0_ref,
                                  kk_ref, ka_ref, hist_ref)
        for idx, o in enumerate(outs):
            acts_scr[idx, rows, :] = o

    r_ref, lw_ref, k_ref, v_ref, al_ref, be_ref, g_ref = (acts_scr.at[i] for i in range(7))
    n_chunks = tt // CHUNK

    def readout_stages(p, rows):
        ln = pl.ds(p * PAIR, PAIR)
        out = yield from _wkv_readout(
            yraw_scr[:, ln], r_ref[rows, ln], k_ref[rows, ln], v_ref[rows, ln],
            g_ref[rows, ln], rk_ref[:, ln], lg_ref[:, ln], lb_ref[:, ln], cn["lane_lo"])
        y_ref[0, rows, ln] = out

    def chunk_stages(p, rows, get_state):
        ln = pl.ds(p * PAIR, PAIR)
        return _wkv_chunk(r_ref[rows, ln], lw_ref[rows, ln], k_ref[rows, ln],
                          v_ref[rows, ln], al_ref[rows, ln], be_ref[rows, ln],
                          get_state, cn)

    def body(i, states):
        def rows_of(j):
            return pl.ds(pl.multiple_of((CHUNKS_IN_FLIGHT * i + j) * CHUNK, CHUNK), CHUNK)

        prev = pl.ds(pl.multiple_of(jnp.maximum(CHUNKS_IN_FLIGHT * i - 1, 0) * CHUNK, CHUNK),
                     CHUNK)
        end_states = [[None] * n_pairs for _ in range(CHUNKS_IN_FLIGHT)]

        def start_state(j, p):
            return states[p] if j == 0 else end_states[j - 1][p]

        def stages(j, p):
            ln = pl.ds(p * PAIR, PAIR)
            rows = rows_of(j)
            y, state = yield from chunk_stages(p, rows, functools.partial(start_state, j, p))
            end_states[j][p] = state
            if j == CHUNKS_IN_FLIGHT - 1:
                yraw_scr[:, ln] = y
                return
            out = yield from _wkv_readout(
                y, r_ref[rows, ln], k_ref[rows, ln], v_ref[rows, ln],
                g_ref[rows, ln], rk_ref[:, ln], lg_ref[:, ln], lb_ref[:, ln], cn["lane_lo"])
            y_ref[0, rows, ln] = out

        _run_interleaved([readout_stages(p, prev) for p in range(n_pairs)]
                         + [stages(j, p) for j in range(CHUNKS_IN_FLIGHT) for p in range(n_pairs)])
        return tuple(end_states[CHUNKS_IN_FLIGHT - 1])

    assert n_chunks % CHUNKS_IN_FLIGHT == 0
    states = lax.fori_loop(0, n_chunks // CHUNKS_IN_FLIGHT, body,
                           tuple(s_scr[p] for p in range(n_pairs)))
    last = pl.ds((n_chunks - 1) * CHUNK, CHUNK)
    _run_interleaved([readout_stages(p, last) for p in range(n_pairs)])
    for p in range(n_pairs):
        s_scr[p] = states[p]
        s_out_ref[0, 2 * p], s_out_ref[0, 2 * p + 1] = _from_blockdiag(states[p])


def _rwkv_step_kernel(x_ref, xp_ref, wlo_ref, wr_ref, wk_ref, wv_ref, mulo_ref, mur_ref, muk_ref,
                      muv_ref, w2d_ref, w2a_ref, w2g_ref, w0_ref, a0_ref, kk_ref, ka_ref,
                      rk_ref, lg_ref, lb_ref, s_in_ref, y_ref, s_out_ref,
                      xb_scr, act_scr, tr_scr, y_scr, *, steps):
    nseq = s_in_ref.shape[3]

    def mixed(w_ref, mu_ref):
        z = jnp.dot(xb_scr[0], w_ref[...], preferred_element_type=F32)
        zprev = jnp.dot(xb_scr[1], w_ref[...], preferred_element_type=F32)
        return z + mu_ref[...] * (zprev - z)

    @pl.when(pl.program_id(0) == 0)
    def _():
        xb_scr[0] = x_ref[...].astype(BF16)
        xb_scr[1] = xp_ref[...].astype(BF16)
        act_scr[...] = _lora_act(mixed(wlo_ref, mulo_ref)).astype(BF16)

    group_refs = {"r": (wr_ref, mur_ref), "k": (wk_ref, muk_ref), "v": (wv_ref, muv_ref)}
    n_heads = s_in_ref.shape[0]
    lane_lo = lax.broadcasted_iota(jnp.int32, (x_ref.shape[0], PAIR), 1) < HEAD

    def head_sums(v):
        return jnp.concatenate([_seg_sum(v[:, q * PAIR:(q + 1) * PAIR], lane_lo)
                                for q in range(n_heads // 2)], axis=1)

    acts = _rwkv_prep(
        lambda: act_scr[...], lambda group: mixed(*group_refs[group]),
        lambda act: tuple(jnp.dot(act, w2[...], preferred_element_type=F32)
                          for w2 in (w2d_ref, w2a_ref, w2g_ref)),
        w0_ref[...], a0_ref[...], kk_ref[...], ka_ref[...], head_sums)
    idx_r, idx_dec, idx_k, idx_v, idx_al, idx_be, idx_g = range(7)
    for idx, act_rows in enumerate(acts):
        for t in range(steps):
            tile = act_rows[t * nseq:(t + 1) * nseq, :].T
            tr_scr[idx, t] = jnp.exp(tile) if idx == idx_dec else tile

    for h in range(n_heads):
        ch = slice(h * HEAD, (h + 1) * HEAD)

        def row_stages(i, h=h, ch=ch):
            s = s_in_ref[h, i]
            for t in range(steps):
                sa = jnp.sum(s * tr_scr[idx_al, t, ch, :], axis=0, keepdims=True)
                yield
                v_i = tr_scr[idx_v, t, pl.ds(h * HEAD + i, 1), :]
                s = (s * tr_scr[idx_dec, t, ch, :] + sa * tr_scr[idx_be, t, ch, :]
                     + v_i * tr_scr[idx_k, t, ch, :])
                y_scr[t, pl.ds(h * HEAD + i, 1), :] = jnp.sum(
                    s * tr_scr[idx_r, t, ch, :], axis=0, keepdims=True)
                yield
            s_out_ref[h, i] = s

        @pl.loop(0, HEAD // STEP_ROWS_IN_FLIGHT)
        def _(blk):
            _run_interleaved([row_stages(blk * STEP_ROWS_IN_FLIGHT + ii)
                              for ii in range(STEP_ROWS_IN_FLIGHT)])

    for t in range(steps):
        outs = []
        for h in range(n_heads):
            ch = slice(h * HEAD, (h + 1) * HEAD)
            y = y_scr[t, ch, :]
            yc = y - jnp.mean(y, axis=0, keepdims=True)
            var = jnp.mean(yc * yc, axis=0, keepdims=True)
            yn = yc * lax.rsqrt(var + GN_EPS) * lg_ref[ch, :] + lb_ref[ch, :]
            rk_sum = jnp.sum(tr_scr[idx_r, t, ch, :] * tr_scr[idx_k, t, ch, :] * rk_ref[ch, :],
                             axis=0, keepdims=True)
            outs.append((yn + rk_sum * tr_scr[idx_v, t, ch, :]) * tr_scr[idx_g, t, ch, :])
        y_ref[t * nseq:(t + 1) * nseq, :] = jnp.concatenate(outs, axis=0).T


def _rwkv_seq(x, wp, tt, tm):
    b, t, _ = x.shape
    row_spec = pl.BlockSpec((1, tt, D_MODEL), lambda i, j: (i, j, 0))
    vec = _const_spec((1, D_MODEL))
    return pl.pallas_call(
        functools.partial(_rwkv_seq_kernel, tm=tm),
        grid=(b, t // tt),
        in_specs=[row_spec, _const_spec((D_MODEL, RWKV_COLS)), _const_spec((1, RWKV_COLS)),
                  _const_spec((LORA, 3 * D_MODEL))] + [vec] * 7,
        out_specs=[row_spec,
                   pl.BlockSpec((1, N_HEADS, HEAD, HEAD), lambda i, j: (i, 0, 0, 0))],
        out_shape=[jax.ShapeDtypeStruct((b, t, D_MODEL), F32),
                   jax.ShapeDtypeStruct((b, N_HEADS, HEAD, HEAD), F32)],
        scratch_shapes=[pltpu.VMEM((SUBLANES, RWKV_COLS), F32),
                        pltpu.VMEM((7, tt, D_MODEL), F32),
                        pltpu.VMEM((N_PAIRS, PAIR, PAIR), F32),
                        pltpu.VMEM((CHUNK, D_MODEL), F32)],
        compiler_params=pltpu.CompilerParams(
            dimension_semantics=("arbitrary", "arbitrary"), vmem_limit_bytes=VMEM_LIMIT),
        name="rwkv_seq",
    )(x, wp["w_in"], wp["mu"], wp["w2ext"], wp["w0"], wp["a0"], wp["k_k"], wp["k_a"],
      wp["r_k"], wp["lnx_g"], wp["lnx_b"])


def _rwkv_step(x_tm, xprev_tm, s_last, wp, steps):
    n = x_tm.shape[0]
    nseq = s_last.shape[3]

    width = STEP_PAIRS * PAIR
    n_blocks = D_MODEL // width

    def pair_cols(rows, group):
        return pl.BlockSpec((rows, width), lambda p: (0, group * n_blocks + p))

    def lora_cols(rows):
        return pl.BlockSpec((rows, LORA), lambda p: (0, 3 * D_MODEL // LORA))

    full = pl.BlockSpec((n, D_MODEL), lambda p: (0, 0), pipeline_mode=pl.Buffered(1))
    col_spec = pl.BlockSpec((width, nseq), lambda p: (p, 0))
    st_spec = pl.BlockSpec((2 * STEP_PAIRS, HEAD, HEAD, nseq), lambda p: (p, 0, 0, 0))
    cols = [jnp.broadcast_to(wp[name].reshape(D_MODEL, 1), (D_MODEL, nseq))
            for name in ("r_k", "lnx_g", "lnx_b")]
    return pl.pallas_call(
        functools.partial(_rwkv_step_kernel, steps=steps),
        grid=(n_blocks,),
        in_specs=[full, full]
        + [lora_cols(D_MODEL)] + [pair_cols(D_MODEL, g) for g in range(3)]
        + [lora_cols(1)] + [pair_cols(1, g) for g in range(3)]
        + [pair_cols(LORA, g) for g in range(3)]
        + [pair_cols(1, 0)] * 4 + [col_spec] * 3 + [st_spec],
        out_specs=[pl.BlockSpec((n, width), lambda p: (0, p)), st_spec],
        out_shape=[jax.ShapeDtypeStruct((n, D_MODEL), F32),
                   jax.ShapeDtypeStruct(s_last.shape, F32)],
        scratch_shapes=[pltpu.VMEM((2, n, D_MODEL), BF16), pltpu.VMEM((n, LORA), BF16),
                        pltpu.VMEM((7, steps, width, nseq), F32),
                        pltpu.VMEM((steps, width, nseq), F32)],
        compiler_params=pltpu.CompilerParams(
            dimension_semantics=("arbitrary",), vmem_limit_bytes=VMEM_LIMIT),
        name="rwkv_step",
    )(x_tm, xprev_tm, *[wp["w_in"]] * 4, *[wp["mu"]] * 4, *[wp["w2ext"]] * 3,
      wp["w0"], wp["a0"], wp["k_k"], wp["k_a"], *cols, s_last)


def _post_kernel(x_ref, ya_ref, ga_ref, gyb_ref, wo_ref, l1g_ref, l1b_ref, wg_ref, wu_ref, wd_ref,
                 l2g_ref, l2b_ref, y_ref):
    x = x_ref[...]
    merged = ga_ref[...] * ya_ref[...] + gyb_ref[...]
    mix = _dot(merged, wo_ref[...])
    h1 = _layer_norm(ALPHA * x + mix, l1g_ref[...], l1b_ref[...])
    h1b = h1.astype(BF16)
    gate = jnp.dot(h1b, wg_ref[...], preferred_element_type=F32)
    up = jnp.dot(h1b, wu_ref[...], preferred_element_type=F32)
    act = (gate * _sigmoid(gate)) * up
    ffn = _dot(act, wd_ref[...])
    y_ref[...] = _layer_norm(ALPHA * h1 + ffn, l2g_ref[...], l2b_ref[...])


def _post(x, ya, ga, gyb, wp, tm):
    n = x.shape[0]
    row_spec = pl.BlockSpec((tm, D_MODEL), lambda i: (i, 0))
    vec = _const_spec((1, D_MODEL))
    return pl.pallas_call(
        _post_kernel,
        grid=(n // tm,),
        in_specs=[row_spec] * 4 + [_const_spec((D_MODEL, D_MODEL)), vec, vec,
                                   _const_spec((D_MODEL, D_FF)), _const_spec((D_MODEL, D_FF)),
                                   _const_spec((D_FF, D_MODEL)), vec, vec],
        out_specs=row_spec,
        out_shape=jax.ShapeDtypeStruct((n, D_MODEL), F32),
        compiler_params=pltpu.CompilerParams(
            dimension_semantics=("arbitrary",), vmem_limit_bytes=VMEM_LIMIT),
        name="post",
    )(x, ya, ga, gyb, wp["w_o"], wp["ln1_g"], wp["ln1_b"], wp["w_gate"], wp["w_up"], wp["w_down"],
      wp["ln2_g"], wp["ln2_b"])


def _prep_weights(w_in, tmix_mu, w0, w2_decay, a0, a2_iclr, g2_gate, k_k, k_a, r_k, lnx_g, lnx_b,
                  conv_w, conv_b, lru_wa, lru_ba, lru_wi, lru_bi, lru_lambda, w_o,
                  ln1_g, ln1_b, w_ffn_gate, w_ffn_up, w_ffn_down, ln2_g, ln2_b):
    row = lambda v: v.reshape(1, -1).astype(F32)
    zeros = lambda r: jnp.zeros((r, D_MODEL), F32)
    w2ext = jnp.concatenate([
        jnp.concatenate([w2_decay, zeros(DECAY_LORA), zeros(DECAY_LORA)], axis=1),
        jnp.concatenate([zeros(ICLR_LORA), a2_iclr, zeros(ICLR_LORA)], axis=1),
        jnp.concatenate([zeros(GATE_LORA), zeros(GATE_LORA), g2_gate], axis=1)], axis=0)

    def gate_blockdiag(w):
        per = LRU_GROUP // LRU_BS
        w4 = w.reshape(LRU_BLOCKS // per, per, LRU_BS, LRU_BS)
        eye = jnp.eye(per, dtype=w.dtype)
        bd = jnp.einsum("gpcd,pq->gpcqd", w4, eye)
        return bd.reshape(LRU_BLOCKS // per, LRU_GROUP, LRU_GROUP).astype(BF16)

    return dict(
        w_in=w_in.astype(BF16),
        mu=row(tmix_mu), w2ext=w2ext.astype(BF16), w0=row(w0), a0=row(a0), k_k=row(k_k),
        k_a=row(k_a), r_k=row(r_k), lnx_g=row(lnx_g), lnx_b=row(lnx_b),
        conv_w=conv_w.astype(F32), conv_b=row(conv_b), wa_bd=gate_blockdiag(lru_wa),
        ba=row(lru_ba), wi_bd=gate_blockdiag(lru_wi), bi=row(lru_bi), lam=row(lru_lambda),
        w_o=w_o.astype(BF16), ln1_g=row(ln1_g), ln1_b=row(ln1_b), w_gate=w_ffn_gate.astype(BF16),
        w_up=w_ffn_up.astype(BF16), w_down=w_ffn_down.astype(BF16), ln2_g=row(ln2_g),
        ln2_b=row(ln2_b))


def _row_tiles(t):
    rwkv_tile = min(t, 8 * CHUNK)
    return rwkv_tile, min(t, 256), min(t, 256), min(t, 256)


def _prompt_layer(x, wp):
    b, t, _ = x.shape
    rwkv_tile, proj_tile, lru_tile, post_tile = _row_tiles(t)
    ya, s_heads = _rwkv_seq(x, wp, rwkv_tile, proj_tile)
    ga, gyb, conv_tail, h_tail = _lru_seq(x, wp, lru_tile)
    y = _post(x.reshape(b * t, D_MODEL), ya.reshape(b * t, D_MODEL), ga.reshape(b * t, D_MODEL),
              gyb.reshape(b * t, D_MODEL), wp, post_tile).reshape(b, t, D_MODEL)
    return (y, x[:, -1], s_heads, conv_tail[:, SUBLANES - (CONV_W - 1):],
            h_tail[:, SUBLANES - 1])


def _sample_layer(x, shift_buf, wkv0, conv_buf, h0, wp):
    b, t, _ = x.shape
    n = b * t
    x_tm = x.transpose(1, 0, 2).reshape(n, D_MODEL)
    xprev_tm = jnp.concatenate([shift_buf, x_tm[:n - b]], axis=0)
    ya_tm, s_last = _rwkv_step(x_tm, xprev_tm, wkv0.transpose(1, 2, 3, 0), wp, t)
    conv_tm = conv_buf.transpose(1, 0, 2).reshape((CONV_W - 1) * b, D_LRU)
    ga_tm, gyb_tm, conv_new_tm, h_last = _lru_step(x_tm, conv_tm, h0, wp, t, b)
    y_tm = _post(x_tm, ya_tm, ga_tm, gyb_tm, wp, min(n, 256))
    y = y_tm.reshape(t, b, D_MODEL).transpose(1, 0, 2)
    conv_new = conv_new_tm.reshape(CONV_W - 1, b, D_LRU).transpose(1, 0, 2)
    return y, x[:, -1], s_last.transpose(3, 0, 1, 2), conv_new, h_last


def kernel(x_prompt, x_sample, state_shift, state_wkv, state_conv, state_lru, w_in, tmix_mu, w0, w2_decay, a0, a2_iclr, g2_gate, k_k, k_a, r_k, lnx_g, lnx_b, conv_w, conv_b, lru_wa, lru_ba, lru_wi, lru_bi, lru_lambda, w_o, ln1_g, ln1_b, w_ffn_gate, w_ffn_up, w_ffn_down, ln2_g, ln2_b):
    params = (w_in, tmix_mu, w0, w2_decay, a0, a2_iclr, g2_gate, k_k, k_a, r_k, lnx_g, lnx_b,
              conv_w, conv_b, lru_wa, lru_ba, lru_wi, lru_bi, lru_lambda, w_o,
              ln1_g, ln1_b, w_ffn_gate, w_ffn_up, w_ffn_down, ln2_g, ln2_b)
    wp = _prep_weights(*[p[0] for p in params])
    yp, sh_p, wkv_p, conv_p, lru_p = _prompt_layer(x_prompt, wp)
    ys, sh_s, wkv_s, conv_s, lru_s = _sample_layer(
        x_sample, state_shift[0], state_wkv[0], state_conv[0], state_lru[0], wp)
    return (yp, ys, sh_p[None], wkv_p[None], conv_p[None], lru_p[None],
            sh_s[None], wkv_s[None], conv_s[None], lru_s[None])
```

```python
import functools
import math

import jax
import jax.numpy as jnp
from jax import lax
from jax.experimental import pallas as pl
from jax.experimental.pallas import tpu as pltpu

F32 = jnp.float32
BF16 = jnp.bfloat16

D_MODEL = 1024
HEAD = 64
N_HEADS = D_MODEL // HEAD
PAIR = 2 * HEAD
N_PAIRS = N_HEADS // 2
DECAY_LORA = 64
ICLR_LORA = 64
GATE_LORA = 128
LORA = DECAY_LORA + ICLR_LORA + GATE_LORA
RWKV_COLS = 3 * D_MODEL + LORA
GN_EPS = HEAD * 1e-5
D_LRU = D_MODEL
LRU_BLOCKS = 16
LRU_BS = D_LRU // LRU_BLOCKS
LRU_GROUP = 256
CONV_W = 4
LRU_C = 8.0
LRU_COLS = 2 * D_LRU + 2 * D_MODEL
N_IN = RWKV_COLS + LRU_COLS
D_FF = 2816
ALPHA = 2.0 ** 0.25
LN_EPS = 1e-5

SUBLANES = 8
CHUNK = 64
CHUNKS_IN_FLIGHT = 4
STEP_ROWS_IN_FLIGHT = 8
STEP_PAIRS = 2
VMEM_LIMIT = 56 * 1024 * 1024


def _softplus(x):
    return jnp.maximum(x, 0.0) + jnp.log1p(jnp.exp(-jnp.abs(x)))


def _sigmoid(x):
    return 0.5 * jnp.tanh(0.5 * x) + 0.5


def _gelu_tanh(x):
    c = math.sqrt(2.0 / math.pi)
    inner = x * (c + (c * 0.044715) * (x * x))
    return (0.5 * x) * (1.0 + jnp.tanh(inner))


def _layer_norm(x, g, b):
    mu = jnp.mean(x, axis=-1, keepdims=True)
    xc = x - mu
    var = jnp.mean(xc * xc, axis=-1, keepdims=True)
    return xc * lax.rsqrt(var + LN_EPS) * g + b


def _dot(a, b):
    return jnp.dot(a.astype(BF16), b.astype(BF16), preferred_element_type=F32)


def _dot_nt(a, b):
    return lax.dot_general(a.astype(BF16), b.astype(BF16), (((1,), (1,)), ((), ())),
                           preferred_element_type=F32)


def _dot_tn(a, b):
    return lax.dot_general(a.astype(BF16), b.astype(BF16), (((0,), (0,)), ((), ())),
                           preferred_element_type=F32)


def _seg_sum(x, lane_lo):
    s0 = jnp.sum(jnp.where(lane_lo, x, 0.0), axis=-1, keepdims=True)
    s1 = jnp.sum(jnp.where(lane_lo, 0.0, x), axis=-1, keepdims=True)
    return jnp.where(lane_lo, s0, s1)


def _head_sums(x):
    lane_lo = lax.broadcasted_iota(jnp.int32, (x.shape[0], PAIR), 1) < HEAD
    return jnp.concatenate([_seg_sum(x[:, p * PAIR:(p + 1) * PAIR], lane_lo)
                            for p in range(N_PAIRS)], axis=1)


def _lora_act(lo):
    lane = lax.broadcasted_iota(jnp.int32, lo.shape, 1)
    return jnp.where(lane < DECAY_LORA, jnp.tanh(lo),
                     jnp.where(lane < DECAY_LORA + ICLR_LORA, lo, _sigmoid(lo)))


def _rwkv_prep(lora_act, mixed, lora_out, w0, a0, k_k, k_a, head_sums):
    act = lora_act()
    r = mixed("r")
    lora_decay, lora_iclr, g = lora_out(act)
    k = mixed("k")
    lw = -math.exp(-0.5) * _sigmoid(w0 + lora_decay)
    a = _sigmoid(a0 + lora_iclr)
    v = mixed("v")
    kkraw = k * k_k
    kk = kkraw * lax.rsqrt(jnp.maximum(head_sums(kkraw * kkraw), 1e-24))
    k2 = k * (1.0 + (a - 1.0) * k_a)
    return r, lw, k2, v, -kk, kk * a, g


def _rwkv_project_rows(x, w_ref, mu_ref, w2_ref, w0_ref, a0_ref, kk_ref, ka_ref, hist_ref):
    xbf = x.astype(BF16)
    tm = xbf.shape[0]
    row = lax.broadcasted_iota(jnp.int32, (SUBLANES, 1), 0)

    def mixed_cols(first_col, n_cols):
        cols = slice(first_col, first_col + n_cols)
        z = jnp.dot(xbf, w_ref[:, cols], preferred_element_type=F32)
        rolled = pltpu.roll(z, 1, 0)
        first = jnp.where(row == 0, hist_ref[SUBLANES - 1:SUBLANES, cols], rolled[0:SUBLANES])
        zprev = jnp.concatenate([first, rolled[SUBLANES:]], axis=0)
        hist_ref[:, cols] = z[tm - SUBLANES:tm]
        return z + mu_ref[:, cols] * (zprev - z)

    def lora_out(act):
        lora = _dot(act, w2_ref[...])
        return tuple(lora[:, i * D_MODEL:(i + 1) * D_MODEL] for i in range(3))

    first_col = {"r": 0, "k": D_MODEL, "v": 2 * D_MODEL}
    return _rwkv_prep(lambda: _lora_act(mixed_cols(3 * D_MODEL, LORA)),
                      lambda group: mixed_cols(first_col[group], D_MODEL), lora_out,
                      w0_ref[...], a0_ref[...], kk_ref[...], ka_ref[...], _head_sums)


def _const_spec(shape):
    nd = len(shape)
    return pl.BlockSpec(shape, lambda *_: (0,) * nd, pipeline_mode=pl.Buffered(1))


def _lru_gate_logits(u, wa_ref, wi_ref):
    ub = u.astype(BF16)
    ra, ia = [], []
    for q in range(u.shape[1] // LRU_GROUP):
        uq = ub[:, q * LRU_GROUP:(q + 1) * LRU_GROUP]
        ra.append(jnp.dot(uq, wa_ref[q], preferred_element_type=F32))
        ia.append(jnp.dot(uq, wi_ref[q], preferred_element_type=F32))
    if len(ra) == 1:
        return ra[0], ia[0]
    return jnp.concatenate(ra, axis=1), jnp.concatenate(ia, axis=1)


def _lru_coeffs(ra, ia, ba, bi, sp):
    ig = _sigmoid(ia + bi)
    half = (-0.5 * LRU_C) * sp
    log_a = half * jnp.tanh(0.5 * (ra + ba)) + half
    a = jnp.exp(log_a)
    m2 = -jnp.tanh(log_a) * (a * a + 1.0)
    mult = jnp.where(m2 > 0.0, m2 * lax.rsqrt(m2), 0.0)
    return a, mult, ig


def _lru_seq_kernel(x_ref, w_ref, cw_ref, cb_ref, wa_ref, ba_ref, wi_ref, bi_ref, lam_ref,
                    ga_ref, gyb_ref, conv_ref, hlast_ref,
                    xhist_ref, hc_ref, a_s, bx_s, h_s):
    t = pl.program_id(1)

    @pl.when(t == 0)
    def _():
        xhist_ref[...] = jnp.zeros_like(xhist_ref)
        hc_ref[...] = jnp.zeros_like(hc_ref)

    xbf = x_ref[0].astype(BF16)
    tm = xbf.shape[0]

    def proj(first_col, n_cols):
        cols = slice(RWKV_COLS + first_col, RWKV_COLS + first_col + n_cols)
        return jnp.dot(xbf, w_ref[:, cols], preferred_element_type=F32)

    xb = proj(0, D_LRU)
    gb = proj(D_LRU, D_LRU)
    row8 = lax.broadcasted_iota(jnp.int32, (SUBLANES, 1), 0)
    hist = xhist_ref[...]

    def shifted(k):
        rolled = pltpu.roll(xb, k, 0)
        first = jnp.where(row8 < k, pltpu.roll(hist, k, 0), rolled[0:SUBLANES])
        return jnp.concatenate([first, rolled[SUBLANES:]], axis=0)

    cw = cw_ref[...]
    u = (cb_ref[...] + cw[0:1] * shifted(3) + cw[1:2] * shifted(2) + cw[2:3] * shifted(1)
         + cw[3:4] * xb)
    last8 = xb[tm - SUBLANES:tm]
    xhist_ref[...] = last8
    conv_ref[0] = last8

    ra, ia = _lru_gate_logits(u, wa_ref, wi_ref)
    gelu_gb = _gelu_tanh(gb)
    zga = proj(2 * D_LRU, D_MODEL)
    sp = _softplus(-lam_ref[...])
    a, mult, ig = _lru_coeffs(ra, ia, ba_ref[...], bi_ref[...], sp)
    first_row = jnp.logical_and(row8 == 0, t == 0)
    mult = jnp.concatenate([jnp.where(first_row, 1.0, mult[0:SUBLANES]), mult[SUBLANES:]], axis=0)
    a_s[...] = a
    bx_s[...] = mult * ig * u
    ga_ref[0] = _sigmoid(zga)
    zgb = proj(2 * D_LRU + D_MODEL, D_MODEL)

    hc = hc_ref[0:1, :]
    for blk in range(tm // SUBLANES):
        rows = slice(blk * SUBLANES, (blk + 1) * SUBLANES)
        av = a_s[rows, :]
        bv = bx_s[rows, :]
        for d in (1, 2, 4):
            a_sh = jnp.where(row8 < d, 1.0, pltpu.roll(av, d, 0))
            b_sh = jnp.where(row8 < d, 0.0, pltpu.roll(bv, d, 0))
            bv = av * b_sh + bv
            av = av * a_sh
        h_blk = bv + av * hc
        h_s[rows, :] = h_blk
        hc = h_blk[SUBLANES - 1:SUBLANES, :]
    hc_ref[0:1, :] = hc
    h = h_s[...]
    hlast_ref[0] = h[tm - SUBLANES:tm]
    gyb_ref[0] = (_sigmoid(zgb) * gelu_gb) * h


def _lru_step_kernel(x_ref, conv0_ref, h0_ref, wxb_ref, wgb_ref, wga_ref, wgm_ref, cw_ref, cb_ref,
                     wa_ref, ba_ref, wi_ref, bi_ref, lam_ref, ga_ref, gyb_ref, conv_ref, hlast_ref,
                     *, steps, nseq):
    xbf = x_ref[...].astype(BF16)
    xb, gb, zga, zgb = (jnp.dot(xbf, w[...], preferred_element_type=F32)
                        for w in (wxb_ref, wgb_ref, wga_ref, wgm_ref))
    n = steps * nseq
    hist = (CONV_W - 1) * nseq
    xext = jnp.concatenate([conv0_ref[...], xb], axis=0)

    def shifted(k):
        return xext[hist - k * nseq:hist - k * nseq + n]

    cw = cw_ref[...]
    u = (cb_ref[...] + cw[0:1] * shifted(3) + cw[1:2] * shifted(2) + cw[2:3] * shifted(1)
         + cw[3:4] * xb)
    conv_ref[...] = xext[n:n + hist]
    sp = _softplus(-lam_ref[...])
    ra, ia = _lru_gate_logits(u, wa_ref, wi_ref)
    a, mult, ig = _lru_coeffs(ra, ia, ba_ref[...], bi_ref[...], sp)
    bx = mult * ig * u
    h = h0_ref[...]
    hs = []
    for s in range(steps):
        h = a[s * nseq:(s + 1) * nseq] * h + bx[s * nseq:(s + 1) * nseq]
        hs.append(h)
    hlast_ref[...] = h
    hall = jnp.concatenate(hs, axis=0)
    ga_ref[...] = _sigmoid(zga)
    gyb_ref[...] = _sigmoid(zgb) * (hall * _gelu_tanh(gb))


def _lru_weight_specs():
    vec = _const_spec((1, D_LRU))
    gate_w = _const_spec((D_LRU // LRU_GROUP, LRU_GROUP, LRU_GROUP))
    return [_const_spec((D_MODEL, N_IN)), _const_spec((CONV_W, D_LRU)), vec,
            gate_w, vec, gate_w, vec, vec]


def _lru_weights(wp):
    return (wp["w_in"], wp["conv_w"], wp["conv_b"], wp["wa_bd"], wp["ba"], wp["wi_bd"],
            wp["bi"], wp["lam"])


def _lru_seq(x, wp, tm):
    b, t, _ = x.shape
    row_spec = pl.BlockSpec((1, tm, D_MODEL), lambda i, j: (i, j, 0))
    tail_spec = pl.BlockSpec((1, SUBLANES, D_LRU), lambda i, j: (i, 0, 0))
    return pl.pallas_call(
        _lru_seq_kernel,
        grid=(b, t // tm),
        in_specs=[row_spec] + _lru_weight_specs(),
        out_specs=[row_spec, row_spec, tail_spec, tail_spec],
        out_shape=[jax.ShapeDtypeStruct((b, t, D_MODEL), F32)] * 2
        + [jax.ShapeDtypeStruct((b, SUBLANES, D_LRU), F32)] * 2,
        scratch_shapes=[pltpu.VMEM((SUBLANES, D_LRU), F32), pltpu.VMEM((SUBLANES, D_LRU), F32),
                        pltpu.VMEM((tm, D_LRU), F32), pltpu.VMEM((tm, D_LRU), F32),
                        pltpu.VMEM((tm, D_LRU), F32)],
        compiler_params=pltpu.CompilerParams(
            dimension_semantics=("arbitrary", "arbitrary"), vmem_limit_bytes=VMEM_LIMIT),
        name="lru_seq",
    )(x, *_lru_weights(wp))


def _lru_step(x_tm, conv_tm, h0, wp, steps, nseq):
    n = steps * nseq
    hist = (CONV_W - 1) * nseq

    def group(r):
        return pl.BlockSpec((r, LRU_GROUP), lambda q: (0, q))

    def w_cols(part):
        first = (RWKV_COLS + part * D_LRU) // LRU_GROUP
        return pl.BlockSpec((D_MODEL, LRU_GROUP), lambda q: (0, first + q))

    gate_w = pl.BlockSpec((1, LRU_GROUP, LRU_GROUP), lambda q: (q, 0, 0))
    x_spec = pl.BlockSpec((n, D_MODEL), lambda q: (0, 0), pipeline_mode=pl.Buffered(1))
    return pl.pallas_call(
        functools.partial(_lru_step_kernel, steps=steps, nseq=nseq),
        grid=(D_LRU // LRU_GROUP,),
        in_specs=[x_spec, group(hist), group(nseq)] + [w_cols(part) for part in range(4)]
        + [group(CONV_W), group(1), gate_w, group(1), gate_w, group(1), group(1)],
        out_specs=[group(n), group(n), group(hist), group(nseq)],
        out_shape=[jax.ShapeDtypeStruct((n, D_MODEL), F32)] * 2
        + [jax.ShapeDtypeStruct((hist, D_LRU), F32), jax.ShapeDtypeStruct((nseq, D_LRU), F32)],
        compiler_params=pltpu.CompilerParams(
            dimension_semantics=("arbitrary",), vmem_limit_bytes=VMEM_LIMIT),
        name="lru_step",
    )(x_tm, conv_tm, h0, *[wp["w_in"]] * 4, wp["conv_w"], wp["conv_b"], wp["wa_bd"], wp["ba"],
      wp["wi_bd"], wp["bi"], wp["lam"])


def _wkv_consts():
    c = CHUNK
    lane = lax.broadcasted_iota(jnp.int32, (c, PAIR), 1)
    row = lax.broadcasted_iota(jnp.int32, (c, PAIR), 0)
    lane_lo = lane < HEAD
    col = lane % c
    strict = col < row
    incl = col <= row
    eye = jnp.where(col == row, 1.0, 0.0).astype(F32)
    r3 = lax.broadcasted_iota(jnp.int32, (c, 3 * c), 0)
    c3 = lax.broadcasted_iota(jnp.int32, (c, 3 * c), 1) % c
    tri3 = jnp.where(c3 <= r3, 1.0, 0.0).astype(BF16)
    rp = lax.broadcasted_iota(jnp.int32, (PAIR, PAIR), 0)
    cp = lax.broadcasted_iota(jnp.int32, (PAIR, PAIR), 1)
    blockdiag = (rp // HEAD) == (cp // HEAD)
    return dict(lane_lo=lane_lo, tri3=tri3, strict=strict, incl=incl, eye=eye,
                blockdiag=blockdiag)


def _split3(x):
    hi = x.astype(BF16)
    rest = x - hi.astype(F32)
    mid = rest.astype(BF16)
    lo = (rest - mid.astype(F32)).astype(BF16)
    return jnp.concatenate([hi, mid, lo], axis=0)


def _split_heads(x, lane_lo):
    zero = jnp.zeros_like(x)
    return jnp.concatenate([jnp.where(lane_lo, x, zero), jnp.where(lane_lo, zero, x)], axis=0)


def _run_interleaved(stage_gens):
    results = [None] * len(stage_gens)
    live = list(range(len(stage_gens)))
    while live:
        still = []
        for idx in live:
            try:
                next(stage_gens[idx])
                still.append(idx)
            except StopIteration as done:
                results[idx] = done.value
        live = still
    return results


def _wkv_chunk(r, lw, k, v, al, be, get_state, cn):
    c = CHUNK
    lane_lo = cn["lane_lo"]
    cum = jnp.dot(cn["tri3"], _split3(lw), preferred_element_type=F32)
    end = cum[c - 1:c, :]
    yield
    e_cum = jnp.exp(cum)
    e_neg = jnp.exp(-cum)
    e_end = jnp.exp(end)
    rt = r * e_cum
    at = al * jnp.exp(cum - lw)
    bt = be * e_neg
    kt = k * e_neg
    lhs2 = jnp.concatenate([at, rt], axis=0).astype(BF16)
    keys = jnp.concatenate([_split_heads(bt, lane_lo), _split_heads(kt, lane_lo)], axis=0)
    gram = _dot_nt(lhs2, keys)
    yield
    a_ab = jnp.where(cn["strict"], gram[0:c, 0:PAIR], 0.0)
    a_ak = jnp.where(cn["strict"], gram[0:c, PAIR:], 0.0)
    a_rb = jnp.where(cn["incl"], gram[c:, 0:PAIR], 0.0)
    a_rk = jnp.where(cn["incl"], gram[c:, PAIR:], 0.0)
    inv = cn["eye"] + a_ab
    n_iter = int(math.log2(c)) - 1
    pw = _dot(a_ab, _split_heads(a_ab, lane_lo))
    yield
    for it in range(n_iter):
        pw_heads = _split_heads(pw, lane_lo)
        if it < n_iter - 1:
            both = _dot(jnp.concatenate([inv, pw], axis=0), pw_heads)
            inv = inv + both[0:c]
            pw = both[c:]
        else:
            inv = inv + _dot(inv, pw_heads)
        yield
    av = _dot(jnp.concatenate([a_ak, a_rk], axis=0), _split_heads(v, lane_lo))
    state = get_state()
    while state is None:
        yield
        state = get_state()
    from_state = _dot_nt(lhs2, state)
    yield
    u = _dot(inv, _split_heads(from_state[0:c] + av[0:c], lane_lo))
    yield
    y = from_state[c:] + av[c:] + _dot(a_rb, _split_heads(u, lane_lo))
    yield
    uv = jnp.concatenate([u, v], axis=0)
    bk = jnp.concatenate([bt * e_end, kt * e_end], axis=0)
    return y, state * e_end + jnp.where(cn["blockdiag"], _dot_tn(uv, bk), 0.0)


def _wkv_readout(y, r, k, v, g, rk, lnx_g, lnx_b, lane_lo):
    mean = _seg_sum(y, lane_lo) * (1.0 / HEAD)
    bonus = _seg_sum(r * k * rk, lane_lo) * v
    yield
    yc = y - mean
    var = _seg_sum(yc * yc, lane_lo) * (1.0 / HEAD)
    yield
    yn = yc * lax.rsqrt(var + GN_EPS) * lnx_g + lnx_b
    return (yn + bonus) * g


def _from_blockdiag(state):
    return state[0:HEAD, 0:HEAD], pltpu.roll(state[HEAD:], HEAD, 1)[:, 0:HEAD]


def _rwkv_seq_kernel(x_ref, w_ref, mu_ref, w2_ref, w0_ref, a0_ref, kk_ref, ka_ref, rk_ref, lg_ref,
                     lb_ref, y_ref, s_out_ref, hist_ref, acts_scr, s_scr, yraw_scr, *, tm):
    t = pl.program_id(1)
    cn = _wkv_consts()
    n_pairs = s_scr.shape[0]
    tt = x_ref.shape[1]

    @pl.when(t == 0)
    def _():
        hist_ref[...] = jnp.zeros_like(hist_ref)
        s_scr[...] = jnp.zeros_like(s_scr)
        yraw_scr[...] = jnp.zeros_like(yraw_scr)

    for sub in range(tt // tm):
        rows = slice(sub * tm, (sub + 1) * tm)
        outs = _rwkv_project_rows(x_ref[0, rows, :], w_ref, mu_ref, w2_ref, w0_ref, a0_ref,
                                  kk_ref, ka_ref, hist_ref)
        for idx, o in enumerate(outs):
            acts_scr[idx, rows, :] = o

    r_ref, lw_ref, k_ref, v_ref, al_ref, be_ref, g_ref = (acts_scr.at[i] for i in range(7))
    n_chunks = tt // CHUNK

    def readout_stages(p, rows):
        ln = pl.ds(p * PAIR, PAIR)
        out = yield from _wkv_readout(
            yraw_scr[:, ln], r_ref[rows, ln], k_ref[rows, ln], v_ref[rows, ln],
            g_ref[rows, ln], rk_ref[:, ln], lg_ref[:, ln], lb_ref[:, ln], cn["lane_lo"])
        y_ref[0, rows, ln] = out

    def chunk_stages(p, rows, get_state):
        ln = pl.ds(p * PAIR, PAIR)
        return _wkv_chunk(r_ref[rows, ln], lw_ref[rows, ln], k_ref[rows, ln],
                          v_ref[rows, ln], al_ref[rows, ln], be_ref[rows, ln],
                          get_state, cn)

    def body(i, states):
        def rows_of(j):
            return pl.ds(pl.multiple_of((CHUNKS_IN_FLIGHT * i + j) * CHUNK, CHUNK), CHUNK)

        prev = pl.ds(pl.multiple_of(jnp.maximum(CHUNKS_IN_FLIGHT * i - 1, 0) * CHUNK, CHUNK),
                     CHUNK)
        end_states = [[None] * n_pairs for _ in range(CHUNKS_IN_FLIGHT)]

        def start_state(j, p):
            return states[p] if j == 0 else end_states[j - 1][p]

        def stages(j, p):
            ln = pl.ds(p * PAIR, PAIR)
            rows = rows_of(j)
            y, state = yield from chunk_stages(p, rows, functools.partial(start_state, j, p))
            end_states[j][p] = state
            if j == CHUNKS_IN_FLIGHT - 1:
                yraw_scr[:, ln] = y
                return
            out = yield from _wkv_readout(
                y, r_ref[rows, ln], k_ref[rows, ln], v_ref[rows, ln],
                g_ref[rows, ln], rk_ref[:, ln], lg_ref[:, ln], lb_ref[:, ln], cn["lane_lo"])
            y_ref[0, rows, ln] = out

        _run_interleaved([readout_stages(p, prev) for p in range(n_pairs)]
                         + [stages(j, p) for j in range(CHUNKS_IN_FLIGHT) for p in range(n_pairs)])
        return tuple(end_states[CHUNKS_IN_FLIGHT - 1])

    assert n_chunks % CHUNKS_IN_FLIGHT == 0
    states = lax.fori_loop(0, n_chunks // CHUNKS_IN_FLIGHT, body,
                           tuple(s_scr[p] for p in range(n_pairs)))
    last = pl.ds((n_chunks - 1) * CHUNK, CHUNK)
    _run_interleaved([readout_stages(p, last) for p in range(n_pairs)])
    for p in range(n_pairs):
        s_scr[p] = states[p]
        s_out_ref[0, 2 * p], s_out_ref[0, 2 * p + 1] = _from_blockdiag(states[p])


def _rwkv_step_kernel(x_ref, xp_ref, wlo_ref, wr_ref, wk_ref, wv_ref, mulo_ref, mur_ref, muk_ref,
                      muv_ref, w2d_ref, w2a_ref, w2g_ref, w0_ref, a0_ref, kk_ref, ka_ref,
                      rk_ref, lg_ref, lb_ref, s_in_ref, y_ref, s_out_ref,
                      xb_scr, act_scr, tr_scr, y_scr, *, steps):
    nseq = s_in_ref.shape[3]

    def mixed(w_ref, mu_ref):
        z = jnp.dot(xb_scr[0], w_ref[...], preferred_element_type=F32)
        zprev = jnp.dot(xb_scr[1], w_ref[...], preferred_element_type=F32)
        return z + mu_ref[...] * (zprev - z)

    @pl.when(pl.program_id(0) == 0)
    def _():
        xb_scr[0] = x_ref[...].astype(BF16)
        xb_scr[1] = xp_ref[...].astype(BF16)
        act_scr[...] = _lora_act(mixed(wlo_ref, mulo_ref)).astype(BF16)

    group_refs = {"r": (wr_ref, mur_ref), "k": (wk_ref, muk_ref), "v": (wv_ref, muv_ref)}
    n_heads = s_in_ref.shape[0]
    lane_lo = lax.broadcasted_iota(jnp.int32, (x_ref.shape[0], PAIR), 1) < HEAD

    def head_sums(v):
        return jnp.concatenate([_seg_sum(v[:, q * PAIR:(q + 1) * PAIR], lane_lo)
                                for q in range(n_heads // 2)], axis=1)

    acts = _rwkv_prep(
        lambda: act_scr[...], lambda group: mixed(*group_refs[group]),
        lambda act: tuple(jnp.dot(act, w2[...], preferred_element_type=F32)
                          for w2 in (w2d_ref, w2a_ref, w2g_ref)),
        w0_ref[...], a0_ref[...], kk_ref[...], ka_ref[...], head_sums)
    idx_r, idx_dec, idx_k, idx_v, idx_al, idx_be, idx_g = range(7)
    for idx, act_rows in enumerate(acts):
        for t in range(steps):
            tile = act_rows[t * nseq:(t + 1) * nseq, :].T
            tr_scr[idx, t] = jnp.exp(tile) if idx == idx_dec else tile

    for h in range(n_heads):
        ch = slice(h * HEAD, (h + 1) * HEAD)

        def row_stages(i, h=h, ch=ch):
            s = s_in_ref[h, i]
            for t in range(steps):
                sa = jnp.sum(s * tr_scr[idx_al, t, ch, :], axis=0, keepdims=True)
                yield
                v_i = tr_scr[idx_v, t, pl.ds(h * HEAD + i, 1), :]
                s = (s * tr_scr[idx_dec, t, ch, :] + sa * tr_scr[idx_be, t, ch, :]
                     + v_i * tr_scr[idx_k, t, ch, :])
                y_scr[t, pl.ds(h * HEAD + i, 1), :] = jnp.sum(
                    s * tr_scr[idx_r, t, ch, :], axis=0, keepdims=True)
                yield
            s_out_ref[h, i] = s

        @pl.loop(0, HEAD // STEP_ROWS_IN_FLIGHT)
        def _(blk):
            _run_interleaved([row_stages(blk * STEP_ROWS_IN_FLIGHT + ii)
                              for ii in range(STEP_ROWS_IN_FLIGHT)])

    for t in range(steps):
        outs = []
        for h in range(n_heads):
            ch = slice(h * HEAD, (h + 1) * HEAD)
            y = y_scr[t, ch, :]
            yc = y - jnp.mean(y, axis=0, keepdims=True)
            var = jnp.mean(yc * yc, axis=0, keepdims=True)
            yn = yc * lax.rsqrt(var + GN_EPS) * lg_ref[ch, :] + lb_ref[ch, :]
            rk_sum = jnp.sum(tr_scr[idx_r, t, ch, :] * tr_scr[idx_k, t, ch, :] * rk_ref[ch, :],
                             axis=0, keepdims=True)
            outs.append((yn + rk_sum * tr_scr[idx_v, t, ch, :]) * tr_scr[idx_g, t, ch, :])
        y_ref[t * nseq:(t + 1) * nseq, :] = jnp.concatenate(outs, axis=0).T


def _rwkv_seq(x, wp, tt, tm):
    b, t, _ = x.shape
    row_spec = pl.BlockSpec((1, tt, D_MODEL), lambda i, j: (i, j, 0))
    vec = _const_spec((1, D_MODEL))
    return pl.pallas_call(
        functools.partial(_rwkv_seq_kernel, tm=tm),
        grid=(b, t // tt),
        in_specs=[row_spec, _const_spec((D_MODEL, RWKV_COLS)), _const_spec((1, RWKV_COLS)),
                  _const_spec((LORA, 3 * D_MODEL))] + [vec] * 7,
        out_specs=[row_spec,
                   pl.BlockSpec((1, N_HEADS, HEAD, HEAD), lambda i, j: (i, 0, 0, 0))],
        out_shape=[jax.ShapeDtypeStruct((b, t, D_MODEL), F32),
                   jax.ShapeDtypeStruct((b, N_HEADS, HEAD, HEAD), F32)],
        scratch_shapes=[pltpu.VMEM((SUBLANES, RWKV_COLS), F32),
                        pltpu.VMEM((7, tt, D_MODEL), F32),
                        pltpu.VMEM((N_PAIRS, PAIR, PAIR), F32),
                        pltpu.VMEM((CHUNK, D_MODEL), F32)],
        compiler_params=pltpu.CompilerParams(
            dimension_semantics=("arbitrary", "arbitrary"), vmem_limit_bytes=VMEM_LIMIT),
        name="rwkv_seq",
    )(x, wp["w_in"], wp["mu"], wp["w2ext"], wp["w0"], wp["a0"], wp["k_k"], wp["k_a"],
      wp["r_k"], wp["lnx_g"], wp["lnx_b"])


def _rwkv_step(x_tm, xprev_tm, s_last, wp, steps):
    n = x_tm.shape[0]
    nseq = s_last.shape[3]

    width = STEP_PAIRS * PAIR
    n_blocks = D_MODEL // width

    def pair_cols(rows, group):
        return pl.BlockSpec((rows, width), lambda p: (0, group * n_blocks + p))

    def lora_cols(rows):
        return pl.BlockSpec((rows, LORA), lambda p: (0, 3 * D_MODEL // LORA))

    full = pl.BlockSpec((n, D_MODEL), lambda p: (0, 0), pipeline_mode=pl.Buffered(1))
    col_spec = pl.BlockSpec((width, nseq), lambda p: (p, 0))
    st_spec = pl.BlockSpec((2 * STEP_PAIRS, HEAD, HEAD, nseq), lambda p: (p, 0, 0, 0))
    cols = [jnp.broadcast_to(wp[name].reshape(D_MODEL, 1), (D_MODEL, nseq))
            for name in ("r_k", "lnx_g", "lnx_b")]
    return pl.pallas_call(
        functools.partial(_rwkv_step_kernel, steps=steps),
        grid=(n_blocks,),
        in_specs=[full, full]
        + [lora_cols(D_MODEL)] + [pair_cols(D_MODEL, g) for g in range(3)]
        + [lora_cols(1)] + [pair_cols(1, g) for g in range(3)]
        + [pair_cols(LORA, g) for g in range(3)]
        + [pair_cols(1, 0)] * 4 + [col_spec] * 3 + [st_spec],
        out_specs=[pl.BlockSpec((n, width), lambda p: (0, p)), st_spec],
        out_shape=[jax.ShapeDtypeStruct((n, D_MODEL), F32),
                   jax.ShapeDtypeStruct(s_last.shape, F32)],
        scratch_shapes=[pltpu.VMEM((2, n, D_MODEL), BF16), pltpu.VMEM((n, LORA), BF16),
                        pltpu.VMEM((7, steps, width, nseq), F32),
                        pltpu.VMEM((steps, width, nseq), F32)],
        compiler_params=pltpu.CompilerParams(
            dimension_semantics=("arbitrary",), vmem_limit_bytes=VMEM_LIMIT),
        name="rwkv_step",
    )(x_tm, xprev_tm, *[wp["w_in"]] * 4, *[wp["mu"]] * 4, *[wp["w2ext"]] * 3,
      wp["w0"], wp["a0"], wp["k_k"], wp["k_a"], *cols, s_last)


def _post_kernel(x_ref, ya_ref, ga_ref, gyb_ref, wo_ref, l1g_ref, l1b_ref, wg_ref, wu_ref, wd_ref,
                 l2g_ref, l2b_ref, y_ref):
    x = x_ref[...]
    merged = ga_ref[...] * ya_ref[...] + gyb_ref[...]
    mix = _dot(merged, wo_ref[...])
    h1 = _layer_norm(ALPHA * x + mix, l1g_ref[...], l1b_ref[...])
    h1b = h1.astype(BF16)
    gate = jnp.dot(h1b, wg_ref[...], preferred_element_type=F32)
    up = jnp.dot(h1b, wu_ref[...], preferred_element_type=F32)
    act = (gate * _sigmoid(gate)) * up
    ffn = _dot(act, wd_ref[...])
    y_ref[...] = _layer_norm(ALPHA * h1 + ffn, l2g_ref[...], l2b_ref[...])


def _post(x, ya, ga, gyb, wp, tm):
    n = x.shape[0]
    row_spec = pl.BlockSpec((tm, D_MODEL), lambda i: (i, 0))
    vec = _const_spec((1, D_MODEL))
    return pl.pallas_call(
        _post_kernel,
        grid=(n // tm,),
        in_specs=[row_spec] * 4 + [_const_spec((D_MODEL, D_MODEL)), vec, vec,
                                   _const_spec((D_MODEL, D_FF)), _const_spec((D_MODEL, D_FF)),
                                   _const_spec((D_FF, D_MODEL)), vec, vec],
        out_specs=row_spec,
        out_shape=jax.ShapeDtypeStruct((n, D_MODEL), F32),
        compiler_params=pltpu.CompilerParams(
            dimension_semantics=("arbitrary",), vmem_limit_bytes=VMEM_LIMIT),
        name="post",
    )(x, ya, ga, gyb, wp["w_o"], wp["ln1_g"], wp["ln1_b"], wp["w_gate"], wp["w_up"], wp["w_down"],
      wp["ln2_g"], wp["ln2_b"])


def _prep_weights(w_in, tmix_mu, w0, w2_decay, a0, a2_iclr, g2_gate, k_k, k_a, r_k, lnx_g, lnx_b,
                  conv_w, conv_b, lru_wa, lru_ba, lru_wi, lru_bi, lru_lambda, w_o,
                  ln1_g, ln1_b, w_ffn_gate, w_ffn_up, w_ffn_down, ln2_g, ln2_b):
    row = lambda v: v.reshape(1, -1).astype(F32)
    zeros = lambda r: jnp.zeros((r, D_MODEL), F32)
    w2ext = jnp.concatenate([
        jnp.concatenate([w2_decay, zeros(DECAY_LORA), zeros(DECAY_LORA)], axis=1),
        jnp.concatenate([zeros(ICLR_LORA), a2_iclr, zeros(ICLR_LORA)], axis=1),
        jnp.concatenate([zeros(GATE_LORA), zeros(GATE_LORA), g2_gate], axis=1)], axis=0)

    def gate_blockdiag(w):
        per = LRU_GROUP // LRU_BS
        w4 = w.reshape(LRU_BLOCKS // per, per, LRU_BS, LRU_BS)
        eye = jnp.eye(per, dtype=w.dtype)
        bd = jnp.einsum("gpcd,pq->gpcqd", w4, eye)
        return bd.reshape(LRU_BLOCKS // per, LRU_GROUP, LRU_GROUP).astype(BF16)

    return dict(
        w_in=w_in.astype(BF16),
        mu=row(tmix_mu), w2ext=w2ext.astype(BF16), w0=row(w0), a0=row(a0), k_k=row(k_k),
        k_a=row(k_a), r_k=row(r_k), lnx_g=row(lnx_g), lnx_b=row(lnx_b),
        conv_w=conv_w.astype(F32), conv_b=row(conv_b), wa_bd=gate_blockdiag(lru_wa),
        ba=row(lru_ba), wi_bd=gate_blockdiag(lru_wi), bi=row(lru_bi), lam=row(lru_lambda),
        w_o=w_o.astype(BF16), ln1_g=row(ln1_g), ln1_b=row(ln1_b), w_gate=w_ffn_gate.astype(BF16),
        w_up=w_ffn_up.astype(BF16), w_down=w_ffn_down.astype(BF16), ln2_g=row(ln2_g),
        ln2_b=row(ln2_b))


def _row_tiles(t):
    rwkv_tile = min(t, 8 * CHUNK)
    return rwkv_tile, min(t, 256), min(t, 256), min(t, 256)


def _prompt_layer(x, wp):
    b, t, _ = x.shape
    rwkv_tile, proj_tile, lru_tile, post_tile = _row_tiles(t)
    ya, s_heads = _rwkv_seq(x, wp, rwkv_tile, proj_tile)
    ga, gyb, conv_tail, h_tail = _lru_seq(x, wp, lru_tile)
    y = _post(x.reshape(b * t, D_MODEL), ya.reshape(b * t, D_MODEL), ga.reshape(b * t, D_MODEL),
              gyb.reshape(b * t, D_MODEL), wp, post_tile).reshape(b, t, D_MODEL)
    return (y, x[:, -1], s_heads, conv_tail[:, SUBLANES - (CONV_W - 1):],
            h_tail[:, SUBLANES - 1])


def _sample_layer(x, shift_buf, wkv0, conv_buf, h0, wp):
    b, t, _ = x.shape
    n = b * t
    x_tm = x.transpose(1, 0, 2).reshape(n, D_MODEL)
    xprev_tm = jnp.concatenate([shift_buf, x_tm[:n - b]], axis=0)
    ya_tm, s_last = _rwkv_step(x_tm, xprev_tm, wkv0.transpose(1, 2, 3, 0), wp, t)
    conv_tm = conv_buf.transpose(1, 0, 2).reshape((CONV_W - 1) * b, D_LRU)
    ga_tm, gyb_tm, conv_new_tm, h_last = _lru_step(x_tm, conv_tm, h0, wp, t, b)
    y_tm = _post(x_tm, ya_tm, ga_tm, gyb_tm, wp, min(n, 256))
    y = y_tm.reshape(t, b, D_MODEL).transpose(1, 0, 2)
    conv_new = conv_new_tm.reshape(CONV_W - 1, b, D_LRU).transpose(1, 0, 2)
    return y, x[:, -1], s_last.transpose(3, 0, 1, 2), conv_new, h_last


def kernel(x_prompt, x_sample, state_shift, state_wkv, state_conv, state_lru, w_in, tmix_mu, w0, w2_decay, a0, a2_iclr, g2_gate, k_k, k_a, r_k, lnx_g, lnx_b, conv_w, conv_b, lru_wa, lru_ba, lru_wi, lru_bi, lru_lambda, w_o, ln1_g, ln1_b, w_ffn_gate, w_ffn_up, w_ffn_down, ln2_g, ln2_b):
    params = (w_in, tmix_mu, w0, w2_decay, a0, a2_iclr, g2_gate, k_k, k_a, r_k, lnx_g, lnx_b,
              conv_w, conv_b, lru_wa, lru_ba, lru_wi, lru_bi, lru_lambda, w_o,
              ln1_g, ln1_b, w_ffn_gate, w_ffn_up, w_ffn_down, ln2_g, ln2_b)
    wp = _prep_weights(*[p[0] for p in params])
    yp, sh_p, wkv_p, conv_p, lru_p = _prompt_layer(x_prompt, wp)
    ys, sh_s, wkv_s, conv_s, lru_s = _sample_layer(
        x_sample, state_shift[0], state_wkv[0], state_conv[0], state_lru[0], wp)
    return (yp, ys, sh_p[None], wkv_p[None], conv_p[None], lru_p[None],
            sh_s[None], wkv_s[None], conv_s[None], lru_s[None])
```
